```python
import math
import jax, jax.numpy as jnp
from jax import lax
import numpy as np

D_MODEL = 1024
BATCH = 4
SEQ = 8192
DEPTH = 2

N_A_LAYERS = DEPTH // 2
N_B_LAYERS = DEPTH - N_A_LAYERS
N_DENSE_LAYERS = (DEPTH + 1) // 2
N_MOE_LAYERS = DEPTH // 2
POOL_WINDOWS = (2, 4, 8, 16)
N_POOL_GROUPS = len(POOL_WINDOWS)
POOL_GROUP_DIM = D_MODEL // N_POOL_GROUPS
HEAD_DIM = 64
N_HEADS = D_MODEL // HEAD_DIM
N_KV_GROUPS = 4
HEADS_PER_GROUP = N_HEADS // N_KV_GROUPS
L_CMP = 32
D_STRIDE = 16
L_SLC = 64
N_SEL = 16
WINDOW = 512
R_CMP = L_CMP // D_STRIDE
R_SLC = L_SLC // D_STRIDE
CMP_HIDDEN = 4 * HEAD_DIM
Q_BLOCK = 128
N_BUCKETS = 32
REL_EXACT = N_BUCKETS // 2
MAX_DISTANCE = 1024
D_FF = 2816
N_EXPERTS = 8
TOP_K = 2
D_FF_EXPERT = 3584
MOE_BLOCK = 512
EPS = 1e-6
NEG_INF = -1e30
SEL_FORCE = 1e6

kernel_name = 'hybrid_pool_nsa_moe_yoco'


def rms_norm(x, g):
    xf = x.astype(jnp.float32)
    y = xf * lax.rsqrt(jnp.mean(xf * xf, axis=-1, keepdims=True) + EPS)
    return (y * g.astype(jnp.float32)).astype(x.dtype)


def modulate(h, shift, scale):
    return h * (1 + scale[:, None, :]) + shift[:, None, :]


def masked_softmax(s, mask):
    s = jnp.where(mask, s, NEG_INF)
    return jnp.where(mask, jax.nn.softmax(s, axis=-1), 0.0)


def rel_bucket(dist):
    d = jnp.maximum(dist, 0)
    ratio = jnp.maximum(d, REL_EXACT).astype(jnp.float32) / REL_EXACT
    large = REL_EXACT + (jnp.log(ratio) / math.log(MAX_DISTANCE / REL_EXACT)
                         * (N_BUCKETS - REL_EXACT)).astype(jnp.int32)
    return jnp.where(d < REL_EXACT, d, jnp.minimum(large, N_BUCKETS - 1))


def swiglu(x, w_gu, w_dn):
    gate, up = jnp.split(x @ w_gu, 2, axis=-1)
    return (jax.nn.silu(gate) * up) @ w_dn


def slc_map_weights():
    w = np.zeros(R_SLC + R_CMP - 1, np.float32)
    for m in range(R_SLC):
        for n in range(R_CMP):
            w[m - n + R_CMP - 1] += 1.0
    return tuple(float(v) for v in w)


def pool_mixer(h, w_grp, scale):
    B, S, _ = h.shape
    hg = h.astype(jnp.float32).reshape(B, S, N_POOL_GROUPS, POOL_GROUP_DIM)
    cs = jnp.pad(jnp.cumsum(hg, axis=1), ((0, 0), (1, 0), (0, 0), (0, 0)))
    t = jnp.arange(S)
    pooled = []
    for gi, w in enumerate(POOL_WINDOWS):
        c_g = cs[:, :, gi]
        lag = jnp.pad(c_g, ((0, 0), (w, 0), (0, 0)))[:, :S + 1]
        cnt = jnp.minimum(t + 1, w).astype(jnp.float32)
        pooled.append((c_g - lag)[:, 1:] / cnt[None, :, None])
    mix = (jnp.stack(pooled, axis=2) - hg).astype(h.dtype)
    y = jnp.einsum('bsgc,gcd->bsgd', mix, w_grp).reshape(B, S, D_MODEL)
    return y * scale


def shared_kv(x, c, kv_ada_w, kv_ada_b, kv_norm_g, kv_w, cmp_pe_k, cmp_pe_v,
              cmp_k_w1, cmp_k_w2, cmp_v_w1, cmp_v_w2, k_gain):
    B, S, _ = x.shape
    shift, scale = jnp.split(jax.nn.silu(c) @ kv_ada_w + kv_ada_b, 2, axis=-1)
    h = modulate(rms_norm(x, kv_norm_g), shift, scale)
    kv = (h @ kv_w).reshape(B, S, 6, N_KV_GROUPS, HEAD_DIM).transpose(2, 0, 3, 1, 4)
    k_c, v_c, k_s, v_s, k_w, v_w = kv[0], kv[1], kv[2], kv[3], kv[4], kv[5]

    def compress(u, pe, w1, w2):
        r = u.reshape(B, N_KV_GROUPS, S // D_STRIDE, D_STRIDE, HEAD_DIM)
        nc = S // D_STRIDE - R_CMP + 1
        blocks = jnp.concatenate([r[:, :, i:i + nc] for i in range(R_CMP)], axis=3) + pe
        flat = blocks.reshape(B, N_KV_GROUPS, nc, L_CMP * HEAD_DIM)
        return jax.nn.gelu(flat @ w1) @ w2

    kc = rms_norm(compress(k_c, cmp_pe_k, cmp_k_w1, cmp_k_w2), k_gain[0])
    vc = compress(v_c, cmp_pe_v, cmp_v_w1, cmp_v_w2)
    ks = rms_norm(k_s, k_gain[1]).reshape(B, N_KV_GROUPS, S // L_SLC, L_SLC, HEAD_DIM)
    vs = v_s.reshape(B, N_KV_GROUPS, S // L_SLC, L_SLC, HEAD_DIM)
    pad = ((0, 0), (0, 0), (WINDOW, 0), (0, 0))
    kw = jnp.pad(rms_norm(k_w, k_gain[2]), pad)
    vw = jnp.pad(v_w, pad)
    return kc, vc, ks, vs, kw, vw


def nsa_mixer(h, kv, w_qg, q_gain, w_o, rel_bias):
    B, S, _ = h.shape
    kc, vc, ks, vs, kw, vw = kv
    G, NH, QB = N_KV_GROUPS, HEADS_PER_GROUP, Q_BLOCK
    NQ = S // QB
    NS = S // L_SLC
    NC = kc.shape[2]
    n_sel = min(N_SEL, NS)
    proj = h @ w_qg
    q = rms_norm(proj[..., :N_HEADS * HEAD_DIM].reshape(B, S, N_HEADS, HEAD_DIM), q_gain) * HEAD_DIM ** -0.5
    gates = jax.nn.sigmoid(proj[..., N_HEADS * HEAD_DIM:]).reshape(B, S, N_HEADS, 3)
    q_blk = q.reshape(B, NQ, QB, G, NH, HEAD_DIM).transpose(0, 1, 3, 4, 2, 5).reshape(B * NQ, G, NH, QB, HEAD_DIM)
    g_blk = gates.reshape(B, NQ, QB, G, NH, 3).transpose(0, 1, 5, 3, 4, 2).reshape(B * NQ, 3, G, NH, QB)
    b_ids = jnp.repeat(jnp.arange(B), NQ)
    q_ids = jnp.tile(jnp.arange(NQ), B)
    cmp_end = jnp.arange(NC) * D_STRIDE + L_CMP - 1
    tab_g = rel_bias.astype(jnp.float32).reshape(N_BUCKETS, G, NH).transpose(1, 0, 2)
    grp = jnp.arange(G)
    map_w = slc_map_weights()
    back = R_SLC * NS + R_CMP + R_SLC - NC - (R_CMP - 1)
    blk = jnp.arange(NS)

    def head_bias(dist):
        return rel_bias.astype(jnp.float32)[rel_bucket(dist)].transpose(2, 0, 1).reshape(G, NH, *dist.shape)

    def block(args):
        qb, gb, b, qi = args
        t = qi * QB + jnp.arange(QB)
        dist_c = t[:, None] - cmp_end[None, :]
        s_c = jnp.einsum('gnqd,gkd->gnqk', qb, kc[b]).astype(jnp.float32) + head_bias(dist_c)
        p_c = masked_softmax(s_c, dist_c >= 0)
        o_c = jnp.einsum('gnqk,gkd->gnqd', p_c.astype(qb.dtype), vc[b])
        imp = jnp.pad(p_c.sum(axis=1), ((0, 0), (0, 0), (R_CMP - 1, back)))
        p_slc = sum(wk * imp[..., k:k + R_SLC * NS:R_SLC] for k, wk in enumerate(map_w))
        cur = t // L_SLC
        forced = (blk[None] == 0) | (blk[None] == cur[:, None]) | (blk[None] == cur[:, None] - 1)
        valid = blk[None] * L_SLC <= t[:, None]
        score = jnp.where(forced, SEL_FORCE, jnp.where(valid, p_slc, -SEL_FORCE))
        _, idx = lax.top_k(score, n_sel)
        k_sel = ks[b][grp[:, None, None], idx]
        v_sel = vs[b][grp[:, None, None], idx]
        pos_s = idx[..., None] * L_SLC + jnp.arange(L_SLC)
        dist_s = t[None, :, None, None] - pos_s
        bias_s = jnp.moveaxis(tab_g[grp[:, None, None, None], rel_bucket(dist_s)], -1, 1)
        s_s = jnp.einsum('gnqd,gqjld->gnqjl', qb, k_sel).astype(jnp.float32) + bias_s
        p_s = masked_softmax(s_s.reshape(G, NH, QB, n_sel * L_SLC),
                             (dist_s >= 0).reshape(G, 1, QB, n_sel * L_SLC)).reshape(G, NH, QB, n_sel, L_SLC)
        o_s = jnp.einsum('gnqjl,gqjld->gnqd', p_s.astype(qb.dtype), v_sel)
        k_win = lax.dynamic_slice_in_dim(kw[b], qi * QB, WINDOW + QB, axis=1)
        v_win = lax.dynamic_slice_in_dim(vw[b], qi * QB, WINDOW + QB, axis=1)
        pos_w = qi * QB - WINDOW + jnp.arange(WINDOW + QB)
        dist_w = t[:, None] - pos_w[None, :]
        mask_w = (dist_w >= 0) & (dist_w < WINDOW) & (pos_w[None, :] >= 0)
        s_w = jnp.einsum('gnqd,gkd->gnqk', qb, k_win).astype(jnp.float32) + head_bias(dist_w)
        p_w = masked_softmax(s_w, mask_w)
        o_w = jnp.einsum('gnqk,gkd->gnqd', p_w.astype(qb.dtype), v_win)
        return gb[0][..., None] * o_c + gb[1][..., None] * o_s + gb[2][..., None] * o_w

    out = lax.map(block, (q_blk, g_blk, b_ids, q_ids))
    out = out.reshape(B, NQ, G, NH, QB, HEAD_DIM).transpose(0, 1, 4, 2, 3, 5).reshape(B, S, N_HEADS * HEAD_DIM)
    return out @ w_o


def moe_ffn(h, w_router, b_router, w_gu, w_dn):
    B, S, D = h.shape
    T = B * S
    xf = h.reshape(T, D)
    logits = (xf @ w_router).astype(jnp.float32) + b_router
    top_logit, top_e = lax.top_k(logits, TOP_K)
    top_w = jax.nn.softmax(top_logit, axis=-1)
    A = T * TOP_K
    flat_e = top_e.reshape(A)
    flat_tok = jnp.repeat(jnp.arange(T), TOP_K)
    flat_w = top_w.reshape(A)
    order = jnp.argsort(flat_e)
    se = flat_e[order]
    counts = jnp.zeros((N_EXPERTS,), jnp.int32).at[flat_e].add(1)
    starts = jnp.cumsum(counts) - counts
    pcounts = (counts + MOE_BLOCK - 1) // MOE_BLOCK * MOE_BLOCK
    pends = jnp.cumsum(pcounts)
    pstarts = pends - pcounts
    dest = pstarts[se] + jnp.arange(A) - starts[se]
    n_blocks = -(-(A + N_EXPERTS * (MOE_BLOCK - 1)) // MOE_BLOCK)
    n_rows = n_blocks * MOE_BLOCK
    row_tok = jnp.zeros((n_rows,), jnp.int32).at[dest].set(flat_tok[order])
    row_w = jnp.zeros((n_rows,), jnp.float32).at[dest].set(flat_w[order])
    blk_e = jnp.minimum(jnp.searchsorted(pends, jnp.arange(n_blocks) * MOE_BLOCK, side='right'), N_EXPERTS - 1)

    def expert_block(args):
        xb, e = args
        return swiglu(xb, w_gu[e], w_dn[e])

    y = lax.map(expert_block, (xf[row_tok].reshape(n_blocks, MOE_BLOCK, D), blk_e)).reshape(n_rows, D)
    out = jnp.zeros((T, D), h.dtype).at[row_tok].add(y * row_w[:, None].astype(y.dtype))
    return out.reshape(B, S, D)


def setup_inputs(seed: int = 0) -> dict:
    key = jax.random.key(seed)
    ks = jax.random.split(key, 28)
    D = D_MODEL
    QD = N_HEADS * HEAD_DIM

    def nrm(k, shape, s):
        return jax.random.normal(k, shape, jnp.float32) * s

    return {
        'x': nrm(ks[0], (BATCH, SEQ, D), 1.0),
        'c': nrm(ks[1], (BATCH, D), 1.0),
        'ada_w': nrm(ks[2], (DEPTH, D, 6 * D), 0.5 * D ** -0.5),
        'ada_b': nrm(ks[3], (DEPTH, 6 * D), 0.02),
        'norm_g': 1.0 + nrm(ks[4], (DEPTH, 2, D), 0.02),
        'pool_w': nrm(ks[5], (N_A_LAYERS, N_POOL_GROUPS, POOL_GROUP_DIM, POOL_GROUP_DIM), POOL_GROUP_DIM ** -0.5),
        'pool_scale': 1.0 + nrm(ks[6], (N_A_LAYERS, D), 0.02),
        'q_w': nrm(ks[7], (N_B_LAYERS, D, QD + 3 * N_HEADS), D ** -0.5),
        'q_gain': 1.0 + nrm(ks[8], (N_B_LAYERS, HEAD_DIM), 0.02),
        'o_w': nrm(ks[9], (N_B_LAYERS, QD, D), QD ** -0.5),
        'kv_ada_w': nrm(ks[10], (D, 2 * D), 0.5 * D ** -0.5),
        'kv_ada_b': nrm(ks[11], (2 * D,), 0.02),
        'kv_norm_g': 1.0 + nrm(ks[12], (D,), 0.02),
        'kv_w': nrm(ks[13], (D, 6 * N_KV_GROUPS * HEAD_DIM), D ** -0.5),
        'cmp_pe_k': nrm(ks[14], (L_CMP, HEAD_DIM), 0.1),
        'cmp_pe_v': nrm(ks[15], (L_CMP, HEAD_DIM), 0.1),
        'cmp_k_w1': nrm(ks[16], (L_CMP * HEAD_DIM, CMP_HIDDEN), (L_CMP * HEAD_DIM) ** -0.5),
        'cmp_k_w2': nrm(ks[17], (CMP_HIDDEN, HEAD_DIM), CMP_HIDDEN ** -0.5),
        'cmp_v_w1': nrm(ks[18], (L_CMP * HEAD_DIM, CMP_HIDDEN), (L_CMP * HEAD_DIM) ** -0.5),
        'cmp_v_w2': nrm(ks[19], (CMP_HIDDEN, HEAD_DIM), CMP_HIDDEN ** -0.5),
        'k_gain': 1.0 + nrm(ks[20], (3, HEAD_DIM), 0.02),
        'rel_bias': nrm(ks[21], (N_BUCKETS, N_HEADS), 0.5),
        'ffn_gu': nrm(ks[22], (N_DENSE_LAYERS, D, 2 * D_FF), D ** -0.5),
        'ffn_dn': nrm(ks[23], (N_DENSE_LAYERS, D_FF, D), D_FF ** -0.5),
        'router_w': nrm(ks[24], (N_MOE_LAYERS, D, N_EXPERTS), D ** -0.5),
        'router_b': nrm(ks[25], (N_MOE_LAYERS, N_EXPERTS), 0.01),
        'exp_gu': nrm(ks[26], (N_MOE_LAYERS, N_EXPERTS, D, 2 * D_FF_EXPERT), D ** -0.5),
        'exp_dn': nrm(ks[27], (N_MOE_LAYERS, N_EXPERTS, D_FF_EXPERT, D), D_FF_EXPERT ** -0.5),
    }


def reference(x, c, ada_w, ada_b, norm_g, pool_w, pool_scale, q_w, q_gain, o_w,
              kv_ada_w, kv_ada_b, kv_norm_g, kv_w, cmp_pe_k, cmp_pe_v,
              cmp_k_w1, cmp_k_w2, cmp_v_w1, cmp_v_w2, k_gain, rel_bias,
              ffn_gu, ffn_dn, router_w, router_b, exp_gu, exp_dn):
    kv = None
    for l in range(DEPTH):
        sh1, sc1, g1, sh2, sc2, g2 = jnp.split(jax.nn.silu(c) @ ada_w[l] + ada_b[l], 6, axis=-1)
        if l >= N_A_LAYERS and kv is None:
            kv = shared_kv(x, c, kv_ada_w, kv_ada_b, kv_norm_g, kv_w, cmp_pe_k, cmp_pe_v,
                           cmp_k_w1, cmp_k_w2, cmp_v_w1, cmp_v_w2, k_gain)
        h = modulate(rms_norm(x, norm_g[l, 0]), sh1, sc1)
        if l < N_A_LAYERS:
            mix = pool_mixer(h, pool_w[l], pool_scale[l])
        else:
            j = l - N_A_LAYERS
            mix = nsa_mixer(h, kv, q_w[j], q_gain[j], o_w[j], rel_bias)
        x = x + g1[:, None, :] * mix
        h = modulate(rms_norm(x, norm_g[l, 1]), sh2, sc2)
        if l % 2 == 0:
            f = swiglu(h, ffn_gu[l // 2], ffn_dn[l // 2])
        else:
            f = moe_ffn(h, router_w[l // 2], router_b[l // 2], exp_gu[l // 2], exp_dn[l // 2])
        x = x + g2[:, None, :] * f
    return x
```

```python
import functools
import math

import jax
import jax.numpy as jnp
import numpy as np
from jax import lax
from jax.experimental import pallas as pl
from jax.experimental.pallas import tpu as pltpu

D_MODEL = 1024
DEPTH = 2
N_A_LAYERS = DEPTH // 2
POOL_WINDOWS = (2, 4, 8, 16)
N_POOL_GROUPS = len(POOL_WINDOWS)
POOL_GROUP_DIM = D_MODEL // N_POOL_GROUPS
HEAD_DIM = 64
N_HEADS = D_MODEL // HEAD_DIM
N_KV_GROUPS = 4
HEADS_PER_GROUP = N_HEADS // N_KV_GROUPS
L_CMP = 32
D_STRIDE = 16
L_SLC = 64
N_SEL = 16
WINDOW = 512
R_CMP = L_CMP // D_STRIDE
R_SLC = L_SLC // D_STRIDE
Q_BLOCK = 128
N_BUCKETS = 32
REL_EXACT = N_BUCKETS // 2
MAX_DISTANCE = 1024
N_EXPERTS = 8
TOP_K = 2
EPS = 1e-6
NEG_INF = -1e30
SEL_FORCE = 1e6

V7X_VMEM_LIMIT_BYTES = 48 * 1024 * 1024
ROW_TILE = 512
FF_TILE_DENSE = 1408
FF_TILE_EXPERT = 512


def _rms_norm(x, g):
    xf = x.astype(jnp.float32)
    y = xf * lax.rsqrt(jnp.mean(xf * xf, axis=-1, keepdims=True) + EPS)
    return (y * g.astype(jnp.float32)).astype(x.dtype)


def _modulate(h, shift, scale):
    return h * (1 + scale[:, None, :]) + shift[:, None, :]


def _masked_softmax(s, mask):
    s = jnp.where(mask, s, NEG_INF)
    return jnp.where(mask, jax.nn.softmax(s, axis=-1), 0.0)


def _rel_bucket(dist):
    d = jnp.maximum(dist, 0)
    ratio = jnp.maximum(d, REL_EXACT).astype(jnp.float32) / REL_EXACT
    large = REL_EXACT + (jnp.log(ratio) / math.log(MAX_DISTANCE / REL_EXACT)
                         * (N_BUCKETS - REL_EXACT)).astype(jnp.int32)
    return jnp.where(d < REL_EXACT, d, jnp.minimum(large, N_BUCKETS - 1))


def _slc_map_weights():
    w = np.zeros(R_SLC + R_CMP - 1, np.float32)
    for m in range(R_SLC):
        for n in range(R_CMP):
            w[m - n + R_CMP - 1] += 1.0
    return tuple(float(v) for v in w)


def _swiglu_body(blk_e_ref, x_ref, wg_ref, wu_ref, wd_ref, o_ref, acc_ref, *, n_ff_steps):
    del blk_e_ref
    j = pl.program_id(1)

    @pl.when(j == 0)
    def _():
        acc_ref[...] = jnp.zeros_like(acc_ref)

    x = x_ref[...]
    gate = jnp.dot(x, wg_ref[0], preferred_element_type=jnp.float32)
    up = jnp.dot(x, wu_ref[0], preferred_element_type=jnp.float32)
    act = (gate * jax.nn.sigmoid(gate) * up).astype(jnp.bfloat16)
    acc_ref[...] += jnp.dot(act, wd_ref[0], preferred_element_type=jnp.float32)

    @pl.when(j == n_ff_steps - 1)
    def _():
        o_ref[...] = acc_ref[...]


def _grouped_swiglu(x_rows, blk_e, w_gu, w_dn, *, ff_tile):
    n_rows, d = x_rows.shape
    n_exp, d_ff, _ = w_dn.shape
    assert n_rows % ROW_TILE == 0 and d_ff % ff_tile == 0
    n_ff_steps = d_ff // ff_tile
    grid = (n_rows // ROW_TILE, n_ff_steps)
    return pl.pallas_call(
        functools.partial(_swiglu_body, n_ff_steps=n_ff_steps),
        grid_spec=pltpu.PrefetchScalarGridSpec(
            num_scalar_prefetch=1,
            grid=grid,
            in_specs=[
                pl.BlockSpec((ROW_TILE, d), lambda i, j, e: (i, 0)),
                pl.BlockSpec((1, d, ff_tile), lambda i, j, e: (e[i], 0, j)),
                pl.BlockSpec((1, d, ff_tile), lambda i, j, e: (e[i], 0, j + n_ff_steps)),
                pl.BlockSpec((1, ff_tile, d), lambda i, j, e: (e[i], j, 0)),
            ],
            out_specs=pl.BlockSpec((ROW_TILE, d), lambda i, j, e: (i, 0)),
            scratch_shapes=[pltpu.VMEM((ROW_TILE, d), jnp.float32)],
        ),
        out_shape=jax.ShapeDtypeStruct((n_rows, d), jnp.float32),
        compiler_params=pltpu.CompilerParams(
            dimension_semantics=("arbitrary", "arbitrary"),
            vmem_limit_bytes=V7X_VMEM_LIMIT_BYTES,
        ),
        name="grouped_swiglu",
    )(blk_e, x_rows, w_gu, w_gu, w_dn)


def _pool_mixer(h, w_grp, scale):
    B, S, _ = h.shape
    hg = h.astype(jnp.float32).reshape(B, S, N_POOL_GROUPS, POOL_GROUP_DIM)
    cs = jnp.pad(jnp.cumsum(hg, axis=1), ((0, 0), (1, 0), (0, 0), (0, 0)))
    t = jnp.arange(S)
    pooled = []
    for gi, w in enumerate(POOL_WINDOWS):
        c_g = cs[:, :, gi]
        lag = jnp.pad(c_g, ((0, 0), (w, 0), (0, 0)))[:, :S + 1]
        cnt = jnp.minimum(t + 1, w).astype(jnp.float32)
        pooled.append((c_g - lag)[:, 1:] / cnt[None, :, None])
    mix = (jnp.stack(pooled, axis=2) - hg).astype(h.dtype)
    y = jnp.einsum('bsgc,gcd->bsgd', mix, w_grp).reshape(B, S, D_MODEL)
    return y * scale


def _shared_kv(x, c, kv_ada_w, kv_ada_b, kv_norm_g, kv_w, cmp_pe_k, cmp_pe_v,
               cmp_k_w1, cmp_k_w2, cmp_v_w1, cmp_v_w2, k_gain):
    B, S, _ = x.shape
    shift, scale = jnp.split(jax.nn.silu(c) @ kv_ada_w + kv_ada_b, 2, axis=-1)
    h = _modulate(_rms_norm(x, kv_norm_g), shift, scale)
    kv = (h @ kv_w).reshape(B, S, 6, N_KV_GROUPS, HEAD_DIM).transpose(2, 0, 3, 1, 4)
    k_c, v_c, k_s, v_s, k_w, v_w = kv[0], kv[1], kv[2], kv[3], kv[4], kv[5]

    def compress(u, pe, w1, w2):
        r = u.reshape(B, N_KV_GROUPS, S // D_STRIDE, D_STRIDE, HEAD_DIM)
        nc = S // D_STRIDE - R_CMP + 1
        blocks = jnp.concatenate([r[:, :, i:i + nc] for i in range(R_CMP)], axis=3) + pe
        flat = blocks.reshape(B, N_KV_GROUPS, nc, L_CMP * HEAD_DIM)
        return jax.nn.gelu(flat @ w1) @ w2

    kc = _rms_norm(compress(k_c, cmp_pe_k, cmp_k_w1, cmp_k_w2), k_gain[0])
    vc = compress(v_c, cmp_pe_v, cmp_v_w1, cmp_v_w2)
    ks = _rms_norm(k_s, k_gain[1]).reshape(B, N_KV_GROUPS, S // L_SLC, L_SLC, HEAD_DIM)
    vs = v_s.reshape(B, N_KV_GROUPS, S // L_SLC, L_SLC, HEAD_DIM)
    pad = ((0, 0), (0, 0), (WINDOW, 0), (0, 0))
    kw = jnp.pad(_rms_norm(k_w, k_gain[2]), pad)
    vw = jnp.pad(v_w, pad)
    return kc, vc, ks, vs, kw, vw


def _nsa_mixer(h, kv, w_qg, q_gain, w_o, rel_bias):
    B, S, _ = h.shape
    kc, vc, ks, vs, kw, vw = kv
    G, NH, QB = N_KV_GROUPS, HEADS_PER_GROUP, Q_BLOCK
    NQ = S // QB
    NS = S // L_SLC
    NC = kc.shape[2]
    n_sel = min(N_SEL, NS)
    proj = h @ w_qg
    q = _rms_norm(proj[..., :N_HEADS * HEAD_DIM].reshape(B, S, N_HEADS, HEAD_DIM), q_gain) * HEAD_DIM ** -0.5
    gates = jax.nn.sigmoid(proj[..., N_HEADS * HEAD_DIM:]).reshape(B, S, N_HEADS, 3)
    q_blk = q.reshape(B, NQ, QB, G, NH, HEAD_DIM).transpose(0, 1, 3, 4, 2, 5).reshape(B * NQ, G, NH, QB, HEAD_DIM)
    g_blk = gates.reshape(B, NQ, QB, G, NH, 3).transpose(0, 1, 5, 3, 4, 2).reshape(B * NQ, 3, G, NH, QB)
    b_ids = jnp.repeat(jnp.arange(B), NQ)
    q_ids = jnp.tile(jnp.arange(NQ), B)
    cmp_end = jnp.arange(NC) * D_STRIDE + L_CMP - 1
    tab_g = rel_bias.astype(jnp.float32).reshape(N_BUCKETS, G, NH).transpose(1, 0, 2)
    grp = jnp.arange(G)
    map_w = _slc_map_weights()
    back = R_SLC * NS + R_CMP + R_SLC - NC - (R_CMP - 1)
    blk = jnp.arange(NS)

    def head_bias(dist):
        return rel_bias.astype(jnp.float32)[_rel_bucket(dist)].transpose(2, 0, 1).reshape(G, NH, *dist.shape)

    def block(args):
        qb, gb, b, qi = args
        t = qi * QB + jnp.arange(QB)
        dist_c = t[:, None] - cmp_end[None, :]
        s_c = jnp.einsum('gnqd,gkd->gnqk', qb, kc[b]).astype(jnp.float32) + head_bias(dist_c)
        p_c = _masked_softmax(s_c, dist_c >= 0)
        o_c = jnp.einsum('gnqk,gkd->gnqd', p_c.astype(qb.dtype), vc[b])
        imp = jnp.pad(p_c.sum(axis=1), ((0, 0), (0, 0), (R_CMP - 1, back)))
        p_slc = sum(wk * imp[..., k:k + R_SLC * NS:R_SLC] for k, wk in enumerate(map_w))
        cur = t // L_SLC
        forced = (blk[None] == 0) | (blk[None] == cur[:, None]) | (blk[None] == cur[:, None] - 1)
        valid = blk[None] * L_SLC <= t[:, None]
        score = jnp.where(forced, SEL_FORCE, jnp.where(valid, p_slc, -SEL_FORCE))
        _, idx = lax.top_k(score, n_sel)
        k_sel = ks[b][grp[:, None, None], idx]
        v_sel = vs[b][grp[:, None, None], idx]
        pos_s = idx[..., None] * L_SLC + jnp.arange(L_SLC)
        dist_s = t[None, :, None, None] - pos_s
        bias_s = jnp.moveaxis(tab_g[grp[:, None, None, None], _rel_bucket(dist_s)], -1, 1)
        s_s = jnp.einsum('gnqd,gqjld->gnqjl', qb, k_sel).astype(jnp.float32) + bias_s
        p_s = _masked_softmax(s_s.reshape(G, NH, QB, n_sel * L_SLC),
                              (dist_s >= 0).reshape(G, 1, QB, n_sel * L_SLC)).reshape(G, NH, QB, n_sel, L_SLC)
        o_s = jnp.einsum('gnqjl,gqjld->gnqd', p_s.astype(qb.dtype), v_sel)
        k_win = lax.dynamic_slice_in_dim(kw[b], qi * QB, WINDOW + QB, axis=1)
        v_win = lax.dynamic_slice_in_dim(vw[b], qi * QB, WINDOW + QB, axis=1)
        pos_w = qi * QB - WINDOW + jnp.arange(WINDOW + QB)
        dist_w = t[:, None] - pos_w[None, :]
        mask_w = (dist_w >= 0) & (dist_w < WINDOW) & (pos_w[None, :] >= 0)
        s_w = jnp.einsum('gnqd,gkd->gnqk', qb, k_win).astype(jnp.float32) + head_bias(dist_w)
        p_w = _masked_softmax(s_w, mask_w)
        o_w = jnp.einsum('gnqk,gkd->gnqd', p_w.astype(qb.dtype), v_win)
        return gb[0][..., None] * o_c + gb[1][..., None] * o_s + gb[2][..., None] * o_w

    out = lax.map(block, (q_blk, g_blk, b_ids, q_ids))
    out = out.reshape(B, NQ, G, NH, QB, HEAD_DIM).transpose(0, 1, 4, 2, 3, 5).reshape(B, S, N_HEADS * HEAD_DIM)
    return out @ w_o


def _moe_ffn(h, w_router, b_router, w_gu, w_dn):
    B, S, D = h.shape
    T = B * S
    xf = h.reshape(T, D)
    logits = jnp.dot(xf, w_router, precision=lax.Precision.HIGHEST).astype(jnp.float32) + b_router
    top_logit, top_e = lax.top_k(logits, TOP_K)
    top_w = jax.nn.softmax(top_logit, axis=-1)
    A = T * TOP_K
    flat_e = top_e.reshape(A)
    flat_tok = jnp.repeat(jnp.arange(T), TOP_K)
    flat_w = top_w.reshape(A)
    order = jnp.argsort(flat_e)
    se = flat_e[order]
    counts = jnp.zeros((N_EXPERTS,), jnp.int32).at[flat_e].add(1)
    starts = jnp.cumsum(counts) - counts
    pcounts = (counts + ROW_TILE - 1) // ROW_TILE * ROW_TILE
    pends = jnp.cumsum(pcounts)
    pstarts = pends - pcounts
    dest = pstarts[se] + jnp.arange(A) - starts[se]
    n_blocks = -(-(A + N_EXPERTS * (ROW_TILE - 1)) // ROW_TILE)
    n_rows = n_blocks * ROW_TILE
    row_tok = jnp.zeros((n_rows,), jnp.int32).at[dest].set(flat_tok[order])
    row_w = jnp.zeros((n_rows,), jnp.float32).at[dest].set(flat_w[order])
    blk_e = jnp.minimum(jnp.searchsorted(pends, jnp.arange(n_blocks) * ROW_TILE, side='right'),
                        N_EXPERTS - 1).astype(jnp.int32)
    x_rows = xf.astype(jnp.bfloat16)[row_tok]
    y = _grouped_swiglu(x_rows, blk_e, w_gu.astype(jnp.bfloat16), w_dn.astype(jnp.bfloat16),
                        ff_tile=FF_TILE_EXPERT)
    out = jnp.zeros((T, D), h.dtype).at[row_tok].add(y * row_w[:, None])
    return out.reshape(B, S, D)


def _dense_ffn(h, w_gu, w_dn):
    B, S, D = h.shape
    T = B * S
    blk_e = jnp.zeros((T // ROW_TILE,), jnp.int32)
    y = _grouped_swiglu(h.reshape(T, D).astype(jnp.bfloat16), blk_e,
                        w_gu.astype(jnp.bfloat16)[None], w_dn.astype(jnp.bfloat16)[None],
                        ff_tile=FF_TILE_DENSE)
    return y.reshape(B, S, D)


def kernel(x, c, ada_w, ada_b, norm_g, pool_w, pool_scale, q_w, q_gain, o_w, kv_ada_w, kv_ada_b, kv_norm_g, kv_w, cmp_pe_k, cmp_pe_v, cmp_k_w1, cmp_k_w2, cmp_v_w1, cmp_v_w2, k_gain, rel_bias, ffn_gu, ffn_dn, router_w, router_b, exp_gu, exp_dn):
    kv = None
    for l in range(DEPTH):
        sh1, sc1, g1, sh2, sc2, g2 = jnp.split(jax.nn.silu(c) @ ada_w[l] + ada_b[l], 6, axis=-1)
        if l >= N_A_LAYERS and kv is None:
            kv = _shared_kv(x, c, kv_ada_w, kv_ada_b, kv_norm_g, kv_w, cmp_pe_k, cmp_pe_v,
                            cmp_k_w1, cmp_k_w2, cmp_v_w1, cmp_v_w2, k_gain)
        h = _modulate(_rms_norm(x, norm_g[l, 0]), sh1, sc1)
        if l < N_A_LAYERS:
            mix = _pool_mixer(h, pool_w[l], pool_scale[l])
        else:
            j = l - N_A_LAYERS
            mix = _nsa_mixer(h, kv, q_w[j], q_gain[j], o_w[j], rel_bias)
        x = x + g1[:, None, :] * mix
        h = _modulate(_rms_norm(x, norm_g[l, 1]), sh2, sc2)
        if l % 2 == 0:
            f = _dense_ffn(h, ffn_gu[l // 2], ffn_dn[l // 2])
        else:
            f = _moe_ffn(h, router_w[l // 2], router_b[l // 2], exp_gu[l // 2], exp_dn[l // 2])
        x = x + g2[:, None, :] * f
    return x
```

```python
import functools
import math

import jax
import jax.numpy as jnp
import numpy as np
from jax import lax
from jax.experimental import pallas as pl
from jax.experimental.pallas import tpu as pltpu

D_MODEL = 1024
DEPTH = 2
N_A_LAYERS = DEPTH // 2
POOL_WINDOWS = (2, 4, 8, 16)
N_POOL_GROUPS = len(POOL_WINDOWS)
POOL_GROUP_DIM = D_MODEL // N_POOL_GROUPS
HEAD_DIM = 64
N_HEADS = D_MODEL // HEAD_DIM
N_KV_GROUPS = 4
HEADS_PER_GROUP = N_HEADS // N_KV_GROUPS
L_CMP = 32
D_STRIDE = 16
L_SLC = 64
N_SEL = 16
WINDOW = 512
R_CMP = L_CMP // D_STRIDE
R_SLC = L_SLC // D_STRIDE
N_BUCKETS = 32
REL_EXACT = N_BUCKETS // 2
MAX_DISTANCE = 1024
N_EXPERTS = 8
TOP_K = 2
EPS = 1e-6
NEG_INF = -1e30
SEL_FORCE = 1e6

V7X_VMEM_LIMIT_BYTES = 48 * 1024 * 1024
LANES = 128
ROW_TILE = 512
FF_TILE_DENSE = 1408
FF_TILE_EXPERT = 512

QB = 128
KEY_TILE = 2 * QB
QROWS = HEADS_PER_GROUP * QB
NBLK = 128
CMP_PAD = 128
FAR_DIST = MAX_DISTANCE
N_TOEP = FAR_DIST // QB + 3
BAND_LEFT = CMP_PAD - 16
assert BAND_LEFT * D_STRIDE + (L_CMP - 1) - 2 * QB >= FAR_DIST


def _rms_norm(x, g):
    xf = x.astype(jnp.float32)
    y = xf * lax.rsqrt(jnp.mean(xf * xf, axis=-1, keepdims=True) + EPS)
    return (y * g.astype(jnp.float32)).astype(x.dtype)


def _modulate(h, shift, scale):
    return h * (1 + scale[:, None, :]) + shift[:, None, :]


def _rel_bucket(dist):
    d = jnp.maximum(dist, 0)
    ratio = jnp.maximum(d, REL_EXACT).astype(jnp.float32) / REL_EXACT
    large = REL_EXACT + (jnp.log(ratio) / math.log(MAX_DISTANCE / REL_EXACT)
                         * (N_BUCKETS - REL_EXACT)).astype(jnp.int32)
    return jnp.where(d < REL_EXACT, d, jnp.minimum(large, N_BUCKETS - 1))


def _swiglu_body(blk_e_ref, x_ref, wg_ref, wu_ref, wd_ref, o_ref, acc_ref, *, n_ff_steps):
    del blk_e_ref
    j = pl.program_id(1)

    @pl.when(j == 0)
    def _():
        acc_ref[...] = jnp.zeros_like(acc_ref)

    x = x_ref[...]
    gate = jnp.dot(x, wg_ref[0], preferred_element_type=jnp.float32)
    up = jnp.dot(x, wu_ref[0], preferred_element_type=jnp.float32)
    act = (gate * jax.nn.sigmoid(gate) * up).astype(jnp.bfloat16)
    acc_ref[...] += jnp.dot(act, wd_ref[0], preferred_element_type=jnp.float32)

    @pl.when(j == n_ff_steps - 1)
    def _():
        o_ref[...] = acc_ref[...]


def _grouped_swiglu(x_rows, blk_e, w_gu, w_dn, *, ff_tile):
    n_rows, d = x_rows.shape
    n_exp, d_ff, _ = w_dn.shape
    assert n_rows % ROW_TILE == 0 and d_ff % ff_tile == 0
    n_ff_steps = d_ff // ff_tile
    grid = (n_rows // ROW_TILE, n_ff_steps)
    return pl.pallas_call(
        functools.partial(_swiglu_body, n_ff_steps=n_ff_steps),
        grid_spec=pltpu.PrefetchScalarGridSpec(
            num_scalar_prefetch=1,
            grid=grid,
            in_specs=[
                pl.BlockSpec((ROW_TILE, d), lambda i, j, e: (i, 0)),
                pl.BlockSpec((1, d, ff_tile), lambda i, j, e: (e[i], 0, j)),
                pl.BlockSpec((1, d, ff_tile), lambda i, j, e: (e[i], 0, j + n_ff_steps)),
                pl.BlockSpec((1, ff_tile, d), lambda i, j, e: (e[i], j, 0)),
            ],
            out_specs=pl.BlockSpec((ROW_TILE, d), lambda i, j, e: (i, 0)),
            scratch_shapes=[pltpu.VMEM((ROW_TILE, d), jnp.float32)],
        ),
        out_shape=jax.ShapeDtypeStruct((n_rows, d), jnp.float32),
        compiler_params=pltpu.CompilerParams(
            dimension_semantics=("arbitrary", "arbitrary"),
            vmem_limit_bytes=V7X_VMEM_LIMIT_BYTES,
        ),
        name="grouped_swiglu",
    )(blk_e, x_rows, w_gu, w_gu, w_dn)


def _bias_tables(rel_bias):
    n_dist = FAR_DIST + 3 * QB
    by_dist = rel_bias.astype(jnp.float32)[_rel_bucket(jnp.arange(n_dist))]
    by_dist = by_dist.T.reshape(N_KV_GROUPS, HEADS_PER_GROUP, n_dist)
    i = np.arange(QB)[:, None]
    j = np.arange(QB)[None, :]
    m = np.arange(-1, N_TOEP - 1)[:, None, None]
    d_toep = np.clip(QB * m + i - j, 0, n_dist - 1)
    toep = jnp.transpose(by_dist[:, :, d_toep], (0, 2, 1, 3, 4))
    jw = np.arange(WINDOW + QB)[None, :]
    d_win = WINDOW + i - jw
    ok_win = (d_win >= 0) & (d_win < WINDOW)
    win = jnp.where(ok_win, by_dist[:, :, np.clip(d_win, 0, n_dist - 1)], NEG_INF)
    c = np.arange(128)[None, None, :]
    par = np.arange(2)[:, None, None]
    d_band = QB * par + i[None] - D_STRIDE * (c - BAND_LEFT) - (L_CMP - 1)
    band = jnp.where(d_band >= 0, by_dist[:, :, np.clip(d_band, 0, n_dist - 1)], NEG_INF)
    band = jnp.transpose(band, (2, 0, 1, 3, 4))
    far = rel_bias.astype(jnp.float32)[N_BUCKETS - 1]
    return toep, win, band, far


def _slc_map_matrix(n_cmp_cols):
    w = np.zeros((n_cmp_cols, NBLK), np.float32)
    for jb in range(NBLK):
        for mm in range(R_SLC):
            for nn in range(R_CMP):
                k = R_SLC * jb + mm - nn
                if 0 <= k < n_cmp_cols:
                    w[k, jb] += 1.0
    return w


def _dot_nt(a, b):
    return lax.dot_general(a, b, (((1,), (1,)), ((), ())), preferred_element_type=jnp.float32)


def _cmp_select_body(far_ref, q_ref, gate_ref, kc_ref, vc_ref, band_ref, wmap_ref, oc_ref, sel_ref,
                     *, n_far):
    qi = pl.program_id(1)
    par = qi % 2
    band0 = pl.multiple_of(16 * (qi // 2 + 1), 16)
    first_band_blk = band0 - CMP_PAD

    col = lax.broadcasted_iota(jnp.int32, (QB, 128), 1)
    row = lax.broadcasted_iota(jnp.int32, (QB, 128), 0)
    neg_pad = jnp.where(first_band_blk + col >= 0, 0.0, NEG_INF)
    far_col = lax.broadcasted_iota(jnp.int32, (1, n_far), 1)
    neg_far = jnp.where(far_col < first_band_blk, 0.0, NEG_INF)

    t = qi * QB + row
    cur = t // L_SLC
    forced = (col == 0) | (col == cur) | (col == cur - 1)
    valid = col * L_SLC <= t

    q_all = q_ref[0]
    wmap_far = wmap_ref[CMP_PAD:CMP_PAD + n_far, :]
    wmap_band = wmap_ref[pl.ds(band0, 128), :]
    oc_heads = []
    for g in range(N_KV_GROUPS):
        q4 = jnp.concatenate(
            [q_all[:, (g * HEADS_PER_GROUP + h) * HEAD_DIM:(g * HEADS_PER_GROUP + h + 1) * HEAD_DIM]
             for h in range(HEADS_PER_GROUP)], axis=0)
        k_far = kc_ref[0, g, CMP_PAD:CMP_PAD + n_far, :]
        k_band = kc_ref[0, g, pl.ds(band0, 128), :]
        v_far = vc_ref[0, g, CMP_PAD:CMP_PAD + n_far, :]
        v_band = vc_ref[0, g, pl.ds(band0, 128), :]
        s_far = _dot_nt(q4, k_far).reshape(HEADS_PER_GROUP, QB, n_far)
        s_band = _dot_nt(q4, k_band).reshape(HEADS_PER_GROUP, QB, 128)
        imp_far = jnp.zeros((QB, n_far), jnp.float32)
        imp_band = jnp.zeros((QB, 128), jnp.float32)
        for h in range(HEADS_PER_GROUP):
            hh = g * HEADS_PER_GROUP + h
            sf = s_far[h] + far_ref[hh] + neg_far
            sb = s_band[h] + band_ref[par, g, h] + neg_pad
            m = jnp.maximum(jnp.max(sf, axis=-1, keepdims=True), jnp.max(sb, axis=-1, keepdims=True))
            has_key = m > 0.5 * NEG_INF
            pf = jnp.where(has_key, jnp.exp(sf - m), 0.0)
            pb = jnp.where(has_key, jnp.exp(sb - m), 0.0)
            l = jnp.sum(pf, axis=-1, keepdims=True) + jnp.sum(pb, axis=-1, keepdims=True)
            inv = jnp.where(l > 0.0, 1.0 / l, 0.0)
            pf = pf * inv
            pb = pb * inv
            imp_far = imp_far + pf
            imp_band = imp_band + pb
            o = (jnp.dot(pf.astype(jnp.bfloat16), v_far, preferred_element_type=jnp.float32)
                 + jnp.dot(pb.astype(jnp.bfloat16), v_band, preferred_element_type=jnp.float32))
            oc_heads.append(o * gate_ref[0, :, 3 * hh:3 * hh + 1])
        p_slc = (jnp.dot(imp_far, wmap_far, preferred_element_type=jnp.float32,
                         precision=lax.Precision.HIGHEST)
                 + jnp.dot(imp_band, wmap_band, preferred_element_type=jnp.float32,
                           precision=lax.Precision.HIGHEST))
        score = jnp.where(forced, SEL_FORCE, jnp.where(valid, p_slc, -SEL_FORCE))
        for _ in range(N_SEL):
            top = jnp.argmax(score, axis=-1, keepdims=True)
            score = jnp.where(col == top, -jnp.inf, score)
        sel_ref[0, g] = jnp.where(score == -jnp.inf, 0.0, NEG_INF).astype(jnp.bfloat16)
    oc_ref[0] = jnp.concatenate(oc_heads, axis=-1)


def _cmp_select(q, gates, kc_pad, vc_pad, band, far, wmap):
    B, S, _ = q.shape
    n_far = S // D_STRIDE
    n_pad = kc_pad.shape[2]
    grid = (B, S // QB)
    return pl.pallas_call(
        functools.partial(_cmp_select_body, n_far=n_far),
        grid=grid,
        in_specs=[
            pl.BlockSpec(memory_space=pltpu.SMEM),
            pl.BlockSpec((1, QB, N_HEADS * HEAD_DIM), lambda b, i: (b, i, 0)),
            pl.BlockSpec((1, QB, 3 * N_HEADS), lambda b, i: (b, i, 0)),
            pl.BlockSpec((1, N_KV_GROUPS, n_pad, HEAD_DIM), lambda b, i: (b, 0, 0, 0)),
            pl.BlockSpec((1, N_KV_GROUPS, n_pad, HEAD_DIM), lambda b, i: (b, 0, 0, 0)),
            pl.BlockSpec((2, N_KV_GROUPS, HEADS_PER_GROUP, QB, 128), lambda b, i: (0, 0, 0, 0, 0)),
            pl.BlockSpec((n_pad, NBLK), lambda b, i: (0, 0)),
        ],
        out_specs=[
            pl.BlockSpec((1, QB, N_HEADS * HEAD_DIM), lambda b, i: (b, i, 0)),
            pl.BlockSpec((1, N_KV_GROUPS, QB, NBLK), lambda b, i: (b, 0, i, 0)),
        ],
        out_shape=[
            jax.ShapeDtypeStruct((B, S, N_HEADS * HEAD_DIM), jnp.float32),
            jax.ShapeDtypeStruct((B, N_KV_GROUPS, S, NBLK), jnp.bfloat16),
        ],
        compiler_params=pltpu.CompilerParams(
            dimension_semantics=("arbitrary", "arbitrary"),
            vmem_limit_bytes=V7X_VMEM_LIMIT_BYTES,
        ),
        name="nsa_cmp_select",
    )(far, q, gates, kc_pad, vc_pad, band, wmap)


def _win_sel_body(q_ref, gate_ref, sel_ref, oc_ref, ks_ref, vs_ref, kw_ref, vw_ref, toep_ref, win_ref,
                  out_ref):
    qi = pl.program_id(2)
    q0 = pl.multiple_of(qi * QB, QB)
    qg = q_ref[0]
    q_heads = [qg[:, h * HEAD_DIM:(h + 1) * HEAD_DIM] for h in range(HEADS_PER_GROUP)]
    q4 = jnp.concatenate(q_heads, axis=0)

    kw = kw_ref[0, 0, :, pl.ds(q0, WINDOW + QB)]
    s_w = jnp.dot(q4, kw, preferred_element_type=jnp.float32)
    wcol = lax.broadcasted_iota(jnp.int32, (1, WINDOW + QB), 1)
    neg_left = jnp.where(q0 + wcol >= WINDOW, 0.0, NEG_INF)
    s_w = s_w.reshape(HEADS_PER_GROUP, QB, WINDOW + QB) + win_ref[0] + neg_left
    m_w = jnp.max(s_w, axis=-1, keepdims=True)
    p_w = jnp.exp(s_w - m_w)
    l_w = jnp.sum(p_w, axis=-1, keepdims=True)
    o_w = jnp.dot(p_w.reshape(QROWS, WINDOW + QB).astype(jnp.bfloat16),
                  vw_ref[0, 0, pl.ds(q0, WINDOW + QB), :], preferred_element_type=jnp.float32)
    o_w = o_w.reshape(HEADS_PER_GROUP, QB, HEAD_DIM) / l_w

    sel = sel_ref[0, 0]
    qa = jnp.concatenate([jnp.concatenate([sel] * HEADS_PER_GROUP, axis=0), q4], axis=-1)
    par = qi % 2
    c_diag = qi // 2

    def scores(c):
        col = pl.multiple_of(c * KEY_TILE, KEY_TILE)
        s = jnp.dot(qa, ks_ref[0, 0, :, pl.ds(col, KEY_TILE)], preferred_element_type=jnp.float32)
        mm = qi - 2 * c
        bias = jnp.concatenate([toep_ref[0, jnp.minimum(mm, N_TOEP - 2) + 1],
                                toep_ref[0, jnp.minimum(mm - 1, N_TOEP - 2) + 1]], axis=-1)
        return s.reshape(HEADS_PER_GROUP, QB, KEY_TILE) + bias, col

    s_d, col_d = scores(c_diag)
    kcol = lax.broadcasted_iota(jnp.int32, (QB, KEY_TILE), 1)
    krow = lax.broadcasted_iota(jnp.int32, (QB, KEY_TILE), 0)
    causal = kcol <= krow + par * QB
    s_d = jnp.where(causal, s_d, NEG_INF)
    m0 = jnp.max(s_d, axis=-1, keepdims=True)
    p_d = jnp.exp(s_d - m0)
    l0 = jnp.sum(p_d, axis=-1, keepdims=True)
    acc0 = jnp.dot(p_d.reshape(QROWS, KEY_TILE).astype(jnp.bfloat16),
                   vs_ref[0, 0, pl.ds(col_d, KEY_TILE), :],
                   preferred_element_type=jnp.float32).reshape(HEADS_PER_GROUP, QB, HEAD_DIM)

    def step(c, carry):
        m, l, acc = carry
        s, col = scores(c)
        m_new = jnp.maximum(m, jnp.max(s, axis=-1, keepdims=True))
        alpha = jnp.exp(m - m_new)
        p = jnp.exp(s - m_new)
        l = alpha * l + jnp.sum(p, axis=-1, keepdims=True)
        pv = jnp.dot(p.reshape(QROWS, KEY_TILE).astype(jnp.bfloat16),
                     vs_ref[0, 0, pl.ds(col, KEY_TILE), :], preferred_element_type=jnp.float32)
        acc = alpha * acc + pv.reshape(HEADS_PER_GROUP, QB, HEAD_DIM)
        return m_new, l, acc

    _, l_s, acc_s = lax.fori_loop(0, c_diag, step, (m0, l0, acc0))
    o_s = acc_s / l_s

    outs = []
    for h in range(HEADS_PER_GROUP):
        g_s = gate_ref[0, 0, :, 3 * h + 1:3 * h + 2]
        g_w = gate_ref[0, 0, :, 3 * h + 2:3 * h + 3]
        outs.append(g_s * o_s[h] + g_w * o_w[h])
    out_ref[0] = oc_ref[0] + jnp.concatenate(outs, axis=-1)


def _win_sel(q, gates, selneg, oc, ks_aug, vs, kw_t, vw_pad, toep, win):
    B, S, _ = q.shape
    gw = HEADS_PER_GROUP * HEAD_DIM
    grid = (B, N_KV_GROUPS, S // QB)
    return pl.pallas_call(
        _win_sel_body,
        grid=grid,
        in_specs=[
            pl.BlockSpec((1, QB, gw), lambda b, g, i: (b, i, g)),
            pl.BlockSpec((1, 1, QB, 3 * HEADS_PER_GROUP), lambda b, g, i: (b, g, i, 0)),
            pl.BlockSpec((1, 1, QB, NBLK), lambda b, g, i: (b, g, i, 0)),
            pl.BlockSpec((1, QB, gw), lambda b, g, i: (b, i, g)),
            pl.BlockSpec((1, 1, NBLK + HEAD_DIM, S), lambda b, g, i: (b, g, 0, 0)),
            pl.BlockSpec((1, 1, S, HEAD_DIM), lambda b, g, i: (b, g, 0, 0)),
            pl.BlockSpec((1, 1, HEAD_DIM, S + WINDOW), lambda b, g, i: (b, g, 0, 0)),
            pl.BlockSpec((1, 1, S + WINDOW, HEAD_DIM), lambda b, g, i: (b, g, 0, 0)),
            pl.BlockSpec((1, N_TOEP, HEADS_PER_GROUP, QB, QB), lambda b, g, i: (g, 0, 0, 0, 0)),
            pl.BlockSpec((1, HEADS_PER_GROUP, QB, WINDOW + QB), lambda b, g, i: (g, 0, 0, 0)),
        ],
        out_specs=pl.BlockSpec((1, QB, gw), lambda b, g, i: (b, i, g)),
        out_shape=jax.ShapeDtypeStruct((B, S, N_HEADS * HEAD_DIM), jnp.float32),
        compiler_params=pltpu.CompilerParams(
            dimension_semantics=("arbitrary", "arbitrary", "arbitrary"),
            vmem_limit_bytes=V7X_VMEM_LIMIT_BYTES,
        ),
        name="nsa_win_sel",
    )(q, gates, selneg, oc, ks_aug, vs, kw_t, vw_pad, toep, win)


def _nsa_attention(q, gates, kv, rel_bias):
    kc, vc, ks, vs, kw, vw = kv
    B, S, _ = q.shape
    assert S % KEY_TILE == 0 and S // L_SLC <= NBLK
    bf = jnp.bfloat16
    n_far = S // D_STRIDE
    toep, win, band, far = _bias_tables(rel_bias)
    cpad = ((0, 0), (0, 0), (CMP_PAD, n_far - kc.shape[2] + CMP_PAD), (0, 0))
    kc_pad = jnp.pad(kc, cpad).astype(bf)
    vc_pad = jnp.pad(vc, cpad).astype(bf)
    wmap = jnp.asarray(np.pad(_slc_map_matrix(n_far), ((CMP_PAD, CMP_PAD), (0, 0))))
    qb = q.astype(bf)
    oc, selneg = _cmp_select(qb, gates, kc_pad, vc_pad, band, far, wmap)
    blk_onehot = jnp.asarray((np.arange(S)[None, :] // L_SLC == np.arange(NBLK)[:, None]), bf)
    ks_aug = jnp.concatenate(
        [jnp.broadcast_to(blk_onehot, (B, N_KV_GROUPS, NBLK, S)), jnp.swapaxes(ks, 2, 3).astype(bf)], axis=2)
    kw_t = jnp.pad(jnp.swapaxes(kw, 2, 3), ((0, 0), (0, 0), (0, 0), (WINDOW, 0))).astype(bf)
    vw_pad = jnp.pad(vw, ((0, 0), (0, 0), (WINDOW, 0), (0, 0))).astype(bf)
    gates_g = jnp.swapaxes(gates.reshape(B, S, N_KV_GROUPS, 3 * HEADS_PER_GROUP), 1, 2)
    return _win_sel(qb, gates_g, selneg, oc, ks_aug, vs.astype(bf), kw_t, vw_pad, toep, win)


def _pool_mixer(h, w_grp, scale):
    B, S, _ = h.shape
    hg = h.astype(jnp.float32).reshape(B, S, N_POOL_GROUPS, POOL_GROUP_DIM)
    cs = jnp.pad(jnp.cumsum(hg, axis=1), ((0, 0), (1, 0), (0, 0), (0, 0)))
    t = jnp.arange(S)
    pooled = []
    for gi, w in enumerate(POOL_WINDOWS):
        c_g = cs[:, :, gi]
        lag = jnp.pad(c_g, ((0, 0), (w, 0), (0, 0)))[:, :S + 1]
        cnt = jnp.minimum(t + 1, w).astype(jnp.float32)
        pooled.append((c_g - lag)[:, 1:] / cnt[None, :, None])
    mix = (jnp.stack(pooled, axis=2) - hg).astype(h.dtype)
    y = jnp.einsum('bsgc,gcd->bsgd', mix, w_grp).reshape(B, S, D_MODEL)
    return y * scale


def _shared_kv(x, c, kv_ada_w, kv_ada_b, kv_norm_g, kv_w, cmp_pe_k, cmp_pe_v,
               cmp_k_w1, cmp_k_w2, cmp_v_w1, cmp_v_w2, k_gain):
    B, S, _ = x.shape
    shift, scale = jnp.split(jax.nn.silu(c) @ kv_ada_w + kv_ada_b, 2, axis=-1)
    h = _modulate(_rms_norm(x, kv_norm_g), shift, scale)
    kv = (h @ kv_w).reshape(B, S, 6, N_KV_GROUPS, HEAD_DIM).transpose(2, 0, 3, 1, 4)
    k_c, v_c, k_s, v_s, k_w, v_w = kv[0], kv[1], kv[2], kv[3], kv[4], kv[5]

    def compress(u, pe, w1, w2):
        r = u.reshape(B, N_KV_GROUPS, S // D_STRIDE, D_STRIDE, HEAD_DIM)
        nc = S // D_STRIDE - R_CMP + 1
        blocks = jnp.concatenate([r[:, :, i:i + nc] for i in range(R_CMP)], axis=3) + pe
        flat = blocks.reshape(B, N_KV_GROUPS, nc, L_CMP * HEAD_DIM)
        return jax.nn.gelu(flat @ w1) @ w2

    kc = _rms_norm(compress(k_c, cmp_pe_k, cmp_k_w1, cmp_k_w2), k_gain[0])
    vc = compress(v_c, cmp_pe_v, cmp_v_w1, cmp_v_w2)
    ks = _rms_norm(k_s, k_gain[1])
    kw = _rms_norm(k_w, k_gain[2])
    return kc, vc, ks, v_s, kw, v_w


def _nsa_mixer(h, kv, w_qg, q_gain, w_o, rel_bias):
    B, S, _ = h.shape
    proj = h @ w_qg
    qd = N_HEADS * HEAD_DIM
    q = _rms_norm(proj[..., :qd].reshape(B, S, N_HEADS, HEAD_DIM), q_gain) * HEAD_DIM ** -0.5
    gates = jax.nn.sigmoid(proj[..., qd:])
    out = _nsa_attention(q.reshape(B, S, qd), gates, kv, rel_bias)
    return out @ w_o


def _moe_ffn(h, w_router, b_router, w_gu, w_dn):
    B, S, D = h.shape
    T = B * S
    xf = h.reshape(T, D)
    logits = jnp.dot(xf, w_router, precision=lax.Precision.HIGHEST).astype(jnp.float32) + b_router
    top_logit, top_e = lax.top_k(logits, TOP_K)
    top_w = jax.nn.softmax(top_logit, axis=-1)
    A = T * TOP_K
    flat_e = top_e.reshape(A)
    flat_tok = jnp.repeat(jnp.arange(T), TOP_K)
    flat_w = top_w.reshape(A)
    order = jnp.argsort(flat_e)
    se = flat_e[order]
    counts = jnp.zeros((N_EXPERTS,), jnp.int32).at[flat_e].add(1)
    starts = jnp.cumsum(counts) - counts
    pcounts = (counts + ROW_TILE - 1) // ROW_TILE * ROW_TILE
    pends = jnp.cumsum(pcounts)
    pstarts = pends - pcounts
    dest = pstarts[se] + jnp.arange(A) - starts[se]
    n_blocks = -(-(A + N_EXPERTS * (ROW_TILE - 1)) // ROW_TILE)
    n_rows = n_blocks * ROW_TILE
    row_tok = jnp.zeros((n_rows,), jnp.int32).at[dest].set(flat_tok[order])
    row_w = jnp.zeros((n_rows,), jnp.float32).at[dest].set(flat_w[order])
    blk_e = jnp.minimum(jnp.searchsorted(pends, jnp.arange(n_blocks) * ROW_TILE, side='right'),
                        N_EXPERTS - 1).astype(jnp.int32)
    x_rows = xf.astype(jnp.bfloat16)[row_tok]
    y = _grouped_swiglu(x_rows, blk_e, w_gu.astype(jnp.bfloat16), w_dn.astype(jnp.bfloat16),
                        ff_tile=FF_TILE_EXPERT)
    out = jnp.zeros((T, D), h.dtype).at[row_tok].add(y * row_w[:, None])
    return out.reshape(B, S, D)


def _dense_ffn(h, w_gu, w_dn):
    B, S, D = h.shape
    T = B * S
    blk_e = jnp.zeros((T // ROW_TILE,), jnp.int32)
    y = _grouped_swiglu(h.reshape(T, D).astype(jnp.bfloat16), blk_e,
                        w_gu.astype(jnp.bfloat16)[None], w_dn.astype(jnp.bfloat16)[None],
                        ff_tile=FF_TILE_DENSE)
    return y.reshape(B, S, D)


def kernel(x, c, ada_w, ada_b, norm_g, pool_w, pool_scale, q_w, q_gain, o_w, kv_ada_w, kv_ada_b, kv_norm_g, kv_w, cmp_pe_k, cmp_pe_v, cmp_k_w1, cmp_k_w2, cmp_v_w1, cmp_v_w2, k_gain, rel_bias, ffn_gu, ffn_dn, router_w, router_b, exp_gu, exp_dn):
    kv = None
    for l in range(DEPTH):
        sh1, sc1, g1, sh2, sc2, g2 = jnp.split(jax.nn.silu(c) @ ada_w[l] + ada_b[l], 6, axis=-1)
        if l >= N_A_LAYERS and kv is None:
            kv = _shared_kv(x, c, kv_ada_w, kv_ada_b, kv_norm_g, kv_w, cmp_pe_k, cmp_pe_v,
                            cmp_k_w1, cmp_k_w2, cmp_v_w1, cmp_v_w2, k_gain)
        h = _modulate(_rms_norm(x, norm_g[l, 0]), sh1, sc1)
        if l < N_A_LAYERS:
            mix = _pool_mixer(h, pool_w[l], pool_scale[l])
        else:
            j = l - N_A_LAYERS
            mix = _nsa_mixer(h, kv, q_w[j], q_gain[j], o_w[j], rel_bias)
        x = x + g1[:, None, :] * mix
        h = _modulate(_rms_norm(x, norm_g[l, 1]), sh2, sc2)
        if l % 2 == 0:
            f = _dense_ffn(h, ffn_gu[l // 2], ffn_dn[l // 2])
        else:
            f = _moe_ffn(h, router_w[l // 2], router_b[l // 2], exp_gu[l // 2], exp_dn[l // 2])
        x = x + g2[:, None, :] * f
    return x
```

```python
import functools
import math

import jax
import jax.numpy as jnp
import numpy as np
from jax import lax
from jax.experimental import pallas as pl
from jax.experimental.pallas import tpu as pltpu

D_MODEL = 1024
DEPTH = 2
N_A_LAYERS = DEPTH // 2
POOL_WINDOWS = (2, 4, 8, 16)
N_POOL_GROUPS = len(POOL_WINDOWS)
POOL_GROUP_DIM = D_MODEL // N_POOL_GROUPS
HEAD_DIM = 64
N_HEADS = D_MODEL // HEAD_DIM
N_KV_GROUPS = 4
HEADS_PER_GROUP = N_HEADS // N_KV_GROUPS
L_CMP = 32
D_STRIDE = 16
L_SLC = 64
N_SEL = 16
WINDOW = 512
R_CMP = L_CMP // D_STRIDE
R_SLC = L_SLC // D_STRIDE
N_BUCKETS = 32
REL_EXACT = N_BUCKETS // 2
MAX_DISTANCE = 1024
N_EXPERTS = 8
TOP_K = 2
EPS = 1e-6
NEG_INF = -1e30
SEL_FORCE = 1e6

V7X_VMEM_LIMIT_BYTES = 48 * 1024 * 1024
LANES = 128
ROW_TILE = 512
FF_TILE_DENSE = 1408
FF_TILE_EXPERT = 512

QB = 128
KEY_TILE = 2 * QB
QROWS = HEADS_PER_GROUP * QB
NBLK = 128
CMP_PAD = 128
FAR_DIST = MAX_DISTANCE
N_TOEP = FAR_DIST // QB + 3
BAND_LEFT = CMP_PAD - 16
assert BAND_LEFT * D_STRIDE + (L_CMP - 1) - 2 * QB >= FAR_DIST


def _rms_norm(x, g):
    xf = x.astype(jnp.float32)
    y = xf * lax.rsqrt(jnp.mean(xf * xf, axis=-1, keepdims=True) + EPS)
    return (y * g.astype(jnp.float32)).astype(x.dtype)


def _modulate(h, shift, scale):
    return h * (1 + scale[:, None, :]) + shift[:, None, :]


def _rel_bucket(dist):
    d = jnp.maximum(dist, 0)
    ratio = jnp.maximum(d, REL_EXACT).astype(jnp.float32) / REL_EXACT
    large = REL_EXACT + (jnp.log(ratio) / math.log(MAX_DISTANCE / REL_EXACT)
                         * (N_BUCKETS - REL_EXACT)).astype(jnp.int32)
    return jnp.where(d < REL_EXACT, d, jnp.minimum(large, N_BUCKETS - 1))


def _swiglu_body(blk_e_ref, x_ref, wg_ref, wu_ref, wd_ref, o_ref, acc_ref, *, n_ff_steps):
    del blk_e_ref
    j = pl.program_id(1)

    @pl.when(j == 0)
    def _():
        acc_ref[...] = jnp.zeros_like(acc_ref)

    x = x_ref[...]
    gate = jnp.dot(x, wg_ref[0], preferred_element_type=jnp.float32)
    up = jnp.dot(x, wu_ref[0], preferred_element_type=jnp.float32)
    act = (gate * jax.nn.sigmoid(gate) * up).astype(jnp.bfloat16)
    acc_ref[...] += jnp.dot(act, wd_ref[0], preferred_element_type=jnp.float32)

    @pl.when(j == n_ff_steps - 1)
    def _():
        o_ref[...] = acc_ref[...]


def _grouped_swiglu(x_rows, blk_e, w_gu, w_dn, *, ff_tile):
    n_rows, d = x_rows.shape
    n_exp, d_ff, _ = w_dn.shape
    assert n_rows % ROW_TILE == 0 and d_ff % ff_tile == 0
    n_ff_steps = d_ff // ff_tile
    grid = (n_rows // ROW_TILE, n_ff_steps)
    return pl.pallas_call(
        functools.partial(_swiglu_body, n_ff_steps=n_ff_steps),
        grid_spec=pltpu.PrefetchScalarGridSpec(
            num_scalar_prefetch=1,
            grid=grid,
            in_specs=[
                pl.BlockSpec((ROW_TILE, d), lambda i, j, e: (i, 0)),
                pl.BlockSpec((1, d, ff_tile), lambda i, j, e: (e[i], 0, j)),
                pl.BlockSpec((1, d, ff_tile), lambda i, j, e: (e[i], 0, j + n_ff_steps)),
                pl.BlockSpec((1, ff_tile, d), lambda i, j, e: (e[i], j, 0)),
            ],
            out_specs=pl.BlockSpec((ROW_TILE, d), lambda i, j, e: (i, 0)),
            scratch_shapes=[pltpu.VMEM((ROW_TILE, d), jnp.float32)],
        ),
        out_shape=jax.ShapeDtypeStruct((n_rows, d), jnp.float32),
        compiler_params=pltpu.CompilerParams(
            dimension_semantics=("arbitrary", "arbitrary"),
            vmem_limit_bytes=V7X_VMEM_LIMIT_BYTES,
        ),
        name="grouped_swiglu",
    )(blk_e, x_rows, w_gu, w_gu, w_dn)


def _bias_tables(rel_bias):
    n_dist = FAR_DIST + 3 * QB
    by_dist = rel_bias.astype(jnp.float32)[_rel_bucket(jnp.arange(n_dist))]
    by_dist = by_dist.T.reshape(N_KV_GROUPS, HEADS_PER_GROUP, n_dist)
    i = np.arange(QB)[:, None]
    j = np.arange(QB)[None, :]
    m = np.arange(-1, N_TOEP - 1)[:, None, None]
    d_toep = np.clip(QB * m + i - j, 0, n_dist - 1)
    toep = jnp.transpose(by_dist[:, :, d_toep], (0, 2, 1, 3, 4))
    jw = np.arange(WINDOW + QB)[None, :]
    d_win = WINDOW + i - jw
    ok_win = (d_win >= 0) & (d_win < WINDOW)
    win = jnp.where(ok_win, by_dist[:, :, np.clip(d_win, 0, n_dist - 1)], NEG_INF)
    c = np.arange(128)[None, None, :]
    par = np.arange(2)[:, None, None]
    d_band = QB * par + i[None] - D_STRIDE * (c - BAND_LEFT) - (L_CMP - 1)
    band = jnp.where(d_band >= 0, by_dist[:, :, np.clip(d_band, 0, n_dist - 1)], NEG_INF)
    band = jnp.transpose(band, (2, 0, 1, 3, 4))
    far = rel_bias.astype(jnp.float32)[N_BUCKETS - 1]
    return toep, win, band, far


def _slc_map_matrix(n_cmp_cols):
    w = np.zeros((n_cmp_cols, NBLK), np.float32)
    for jb in range(NBLK):
        for mm in range(R_SLC):
            for nn in range(R_CMP):
                k = R_SLC * jb + mm - nn
                if 0 <= k < n_cmp_cols:
                    w[k, jb] += 1.0
    return w


def _dot_nt(a, b):
    return lax.dot_general(a, b, (((1,), (1,)), ((), ())), preferred_element_type=jnp.float32)


def _cmp_select_body(far_ref, q_ref, gate_ref, kc_ref, vc_ref, band_ref, wmap_ref, oc_ref, sel_ref,
                     *, n_far):
    qi = pl.program_id(1)
    par = qi % 2
    band0 = pl.multiple_of(16 * (qi // 2 + 1), 16)
    first_band_blk = band0 - CMP_PAD

    col = lax.broadcasted_iota(jnp.int32, (QB, 128), 1)
    row = lax.broadcasted_iota(jnp.int32, (QB, 128), 0)
    neg_pad = jnp.where(first_band_blk + col >= 0, 0.0, NEG_INF)
    far_col = lax.broadcasted_iota(jnp.int32, (1, n_far), 1)
    neg_far = jnp.where(far_col < first_band_blk, 0.0, NEG_INF)

    t = qi * QB + row
    cur = t // L_SLC
    forced = (col == 0) | (col == cur) | (col == cur - 1)
    valid = col * L_SLC <= t

    q_all = q_ref[0]
    wmap_far = wmap_ref[CMP_PAD:CMP_PAD + n_far, :]
    wmap_band = wmap_ref[pl.ds(band0, 128), :]
    oc_heads = []
    for g in range(N_KV_GROUPS):
        q4 = jnp.concatenate(
            [q_all[:, (g * HEADS_PER_GROUP + h) * HEAD_DIM:(g * HEADS_PER_GROUP + h + 1) * HEAD_DIM]
             for h in range(HEADS_PER_GROUP)], axis=0)
        k_far = kc_ref[0, g, CMP_PAD:CMP_PAD + n_far, :]
        k_band = kc_ref[0, g, pl.ds(band0, 128), :]
        v_far = vc_ref[0, g, CMP_PAD:CMP_PAD + n_far, :]
        v_band = vc_ref[0, g, pl.ds(band0, 128), :]
        s_far = _dot_nt(q4, k_far).reshape(HEADS_PER_GROUP, QB, n_far)
        s_band = _dot_nt(q4, k_band).reshape(HEADS_PER_GROUP, QB, 128)
        imp_far = jnp.zeros((QB, n_far), jnp.float32)
        imp_band = jnp.zeros((QB, 128), jnp.float32)
        for h in range(HEADS_PER_GROUP):
            hh = g * HEADS_PER_GROUP + h
            sf = s_far[h] + far_ref[hh] + neg_far
            sb = s_band[h] + band_ref[par, g, h] + neg_pad
            m = jnp.maximum(jnp.max(sf, axis=-1, keepdims=True), jnp.max(sb, axis=-1, keepdims=True))
            has_key = m > 0.5 * NEG_INF
            pf = jnp.where(has_key, jnp.exp(sf - m), 0.0)
            pb = jnp.where(has_key, jnp.exp(sb - m), 0.0)
            l = jnp.sum(pf, axis=-1, keepdims=True) + jnp.sum(pb, axis=-1, keepdims=True)
            inv = jnp.where(l > 0.0, 1.0 / l, 0.0)
            pf = pf * inv
            pb = pb * inv
            imp_far = imp_far + pf
            imp_band = imp_band + pb
            o = (jnp.dot(pf.astype(jnp.bfloat16), v_far, preferred_element_type=jnp.float32)
                 + jnp.dot(pb.astype(jnp.bfloat16), v_band, preferred_element_type=jnp.float32))
            oc_heads.append(o * gate_ref[0, :, 3 * hh:3 * hh + 1])
        p_slc = (jnp.dot(imp_far, wmap_far, preferred_element_type=jnp.float32,
                         precision=lax.Precision.HIGHEST)
                 + jnp.dot(imp_band, wmap_band, preferred_element_type=jnp.float32,
                           precision=lax.Precision.HIGHEST))
        score = jnp.where(forced, SEL_FORCE, jnp.where(valid, p_slc, -SEL_FORCE))
        for _ in range(N_SEL):
            top = jnp.argmax(score, axis=-1, keepdims=True)
            score = jnp.where(col == top, -jnp.inf, score)
        sel_ref[0, g] = jnp.where(score == -jnp.inf, 0.0, NEG_INF).astype(jnp.bfloat16)
    oc_ref[0] = jnp.concatenate(oc_heads, axis=-1)


def _cmp_select(q, gates, kc_pad, vc_pad, band, far, wmap):
    B, S, _ = q.shape
    n_far = S // D_STRIDE
    n_pad = kc_pad.shape[2]
    grid = (B, S // QB)
    return pl.pallas_call(
        functools.partial(_cmp_select_body, n_far=n_far),
        grid=grid,
        in_specs=[
            pl.BlockSpec(memory_space=pltpu.SMEM),
            pl.BlockSpec((1, QB, N_HEADS * HEAD_DIM), lambda b, i: (b, i, 0)),
            pl.BlockSpec((1, QB, 3 * N_HEADS), lambda b, i: (b, i, 0)),
            pl.BlockSpec((1, N_KV_GROUPS, n_pad, HEAD_DIM), lambda b, i: (b, 0, 0, 0)),
            pl.BlockSpec((1, N_KV_GROUPS, n_pad, HEAD_DIM), lambda b, i: (b, 0, 0, 0)),
            pl.BlockSpec((2, N_KV_GROUPS, HEADS_PER_GROUP, QB, 128), lambda b, i: (0, 0, 0, 0, 0)),
            pl.BlockSpec((n_pad, NBLK), lambda b, i: (0, 0)),
        ],
        out_specs=[
            pl.BlockSpec((1, QB, N_HEADS * HEAD_DIM), lambda b, i: (b, i, 0)),
            pl.BlockSpec((1, N_KV_GROUPS, QB, NBLK), lambda b, i: (b, 0, i, 0)),
        ],
        out_shape=[
            jax.ShapeDtypeStruct((B, S, N_HEADS * HEAD_DIM), jnp.float32),
            jax.ShapeDtypeStruct((B, N_KV_GROUPS, S, NBLK), jnp.bfloat16),
        ],
        compiler_params=pltpu.CompilerParams(
            dimension_semantics=("arbitrary", "arbitrary"),
            vmem_limit_bytes=V7X_VMEM_LIMIT_BYTES,
        ),
        name="nsa_cmp_select",
    )(far, q, gates, kc_pad, vc_pad, band, wmap)


def _win_sel_body(q_ref, gate_ref, sel_ref, oc_ref, ks_ref, vs_ref, kw_ref, vw_ref, toep_ref, win_ref,
                  out_ref, s0_scr, s1_scr, p0_scr, p1_scr, m_scr, acc_scr):
    qi = pl.program_id(2)
    q0 = pl.multiple_of(qi * QB, QB)
    qg = q_ref[0]
    q_heads = [qg[:, h * HEAD_DIM:(h + 1) * HEAD_DIM] for h in range(HEADS_PER_GROUP)]
    q4 = jnp.concatenate(q_heads, axis=0)

    kw = kw_ref[0, 0, :, pl.ds(q0, WINDOW + QB)]
    s_w = jnp.dot(q4, kw, preferred_element_type=jnp.float32)
    wcol = lax.broadcasted_iota(jnp.int32, (1, WINDOW + QB), 1)
    neg_left = jnp.where(q0 + wcol >= WINDOW, 0.0, NEG_INF)
    s_w = s_w.reshape(HEADS_PER_GROUP, QB, WINDOW + QB) + win_ref[0] + neg_left
    m_w = jnp.max(s_w, axis=-1, keepdims=True)
    p_w = jnp.exp(s_w - m_w)
    o_w = jnp.dot(p_w.reshape(QROWS, WINDOW + QB).astype(jnp.bfloat16),
                  vw_ref[0, 0, pl.ds(q0, WINDOW + QB), :], preferred_element_type=jnp.float32)
    o_w = o_w.reshape(HEADS_PER_GROUP, QB, LANES)
    o_w = o_w[:, :, :HEAD_DIM] / o_w[:, :, HEAD_DIM:HEAD_DIM + 1]

    sel = sel_ref[0, 0]
    qa = jnp.concatenate([jnp.concatenate([sel] * HEADS_PER_GROUP, axis=0), q4], axis=-1)
    par = qi % 2
    c_diag = qi // 2

    def scores(c):
        col = pl.multiple_of(c * KEY_TILE, KEY_TILE)
        s = jnp.dot(qa, ks_ref[0, 0, :, pl.ds(col, KEY_TILE)], preferred_element_type=jnp.float32)
        mm = qi - 2 * c
        bias = jnp.concatenate([toep_ref[0, jnp.minimum(mm, N_TOEP - 2) + 1],
                                toep_ref[0, jnp.minimum(mm - 1, N_TOEP - 2) + 1]], axis=-1)
        return s.reshape(HEADS_PER_GROUP, QB, KEY_TILE) + bias, col

    s_d, col_d = scores(c_diag)
    kcol = lax.broadcasted_iota(jnp.int32, (QB, KEY_TILE), 1)
    krow = lax.broadcasted_iota(jnp.int32, (QB, KEY_TILE), 0)
    causal = kcol <= krow + par * QB
    s_d = jnp.where(causal, s_d, NEG_INF)
    m0 = jnp.max(s_d, axis=-1, keepdims=True)
    s_slots = (s0_scr, s1_scr)
    p_slots = (p0_scr, p1_scr)
    p_slots[0][...] = jnp.exp(s_d - m0).reshape(QROWS, KEY_TILE).astype(jnp.bfloat16)
    m_scr[...] = jnp.broadcast_to(m0, (HEADS_PER_GROUP, QB, LANES))
    acc_scr[...] = jnp.zeros_like(acc_scr)
    s_slots[0][...] = scores(0)[0]

    def weighted_values(p, col):
        return jnp.dot(p, vs_ref[0, 0, pl.ds(pl.multiple_of(col, KEY_TILE), KEY_TILE), :],
                       preferred_element_type=jnp.float32)

    def half_step(c, col_prev, cur):
        pv = weighted_values(p_slots[cur][...], col_prev)
        s_slots[1 - cur][...] = scores(jnp.minimum(c + 1, c_diag - 1))[0]
        for h in range(HEADS_PER_GROUP):
            s = s_slots[cur][h]
            m_old = m_scr[h]
            m_new = jnp.maximum(m_old, jnp.max(s, axis=-1, keepdims=True))
            alpha = jnp.exp(m_old - m_new)
            p = jnp.exp(s - jnp.concatenate([m_new] * (KEY_TILE // LANES), axis=-1))
            p_slots[1 - cur][h * QB:(h + 1) * QB, :] = p.astype(jnp.bfloat16)
            m_scr[h] = m_new
            acc_scr[h * QB:(h + 1) * QB, :] = alpha * (acc_scr[h * QB:(h + 1) * QB, :]
                                                      + pv[h * QB:(h + 1) * QB, :])
        return c * KEY_TILE

    def step_pair(j, col_prev):
        return half_step(2 * j + 1, half_step(2 * j, col_prev, 0), 1)

    col_last = lax.fori_loop(0, c_diag // 2, step_pair, col_d)
    odd = c_diag % 2 == 1
    col_last = lax.cond(odd, lambda col: half_step(c_diag - 1, col, 0), lambda col: col, col_last)
    p_last = jnp.where(odd, p_slots[1][...], p_slots[0][...])
    acc_s = (acc_scr[...] + weighted_values(p_last, col_last)).reshape(HEADS_PER_GROUP, QB, LANES)
    o_s = acc_s[:, :, :HEAD_DIM] / acc_s[:, :, HEAD_DIM:HEAD_DIM + 1]

    outs = []
    for h in range(HEADS_PER_GROUP):
        g_s = gate_ref[0, 0, :, 3 * h + 1:3 * h + 2]
        g_w = gate_ref[0, 0, :, 3 * h + 2:3 * h + 3]
        outs.append(g_s * o_s[h] + g_w * o_w[h])
    out_ref[0] = oc_ref[0] + jnp.concatenate(outs, axis=-1)


def _win_sel(q, gates, selneg, oc, ks_aug, vs, kw_t, vw_pad, toep, win):
    B, S, _ = q.shape
    gw = HEADS_PER_GROUP * HEAD_DIM
    grid = (B, N_KV_GROUPS, S // QB)
    return pl.pallas_call(
        _win_sel_body,
        grid=grid,
        in_specs=[
            pl.BlockSpec((1, QB, gw), lambda b, g, i: (b, i, g)),
            pl.BlockSpec((1, 1, QB, 3 * HEADS_PER_GROUP), lambda b, g, i: (b, g, i, 0)),
            pl.BlockSpec((1, 1, QB, NBLK), lambda b, g, i: (b, g, i, 0)),
            pl.BlockSpec((1, QB, gw), lambda b, g, i: (b, i, g)),
            pl.BlockSpec((1, 1, NBLK + HEAD_DIM, S), lambda b, g, i: (b, g, 0, 0)),
            pl.BlockSpec((1, 1, S, LANES), lambda b, g, i: (b, g, 0, 0)),
            pl.BlockSpec((1, 1, HEAD_DIM, S + WINDOW), lambda b, g, i: (b, g, 0, 0)),
            pl.BlockSpec((1, 1, S + WINDOW, LANES), lambda b, g, i: (b, g, 0, 0)),
            pl.BlockSpec((1, N_TOEP, HEADS_PER_GROUP, QB, QB), lambda b, g, i: (g, 0, 0, 0, 0)),
            pl.BlockSpec((1, HEADS_PER_GROUP, QB, WINDOW + QB), lambda b, g, i: (g, 0, 0, 0)),
        ],
        out_specs=pl.BlockSpec((1, QB, gw), lambda b, g, i: (b, i, g)),
        out_shape=jax.ShapeDtypeStruct((B, S, N_HEADS * HEAD_DIM), jnp.float32),
        scratch_shapes=[
            pltpu.VMEM((HEADS_PER_GROUP, QB, KEY_TILE), jnp.float32),
            pltpu.VMEM((HEADS_PER_GROUP, QB, KEY_TILE), jnp.float32),
            pltpu.VMEM((QROWS, KEY_TILE), jnp.bfloat16),
            pltpu.VMEM((QROWS, KEY_TILE), jnp.bfloat16),
            pltpu.VMEM((HEADS_PER_GROUP, QB, LANES), jnp.float32),
            pltpu.VMEM((QROWS, LANES), jnp.float32),
        ],
        compiler_params=pltpu.CompilerParams(
            dimension_semantics=("arbitrary", "arbitrary", "arbitrary"),
            vmem_limit_bytes=V7X_VMEM_LIMIT_BYTES,
        ),
        name="nsa_win_sel",
    )(q, gates, selneg, oc, ks_aug, vs, kw_t, vw_pad, toep, win)


def _nsa_attention(q, gates, kv, rel_bias):
    kc, vc, ks, vs, kw, vw = kv
    B, S, _ = q.shape
    assert S % KEY_TILE == 0 and S // L_SLC <= NBLK
    bf = jnp.bfloat16
    n_far = S // D_STRIDE
    toep, win, band, far = _bias_tables(rel_bias)
    cpad = ((0, 0), (0, 0), (CMP_PAD, n_far - kc.shape[2] + CMP_PAD), (0, 0))
    kc_pad = jnp.pad(kc, cpad).astype(bf)
    vc_pad = jnp.pad(vc, cpad).astype(bf)
    wmap = jnp.asarray(np.pad(_slc_map_matrix(n_far), ((CMP_PAD, CMP_PAD), (0, 0))))
    qb = q.astype(bf)
    oc, selneg = _cmp_select(qb, gates, kc_pad, vc_pad, band, far, wmap)
    blk_onehot = jnp.asarray((np.arange(S)[None, :] // L_SLC == np.arange(NBLK)[:, None]), bf)
    ks_aug = jnp.concatenate(
        [jnp.broadcast_to(blk_onehot, (B, N_KV_GROUPS, NBLK, S)), jnp.swapaxes(ks, 2, 3).astype(bf)], axis=2)
    kw_t = jnp.pad(jnp.swapaxes(kw, 2, 3), ((0, 0), (0, 0), (0, 0), (WINDOW, 0))).astype(bf)

    def with_ones(v):
        ones = jnp.ones(v.shape[:-1] + (1,), v.dtype)
        return jnp.pad(jnp.concatenate([v, ones], axis=-1), ((0, 0),) * 3 + ((0, LANES - HEAD_DIM - 1),)).astype(bf)

    vw_aug = jnp.pad(with_ones(vw), ((0, 0), (0, 0), (WINDOW, 0), (0, 0)))
    gates_g = jnp.swapaxes(gates.reshape(B, S, N_KV_GROUPS, 3 * HEADS_PER_GROUP), 1, 2)
    return _win_sel(qb, gates_g, selneg, oc, ks_aug, with_ones(vs), kw_t, vw_aug, toep, win)


def _pool_mixer(h, w_grp, scale):
    B, S, _ = h.shape
    hg = h.astype(jnp.float32).reshape(B, S, N_POOL_GROUPS, POOL_GROUP_DIM)
    cs = jnp.pad(jnp.cumsum(hg, axis=1), ((0, 0), (1, 0), (0, 0), (0, 0)))
    t = jnp.arange(S)
    pooled = []
    for gi, w in enumerate(POOL_WINDOWS):
        c_g = cs[:, :, gi]
        lag = jnp.pad(c_g, ((0, 0), (w, 0), (0, 0)))[:, :S + 1]
        cnt = jnp.minimum(t + 1, w).astype(jnp.float32)
        pooled.append((c_g - lag)[:, 1:] / cnt[None, :, None])
    mix = (jnp.stack(pooled, axis=2) - hg).astype(h.dtype)
    y = jnp.einsum('bsgc,gcd->bsgd', mix, w_grp).reshape(B, S, D_MODEL)
    return y * scale


def _shared_kv(x, c, kv_ada_w, kv_ada_b, kv_norm_g, kv_w, cmp_pe_k, cmp_pe_v,
               cmp_k_w1, cmp_k_w2, cmp_v_w1, cmp_v_w2, k_gain):
    B, S, _ = x.shape
    shift, scale = jnp.split(jax.nn.silu(c) @ kv_ada_w + kv_ada_b, 2, axis=-1)
    h = _modulate(_rms_norm(x, kv_norm_g), shift, scale)
    kv = (h @ kv_w).reshape(B, S, 6, N_KV_GROUPS, HEAD_DIM).transpose(2, 0, 3, 1, 4)
    k_c, v_c, k_s, v_s, k_w, v_w = kv[0], kv[1], kv[2], kv[3], kv[4], kv[5]

    def compress(u, pe, w1, w2):
        r = u.reshape(B, N_KV_GROUPS, S // D_STRIDE, D_STRIDE, HEAD_DIM)
        nc = S // D_STRIDE - R_CMP + 1
        blocks = jnp.concatenate([r[:, :, i:i + nc] for i in range(R_CMP)], axis=3) + pe
        flat = blocks.reshape(B, N_KV_GROUPS, nc, L_CMP * HEAD_DIM)
        return jax.nn.gelu(flat @ w1) @ w2

    kc = _rms_norm(compress(k_c, cmp_pe_k, cmp_k_w1, cmp_k_w2), k_gain[0])
    vc = compress(v_c, cmp_pe_v, cmp_v_w1, cmp_v_w2)
    ks = _rms_norm(k_s, k_gain[1])
    kw = _rms_norm(k_w, k_gain[2])
    return kc, vc, ks, v_s, kw, v_w


def _nsa_mixer(h, kv, w_qg, q_gain, w_o, rel_bias):
    B, S, _ = h.shape
    proj = h @ w_qg
    qd = N_HEADS * HEAD_DIM
    q = _rms_norm(proj[..., :qd].reshape(B, S, N_HEADS, HEAD_DIM), q_gain) * HEAD_DIM ** -0.5
    gates = jax.nn.sigmoid(proj[..., qd:])
    out = _nsa_attention(q.reshape(B, S, qd), gates, kv, rel_bias)
    return out @ w_o


def _moe_ffn(h, w_router, b_router, w_gu, w_dn):
    B, S, D = h.shape
    T = B * S
    xf = h.reshape(T, D)
    logits = jnp.dot(xf, w_router, precision=lax.Precision.HIGHEST).astype(jnp.float32) + b_router
    top_logit, top_e = lax.top_k(logits, TOP_K)
    top_w = jax.nn.softmax(top_logit, axis=-1)
    A = T * TOP_K
    flat_e = top_e.reshape(A)
    flat_tok = jnp.repeat(jnp.arange(T), TOP_K)
    flat_w = top_w.reshape(A)
    order = jnp.argsort(flat_e)
    se = flat_e[order]
    counts = jnp.zeros((N_EXPERTS,), jnp.int32).at[flat_e].add(1)
    starts = jnp.cumsum(counts) - counts
    pcounts = (counts + ROW_TILE - 1) // ROW_TILE * ROW_TILE
    pends = jnp.cumsum(pcounts)
    pstarts = pends - pcounts
    dest = pstarts[se] + jnp.arange(A) - starts[se]
    n_blocks = -(-(A + N_EXPERTS * (ROW_TILE - 1)) // ROW_TILE)
    n_rows = n_blocks * ROW_TILE
    row_tok = jnp.zeros((n_rows,), jnp.int32).at[dest].set(flat_tok[order])
    row_w = jnp.zeros((n_rows,), jnp.float32).at[dest].set(flat_w[order])
    blk_e = jnp.minimum(jnp.searchsorted(pends, jnp.arange(n_blocks) * ROW_TILE, side='right'),
                        N_EXPERTS - 1).astype(jnp.int32)
    x_rows = xf.astype(jnp.bfloat16)[row_tok]
    y = _grouped_swiglu(x_rows, blk_e, w_gu.astype(jnp.bfloat16), w_dn.astype(jnp.bfloat16),
                        ff_tile=FF_TILE_EXPERT)
    out = jnp.zeros((T, D), h.dtype).at[row_tok].add(y * row_w[:, None])
    return out.reshape(B, S, D)


def _dense_ffn(h, w_gu, w_dn):
    B, S, D = h.shape
    T = B * S
    blk_e = jnp.zeros((T // ROW_TILE,), jnp.int32)
    y = _grouped_swiglu(h.reshape(T, D).astype(jnp.bfloat16), blk_e,
                        w_gu.astype(jnp.bfloat16)[None], w_dn.astype(jnp.bfloat16)[None],
                        ff_tile=FF_TILE_DENSE)
    return y.reshape(B, S, D)


def kernel(x, c, ada_w, ada_b, norm_g, pool_w, pool_scale, q_w, q_gain, o_w, kv_ada_w, kv_ada_b, kv_norm_g, kv_w, cmp_pe_k, cmp_pe_v, cmp_k_w1, cmp_k_w2, cmp_v_w1, cmp_v_w2, k_gain, rel_bias, ffn_gu, ffn_dn, router_w, router_b, exp_gu, exp_dn):
    kv = None
    for l in range(DEPTH):
        sh1, sc1, g1, sh2, sc2, g2 = jnp.split(jax.nn.silu(c) @ ada_w[l] + ada_b[l], 6, axis=-1)
        if l >= N_A_LAYERS and kv is None:
            kv = _shared_kv(x, c, kv_ada_w, kv_ada_b, kv_norm_g, kv_w, cmp_pe_k, cmp_pe_v,
                            cmp_k_w1, cmp_k_w2, cmp_v_w1, cmp_v_w2, k_gain)
        h = _modulate(_rms_norm(x, norm_g[l, 0]), sh1, sc1)
        if l < N_A_LAYERS:
            mix = _pool_mixer(h, pool_w[l], pool_scale[l])
        else:
            j = l - N_A_LAYERS
            mix = _nsa_mixer(h, kv, q_w[j], q_gain[j], o_w[j], rel_bias)
        x = x + g1[:, None, :] * mix
        h = _modulate(_rms_norm(x, norm_g[l, 1]), sh2, sc2)
        if l % 2 == 0:
            f = _dense_ffn(h, ffn_gu[l // 2], ffn_dn[l // 2])
        else:
            f = _moe_ffn(h, router_w[l // 2], router_b[l // 2], exp_gu[l // 2], exp_dn[l // 2])
        x = x + g2[:, None, :] * f
    return x
```

```python
import functools
import math

import jax
import jax.numpy as jnp
import numpy as np
from jax import lax
from jax.experimental import pallas as pl
from jax.experimental.pallas import tpu as pltpu

D_MODEL = 1024
DEPTH = 2
N_A_LAYERS = DEPTH // 2
POOL_WINDOWS = (2, 4, 8, 16)
N_POOL_GROUPS = len(POOL_WINDOWS)
POOL_GROUP_DIM = D_MODEL // N_POOL_GROUPS
HEAD_DIM = 64
N_HEADS = D_MODEL // HEAD_DIM
N_KV_GROUPS = 4
HEADS_PER_GROUP = N_HEADS // N_KV_GROUPS
L_CMP = 32
D_STRIDE = 16
L_SLC = 64
N_SEL = 16
WINDOW = 512
R_CMP = L_CMP // D_STRIDE
R_SLC = L_SLC // D_STRIDE
N_BUCKETS = 32
REL_EXACT = N_BUCKETS // 2
MAX_DISTANCE = 1024
N_EXPERTS = 8
TOP_K = 2
EPS = 1e-6
NEG_INF = -1e30
SEL_FORCE = 1e6

V7X_VMEM_LIMIT_BYTES = 48 * 1024 * 1024
LANES = 128
ROW_TILE = 512
TOK_TILE = 256
FF_TILE_DENSE = 1408
FF_TILE_EXPERT = 512

QB = 128
KEY_TILE = 2 * QB
QROWS = HEADS_PER_GROUP * QB
NBLK = 128
CMP_PAD = 128
FAR_DIST = MAX_DISTANCE
N_TOEP = FAR_DIST // QB + 3
BAND_LEFT = CMP_PAD - 16
assert BAND_LEFT * D_STRIDE + (L_CMP - 1) - 2 * QB >= FAR_DIST


def _rms_norm(x, g):
    xf = x.astype(jnp.float32)
    y = xf * lax.rsqrt(jnp.mean(xf * xf, axis=-1, keepdims=True) + EPS)
    return (y * g.astype(jnp.float32)).astype(x.dtype)


def _modulate(h, shift, scale):
    return h * (1 + scale[:, None, :]) + shift[:, None, :]


def _rel_bucket(dist):
    d = jnp.maximum(dist, 0)
    ratio = jnp.maximum(d, REL_EXACT).astype(jnp.float32) / REL_EXACT
    large = REL_EXACT + (jnp.log(ratio) / math.log(MAX_DISTANCE / REL_EXACT)
                         * (N_BUCKETS - REL_EXACT)).astype(jnp.int32)
    return jnp.where(d < REL_EXACT, d, jnp.minimum(large, N_BUCKETS - 1))


def _norm_modulate(x, gain, shift, scale):
    y = x * lax.rsqrt(jnp.mean(x * x, axis=-1, keepdims=True) + EPS)
    return (y * gain) * (1.0 + scale) + shift


def _swiglu_step(xb_ref, wg_ref, wu_ref, wd_ref, acc_ref, j):
    @pl.when(j == 0)
    def _():
        acc_ref[...] = jnp.zeros_like(acc_ref)

    x = xb_ref[...]
    gate = jnp.dot(x, wg_ref[0], preferred_element_type=jnp.float32)
    up = jnp.dot(x, wu_ref[0], preferred_element_type=jnp.float32)
    act = (gate * jax.nn.sigmoid(gate) * up).astype(jnp.bfloat16)
    acc_ref[...] += jnp.dot(act, wd_ref[0], preferred_element_type=jnp.float32)


def _grouped_swiglu_body(blk_e_ref, x_ref, wg_ref, wu_ref, wd_ref, o_ref, acc_ref, xb_ref, *, n_ff_steps):
    del blk_e_ref
    j = pl.program_id(1)

    @pl.when(j == 0)
    def _():
        xb_ref[...] = x_ref[...].astype(jnp.bfloat16)

    _swiglu_step(xb_ref, wg_ref, wu_ref, wd_ref, acc_ref, j)

    @pl.when(j == n_ff_steps - 1)
    def _():
        o_ref[...] = acc_ref[...]


def _grouped_swiglu(x_rows, blk_e, w_gu, w_dn, *, ff_tile):
    n_rows, d = x_rows.shape
    d_ff = w_dn.shape[1]
    assert n_rows % ROW_TILE == 0 and d_ff % ff_tile == 0
    n_ff_steps = d_ff // ff_tile
    grid = (n_rows // ROW_TILE, n_ff_steps)
    return pl.pallas_call(
        functools.partial(_grouped_swiglu_body, n_ff_steps=n_ff_steps),
        grid_spec=pltpu.PrefetchScalarGridSpec(
            num_scalar_prefetch=1,
            grid=grid,
            in_specs=[
                pl.BlockSpec((ROW_TILE, d), lambda i, j, e: (i, 0)),
                pl.BlockSpec((1, d, ff_tile), lambda i, j, e: (e[i], 0, j)),
                pl.BlockSpec((1, d, ff_tile), lambda i, j, e: (e[i], 0, j + n_ff_steps)),
                pl.BlockSpec((1, ff_tile, d), lambda i, j, e: (e[i], j, 0)),
            ],
            out_specs=pl.BlockSpec((ROW_TILE, d), lambda i, j, e: (i, 0)),
            scratch_shapes=[pltpu.VMEM((ROW_TILE, d), jnp.float32), pltpu.VMEM((ROW_TILE, d), jnp.bfloat16)],
        ),
        out_shape=jax.ShapeDtypeStruct((n_rows, d), jnp.float32),
        compiler_params=pltpu.CompilerParams(
            dimension_semantics=("arbitrary", "arbitrary"),
            vmem_limit_bytes=V7X_VMEM_LIMIT_BYTES,
        ),
        name="grouped_swiglu",
    )(blk_e, x_rows, w_gu, w_gu, w_dn)


def _dense_ffn_body(x_ref, gain_ref, shift_ref, scale_ref, gate2_ref, wg_ref, wu_ref, wd_ref, o_ref,
                    acc_ref, xb_ref, *, n_ff_steps):
    j = pl.program_id(1)

    @pl.when(j == 0)
    def _():
        xb_ref[...] = _norm_modulate(x_ref[...], gain_ref[...], shift_ref[0], scale_ref[0]).astype(jnp.bfloat16)

    _swiglu_step(xb_ref, wg_ref, wu_ref, wd_ref, acc_ref, j)

    @pl.when(j == n_ff_steps - 1)
    def _():
        o_ref[...] = x_ref[...] + gate2_ref[0] * acc_ref[...]


def _dense_ffn(x, gain, shift, scale, gate2, w_gu, w_dn):
    B, S, D = x.shape
    d_ff = w_dn.shape[0]
    n_ff_steps = d_ff // FF_TILE_DENSE
    tiles_per_batch = S // ROW_TILE
    per_batch = pl.BlockSpec((1, 1, D), lambda i, j: (i // tiles_per_batch, 0, 0))
    out = pl.pallas_call(
        functools.partial(_dense_ffn_body, n_ff_steps=n_ff_steps),
        grid=(B * S // ROW_TILE, n_ff_steps),
        in_specs=[
            pl.BlockSpec((ROW_TILE, D), lambda i, j: (i, 0)),
            pl.BlockSpec((1, D), lambda i, j: (0, 0)),
            per_batch, per_batch, per_batch,
            pl.BlockSpec((1, D, FF_TILE_DENSE), lambda i, j: (0, 0, j)),
            pl.BlockSpec((1, D, FF_TILE_DENSE), lambda i, j: (0, 0, j + n_ff_steps)),
            pl.BlockSpec((1, FF_TILE_DENSE, D), lambda i, j: (0, j, 0)),
        ],
        out_specs=pl.BlockSpec((ROW_TILE, D), lambda i, j: (i, 0)),
        out_shape=jax.ShapeDtypeStruct((B * S, D), jnp.float32),
        scratch_shapes=[pltpu.VMEM((ROW_TILE, D), jnp.float32), pltpu.VMEM((ROW_TILE, D), jnp.bfloat16)],
        compiler_params=pltpu.CompilerParams(
            dimension_semantics=("arbitrary", "arbitrary"),
            vmem_limit_bytes=V7X_VMEM_LIMIT_BYTES,
        ),
        name="dense_ffn",
    )(x.reshape(B * S, D), gain.reshape(1, D), shift[:, None, :], scale[:, None, :], gate2[:, None, :],
      w_gu.astype(jnp.bfloat16)[None], w_gu.astype(jnp.bfloat16)[None], w_dn.astype(jnp.bfloat16)[None])
    return out.reshape(B, S, D)


def _row_copies_wait(src_ref, dst_ref, sem, n_rows):
    pltpu.make_async_copy(src_ref.at[pl.ds(0, n_rows)], dst_ref.at[pl.ds(0, n_rows)], sem).wait()


def _dispatch_body(dest_ref, x_ref, gain_ref, shift_ref, scale_ref, rows_in_ref, rows_ref, h_ref, sem):
    del rows_in_ref
    h_ref[...] = _norm_modulate(x_ref[...], gain_ref[...], shift_ref[0], scale_ref[0])

    def issue(r, carry):
        for k in range(TOP_K):
            pltpu.make_async_copy(h_ref.at[pl.ds(r, 1)], rows_ref.at[pl.ds(dest_ref[TOP_K * r + k], 1)],
                                  sem).start()
        return carry

    lax.fori_loop(0, TOK_TILE, issue, 0)
    for k in range(TOP_K):
        _row_copies_wait(h_ref, rows_ref, sem, TOK_TILE)


def _moe_dispatch(x, gain, shift, scale, dest, n_rows):
    B, S, D = x.shape
    T = B * S
    tiles_per_batch = S // TOK_TILE
    per_batch = pl.BlockSpec((1, 1, D), lambda i: (i // tiles_per_batch, 0, 0))
    return pl.pallas_call(
        _dispatch_body,
        grid=(T // TOK_TILE,),
        in_specs=[
            pl.BlockSpec((TOP_K * TOK_TILE,), lambda i: (i,), memory_space=pltpu.SMEM),
            pl.BlockSpec((TOK_TILE, D), lambda i: (i, 0)),
            pl.BlockSpec((1, D), lambda i: (0, 0)),
            per_batch, per_batch,
            pl.BlockSpec(memory_space=pl.ANY),
        ],
        out_specs=pl.BlockSpec(memory_space=pl.ANY),
        out_shape=jax.ShapeDtypeStruct((n_rows, D), jnp.float32),
        scratch_shapes=[pltpu.VMEM((TOK_TILE, D), jnp.float32), pltpu.SemaphoreType.DMA(())],
        input_output_aliases={5: 0},
        compiler_params=pltpu.CompilerParams(dimension_semantics=("arbitrary",)),
        name="moe_dispatch",
    )(dest.reshape(T * TOP_K), x.reshape(T, D), gain.reshape(1, D), shift[:, None, :], scale[:, None, :],
      jnp.zeros((n_rows, D), jnp.float32))


def _combine_body(dest_ref, x_ref, w_ref, gate2_ref, y_ref, o_ref, buf_ref, sem):
    def issue(r, carry):
        for k in range(TOP_K):
            pltpu.make_async_copy(y_ref.at[pl.ds(dest_ref[TOP_K * r + k], 1)], buf_ref.at[k, pl.ds(r, 1)],
                                  sem).start()
        return carry

    lax.fori_loop(0, TOK_TILE, issue, 0)
    for k in range(TOP_K):
        _row_copies_wait(y_ref, buf_ref.at[k], sem, TOK_TILE)
    w = w_ref[...]
    f = w[:, 0:1] * buf_ref[0] + w[:, 1:2] * buf_ref[1]
    o_ref[...] = x_ref[...] + gate2_ref[0] * f


def _moe_combine(x, top_w, gate2, y, dest):
    B, S, D = x.shape
    T = B * S
    tiles_per_batch = S // TOK_TILE
    out = pl.pallas_call(
        _combine_body,
        grid=(T // TOK_TILE,),
        in_specs=[
            pl.BlockSpec((TOP_K * TOK_TILE,), lambda i: (i,), memory_space=pltpu.SMEM),
            pl.BlockSpec((TOK_TILE, D), lambda i: (i, 0)),
            pl.BlockSpec((TOK_TILE, TOP_K), lambda i: (i, 0)),
            pl.BlockSpec((1, 1, D), lambda i: (i // tiles_per_batch, 0, 0)),
            pl.BlockSpec(memory_space=pl.ANY),
        ],
        out_specs=pl.BlockSpec((TOK_TILE, D), lambda i: (i, 0)),
        out_shape=jax.ShapeDtypeStruct((T, D), jnp.float32),
        scratch_shapes=[pltpu.VMEM((TOP_K, TOK_TILE, D), jnp.float32), pltpu.SemaphoreType.DMA(())],
        compiler_params=pltpu.CompilerParams(dimension_semantics=("arbitrary",)),
        name="moe_combine",
    )(dest.reshape(T * TOP_K), x.reshape(T, D), top_w, gate2[:, None, :], y)
    return out.reshape(B, S, D)


def _bias_tables(rel_bias):
    x0 = (N_TOEP - 1) * QB
    width = x0 + 2 * QB
    period = width + QB
    n = np.arange(period)
    n = np.where(n < width, n, n - period)
    by_dist = rel_bias.astype(jnp.float32)[_rel_bucket(jnp.asarray(np.maximum(x0 - n, 0)))].T
    strip = jnp.tile(by_dist, (1, QB))[:, :QB * (period - 1)].reshape(N_HEADS, QB, period - 1)[:, :, :width]
    far = rel_bias.astype(jnp.float32)[N_BUCKETS - 1]

    i = np.arange(QB)[:, None]
    toep = jnp.stack([strip[:, :, x0 - QB * m:x0 - QB * m + QB] for m in range(-1, N_TOEP - 1)], axis=1)
    toep = toep.reshape(N_KV_GROUPS, HEADS_PER_GROUP, N_TOEP, QB, QB).transpose(0, 2, 1, 3, 4)
    d_win = WINDOW + i - np.arange(WINDOW + QB)[None, :]
    win = jnp.where((d_win >= 0) & (d_win < WINDOW), strip[:, :, x0 - WINDOW:x0 + QB], NEG_INF)
    win = win.reshape(N_KV_GROUPS, HEADS_PER_GROUP, QB, WINDOW + QB)
    bands = []
    for par in range(2):
        off = x0 - QB * par - D_STRIDE * BAND_LEFT + (L_CMP - 1)
        c_first = -(off // D_STRIDE)
        cols = strip[:, :, D_STRIDE * c_first + off::D_STRIDE][:, :, :128 - c_first]
        left = jnp.broadcast_to(far[:, None, None], (N_HEADS, QB, c_first))
        right = jnp.zeros((N_HEADS, QB, 128 - c_first - cols.shape[2]), jnp.float32)
        d_band = QB * par + i - D_STRIDE * (np.arange(128)[None, :] - BAND_LEFT) - (L_CMP - 1)
        assert (d_band[:, 128 - right.shape[2]:] < 0).all()
        bands.append(jnp.where(d_band >= 0, jnp.concatenate([left, cols, right], axis=-1), NEG_INF))
    band = jnp.stack(bands).reshape(2, N_KV_GROUPS, HEADS_PER_GROUP, QB, 128)
    return toep, win, band, far


def _slc_map_matrix(n_cmp_cols):
    w = np.zeros((n_cmp_cols, NBLK), np.float32)
    for jb in range(NBLK):
        for mm in range(R_SLC):
            for nn in range(R_CMP):
                k = R_SLC * jb + mm - nn
                if 0 <= k < n_cmp_cols:
                    w[k, jb] += 1.0
    return w


def _dot_nt(a, b):
    return lax.dot_general(a, b, (((1,), (1,)), ((), ())), preferred_element_type=jnp.float32)


def _cmp_select_body(far_ref, q_ref, gate_ref, kc_ref, vc_ref, band_ref, wmap_ref, oc_ref, sel_ref,
                     *, n_far):
    qi = pl.program_id(1)
    par = qi % 2
    band0 = pl.multiple_of(16 * (qi // 2 + 1), 16)
    first_band_blk = band0 - CMP_PAD

    col = lax.broadcasted_iota(jnp.int32, (QB, 128), 1)
    row = lax.broadcasted_iota(jnp.int32, (QB, 128), 0)
    neg_pad = jnp.where(first_band_blk + col >= 0, 0.0, NEG_INF)
    far_col = lax.broadcasted_iota(jnp.int32, (1, n_far), 1)
    neg_far = jnp.where(far_col < first_band_blk, 0.0, NEG_INF)

    t = qi * QB + row
    cur = t // L_SLC
    forced = (col == 0) | (col == cur) | (col == cur - 1)
    valid = col * L_SLC <= t

    q_all = q_ref[0]
    wmap_all = jnp.concatenate([wmap_ref[CMP_PAD:CMP_PAD + n_far, :], wmap_ref[pl.ds(band0, 128), :]], axis=0)
    colf = col.astype(jnp.float32)
    oc_heads = []
    for g in range(N_KV_GROUPS):
        q4 = jnp.concatenate(
            [q_all[:, (g * HEADS_PER_GROUP + h) * HEAD_DIM:(g * HEADS_PER_GROUP + h + 1) * HEAD_DIM]
             for h in range(HEADS_PER_GROUP)], axis=0)
        k_far = kc_ref[0, g, CMP_PAD:CMP_PAD + n_far, :]
        k_band = kc_ref[0, g, pl.ds(band0, 128), :]
        v_far = vc_ref[0, g, CMP_PAD:CMP_PAD + n_far, :]
        v_band = vc_ref[0, g, pl.ds(band0, 128), :]
        s_far = _dot_nt(q4, k_far).reshape(HEADS_PER_GROUP, QB, n_far)
        s_band = _dot_nt(q4, k_band).reshape(HEADS_PER_GROUP, QB, 128)
        v_all = jnp.concatenate([v_far, v_band], axis=0)
        imp = jnp.zeros((QB, n_far + 128), jnp.float32)
        for h in range(HEADS_PER_GROUP):
            hh = g * HEADS_PER_GROUP + h
            s = jnp.concatenate([s_far[h] + far_ref[hh] + neg_far,
                                 s_band[h] + band_ref[par, g, h] + neg_pad], axis=-1)
            m = jnp.max(s, axis=-1, keepdims=True)
            p = jnp.where(m > 0.5 * NEG_INF, jnp.exp(s - m), 0.0)
            l = jnp.sum(p, axis=-1, keepdims=True)
            p = p * jnp.where(l > 0.0, 1.0 / l, 0.0)
            imp = imp + p
            o = jnp.dot(p.astype(jnp.bfloat16), v_all, preferred_element_type=jnp.float32)
            oc_heads.append(o * gate_ref[0, :, 3 * hh:3 * hh + 1])
        p_slc = jnp.zeros((QB, NBLK), jnp.float32)
        rest = imp
        for _ in range(3):
            term = rest.astype(jnp.bfloat16)
            p_slc = p_slc + jnp.dot(term, wmap_all, preferred_element_type=jnp.float32)
            rest = rest - term.astype(jnp.float32)
        score = jnp.where(forced, SEL_FORCE, jnp.where(valid, p_slc, -SEL_FORCE))
        for _ in range(N_SEL):
            best = jnp.max(score, axis=-1, keepdims=True)
            first = jnp.min(jnp.where(score == best, colf, float(NBLK)), axis=-1, keepdims=True)
            score = jnp.where(colf == first, -jnp.inf, score)
        sel_ref[0, g] = jnp.where(score == -jnp.inf, 0.0, NEG_INF).astype(jnp.bfloat16)
    oc_ref[0] = jnp.concatenate(oc_heads, axis=-1)


def _cmp_select(q, gates, kc_pad, vc_pad, band, far, wmap):
    B, S, _ = q.shape
    n_far = S // D_STRIDE
    n_pad = kc_pad.shape[2]
    grid = (B, S // QB)
    return pl.pallas_call(
        functools.partial(_cmp_select_body, n_far=n_far),
        grid=grid,
        in_specs=[
            pl.BlockSpec(memory_space=pltpu.SMEM),
            pl.BlockSpec((1, QB, N_HEADS * HEAD_DIM), lambda b, i: (b, i, 0)),
            pl.BlockSpec((1, QB, 3 * N_HEADS), lambda b, i: (b, i, 0)),
            pl.BlockSpec((1, N_KV_GROUPS, n_pad, HEAD_DIM), lambda b, i: (b, 0, 0, 0)),
            pl.BlockSpec((1, N_KV_GROUPS, n_pad, HEAD_DIM), lambda b, i: (b, 0, 0, 0)),
            pl.BlockSpec((2, N_KV_GROUPS, HEADS_PER_GROUP, QB, 128), lambda b, i: (0, 0, 0, 0, 0)),
            pl.BlockSpec((n_pad, NBLK), lambda b, i: (0, 0)),
        ],
        out_specs=[
            pl.BlockSpec((1, QB, N_HEADS * HEAD_DIM), lambda b, i: (b, i, 0)),
            pl.BlockSpec((1, N_KV_GROUPS, QB, NBLK), lambda b, i: (b, 0, i, 0)),
        ],
        out_shape=[
            jax.ShapeDtypeStruct((B, S, N_HEADS * HEAD_DIM), jnp.float32),
            jax.ShapeDtypeStruct((B, N_KV_GROUPS, S, NBLK), jnp.bfloat16),
        ],
        compiler_params=pltpu.CompilerParams(
            dimension_semantics=("arbitrary", "arbitrary"),
            vmem_limit_bytes=V7X_VMEM_LIMIT_BYTES,
        ),
        name="nsa_cmp_select",
    )(far, q, gates, kc_pad, vc_pad, band, wmap)


def _win_sel_body(q_ref, gate_ref, sel_ref, oc_ref, ks_ref, vs_ref, kw_ref, vw_ref, toep_ref, win_ref,
                  out_ref, s0_scr, s1_scr, p0_scr, p1_scr, m_scr, acc_scr):
    qi = pl.program_id(2)
    q0 = pl.multiple_of(qi * QB, QB)
    qg = q_ref[0]
    q_heads = [qg[:, h * HEAD_DIM:(h + 1) * HEAD_DIM] for h in range(HEADS_PER_GROUP)]
    q4 = jnp.concatenate(q_heads, axis=0)

    kw = kw_ref[0, 0, :, pl.ds(q0, WINDOW + QB)]
    s_w = jnp.dot(q4, kw, preferred_element_type=jnp.float32)
    wcol = lax.broadcasted_iota(jnp.int32, (1, WINDOW + QB), 1)
    neg_left = jnp.where(q0 + wcol >= WINDOW, 0.0, NEG_INF)
    s_w = s_w.reshape(HEADS_PER_GROUP, QB, WINDOW + QB) + win_ref[0] + neg_left
    m_w = jnp.max(s_w, axis=-1, keepdims=True)
    p_w = jnp.exp(s_w - m_w)
    o_w = jnp.dot(p_w.reshape(QROWS, WINDOW + QB).astype(jnp.bfloat16),
                  vw_ref[0, 0, pl.ds(q0, WINDOW + QB), :], preferred_element_type=jnp.float32)
    o_w = o_w.reshape(HEADS_PER_GROUP, QB, LANES)
    o_w = o_w[:, :, :HEAD_DIM] / o_w[:, :, HEAD_DIM:HEAD_DIM + 1]

    sel = sel_ref[0, 0]
    qa = jnp.concatenate([jnp.concatenate([sel] * HEADS_PER_GROUP, axis=0), q4], axis=-1)
    par = qi % 2
    c_diag = qi // 2

    def scores(c):
        col = pl.multiple_of(c * KEY_TILE, KEY_TILE)
        s = jnp.dot(qa, ks_ref[0, 0, :, pl.ds(col, KEY_TILE)], preferred_element_type=jnp.float32)
        mm = qi - 2 * c
        bias = jnp.concatenate([toep_ref[0, jnp.minimum(mm, N_TOEP - 2) + 1],
                                toep_ref[0, jnp.minimum(mm - 1, N_TOEP - 2) + 1]], axis=-1)
        return s.reshape(HEADS_PER_GROUP, QB, KEY_TILE) + bias, col

    s_d, col_d = scores(c_diag)
    kcol = lax.broadcasted_iota(jnp.int32, (QB, KEY_TILE), 1)
    krow = lax.broadcasted_iota(jnp.int32, (QB, KEY_TILE), 0)
    causal = kcol <= krow + par * QB
    s_d = jnp.where(causal, s_d, NEG_INF)
    m0 = jnp.max(s_d, axis=-1, keepdims=True)
    s_slots = (s0_scr, s1_scr)
    p_slots = (p0_scr, p1_scr)
    p_slots[0][...] = jnp.exp(s_d - m0).reshape(QROWS, KEY_TILE).astype(jnp.bfloat16)
    m_scr[...] = jnp.broadcast_to(m0, (HEADS_PER_GROUP, QB, LANES))
    acc_scr[...] = jnp.zeros_like(acc_scr)
    s_slots[0][...] = scores(0)[0]

    def weighted_values(p, col):
        return jnp.dot(p, vs_ref[0, 0, pl.ds(pl.multiple_of(col, KEY_TILE), KEY_TILE), :],
                       preferred_element_type=jnp.float32)

    def half_step(c, col_prev, cur):
        pv = weighted_values(p_slots[cur][...], col_prev)
        s_slots[1 - cur][...] = scores(jnp.minimum(c + 1, c_diag - 1))[0]
        for h in range(HEADS_PER_GROUP):
            s = s_slots[cur][h]
            m_old = m_scr[h]
            m_new = jnp.maximum(m_old, jnp.max(s, axis=-1, keepdims=True))
            alpha = jnp.exp(m_old - m_new)
            p = jnp.exp(s - jnp.concatenate([m_new] * (KEY_TILE // LANES), axis=-1))
            p_slots[1 - cur][h * QB:(h + 1) * QB, :] = p.astype(jnp.bfloat16)
            m_scr[h] = m_new
            acc_scr[h * QB:(h + 1) * QB, :] = alpha * (acc_scr[h * QB:(h + 1) * QB, :]
                                                      + pv[h * QB:(h + 1) * QB, :])
        return c * KEY_TILE

    def step_pair(j, col_prev):
        return half_step(2 * j + 1, half_step(2 * j, col_prev, 0), 1)

    col_last = lax.fori_loop(0, c_diag // 2, step_pair, col_d)
    odd = c_diag % 2 == 1
    col_last = lax.cond(odd, lambda col: half_step(c_diag - 1, col, 0), lambda col: col, col_last)
    p_last = jnp.where(odd, p_slots[1][...], p_slots[0][...])
    acc_s = (acc_scr[...] + weighted_values(p_last, col_last)).reshape(HEADS_PER_GROUP, QB, LANES)
    o_s = acc_s[:, :, :HEAD_DIM] / acc_s[:, :, HEAD_DIM:HEAD_DIM + 1]

    outs = []
    for h in range(HEADS_PER_GROUP):
        g_s = gate_ref[0, 0, :, 3 * h + 1:3 * h + 2]
        g_w = gate_ref[0, 0, :, 3 * h + 2:3 * h + 3]
        outs.append(g_s * o_s[h] + g_w * o_w[h])
    out_ref[0] = oc_ref[0] + jnp.concatenate(outs, axis=-1)


def _win_sel(q, gates, selneg, oc, ks_aug, vs, kw_t, vw_pad, toep, win):
    B, S, _ = q.shape
    gw = HEADS_PER_GROUP * HEAD_DIM
    grid = (B, N_KV_GROUPS, S // QB)
    return pl.pallas_call(
        _win_sel_body,
        grid=grid,
        in_specs=[
            pl.BlockSpec((1, QB, gw), lambda b, g, i: (b, i, g)),
            pl.BlockSpec((1, 1, QB, 3 * HEADS_PER_GROUP), lambda b, g, i: (b, g, i, 0)),
            pl.BlockSpec((1, 1, QB, NBLK), lambda b, g, i: (b, g, i, 0)),
            pl.BlockSpec((1, QB, gw), lambda b, g, i: (b, i, g)),
            pl.BlockSpec((1, 1, NBLK + HEAD_DIM, S), lambda b, g, i: (b, g, 0, 0)),
            pl.BlockSpec((1, 1, S, LANES), lambda b, g, i: (b, g, 0, 0)),
            pl.BlockSpec((1, 1, HEAD_DIM, S + WINDOW), lambda b, g, i: (b, g, 0, 0)),
            pl.BlockSpec((1, 1, S + WINDOW, LANES), lambda b, g, i: (b, g, 0, 0)),
            pl.BlockSpec((1, N_TOEP, HEADS_PER_GROUP, QB, QB), lambda b, g, i: (g, 0, 0, 0, 0)),
            pl.BlockSpec((1, HEADS_PER_GROUP, QB, WINDOW + QB), lambda b, g, i: (g, 0, 0, 0)),
        ],
        out_specs=pl.BlockSpec((1, QB, gw), lambda b, g, i: (b, i, g)),
        out_shape=jax.ShapeDtypeStruct((B, S, N_HEADS * HEAD_DIM), jnp.float32),
        scratch_shapes=[
            pltpu.VMEM((HEADS_PER_GROUP, QB, KEY_TILE), jnp.float32),
            pltpu.VMEM((HEADS_PER_GROUP, QB, KEY_TILE), jnp.float32),
            pltpu.VMEM((QROWS, KEY_TILE), jnp.bfloat16),
            pltpu.VMEM((QROWS, KEY_TILE), jnp.bfloat16),
            pltpu.VMEM((HEADS_PER_GROUP, QB, LANES), jnp.float32),
            pltpu.VMEM((QROWS, LANES), jnp.float32),
        ],
        compiler_params=pltpu.CompilerParams(
            dimension_semantics=("arbitrary", "arbitrary", "arbitrary"),
            vmem_limit_bytes=V7X_VMEM_LIMIT_BYTES,
        ),
        name="nsa_win_sel",
    )(q, gates, selneg, oc, ks_aug, vs, kw_t, vw_pad, toep, win)


def _nsa_attention(q, gates, kv, rel_bias):
    kc, vc, ks, vs, kw, vw = kv
    B, S, _ = q.shape
    assert S % KEY_TILE == 0 and S // L_SLC <= NBLK
    bf = jnp.bfloat16
    n_far = S // D_STRIDE
    toep, win, band, far = _bias_tables(rel_bias)
    cpad = ((0, 0), (0, 0), (CMP_PAD, n_far - kc.shape[2] + CMP_PAD), (0, 0))
    kc_pad = jnp.pad(kc, cpad).astype(bf)
    vc_pad = jnp.pad(vc, cpad).astype(bf)
    wmap = jnp.asarray(np.pad(_slc_map_matrix(n_far), ((CMP_PAD, CMP_PAD), (0, 0))), bf)
    qb = q.astype(bf)
    oc, selneg = _cmp_select(qb, gates, kc_pad, vc_pad, band, far, wmap)
    blk_onehot = jnp.asarray((np.arange(S)[None, :] // L_SLC == np.arange(NBLK)[:, None]), bf)
    ks_aug = jnp.concatenate(
        [jnp.broadcast_to(blk_onehot, (B, N_KV_GROUPS, NBLK, S)), jnp.swapaxes(ks, 2, 3).astype(bf)], axis=2)
    kw_t = jnp.pad(jnp.swapaxes(kw, 2, 3), ((0, 0), (0, 0), (0, 0), (WINDOW, 0))).astype(bf)

    def with_ones(v):
        ones = jnp.ones(v.shape[:-1] + (1,), v.dtype)
        return jnp.pad(jnp.concatenate([v, ones], axis=-1), ((0, 0),) * 3 + ((0, LANES - HEAD_DIM - 1),)).astype(bf)

    vw_aug = jnp.pad(with_ones(vw), ((0, 0), (0, 0), (WINDOW, 0), (0, 0)))
    gates_g = jnp.swapaxes(gates.reshape(B, S, N_KV_GROUPS, 3 * HEADS_PER_GROUP), 1, 2)
    return _win_sel(qb, gates_g, selneg, oc, ks_aug, with_ones(vs), kw_t, vw_aug, toep, win)


def _pool_mixer(h, w_grp, scale):
    B, S, _ = h.shape
    hg = h.astype(jnp.float32).reshape(B, S, N_POOL_GROUPS, POOL_GROUP_DIM)
    cs = jnp.pad(jnp.cumsum(hg, axis=1), ((0, 0), (1, 0), (0, 0), (0, 0)))
    t = jnp.arange(S)
    pooled = []
    for gi, w in enumerate(POOL_WINDOWS):
        c_g = cs[:, :, gi]
        lag = jnp.pad(c_g, ((0, 0), (w, 0), (0, 0)))[:, :S + 1]
        cnt = jnp.minimum(t + 1, w).astype(jnp.float32)
        pooled.append((c_g - lag)[:, 1:] / cnt[None, :, None])
    mix = (jnp.stack(pooled, axis=2) - hg).astype(h.dtype)
    y = jnp.einsum('bsgc,gcd->bsgd', mix, w_grp).reshape(B, S, D_MODEL)
    return y * scale


def _shared_kv(x, c, kv_ada_w, kv_ada_b, kv_norm_g, kv_w, cmp_pe_k, cmp_pe_v,
               cmp_k_w1, cmp_k_w2, cmp_v_w1, cmp_v_w2, k_gain):
    B, S, _ = x.shape
    shift, scale = jnp.split(jax.nn.silu(c) @ kv_ada_w + kv_ada_b, 2, axis=-1)
    h = _modulate(_rms_norm(x, kv_norm_g), shift, scale)
    kv = (h @ kv_w).reshape(B, S, 6, N_KV_GROUPS, HEAD_DIM).transpose(2, 0, 3, 1, 4)
    k_c, v_c, k_s, v_s, k_w, v_w = kv[0], kv[1], kv[2], kv[3], kv[4], kv[5]

    def compress(u, pe, w1, w2):
        r = u.reshape(B, N_KV_GROUPS, S // D_STRIDE, D_STRIDE, HEAD_DIM)
        nc = S // D_STRIDE - R_CMP + 1
        blocks = jnp.concatenate([r[:, :, i:i + nc] for i in range(R_CMP)], axis=3) + pe
        flat = blocks.reshape(B, N_KV_GROUPS, nc, L_CMP * HEAD_DIM)
        return jax.nn.gelu(flat @ w1) @ w2

    kc = _rms_norm(compress(k_c, cmp_pe_k, cmp_k_w1, cmp_k_w2), k_gain[0])
    vc = compress(v_c, cmp_pe_v, cmp_v_w1, cmp_v_w2)
    ks = _rms_norm(k_s, k_gain[1])
    kw = _rms_norm(k_w, k_gain[2])
    return kc, vc, ks, v_s, kw, v_w


def _nsa_mixer(h, kv, w_qg, q_gain, w_o, rel_bias):
    B, S, _ = h.shape
    proj = h @ w_qg
    qd = N_HEADS * HEAD_DIM
    q = _rms_norm(proj[..., :qd].reshape(B, S, N_HEADS, HEAD_DIM), q_gain) * HEAD_DIM ** -0.5
    gates = jax.nn.sigmoid(proj[..., qd:])
    out = _nsa_attention(q.reshape(B, S, qd), gates, kv, rel_bias)
    return out @ w_o


def _route(logits):
    T = logits.shape[0]
    lane = jnp.arange(N_EXPERTS)[None, :]
    l0 = jnp.max(logits, axis=-1)
    e0 = jnp.argmax(logits, axis=-1)
    rest = jnp.where(lane == e0[:, None], -jnp.inf, logits)
    l1 = jnp.max(rest, axis=-1)
    e1 = jnp.argmax(rest, axis=-1)
    z = jnp.exp(l1 - l0)
    top_w = jnp.stack([1.0 / (1.0 + z), z / (1.0 + z)], axis=-1)
    oh = [(lane == e[:, None]).astype(jnp.int32) for e in (e0, e1)]
    cnt = oh[0] + oh[1]
    before = jnp.cumsum(cnt, axis=0) - cnt
    counts = before[-1] + cnt[-1]
    pcounts = (counts + ROW_TILE - 1) // ROW_TILE * ROW_TILE
    pends = jnp.cumsum(pcounts)
    pstarts = pends - pcounts
    dest = jnp.stack([jnp.sum((pstarts[None, :] + before) * o, axis=-1) for o in oh], axis=-1)
    n_blocks = -(-(T * TOP_K + N_EXPERTS * (ROW_TILE - 1)) // ROW_TILE)
    blk_start = jnp.arange(n_blocks)[:, None] * ROW_TILE
    blk_e = jnp.minimum(jnp.sum(pends[None, :] <= blk_start, axis=-1), N_EXPERTS - 1).astype(jnp.int32)
    return top_w, dest.astype(jnp.int32), blk_e, n_blocks * ROW_TILE


def _moe_ffn(x, gain, shift, scale, gate2, w_router, b_router, w_gu, w_dn):
    B, S, D = x.shape
    h = _modulate(_rms_norm(x, gain), shift, scale)
    logits = (h.reshape(B * S, D) @ w_router).astype(jnp.float32) + b_router
    top_w, dest, blk_e, n_rows = _route(logits)
    rows = _moe_dispatch(x, gain, shift, scale, dest, n_rows)
    y = _grouped_swiglu(rows, blk_e, w_gu.astype(jnp.bfloat16), w_dn.astype(jnp.bfloat16),
                        ff_tile=FF_TILE_EXPERT)
    return _moe_combine(x, top_w, gate2, y, dest)


def kernel(x, c, ada_w, ada_b, norm_g, pool_w, pool_scale, q_w, q_gain, o_w, kv_ada_w, kv_ada_b, kv_norm_g, kv_w, cmp_pe_k, cmp_pe_v, cmp_k_w1, cmp_k_w2, cmp_v_w1, cmp_v_w2, k_gain, rel_bias, ffn_gu, ffn_dn, router_w, router_b, exp_gu, exp_dn):
    kv = None
    for l in range(DEPTH):
        sh1, sc1, g1, sh2, sc2, g2 = jnp.split(jax.nn.silu(c) @ ada_w[l] + ada_b[l], 6, axis=-1)
        if l >= N_A_LAYERS and kv is None:
            kv = _shared_kv(x, c, kv_ada_w, kv_ada_b, kv_norm_g, kv_w, cmp_pe_k, cmp_pe_v,
                            cmp_k_w1, cmp_k_w2, cmp_v_w1, cmp_v_w2, k_gain)
        h = _modulate(_rms_norm(x, norm_g[l, 0]), sh1, sc1)
        if l < N_A_LAYERS:
            mix = _pool_mixer(h, pool_w[l], pool_scale[l])
        else:
            j = l - N_A_LAYERS
            mix = _nsa_mixer(h, kv, q_w[j], q_gain[j], o_w[j], rel_bias)
        x = x + g1[:, None, :] * mix
        if l % 2 == 0:
            x = _dense_ffn(x, norm_g[l, 1], sh2, sc2, g2, ffn_gu[l // 2], ffn_dn[l // 2])
        else:
            x = _moe_ffn(x, norm_g[l, 1], sh2, sc2, g2, router_w[l // 2], router_b[l // 2],
                         exp_gu[l // 2], exp_dn[l // 2])
    return x
```

```python
import functools
import math

import jax
import jax.numpy as jnp
import numpy as np
from jax import lax
from jax.experimental import pallas as pl
from jax.experimental.pallas import tpu as pltpu

D_MODEL = 1024
DEPTH = 2
N_A_LAYERS = DEPTH // 2
POOL_WINDOWS = (2, 4, 8, 16)
N_POOL_GROUPS = len(POOL_WINDOWS)
POOL_GROUP_DIM = D_MODEL // N_POOL_GROUPS
HEAD_DIM = 64
N_HEADS = D_MODEL // HEAD_DIM
N_KV_GROUPS = 4
HEADS_PER_GROUP = N_HEADS // N_KV_GROUPS
L_CMP = 32
D_STRIDE = 16
L_SLC = 64
N_SEL = 16
WINDOW = 512
R_CMP = L_CMP // D_STRIDE
R_SLC = L_SLC // D_STRIDE
N_BUCKETS = 32
REL_EXACT = N_BUCKETS // 2
MAX_DISTANCE = 1024
N_EXPERTS = 8
TOP_K = 2
EPS = 1e-6
NEG_INF = -1e30
SEL_FORCE = 1e6
LOG2_E = math.log2(math.e)

V7X_VMEM_LIMIT_BYTES = 48 * 1024 * 1024
LANES = 128
ROW_TILE = 512
TOK_TILE = 256
FF_TILE_DENSE = 1408
FF_TILE_EXPERT = 896

QB = 128
KEY_TILE = 2 * QB
SEL_UNROLL = 4
QROWS = HEADS_PER_GROUP * QB
NBLK = 128
CMP_PAD = 128
FAR_DIST = MAX_DISTANCE
N_TOEP = FAR_DIST // QB + 3
BAND_LEFT = CMP_PAD - 16
assert BAND_LEFT * D_STRIDE + (L_CMP - 1) - 2 * QB >= FAR_DIST


def _rms_norm(x, g):
    xf = x.astype(jnp.float32)
    y = xf * lax.rsqrt(jnp.mean(xf * xf, axis=-1, keepdims=True) + EPS)
    return (y * g.astype(jnp.float32)).astype(x.dtype)


def _modulate(h, shift, scale):
    return h * (1 + scale[:, None, :]) + shift[:, None, :]


def _rel_bucket(dist):
    d = jnp.maximum(dist, 0)
    ratio = jnp.maximum(d, REL_EXACT).astype(jnp.float32) / REL_EXACT
    large = REL_EXACT + (jnp.log(ratio) / math.log(MAX_DISTANCE / REL_EXACT)
                         * (N_BUCKETS - REL_EXACT)).astype(jnp.int32)
    return jnp.where(d < REL_EXACT, d, jnp.minimum(large, N_BUCKETS - 1))


def _norm_modulate(x, gain, shift, scale):
    y = x * lax.rsqrt(jnp.mean(x * x, axis=-1, keepdims=True) + EPS)
    return (y * gain) * (1.0 + scale) + shift


def _swiglu_step(xb_ref, wg_ref, wu_ref, wd_ref, acc_ref, j):
    @pl.when(j == 0)
    def _():
        acc_ref[...] = jnp.zeros_like(acc_ref)

    x = xb_ref[...]
    gate = jnp.dot(x, wg_ref[0], preferred_element_type=jnp.float32)
    up = jnp.dot(x, wu_ref[0], preferred_element_type=jnp.float32)
    act = (gate * jax.nn.sigmoid(gate) * up).astype(jnp.bfloat16)
    acc_ref[...] += jnp.dot(act, wd_ref[0], preferred_element_type=jnp.float32)


def _grouped_swiglu_body(blk_e_ref, x_ref, wg_ref, wu_ref, wd_ref, o_ref, acc_ref, xb_ref, *, n_ff_steps):
    del blk_e_ref
    j = pl.program_id(1)

    @pl.when(j == 0)
    def _():
        xb_ref[...] = x_ref[...].astype(jnp.bfloat16)

    _swiglu_step(xb_ref, wg_ref, wu_ref, wd_ref, acc_ref, j)

    @pl.when(j == n_ff_steps - 1)
    def _():
        o_ref[...] = acc_ref[...]


def _grouped_swiglu(x_rows, blk_e, w_gu, w_dn, *, ff_tile):
    n_rows, d = x_rows.shape
    d_ff = w_dn.shape[1]
    assert n_rows % ROW_TILE == 0 and d_ff % ff_tile == 0
    n_ff_steps = d_ff // ff_tile
    grid = (n_rows // ROW_TILE, n_ff_steps)
    return pl.pallas_call(
        functools.partial(_grouped_swiglu_body, n_ff_steps=n_ff_steps),
        grid_spec=pltpu.PrefetchScalarGridSpec(
            num_scalar_prefetch=1,
            grid=grid,
            in_specs=[
                pl.BlockSpec((ROW_TILE, d), lambda i, j, e: (i, 0)),
                pl.BlockSpec((1, d, ff_tile), lambda i, j, e: (e[i], 0, j)),
                pl.BlockSpec((1, d, ff_tile), lambda i, j, e: (e[i], 0, j + n_ff_steps)),
                pl.BlockSpec((1, ff_tile, d), lambda i, j, e: (e[i], j, 0)),
            ],
            out_specs=pl.BlockSpec((ROW_TILE, d), lambda i, j, e: (i, 0)),
            scratch_shapes=[pltpu.VMEM((ROW_TILE, d), jnp.float32), pltpu.VMEM((ROW_TILE, d), jnp.bfloat16)],
        ),
        out_shape=jax.ShapeDtypeStruct((n_rows, d), jnp.float32),
        compiler_params=pltpu.CompilerParams(
            dimension_semantics=("arbitrary", "arbitrary"),
            vmem_limit_bytes=V7X_VMEM_LIMIT_BYTES,
        ),
        name="grouped_swiglu",
    )(blk_e, x_rows, w_gu, w_gu, w_dn)


def _dense_ffn_body(x_ref, gain_ref, shift_ref, scale_ref, gate2_ref, wg_ref, wu_ref, wd_ref, o_ref,
                    acc_ref, xb_ref, *, n_ff_steps):
    j = pl.program_id(1)

    @pl.when(j == 0)
    def _():
        xb_ref[...] = _norm_modulate(x_ref[...], gain_ref[...], shift_ref[0], scale_ref[0]).astype(jnp.bfloat16)

    _swiglu_step(xb_ref, wg_ref, wu_ref, wd_ref, acc_ref, j)

    @pl.when(j == n_ff_steps - 1)
    def _():
        o_ref[...] = x_ref[...] + gate2_ref[0] * acc_ref[...]


def _dense_ffn(x, gain, shift, scale, gate2, w_gu, w_dn):
    B, S, D = x.shape
    d_ff = w_dn.shape[0]
    n_ff_steps = d_ff // FF_TILE_DENSE
    tiles_per_batch = S // ROW_TILE
    per_batch = pl.BlockSpec((1, 1, D), lambda i, j: (i // tiles_per_batch, 0, 0))
    out = pl.pallas_call(
        functools.partial(_dense_ffn_body, n_ff_steps=n_ff_steps),
        grid=(B * S // ROW_TILE, n_ff_steps),
        in_specs=[
            pl.BlockSpec((ROW_TILE, D), lambda i, j: (i, 0)),
            pl.BlockSpec((1, D), lambda i, j: (0, 0)),
            per_batch, per_batch, per_batch,
            pl.BlockSpec((1, D, FF_TILE_DENSE), lambda i, j: (0, 0, j)),
            pl.BlockSpec((1, D, FF_TILE_DENSE), lambda i, j: (0, 0, j + n_ff_steps)),
            pl.BlockSpec((1, FF_TILE_DENSE, D), lambda i, j: (0, j, 0)),
        ],
        out_specs=pl.BlockSpec((ROW_TILE, D), lambda i, j: (i, 0)),
        out_shape=jax.ShapeDtypeStruct((B * S, D), jnp.float32),
        scratch_shapes=[pltpu.VMEM((ROW_TILE, D), jnp.float32), pltpu.VMEM((ROW_TILE, D), jnp.bfloat16)],
        compiler_params=pltpu.CompilerParams(
            dimension_semantics=("arbitrary", "arbitrary"),
            vmem_limit_bytes=V7X_VMEM_LIMIT_BYTES,
        ),
        name="dense_ffn",
    )(x.reshape(B * S, D), gain.reshape(1, D), shift[:, None, :], scale[:, None, :], gate2[:, None, :],
      w_gu.astype(jnp.bfloat16)[None], w_gu.astype(jnp.bfloat16)[None], w_dn.astype(jnp.bfloat16)[None])
    return out.reshape(B, S, D)


def _row_copies_wait(src_ref, dst_ref, sem, n_rows):
    pltpu.make_async_copy(src_ref.at[pl.ds(0, n_rows)], dst_ref.at[pl.ds(0, n_rows)], sem).wait()


def _dispatch_body(dest_ref, x_ref, gain_ref, shift_ref, scale_ref, rows_in_ref, rows_ref, h_ref, sem):
    del rows_in_ref
    h_ref[...] = _norm_modulate(x_ref[...], gain_ref[...], shift_ref[0], scale_ref[0])

    def issue(r, carry):
        for k in range(TOP_K):
            pltpu.make_async_copy(h_ref.at[pl.ds(r, 1)], rows_ref.at[pl.ds(dest_ref[TOP_K * r + k], 1)],
                                  sem).start()
        return carry

    lax.fori_loop(0, TOK_TILE, issue, 0)
    for k in range(TOP_K):
        _row_copies_wait(h_ref, rows_ref, sem, TOK_TILE)


def _moe_dispatch(x, gain, shift, scale, dest, n_rows):
    B, S, D = x.shape
    T = B * S
    tiles_per_batch = S // TOK_TILE
    per_batch = pl.BlockSpec((1, 1, D), lambda i: (i // tiles_per_batch, 0, 0))
    return pl.pallas_call(
        _dispatch_body,
        grid=(T // TOK_TILE,),
        in_specs=[
            pl.BlockSpec((TOP_K * TOK_TILE,), lambda i: (i,), memory_space=pltpu.SMEM),
            pl.BlockSpec((TOK_TILE, D), lambda i: (i, 0)),
            pl.BlockSpec((1, D), lambda i: (0, 0)),
            per_batch, per_batch,
            pl.BlockSpec(memory_space=pl.ANY),
        ],
        out_specs=pl.BlockSpec(memory_space=pl.ANY),
        out_shape=jax.ShapeDtypeStruct((n_rows, D), jnp.float32),
        scratch_shapes=[pltpu.VMEM((TOK_TILE, D), jnp.float32), pltpu.SemaphoreType.DMA(())],
        input_output_aliases={5: 0},
        compiler_params=pltpu.CompilerParams(dimension_semantics=("arbitrary",)),
        name="moe_dispatch",
    )(dest.reshape(T * TOP_K), x.reshape(T, D), gain.reshape(1, D), shift[:, None, :], scale[:, None, :],
      jnp.zeros((n_rows, D), jnp.float32))


def _combine_body(dest_ref, x_ref, w_ref, gate2_ref, y_ref, o_ref, buf_ref, sem):
    def issue(r, carry):
        for k in range(TOP_K):
            pltpu.make_async_copy(y_ref.at[pl.ds(dest_ref[TOP_K * r + k], 1)], buf_ref.at[k, pl.ds(r, 1)],
                                  sem).start()
        return carry

    lax.fori_loop(0, TOK_TILE, issue, 0)
    for k in range(TOP_K):
        _row_copies_wait(y_ref, buf_ref.at[k], sem, TOK_TILE)
    w = w_ref[...]
    f = w[:, 0:1] * buf_ref[0] + w[:, 1:2] * buf_ref[1]
    o_ref[...] = x_ref[...] + gate2_ref[0] * f


def _moe_combine(x, top_w, gate2, y, dest):
    B, S, D = x.shape
    T = B * S
    tiles_per_batch = S // TOK_TILE
    out = pl.pallas_call(
        _combine_body,
        grid=(T // TOK_TILE,),
        in_specs=[
            pl.BlockSpec((TOP_K * TOK_TILE,), lambda i: (i,), memory_space=pltpu.SMEM),
            pl.BlockSpec((TOK_TILE, D), lambda i: (i, 0)),
            pl.BlockSpec((TOK_TILE, TOP_K), lambda i: (i, 0)),
            pl.BlockSpec((1, 1, D), lambda i: (i // tiles_per_batch, 0, 0)),
            pl.BlockSpec(memory_space=pl.ANY),
        ],
        out_specs=pl.BlockSpec((TOK_TILE, D), lambda i: (i, 0)),
        out_shape=jax.ShapeDtypeStruct((T, D), jnp.float32),
        scratch_shapes=[pltpu.VMEM((TOP_K, TOK_TILE, D), jnp.float32), pltpu.SemaphoreType.DMA(())],
        compiler_params=pltpu.CompilerParams(dimension_semantics=("arbitrary",)),
        name="moe_combine",
    )(dest.reshape(T * TOP_K), x.reshape(T, D), top_w, gate2[:, None, :], y)
    return out.reshape(B, S, D)


def _bias_tables(rel_bias):
    x0 = (N_TOEP - 1) * QB
    width = x0 + 2 * QB
    period = width + QB
    n = np.arange(period)
    n = np.where(n < width, n, n - period)
    by_dist = rel_bias.astype(jnp.float32)[_rel_bucket(jnp.asarray(np.maximum(x0 - n, 0)))].T
    strip = jnp.tile(by_dist, (1, QB))[:, :QB * (period - 1)].reshape(N_HEADS, QB, period - 1)[:, :, :width]
    far = rel_bias.astype(jnp.float32)[N_BUCKETS - 1]

    i = np.arange(QB)[:, None]
    toep = jnp.stack([strip[:, :, x0 - QB * m:x0 - QB * m + QB] for m in range(-1, N_TOEP - 1)], axis=1)
    d_toep = QB * np.arange(-1, N_TOEP - 1)[:, None, None] + i[None] - np.arange(QB)[None, None, :]
    toep = jnp.where(d_toep >= 0, toep * LOG2_E, NEG_INF)
    toep = toep.reshape(N_KV_GROUPS, HEADS_PER_GROUP, N_TOEP, QB, QB).transpose(0, 2, 1, 3, 4)
    d_win = WINDOW + i - np.arange(WINDOW + QB)[None, :]
    win = jnp.where((d_win >= 0) & (d_win < WINDOW), strip[:, :, x0 - WINDOW:x0 + QB], NEG_INF)
    win = win.reshape(N_KV_GROUPS, HEADS_PER_GROUP, QB, WINDOW + QB)
    bands = []
    for par in range(2):
        off = x0 - QB * par - D_STRIDE * BAND_LEFT + (L_CMP - 1)
        c_first = -(off // D_STRIDE)
        cols = strip[:, :, D_STRIDE * c_first + off::D_STRIDE][:, :, :128 - c_first]
        left = jnp.broadcast_to(far[:, None, None], (N_HEADS, QB, c_first))
        right = jnp.zeros((N_HEADS, QB, 128 - c_first - cols.shape[2]), jnp.float32)
        d_band = QB * par + i - D_STRIDE * (np.arange(128)[None, :] - BAND_LEFT) - (L_CMP - 1)
        assert (d_band[:, 128 - right.shape[2]:] < 0).all()
        bands.append(jnp.where(d_band >= 0, jnp.concatenate([left, cols, right], axis=-1), NEG_INF))
    band = jnp.stack(bands).reshape(2, N_KV_GROUPS, HEADS_PER_GROUP, QB, 128)
    return toep, win, band, far


def _slc_map_matrix(n_cmp_cols):
    w = np.zeros((n_cmp_cols, NBLK), np.float32)
    for jb in range(NBLK):
        for mm in range(R_SLC):
            for nn in range(R_CMP):
                k = R_SLC * jb + mm - nn
                if 0 <= k < n_cmp_cols:
                    w[k, jb] += 1.0
    return w


def _dot_nt(a, b):
    return lax.dot_general(a, b, (((1,), (1,)), ((), ())), preferred_element_type=jnp.float32)


def _cmp_select_body(far_ref, q_ref, gate_ref, kc_ref, vc_ref, band_ref, wmap_ref, oc_ref, sel_ref,
                     *, n_far):
    qi = pl.program_id(1)
    par = qi % 2
    band0 = pl.multiple_of(16 * (qi // 2 + 1), 16)
    first_band_blk = band0 - CMP_PAD

    col = lax.broadcasted_iota(jnp.int32, (QB, 128), 1)
    row = lax.broadcasted_iota(jnp.int32, (QB, 128), 0)
    neg_pad = jnp.where(first_band_blk + col >= 0, 0.0, NEG_INF)
    far_col = lax.broadcasted_iota(jnp.int32, (1, n_far), 1)
    neg_far = jnp.where(far_col < first_band_blk, 0.0, NEG_INF)

    t = qi * QB + row
    cur = t // L_SLC
    forced = (col == 0) | (col == cur) | (col == cur - 1)
    valid = col * L_SLC <= t

    q_all = q_ref[0]
    wmap_all = jnp.concatenate([wmap_ref[CMP_PAD:CMP_PAD + n_far, :], wmap_ref[pl.ds(band0, 128), :]], axis=0)
    colf = col.astype(jnp.float32)
    oc_heads = []
    for g in range(N_KV_GROUPS):
        q4 = jnp.concatenate(
            [q_all[:, (g * HEADS_PER_GROUP + h) * HEAD_DIM:(g * HEADS_PER_GROUP + h + 1) * HEAD_DIM]
             for h in range(HEADS_PER_GROUP)], axis=0)
        k_far = kc_ref[0, g, CMP_PAD:CMP_PAD + n_far, :]
        k_band = kc_ref[0, g, pl.ds(band0, 128), :]
        v_far = vc_ref[0, g, CMP_PAD:CMP_PAD + n_far, :]
        v_band = vc_ref[0, g, pl.ds(band0, 128), :]
        s_far = _dot_nt(q4, k_far).reshape(HEADS_PER_GROUP, QB, n_far)
        s_band = _dot_nt(q4, k_band).reshape(HEADS_PER_GROUP, QB, 128)
        v_all = jnp.concatenate([v_far, v_band], axis=0)
        imp = jnp.zeros((QB, n_far + 128), jnp.float32)
        for h in range(HEADS_PER_GROUP):
            hh = g * HEADS_PER_GROUP + h
            s = jnp.concatenate([s_far[h] + far_ref[hh] + neg_far,
                                 s_band[h] + band_ref[par, g, h] + neg_pad], axis=-1)
            m = jnp.max(s, axis=-1, keepdims=True)
            p = jnp.where(m > 0.5 * NEG_INF, jnp.exp(s - m), 0.0)
            l = jnp.sum(p, axis=-1, keepdims=True)
            p = p * jnp.where(l > 0.0, 1.0 / l, 0.0)
            imp = imp + p
            o = jnp.dot(p.astype(jnp.bfloat16), v_all, preferred_element_type=jnp.float32)
            oc_heads.append(o * gate_ref[0, :, 3 * hh:3 * hh + 1])
        p_slc = jnp.zeros((QB, NBLK), jnp.float32)
        rest = imp
        for _ in range(3):
            term = rest.astype(jnp.bfloat16)
            p_slc = p_slc + jnp.dot(term, wmap_all, preferred_element_type=jnp.float32)
            rest = rest - term.astype(jnp.float32)
        score = jnp.where(forced, SEL_FORCE, jnp.where(valid, p_slc, -SEL_FORCE))
        for _ in range(N_SEL):
            best = jnp.max(score, axis=-1, keepdims=True)
            first = jnp.min(jnp.where(score == best, colf, float(NBLK)), axis=-1, keepdims=True)
            score = jnp.where(colf == first, -jnp.inf, score)
        sel_ref[0, g] = jnp.where(score == -jnp.inf, 0.0, NEG_INF).astype(jnp.bfloat16)
    oc_ref[0] = jnp.concatenate(oc_heads, axis=-1)


def _cmp_select(q, gates, kc_pad, vc_pad, band, far, wmap):
    B, S, _ = q.shape
    n_far = S // D_STRIDE
    n_pad = kc_pad.shape[2]
    grid = (B, S // QB)
    return pl.pallas_call(
        functools.partial(_cmp_select_body, n_far=n_far),
        grid=grid,
        in_specs=[
            pl.BlockSpec(memory_space=pltpu.SMEM),
            pl.BlockSpec((1, QB, N_HEADS * HEAD_DIM), lambda b, i: (b, i, 0)),
            pl.BlockSpec((1, QB, 3 * N_HEADS), lambda b, i: (b, i, 0)),
            pl.BlockSpec((1, N_KV_GROUPS, n_pad, HEAD_DIM), lambda b, i: (b, 0, 0, 0)),
            pl.BlockSpec((1, N_KV_GROUPS, n_pad, HEAD_DIM), lambda b, i: (b, 0, 0, 0)),
            pl.BlockSpec((2, N_KV_GROUPS, HEADS_PER_GROUP, QB, 128), lambda b, i: (0, 0, 0, 0, 0)),
            pl.BlockSpec((n_pad, NBLK), lambda b, i: (0, 0)),
        ],
        out_specs=[
            pl.BlockSpec((1, QB, N_HEADS * HEAD_DIM), lambda b, i: (b, i, 0)),
            pl.BlockSpec((1, N_KV_GROUPS, QB, NBLK), lambda b, i: (b, 0, i, 0)),
        ],
        out_shape=[
            jax.ShapeDtypeStruct((B, S, N_HEADS * HEAD_DIM), jnp.float32),
            jax.ShapeDtypeStruct((B, N_KV_GROUPS, S, NBLK), jnp.bfloat16),
        ],
        compiler_params=pltpu.CompilerParams(
            dimension_semantics=("arbitrary", "arbitrary"),
            vmem_limit_bytes=V7X_VMEM_LIMIT_BYTES,
        ),
        name="nsa_cmp_select",
    )(far, q, gates, kc_pad, vc_pad, band, wmap)


def _win_sel_body(q_ref, gate_ref, sel_ref, oc_ref, ks_ref, vs_ref, kw_ref, vw_ref, toep_ref, win_ref,
                  out_ref, s0_scr, s1_scr, p0_scr, p1_scr, m_scr, acc_scr):
    qi = pl.program_id(2)
    q0 = pl.multiple_of(qi * QB, QB)
    qg = q_ref[0]
    q_heads = [qg[:, h * HEAD_DIM:(h + 1) * HEAD_DIM] for h in range(HEADS_PER_GROUP)]
    q4 = jnp.concatenate(q_heads, axis=0)

    kw = kw_ref[0, 0, :, pl.ds(q0, WINDOW + QB)]
    s_w = jnp.dot(q4, kw, preferred_element_type=jnp.float32)
    wcol = lax.broadcasted_iota(jnp.int32, (1, WINDOW + QB), 1)
    neg_left = jnp.where(q0 + wcol >= WINDOW, 0.0, NEG_INF)
    s_w = s_w.reshape(HEADS_PER_GROUP, QB, WINDOW + QB) + win_ref[0] + neg_left
    m_w = jnp.max(s_w, axis=-1, keepdims=True)
    p_w = jnp.exp(s_w - m_w)
    o_w = jnp.dot(p_w.reshape(QROWS, WINDOW + QB).astype(jnp.bfloat16),
                  vw_ref[0, 0, pl.ds(q0, WINDOW + QB), :], preferred_element_type=jnp.float32)
    o_w = o_w.reshape(HEADS_PER_GROUP, QB, LANES)
    o_w = o_w[:, :, :HEAD_DIM] / o_w[:, :, HEAD_DIM:HEAD_DIM + 1]

    sel = sel_ref[0, 0]
    qa = jnp.concatenate([jnp.concatenate([sel] * HEADS_PER_GROUP, axis=0), q4], axis=-1)
    c_diag = qi // 2

    def scores(c):
        col = pl.multiple_of(c * KEY_TILE, KEY_TILE)
        s = jnp.dot(qa, ks_ref[0, 0, :, pl.ds(col, KEY_TILE)], preferred_element_type=jnp.float32)
        mm = qi - 2 * c
        bias = jnp.concatenate([toep_ref[0, jnp.clip(mm + 1, 0, N_TOEP - 1)],
                                toep_ref[0, jnp.clip(mm, 0, N_TOEP - 1)]], axis=-1)
        return s.reshape(HEADS_PER_GROUP, QB, KEY_TILE) + bias

    s_slots = (s0_scr, s1_scr)
    p_slots = (p0_scr, p1_scr)
    m_scr[...] = jnp.full(m_scr.shape, NEG_INF, jnp.float32)
    acc_scr[...] = jnp.zeros_like(acc_scr)
    p_slots[0][...] = jnp.zeros_like(p0_scr)
    s_slots[0][...] = scores(0)

    def weighted_values(p, col):
        return jnp.dot(p, vs_ref[0, 0, pl.ds(pl.multiple_of(col, KEY_TILE), KEY_TILE), :],
                       preferred_element_type=jnp.float32)

    def half_step(c, col_prev, cur):
        pv = weighted_values(p_slots[cur][...], col_prev)
        s_slots[1 - cur][...] = scores(c + 1)
        for h in range(HEADS_PER_GROUP):
            s = s_slots[cur][h]
            m_old = m_scr[h]
            m_new = jnp.maximum(m_old, jnp.max(s, axis=-1, keepdims=True))
            alpha = jnp.exp2(m_old - m_new)
            p = jnp.exp2(s - jnp.concatenate([m_new] * (KEY_TILE // LANES), axis=-1))
            p_slots[1 - cur][h * QB:(h + 1) * QB, :] = p.astype(jnp.bfloat16)
            m_scr[h] = m_new
            acc_scr[h * QB:(h + 1) * QB, :] = alpha * (acc_scr[h * QB:(h + 1) * QB, :]
                                                      + pv[h * QB:(h + 1) * QB, :])
        return c * KEY_TILE

    def unrolled_steps(j, col_prev):
        for u in range(SEL_UNROLL):
            col_prev = half_step(SEL_UNROLL * j + u, col_prev, u % 2)
        return col_prev

    col_last = lax.fori_loop(0, (c_diag + SEL_UNROLL) // SEL_UNROLL, unrolled_steps, 0)
    acc_s = (acc_scr[...] + weighted_values(p_slots[0][...], col_last)).reshape(HEADS_PER_GROUP, QB, LANES)
    o_s = acc_s[:, :, :HEAD_DIM] / acc_s[:, :, HEAD_DIM:HEAD_DIM + 1]

    outs = []
    for h in range(HEADS_PER_GROUP):
        g_s = gate_ref[0, 0, :, 3 * h + 1:3 * h + 2]
        g_w = gate_ref[0, 0, :, 3 * h + 2:3 * h + 3]
        outs.append(g_s * o_s[h] + g_w * o_w[h])
    out_ref[0] = oc_ref[0] + jnp.concatenate(outs, axis=-1)


def _win_sel(q, gates, selneg, oc, ks_aug, vs, kw_t, vw_pad, toep, win):
    B, S, _ = q.shape
    gw = HEADS_PER_GROUP * HEAD_DIM
    grid = (B, N_KV_GROUPS, S // QB)
    return pl.pallas_call(
        _win_sel_body,
        grid=grid,
        in_specs=[
            pl.BlockSpec((1, QB, gw), lambda b, g, i: (b, i, g)),
            pl.BlockSpec((1, 1, QB, 3 * HEADS_PER_GROUP), lambda b, g, i: (b, g, i, 0)),
            pl.BlockSpec((1, 1, QB, NBLK), lambda b, g, i: (b, g, i, 0)),
            pl.BlockSpec((1, QB, gw), lambda b, g, i: (b, i, g)),
            pl.BlockSpec((1, 1, NBLK + HEAD_DIM, ks_aug.shape[3]), lambda b, g, i: (b, g, 0, 0)),
            pl.BlockSpec((1, 1, vs.shape[2], LANES), lambda b, g, i: (b, g, 0, 0)),
            pl.BlockSpec((1, 1, HEAD_DIM, S + WINDOW), lambda b, g, i: (b, g, 0, 0)),
            pl.BlockSpec((1, 1, S + WINDOW, LANES), lambda b, g, i: (b, g, 0, 0)),
            pl.BlockSpec((1, N_TOEP, HEADS_PER_GROUP, QB, QB), lambda b, g, i: (g, 0, 0, 0, 0)),
            pl.BlockSpec((1, HEADS_PER_GROUP, QB, WINDOW + QB), lambda b, g, i: (g, 0, 0, 0)),
        ],
        out_specs=pl.BlockSpec((1, QB, gw), lambda b, g, i: (b, i, g)),
        out_shape=jax.ShapeDtypeStruct((B, S, N_HEADS * HEAD_DIM), jnp.float32),
        scratch_shapes=[
            pltpu.VMEM((HEADS_PER_GROUP, QB, KEY_TILE), jnp.float32),
            pltpu.VMEM((HEADS_PER_GROUP, QB, KEY_TILE), jnp.float32),
            pltpu.VMEM((QROWS, KEY_TILE), jnp.bfloat16),
            pltpu.VMEM((QROWS, KEY_TILE), jnp.bfloat16),
            pltpu.VMEM((HEADS_PER_GROUP, QB, LANES), jnp.float32),
            pltpu.VMEM((QROWS, LANES), jnp.float32),
        ],
        compiler_params=pltpu.CompilerParams(
            dimension_semantics=("arbitrary", "arbitrary", "arbitrary"),
            vmem_limit_bytes=V7X_VMEM_LIMIT_BYTES,
        ),
        name="nsa_win_sel",
    )(q, gates, selneg, oc, ks_aug, vs, kw_t, vw_pad, toep, win)


def _nsa_attention(q, gates, kv, rel_bias):
    kc, vc, ks, vs, kw, vw = kv
    B, S, _ = q.shape
    assert S % KEY_TILE == 0 and S // L_SLC <= NBLK
    bf = jnp.bfloat16
    n_far = S // D_STRIDE
    toep, win, band, far = _bias_tables(rel_bias)
    cpad = ((0, 0), (0, 0), (CMP_PAD, n_far - kc.shape[2] + CMP_PAD), (0, 0))
    kc_pad = jnp.pad(kc, cpad).astype(bf)
    vc_pad = jnp.pad(vc, cpad).astype(bf)
    wmap = jnp.asarray(np.pad(_slc_map_matrix(n_far), ((CMP_PAD, CMP_PAD), (0, 0))), bf)
    qb = q.astype(bf)
    oc, selneg = _cmp_select(qb, gates, kc_pad, vc_pad, band, far, wmap)
    blk_onehot = jnp.asarray((np.arange(S)[None, :] // L_SLC == np.arange(NBLK)[:, None]), bf)
    ks_aug = jnp.concatenate(
        [jnp.broadcast_to(blk_onehot, (B, N_KV_GROUPS, NBLK, S)),
         (jnp.swapaxes(ks, 2, 3) * LOG2_E).astype(bf)], axis=2)
    tail = SEL_UNROLL * KEY_TILE
    ks_aug = jnp.pad(ks_aug, ((0, 0), (0, 0), (0, 0), (0, tail)))
    kw_t = jnp.pad(jnp.swapaxes(kw, 2, 3), ((0, 0), (0, 0), (0, 0), (WINDOW, 0))).astype(bf)

    def with_ones(v):
        ones = jnp.ones(v.shape[:-1] + (1,), v.dtype)
        return jnp.pad(jnp.concatenate([v, ones], axis=-1), ((0, 0),) * 3 + ((0, LANES - HEAD_DIM - 1),)).astype(bf)

    vw_aug = jnp.pad(with_ones(vw), ((0, 0), (0, 0), (WINDOW, 0), (0, 0)))
    gates_g = jnp.swapaxes(gates.reshape(B, S, N_KV_GROUPS, 3 * HEADS_PER_GROUP), 1, 2)
    vs_aug = jnp.pad(with_ones(vs), ((0, 0), (0, 0), (0, tail), (0, 0)))
    return _win_sel(qb, gates_g, selneg, oc, ks_aug, vs_aug, kw_t, vw_aug, toep, win)


def _pool_mixer(h, w_grp, scale):
    B, S, _ = h.shape
    hg = h.astype(jnp.float32).reshape(B, S, N_POOL_GROUPS, POOL_GROUP_DIM)
    cs = jnp.pad(jnp.cumsum(hg, axis=1), ((0, 0), (1, 0), (0, 0), (0, 0)))
    t = jnp.arange(S)
    pooled = []
    for gi, w in enumerate(POOL_WINDOWS):
        c_g = cs[:, :, gi]
        lag = jnp.pad(c_g, ((0, 0), (w, 0), (0, 0)))[:, :S + 1]
        cnt = jnp.minimum(t + 1, w).astype(jnp.float32)
        pooled.append((c_g - lag)[:, 1:] / cnt[None, :, None])
    mix = (jnp.stack(pooled, axis=2) - hg).astype(h.dtype)
    y = jnp.einsum('bsgc,gcd->bsgd', mix, w_grp).reshape(B, S, D_MODEL)
    return y * scale


def _shared_kv(x, c, kv_ada_w, kv_ada_b, kv_norm_g, kv_w, cmp_pe_k, cmp_pe_v,
               cmp_k_w1, cmp_k_w2, cmp_v_w1, cmp_v_w2, k_gain):
    B, S, _ = x.shape
    shift, scale = jnp.split(jax.nn.silu(c) @ kv_ada_w + kv_ada_b, 2, axis=-1)
    h = _modulate(_rms_norm(x, kv_norm_g), shift, scale)
    kv = (h @ kv_w).reshape(B, S, 6, N_KV_GROUPS, HEAD_DIM).transpose(2, 0, 3, 1, 4)
    k_c, v_c, k_s, v_s, k_w, v_w = kv[0], kv[1], kv[2], kv[3], kv[4], kv[5]

    def compress(u, pe, w1, w2):
        r = u.reshape(B, N_KV_GROUPS, S // D_STRIDE, D_STRIDE, HEAD_DIM)
        nc = S // D_STRIDE - R_CMP + 1
        blocks = jnp.concatenate([r[:, :, i:i + nc] for i in range(R_CMP)], axis=3) + pe
        flat = blocks.reshape(B, N_KV_GROUPS, nc, L_CMP * HEAD_DIM)
        return jax.nn.gelu(flat @ w1) @ w2

    kc = _rms_norm(compress(k_c, cmp_pe_k, cmp_k_w1, cmp_k_w2), k_gain[0])
    vc = compress(v_c, cmp_pe_v, cmp_v_w1, cmp_v_w2)
    ks = _rms_norm(k_s, k_gain[1])
    kw = _rms_norm(k_w, k_gain[2])
    return kc, vc, ks, v_s, kw, v_w


def _nsa_mixer(h, kv, w_qg, q_gain, w_o, rel_bias):
    B, S, _ = h.shape
    proj = h @ w_qg
    qd = N_HEADS * HEAD_DIM
    q = _rms_norm(proj[..., :qd].reshape(B, S, N_HEADS, HEAD_DIM), q_gain) * HEAD_DIM ** -0.5
    gates = jax.nn.sigmoid(proj[..., qd:])
    out = _nsa_attention(q.reshape(B, S, qd), gates, kv, rel_bias)
    return out @ w_o


def _route(logits):
    T = logits.shape[0]
    lane = jnp.arange(N_EXPERTS)[None, :]
    l0 = jnp.max(logits, axis=-1)
    e0 = jnp.argmax(logits, axis=-1)
    rest = jnp.where(lane == e0[:, None], -jnp.inf, logits)
    l1 = jnp.max(rest, axis=-1)
    e1 = jnp.argmax(rest, axis=-1)
    z = jnp.exp(l1 - l0)
    top_w = jnp.stack([1.0 / (1.0 + z), z / (1.0 + z)], axis=-1)
    oh = [(lane == e[:, None]).astype(jnp.int32) for e in (e0, e1)]
    cnt = oh[0] + oh[1]
    before = jnp.cumsum(cnt, axis=0) - cnt
    counts = before[-1] + cnt[-1]
    pcounts = (counts + ROW_TILE - 1) // ROW_TILE * ROW_TILE
    pends = jnp.cumsum(pcounts)
    pstarts = pends - pcounts
    dest = jnp.stack([jnp.sum((pstarts[None, :] + before) * o, axis=-1) for o in oh], axis=-1)
    n_blocks = -(-(T * TOP_K + N_EXPERTS * (ROW_TILE - 1)) // ROW_TILE)
    blk_start = jnp.arange(n_blocks)[:, None] * ROW_TILE
    blk_e = jnp.minimum(jnp.sum(pends[None, :] <= blk_start, axis=-1), N_EXPERTS - 1).astype(jnp.int32)
    return top_w, dest.astype(jnp.int32), blk_e, n_blocks * ROW_TILE


def _moe_ffn(x, gain, shift, scale, gate2, w_router, b_router, w_gu, w_dn):
    B, S, D = x.shape
    h = _modulate(_rms_norm(x, gain), shift, scale)
    logits = (h.reshape(B * S, D) @ w_router).astype(jnp.float32) + b_router
    top_w, dest, blk_e, n_rows = _route(logits)
    rows = _moe_dispatch(x, gain, shift, scale, dest, n_rows)
    y = _grouped_swiglu(rows, blk_e, w_gu.astype(jnp.bfloat16), w_dn.astype(jnp.bfloat16),
                        ff_tile=FF_TILE_EXPERT)
    return _moe_combine(x, top_w, gate2, y, dest)


def kernel(x, c, ada_w, ada_b, norm_g, pool_w, pool_scale, q_w, q_gain, o_w, kv_ada_w, kv_ada_b, kv_norm_g, kv_w, cmp_pe_k, cmp_pe_v, cmp_k_w1, cmp_k_w2, cmp_v_w1, cmp_v_w2, k_gain, rel_bias, ffn_gu, ffn_dn, router_w, router_b, exp_gu, exp_dn):
    kv = None
    for l in range(DEPTH):
        sh1, sc1, g1, sh2, sc2, g2 = jnp.split(jax.nn.silu(c) @ ada_w[l] + ada_b[l], 6, axis=-1)
        if l >= N_A_LAYERS and kv is None:
            kv = _shared_kv(x, c, kv_ada_w, kv_ada_b, kv_norm_g, kv_w, cmp_pe_k, cmp_pe_v,
                            cmp_k_w1, cmp_k_w2, cmp_v_w1, cmp_v_w2, k_gain)
        h = _modulate(_rms_norm(x, norm_g[l, 0]), sh1, sc1)
        if l < N_A_LAYERS:
            mix = _pool_mixer(h, pool_w[l], pool_scale[l])
        else:
            j = l - N_A_LAYERS
            mix = _nsa_mixer(h, kv, q_w[j], q_gain[j], o_w[j], rel_bias)
        x = x + g1[:, None, :] * mix
        if l % 2 == 0:
            x = _dense_ffn(x, norm_g[l, 1], sh2, sc2, g2, ffn_gu[l // 2], ffn_dn[l // 2])
        else:
            x = _moe_ffn(x, norm_g[l, 1], sh2, sc2, g2, router_w[l // 2], router_b[l // 2],
                         exp_gu[l // 2], exp_dn[l // 2])
    return x
```

```python
import functools
import math

import jax
import jax.numpy as jnp
import numpy as np
from jax import lax
from jax.experimental import pallas as pl
from jax.experimental.pallas import tpu as pltpu

D_MODEL = 1024
DEPTH = 2
N_A_LAYERS = DEPTH // 2
POOL_WINDOWS = (2, 4, 8, 16)
N_POOL_GROUPS = len(POOL_WINDOWS)
POOL_GROUP_DIM = D_MODEL // N_POOL_GROUPS
HEAD_DIM = 64
N_HEADS = D_MODEL // HEAD_DIM
N_KV_GROUPS = 4
HEADS_PER_GROUP = N_HEADS // N_KV_GROUPS
L_CMP = 32
D_STRIDE = 16
L_SLC = 64
N_SEL = 16
WINDOW = 512
R_CMP = L_CMP // D_STRIDE
R_SLC = L_SLC // D_STRIDE
N_BUCKETS = 32
REL_EXACT = N_BUCKETS // 2
MAX_DISTANCE = 1024
N_EXPERTS = 8
TOP_K = 2
EPS = 1e-6
NEG_INF = -1e30
SEL_FORCE = 1e6
LOG2_E = math.log2(math.e)

V7X_VMEM_LIMIT_BYTES = 48 * 1024 * 1024
LANES = 128
ROW_TILE = 512
TOK_TILE = 256
FF_TILE_DENSE = 1408
FF_TILE_EXPERT = 896
POOL_TILE = 512
POOL_HALO = 16

QB = 128
KEY_TILE = 2 * QB
SEL_UNROLL = 4
QROWS = HEADS_PER_GROUP * QB
GROUP_W = HEADS_PER_GROUP * HEAD_DIM
NBLK = 128
CMP_PAD = 128
FAR_DIST = MAX_DISTANCE
N_TOEP = FAR_DIST // QB + 3
BAND_LEFT = CMP_PAD - 16
assert BAND_LEFT * D_STRIDE + (L_CMP - 1) - 2 * QB >= FAR_DIST


def _rms_norm(x, g):
    xf = x.astype(jnp.float32)
    y = xf * lax.rsqrt(jnp.mean(xf * xf, axis=-1, keepdims=True) + EPS)
    return (y * g.astype(jnp.float32)).astype(x.dtype)


def _modulate(h, shift, scale):
    return h * (1 + scale[:, None, :]) + shift[:, None, :]


def _rel_bucket(dist):
    d = jnp.maximum(dist, 0)
    ratio = jnp.maximum(d, REL_EXACT).astype(jnp.float32) / REL_EXACT
    large = REL_EXACT + (jnp.log(ratio) / math.log(MAX_DISTANCE / REL_EXACT)
                         * (N_BUCKETS - REL_EXACT)).astype(jnp.int32)
    return jnp.where(d < REL_EXACT, d, jnp.minimum(large, N_BUCKETS - 1))


def _norm_modulate(x, gain, shift, scale):
    y = x * lax.rsqrt(jnp.mean(x * x, axis=-1, keepdims=True) + EPS)
    return (y * gain) * (1.0 + scale) + shift


def _swiglu_step(xb_ref, wg_ref, wu_ref, wd_ref, acc_ref, j):
    @pl.when(j == 0)
    def _():
        acc_ref[...] = jnp.zeros_like(acc_ref)

    x = xb_ref[...]
    gate = jnp.dot(x, wg_ref[0], preferred_element_type=jnp.float32)
    up = jnp.dot(x, wu_ref[0], preferred_element_type=jnp.float32)
    act = (gate * jax.nn.sigmoid(gate) * up).astype(jnp.bfloat16)
    acc_ref[...] += jnp.dot(act, wd_ref[0], preferred_element_type=jnp.float32)


def _grouped_swiglu_body(blk_e_ref, x_ref, wg_ref, wu_ref, wd_ref, o_ref, acc_ref, xb_ref, *, n_ff_steps):
    del blk_e_ref
    j = pl.program_id(1)

    @pl.when(j == 0)
    def _():
        xb_ref[...] = x_ref[...].astype(jnp.bfloat16)

    _swiglu_step(xb_ref, wg_ref, wu_ref, wd_ref, acc_ref, j)

    @pl.when(j == n_ff_steps - 1)
    def _():
        o_ref[...] = acc_ref[...]


def _grouped_swiglu(x_rows, blk_e, w_gu, w_dn, *, ff_tile):
    n_rows, d = x_rows.shape
    d_ff = w_dn.shape[1]
    assert n_rows % ROW_TILE == 0 and d_ff % ff_tile == 0
    n_ff_steps = d_ff // ff_tile
    grid = (n_rows // ROW_TILE, n_ff_steps)
    return pl.pallas_call(
        functools.partial(_grouped_swiglu_body, n_ff_steps=n_ff_steps),
        grid_spec=pltpu.PrefetchScalarGridSpec(
            num_scalar_prefetch=1,
            grid=grid,
            in_specs=[
                pl.BlockSpec((ROW_TILE, d), lambda i, j, e: (i, 0)),
                pl.BlockSpec((1, d, ff_tile), lambda i, j, e: (e[i], 0, j)),
                pl.BlockSpec((1, d, ff_tile), lambda i, j, e: (e[i], 0, j + n_ff_steps)),
                pl.BlockSpec((1, ff_tile, d), lambda i, j, e: (e[i], j, 0)),
            ],
            out_specs=pl.BlockSpec((ROW_TILE, d), lambda i, j, e: (i, 0)),
            scratch_shapes=[pltpu.VMEM((ROW_TILE, d), jnp.float32), pltpu.VMEM((ROW_TILE, d), jnp.bfloat16)],
        ),
        out_shape=jax.ShapeDtypeStruct((n_rows, d), jnp.float32),
        compiler_params=pltpu.CompilerParams(
            dimension_semantics=("arbitrary", "arbitrary"),
            vmem_limit_bytes=V7X_VMEM_LIMIT_BYTES,
        ),
        name="grouped_swiglu",
    )(blk_e, x_rows, w_gu, w_gu, w_dn)


def _dense_ffn_body(x_ref, gain_ref, shift_ref, scale_ref, gate2_ref, wg_ref, wu_ref, wd_ref, o_ref,
                    acc_ref, xb_ref, *, n_ff_steps):
    j = pl.program_id(1)

    @pl.when(j == 0)
    def _():
        xb_ref[...] = _norm_modulate(x_ref[...], gain_ref[...], shift_ref[0], scale_ref[0]).astype(jnp.bfloat16)

    _swiglu_step(xb_ref, wg_ref, wu_ref, wd_ref, acc_ref, j)

    @pl.when(j == n_ff_steps - 1)
    def _():
        o_ref[...] = x_ref[...] + gate2_ref[0] * acc_ref[...]


def _dense_ffn(x, gain, shift, scale, gate2, w_gu, w_dn):
    B, S, D = x.shape
    d_ff = w_dn.shape[0]
    n_ff_steps = d_ff // FF_TILE_DENSE
    tiles_per_batch = S // ROW_TILE
    per_batch = pl.BlockSpec((1, 1, D), lambda i, j: (i // tiles_per_batch, 0, 0))
    out = pl.pallas_call(
        functools.partial(_dense_ffn_body, n_ff_steps=n_ff_steps),
        grid=(B * S // ROW_TILE, n_ff_steps),
        in_specs=[
            pl.BlockSpec((ROW_TILE, D), lambda i, j: (i, 0)),
            pl.BlockSpec((1, D), lambda i, j: (0, 0)),
            per_batch, per_batch, per_batch,
            pl.BlockSpec((1, D, FF_TILE_DENSE), lambda i, j: (0, 0, j)),
            pl.BlockSpec((1, D, FF_TILE_DENSE), lambda i, j: (0, 0, j + n_ff_steps)),
            pl.BlockSpec((1, FF_TILE_DENSE, D), lambda i, j: (0, j, 0)),
        ],
        out_specs=pl.BlockSpec((ROW_TILE, D), lambda i, j: (i, 0)),
        out_shape=jax.ShapeDtypeStruct((B * S, D), jnp.float32),
        scratch_shapes=[pltpu.VMEM((ROW_TILE, D), jnp.float32), pltpu.VMEM((ROW_TILE, D), jnp.bfloat16)],
        compiler_params=pltpu.CompilerParams(
            dimension_semantics=("arbitrary", "arbitrary"),
            vmem_limit_bytes=V7X_VMEM_LIMIT_BYTES,
        ),
        name="dense_ffn",
    )(x.reshape(B * S, D), gain.reshape(1, D), shift[:, None, :], scale[:, None, :], gate2[:, None, :],
      w_gu.astype(jnp.bfloat16)[None], w_gu.astype(jnp.bfloat16)[None], w_dn.astype(jnp.bfloat16)[None])
    return out.reshape(B, S, D)


def _row_copies_wait(src_ref, dst_ref, sem, n_rows):
    pltpu.make_async_copy(src_ref.at[pl.ds(0, n_rows)], dst_ref.at[pl.ds(0, n_rows)], sem).wait()


def _dispatch_body(dest_ref, x_ref, gain_ref, shift_ref, scale_ref, rows_in_ref, rows_ref, h_ref, sem):
    del rows_in_ref
    h_ref[...] = _norm_modulate(x_ref[...], gain_ref[...], shift_ref[0], scale_ref[0])

    def issue(r, carry):
        for k in range(TOP_K):
            pltpu.make_async_copy(h_ref.at[pl.ds(r, 1)], rows_ref.at[pl.ds(dest_ref[TOP_K * r + k], 1)],
                                  sem).start()
        return carry

    lax.fori_loop(0, TOK_TILE, issue, 0)
    for k in range(TOP_K):
        _row_copies_wait(h_ref, rows_ref, sem, TOK_TILE)


def _moe_dispatch(x, gain, shift, scale, dest, n_rows):
    B, S, D = x.shape
    T = B * S
    tiles_per_batch = S // TOK_TILE
    per_batch = pl.BlockSpec((1, 1, D), lambda i: (i // tiles_per_batch, 0, 0))
    return pl.pallas_call(
        _dispatch_body,
        grid=(T // TOK_TILE,),
        in_specs=[
            pl.BlockSpec((TOP_K * TOK_TILE,), lambda i: (i,), memory_space=pltpu.SMEM),
            pl.BlockSpec((TOK_TILE, D), lambda i: (i, 0)),
            pl.BlockSpec((1, D), lambda i: (0, 0)),
            per_batch, per_batch,
            pl.BlockSpec(memory_space=pl.ANY),
        ],
        out_specs=pl.BlockSpec(memory_space=pl.ANY),
        out_shape=jax.ShapeDtypeStruct((n_rows, D), jnp.float32),
        scratch_shapes=[pltpu.VMEM((TOK_TILE, D), jnp.float32), pltpu.SemaphoreType.DMA(())],
        input_output_aliases={5: 0},
        compiler_params=pltpu.CompilerParams(dimension_semantics=("arbitrary",)),
        name="moe_dispatch",
    )(dest.reshape(T * TOP_K), x.reshape(T, D), gain.reshape(1, D), shift[:, None, :], scale[:, None, :],
      jnp.zeros((n_rows, D), jnp.float32))


def _combine_body(dest_ref, x_ref, w_ref, gate2_ref, y_ref, o_ref, buf_ref, sem):
    def issue(r, carry):
        for k in range(TOP_K):
            pltpu.make_async_copy(y_ref.at[pl.ds(dest_ref[TOP_K * r + k], 1)], buf_ref.at[k, pl.ds(r, 1)],
                                  sem).start()
        return carry

    lax.fori_loop(0, TOK_TILE, issue, 0)
    for k in range(TOP_K):
        _row_copies_wait(y_ref, buf_ref.at[k], sem, TOK_TILE)
    w = w_ref[...]
    f = w[:, 0:1] * buf_ref[0] + w[:, 1:2] * buf_ref[1]
    o_ref[...] = x_ref[...] + gate2_ref[0] * f


def _moe_combine(x, top_w, gate2, y, dest):
    B, S, D = x.shape
    T = B * S
    tiles_per_batch = S // TOK_TILE
    out = pl.pallas_call(
        _combine_body,
        grid=(T // TOK_TILE,),
        in_specs=[
            pl.BlockSpec((TOP_K * TOK_TILE,), lambda i: (i,), memory_space=pltpu.SMEM),
            pl.BlockSpec((TOK_TILE, D), lambda i: (i, 0)),
            pl.BlockSpec((TOK_TILE, TOP_K), lambda i: (i, 0)),
            pl.BlockSpec((1, 1, D), lambda i: (i // tiles_per_batch, 0, 0)),
            pl.BlockSpec(memory_space=pl.ANY),
        ],
        out_specs=pl.BlockSpec((TOK_TILE, D), lambda i: (i, 0)),
        out_shape=jax.ShapeDtypeStruct((T, D), jnp.float32),
        scratch_shapes=[pltpu.VMEM((TOP_K, TOK_TILE, D), jnp.float32), pltpu.SemaphoreType.DMA(())],
        compiler_params=pltpu.CompilerParams(dimension_semantics=("arbitrary",)),
        name="moe_combine",
    )(dest.reshape(T * TOP_K), x.reshape(T, D), top_w, gate2[:, None, :], y)
    return out.reshape(B, S, D)


def _bias_tables(rel_bias):
    x0 = (N_TOEP - 1) * QB
    width = x0 + 2 * QB
    period = width + QB
    n = np.arange(period)
    n = np.where(n < width, n, n - period)
    by_dist = rel_bias.astype(jnp.float32)[_rel_bucket(jnp.asarray(np.maximum(x0 - n, 0)))].T
    strip = jnp.tile(by_dist, (1, QB))[:, :QB * (period - 1)].reshape(N_HEADS, QB, period - 1)[:, :, :width]
    far = rel_bias.astype(jnp.float32)[N_BUCKETS - 1]

    i = np.arange(QB)[:, None]
    toep = jnp.stack([strip[:, :, x0 - QB * m:x0 - QB * m + QB] for m in range(-1, N_TOEP - 1)], axis=1)
    d_toep = QB * np.arange(-1, N_TOEP - 1)[:, None, None] + i[None] - np.arange(QB)[None, None, :]
    toep = jnp.where(d_toep >= 0, toep * LOG2_E, NEG_INF)
    toep = toep.reshape(N_KV_GROUPS, HEADS_PER_GROUP, N_TOEP, QB, QB).transpose(0, 2, 1, 3, 4)
    d_win = WINDOW + i - np.arange(WINDOW + QB)[None, :]
    win = jnp.where((d_win >= 0) & (d_win < WINDOW), strip[:, :, x0 - WINDOW:x0 + QB], NEG_INF)
    win = win.reshape(N_KV_GROUPS, HEADS_PER_GROUP, QB, WINDOW + QB)
    bands = []
    for par in range(2):
        off = x0 - QB * par - D_STRIDE * BAND_LEFT + (L_CMP - 1)
        c_first = -(off // D_STRIDE)
        cols = strip[:, :, D_STRIDE * c_first + off::D_STRIDE][:, :, :128 - c_first]
        left = jnp.broadcast_to(far[:, None, None], (N_HEADS, QB, c_first))
        right = jnp.zeros((N_HEADS, QB, 128 - c_first - cols.shape[2]), jnp.float32)
        d_band = QB * par + i - D_STRIDE * (np.arange(128)[None, :] - BAND_LEFT) - (L_CMP - 1)
        assert (d_band[:, 128 - right.shape[2]:] < 0).all()
        bands.append(jnp.where(d_band >= 0, jnp.concatenate([left, cols, right], axis=-1), NEG_INF))
    band = jnp.stack(bands).reshape(2, N_KV_GROUPS, HEADS_PER_GROUP, QB, 128)
    return toep, win, band, far


def _slc_map_matrix(n_cmp_cols):
    w = np.zeros((n_cmp_cols, NBLK), np.float32)
    for jb in range(NBLK):
        for mm in range(R_SLC):
            for nn in range(R_CMP):
                k = R_SLC * jb + mm - nn
                if 0 <= k < n_cmp_cols:
                    w[k, jb] += 1.0
    return w


def _dot_nt(a, b):
    return lax.dot_general(a, b, (((1,), (1,)), ((), ())), preferred_element_type=jnp.float32)


def _cmp_select_body(far_ref, q_ref, gate_ref, kc_ref, vc_ref, band_ref, wmap_ref, oc_ref, sel_ref,
                     *, n_far):
    qi = pl.program_id(1)
    par = qi % 2
    band0 = pl.multiple_of(16 * (qi // 2 + 1), 16)
    first_band_blk = band0 - CMP_PAD

    col = lax.broadcasted_iota(jnp.int32, (QB, 128), 1)
    row = lax.broadcasted_iota(jnp.int32, (QB, 128), 0)
    neg_pad = jnp.where(first_band_blk + col >= 0, 0.0, NEG_INF)
    far_col = lax.broadcasted_iota(jnp.int32, (1, n_far), 1)
    neg_far = jnp.where(far_col < first_band_blk, 0.0, NEG_INF)

    t = qi * QB + row
    cur = t // L_SLC
    forced = (col == 0) | (col == cur) | (col == cur - 1)
    valid = col * L_SLC <= t

    q_all = q_ref[0]
    wmap_all = jnp.concatenate([wmap_ref[CMP_PAD:CMP_PAD + n_far, :], wmap_ref[pl.ds(band0, 128), :]], axis=0)
    colf = col.astype(jnp.float32)
    oc_heads = []
    for g in range(N_KV_GROUPS):
        q4 = jnp.concatenate(
            [q_all[:, (g * HEADS_PER_GROUP + h) * HEAD_DIM:(g * HEADS_PER_GROUP + h + 1) * HEAD_DIM]
             for h in range(HEADS_PER_GROUP)], axis=0)
        k_far = kc_ref[0, g, CMP_PAD:CMP_PAD + n_far, :]
        k_band = kc_ref[0, g, pl.ds(band0, 128), :]
        v_far = vc_ref[0, g, CMP_PAD:CMP_PAD + n_far, :]
        v_band = vc_ref[0, g, pl.ds(band0, 128), :]
        s_far = _dot_nt(q4, k_far).reshape(HEADS_PER_GROUP, QB, n_far)
        s_band = _dot_nt(q4, k_band).reshape(HEADS_PER_GROUP, QB, 128)
        v_all = jnp.concatenate([v_far, v_band], axis=0)
        imp = jnp.zeros((QB, n_far + 128), jnp.float32)
        for h in range(HEADS_PER_GROUP):
            hh = g * HEADS_PER_GROUP + h
            s = jnp.concatenate([s_far[h] + far_ref[hh] + neg_far,
                                 s_band[h] + band_ref[par, g, h] + neg_pad], axis=-1)
            m = jnp.max(s, axis=-1, keepdims=True)
            p = jnp.where(m > 0.5 * NEG_INF, jnp.exp(s - m), 0.0)
            l = jnp.sum(p, axis=-1, keepdims=True)
            p = p * jnp.where(l > 0.0, 1.0 / l, 0.0)
            imp = imp + p
            o = jnp.dot(p.astype(jnp.bfloat16), v_all, preferred_element_type=jnp.float32)
            oc_heads.append(o * gate_ref[0, :, 3 * hh:3 * hh + 1])
        p_slc = jnp.zeros((QB, NBLK), jnp.float32)
        rest = imp
        for _ in range(3):
            term = rest.astype(jnp.bfloat16)
            p_slc = p_slc + jnp.dot(term, wmap_all, preferred_element_type=jnp.float32)
            rest = rest - term.astype(jnp.float32)
        score = jnp.where(forced, SEL_FORCE, jnp.where(valid, p_slc, -SEL_FORCE))
        for _ in range(N_SEL):
            best = jnp.max(score, axis=-1, keepdims=True)
            first = jnp.min(jnp.where(score == best, colf, float(NBLK)), axis=-1, keepdims=True)
            score = jnp.where(colf == first, -jnp.inf, score)
        sel_ref[0, g] = jnp.where(score == -jnp.inf, 0.0, NEG_INF).astype(jnp.bfloat16)
    oc_ref[0] = jnp.concatenate(oc_heads, axis=-1)


def _cmp_select(q, gates, kc_pad, vc_pad, band, far, wmap):
    B, S, _ = q.shape
    n_far = S // D_STRIDE
    n_pad = kc_pad.shape[2]
    grid = (B, S // QB)
    return pl.pallas_call(
        functools.partial(_cmp_select_body, n_far=n_far),
        grid=grid,
        in_specs=[
            pl.BlockSpec(memory_space=pltpu.SMEM),
            pl.BlockSpec((1, QB, N_HEADS * HEAD_DIM), lambda b, i: (b, i, 0)),
            pl.BlockSpec((1, QB, LANES), lambda b, i: (b, i, 0)),
            pl.BlockSpec((1, N_KV_GROUPS, n_pad, HEAD_DIM), lambda b, i: (b, 0, 0, 0)),
            pl.BlockSpec((1, N_KV_GROUPS, n_pad, HEAD_DIM), lambda b, i: (b, 0, 0, 0)),
            pl.BlockSpec((2, N_KV_GROUPS, HEADS_PER_GROUP, QB, 128), lambda b, i: (0, 0, 0, 0, 0)),
            pl.BlockSpec((n_pad, NBLK), lambda b, i: (0, 0)),
        ],
        out_specs=[
            pl.BlockSpec((1, QB, N_HEADS * HEAD_DIM), lambda b, i: (b, i, 0)),
            pl.BlockSpec((1, N_KV_GROUPS, QB, NBLK), lambda b, i: (b, 0, i, 0)),
        ],
        out_shape=[
            jax.ShapeDtypeStruct((B, S, N_HEADS * HEAD_DIM), jnp.float32),
            jax.ShapeDtypeStruct((B, N_KV_GROUPS, S, NBLK), jnp.bfloat16),
        ],
        compiler_params=pltpu.CompilerParams(
            dimension_semantics=("arbitrary", "arbitrary"),
            vmem_limit_bytes=V7X_VMEM_LIMIT_BYTES,
        ),
        name="nsa_cmp_select",
    )(far, q, gates, kc_pad, vc_pad, band, wmap)


def _win_sel_body(q_ref, gate_ref, sel_ref, oc_ref, ks_ref, vs_ref, kw_ref, vw_ref, toep_ref, win_ref,
                  out_ref, s0_scr, s1_scr, p0_scr, p1_scr, m_scr, acc_scr):
    qi = pl.program_id(2)
    q0 = pl.multiple_of(qi * QB, QB)
    qg = q_ref[0]
    q_heads = [qg[:, h * HEAD_DIM:(h + 1) * HEAD_DIM] for h in range(HEADS_PER_GROUP)]
    q4 = jnp.concatenate(q_heads, axis=0)

    kw = kw_ref[0, 0, :, pl.ds(q0, WINDOW + QB)]
    s_w = jnp.dot(q4, kw, preferred_element_type=jnp.float32)
    wcol = lax.broadcasted_iota(jnp.int32, (1, WINDOW + QB), 1)
    neg_left = jnp.where(q0 + wcol >= WINDOW, 0.0, NEG_INF)
    s_w = s_w.reshape(HEADS_PER_GROUP, QB, WINDOW + QB) + win_ref[0] + neg_left
    m_w = jnp.max(s_w, axis=-1, keepdims=True)
    p_w = jnp.exp(s_w - m_w)
    o_w = jnp.dot(p_w.reshape(QROWS, WINDOW + QB).astype(jnp.bfloat16),
                  vw_ref[0, 0, pl.ds(q0, WINDOW + QB), :], preferred_element_type=jnp.float32)
    o_w = o_w.reshape(HEADS_PER_GROUP, QB, LANES)
    o_w = o_w[:, :, :HEAD_DIM] / o_w[:, :, HEAD_DIM:HEAD_DIM + 1]

    sel = sel_ref[0, 0]
    qa = jnp.concatenate([jnp.concatenate([sel] * HEADS_PER_GROUP, axis=0), q4], axis=-1)
    c_diag = qi // 2

    def scores(c):
        col = pl.multiple_of(c * KEY_TILE, KEY_TILE)
        s = jnp.dot(qa, ks_ref[0, 0, :, pl.ds(col, KEY_TILE)], preferred_element_type=jnp.float32)
        mm = qi - 2 * c
        bias = jnp.concatenate([toep_ref[0, jnp.clip(mm + 1, 0, N_TOEP - 1)],
                                toep_ref[0, jnp.clip(mm, 0, N_TOEP - 1)]], axis=-1)
        return s.reshape(HEADS_PER_GROUP, QB, KEY_TILE) + bias

    s_slots = (s0_scr, s1_scr)
    p_slots = (p0_scr, p1_scr)
    m_scr[...] = jnp.full(m_scr.shape, NEG_INF, jnp.float32)
    acc_scr[...] = jnp.zeros_like(acc_scr)
    p_slots[0][...] = jnp.zeros_like(p0_scr)
    s_slots[0][...] = scores(0)

    def weighted_values(p, col):
        return jnp.dot(p, vs_ref[0, 0, pl.ds(pl.multiple_of(col, KEY_TILE), KEY_TILE), :],
                       preferred_element_type=jnp.float32)

    def half_step(c, col_prev, cur):
        pv = weighted_values(p_slots[cur][...], col_prev)
        s_slots[1 - cur][...] = scores(c + 1)
        for h in range(HEADS_PER_GROUP):
            s = s_slots[cur][h]
            m_old = m_scr[h]
            m_new = jnp.maximum(m_old, jnp.max(s, axis=-1, keepdims=True))
            alpha = jnp.exp2(m_old - m_new)
            p = jnp.exp2(s - jnp.concatenate([m_new] * (KEY_TILE // LANES), axis=-1))
            p_slots[1 - cur][h * QB:(h + 1) * QB, :] = p.astype(jnp.bfloat16)
            m_scr[h] = m_new
            acc_scr[h * QB:(h + 1) * QB, :] = alpha * (acc_scr[h * QB:(h + 1) * QB, :]
                                                      + pv[h * QB:(h + 1) * QB, :])
        return c * KEY_TILE

    def unrolled_steps(j, col_prev):
        for u in range(SEL_UNROLL):
            col_prev = half_step(SEL_UNROLL * j + u, col_prev, u % 2)
        return col_prev

    col_last = lax.fori_loop(0, (c_diag + SEL_UNROLL) // SEL_UNROLL, unrolled_steps, 0)
    acc_s = (acc_scr[...] + weighted_values(p_slots[0][...], col_last)).reshape(HEADS_PER_GROUP, QB, LANES)
    o_s = acc_s[:, :, :HEAD_DIM] / acc_s[:, :, HEAD_DIM:HEAD_DIM + 1]

    outs = []
    for h in range(HEADS_PER_GROUP):
        g_s = gate_ref[0, 0, :, 3 * h + 1:3 * h + 2]
        g_w = gate_ref[0, 0, :, 3 * h + 2:3 * h + 3]
        outs.append(g_s * o_s[h] + g_w * o_w[h])
    out_ref[0] = oc_ref[0] + jnp.concatenate(outs, axis=-1)


def _win_sel(q, gates, selneg, oc, ks_aug, vs, kw_t, vw_pad, toep, win):
    B, S, _ = q.shape
    gw = HEADS_PER_GROUP * HEAD_DIM
    grid = (B, N_KV_GROUPS, S // QB)
    return pl.pallas_call(
        _win_sel_body,
        grid=grid,
        in_specs=[
            pl.BlockSpec((1, QB, gw), lambda b, g, i: (b, i, g)),
            pl.BlockSpec((1, 1, QB, 3 * HEADS_PER_GROUP), lambda b, g, i: (b, g, i, 0)),
            pl.BlockSpec((1, 1, QB, NBLK), lambda b, g, i: (b, g, i, 0)),
            pl.BlockSpec((1, QB, gw), lambda b, g, i: (b, i, g)),
            pl.BlockSpec((1, 1, NBLK + HEAD_DIM, ks_aug.shape[3]), lambda b, g, i: (b, g, 0, 0)),
            pl.BlockSpec((1, 1, vs.shape[2], LANES), lambda b, g, i: (b, g, 0, 0)),
            pl.BlockSpec((1, 1, HEAD_DIM, S + WINDOW), lambda b, g, i: (b, g, 0, 0)),
            pl.BlockSpec((1, 1, S + WINDOW, LANES), lambda b, g, i: (b, g, 0, 0)),
            pl.BlockSpec((1, N_TOEP, HEADS_PER_GROUP, QB, QB), lambda b, g, i: (g, 0, 0, 0, 0)),
            pl.BlockSpec((1, HEADS_PER_GROUP, QB, WINDOW + QB), lambda b, g, i: (g, 0, 0, 0)),
        ],
        out_specs=pl.BlockSpec((1, QB, gw), lambda b, g, i: (b, i, g)),
        out_shape=jax.ShapeDtypeStruct((B, S, N_HEADS * HEAD_DIM), jnp.float32),
        scratch_shapes=[
            pltpu.VMEM((HEADS_PER_GROUP, QB, KEY_TILE), jnp.float32),
            pltpu.VMEM((HEADS_PER_GROUP, QB, KEY_TILE), jnp.float32),
            pltpu.VMEM((QROWS, KEY_TILE), jnp.bfloat16),
            pltpu.VMEM((QROWS, KEY_TILE), jnp.bfloat16),
            pltpu.VMEM((HEADS_PER_GROUP, QB, LANES), jnp.float32),
            pltpu.VMEM((QROWS, LANES), jnp.float32),
        ],
        compiler_params=pltpu.CompilerParams(
            dimension_semantics=("arbitrary", "arbitrary", "arbitrary"),
            vmem_limit_bytes=V7X_VMEM_LIMIT_BYTES,
        ),
        name="nsa_win_sel",
    )(q, gates, selneg, oc, ks_aug, vs, kw_t, vw_pad, toep, win)


def _nsa_attention(q, gates, kc, vc, kvsw, rel_bias):
    B, S, _ = q.shape
    assert S % KEY_TILE == 0 and S // L_SLC <= NBLK
    bf = jnp.bfloat16
    n_far = S // D_STRIDE
    toep, win, band, far = _bias_tables(rel_bias)
    cpad = ((0, 0), (0, 0), (CMP_PAD, n_far - kc.shape[2] + CMP_PAD), (0, 0))
    kc_pad = jnp.pad(kc, cpad).astype(bf)
    vc_pad = jnp.pad(vc, cpad).astype(bf)
    wmap = jnp.asarray(np.pad(_slc_map_matrix(n_far), ((CMP_PAD, CMP_PAD), (0, 0))), bf)
    oc, selneg = _cmp_select(q, gates, kc_pad, vc_pad, band, far, wmap)
    ks, vs, kw, vw = (kvsw.reshape(B, S, 4, N_KV_GROUPS, HEAD_DIM)[:, :, n] for n in range(4))
    blk_onehot = jnp.asarray((np.arange(S)[None, :] // L_SLC == np.arange(NBLK)[:, None]), bf)
    ks_aug = jnp.concatenate(
        [jnp.broadcast_to(blk_onehot, (B, N_KV_GROUPS, NBLK, S)), ks.transpose(0, 2, 3, 1)], axis=2)
    tail = SEL_UNROLL * KEY_TILE
    ks_aug = jnp.pad(ks_aug, ((0, 0), (0, 0), (0, 0), (0, tail)))
    kw_t = jnp.pad(kw.transpose(0, 2, 3, 1), ((0, 0), (0, 0), (0, 0), (WINDOW, 0)))

    def with_ones(v):
        v = v.transpose(0, 2, 1, 3)
        ones = jnp.ones(v.shape[:-1] + (1,), v.dtype)
        return jnp.pad(jnp.concatenate([v, ones], axis=-1), ((0, 0),) * 3 + ((0, LANES - HEAD_DIM - 1),))

    vw_aug = jnp.pad(with_ones(vw), ((0, 0), (0, 0), (WINDOW, 0), (0, 0)))
    vs_aug = jnp.pad(with_ones(vs), ((0, 0), (0, 0), (0, tail), (0, 0)))
    gates_g = gates[:, :, :3 * N_HEADS].reshape(B, S, N_KV_GROUPS, 3 * HEADS_PER_GROUP).swapaxes(1, 2)
    return _win_sel(q, gates_g, selneg, oc, ks_aug, vs_aug, kw_t, vw_aug, toep, win)


def _pool_layer_body(x_ref, halo_ref, gain_ref, shift_ref, scale_ref, gate_ref, pscale_ref, w_ref, o_ref):
    i = pl.program_id(1)
    x = x_ref[0]
    gain, shift, scale = gain_ref[...], shift_ref[0], scale_ref[0]
    h = _norm_modulate(x, gain, shift, scale)
    halo = _norm_modulate(halo_ref[0], gain, shift, scale) * (i > 0).astype(jnp.float32)
    hx = jnp.concatenate([halo, h], axis=0)
    t = i * POOL_TILE + lax.broadcasted_iota(jnp.int32, (POOL_TILE, 1), 0)
    outs = []
    run = hx
    width = 1
    for g, w in enumerate(POOL_WINDOWS):
        while width < w:
            run = run + pltpu.roll(run, width, axis=0)
            width *= 2
        lanes = slice(0, POOL_GROUP_DIM)
        cnt = jnp.minimum(t + 1, w).astype(jnp.float32)
        mix = run[POOL_HALO:, lanes] / cnt - h[:, g * POOL_GROUP_DIM:(g + 1) * POOL_GROUP_DIM]
        outs.append(jnp.dot(mix.astype(jnp.bfloat16), w_ref[g], preferred_element_type=jnp.float32))
        run = run[:, POOL_GROUP_DIM:]
    y = jnp.concatenate(outs, axis=-1) * pscale_ref[...]
    o_ref[0] = x + gate_ref[0] * y


def _pool_layer(x, gain, shift, scale, gate1, w_grp, pool_scale):
    B, S, D = x.shape
    assert S % POOL_TILE == 0 and POOL_HALO >= max(POOL_WINDOWS) - 1
    per_batch = pl.BlockSpec((1, 1, D), lambda b, i: (b, 0, 0))
    halo_blocks = POOL_TILE // POOL_HALO
    return pl.pallas_call(
        _pool_layer_body,
        grid=(B, S // POOL_TILE),
        in_specs=[
            pl.BlockSpec((1, POOL_TILE, D), lambda b, i: (b, i, 0)),
            pl.BlockSpec((1, POOL_HALO, D), lambda b, i: (b, jnp.maximum(i * halo_blocks - 1, 0), 0)),
            pl.BlockSpec((1, D), lambda b, i: (0, 0)),
            per_batch, per_batch, per_batch,
            pl.BlockSpec((1, D), lambda b, i: (0, 0)),
            pl.BlockSpec((N_POOL_GROUPS, POOL_GROUP_DIM, POOL_GROUP_DIM), lambda b, i: (0, 0, 0)),
        ],
        out_specs=pl.BlockSpec((1, POOL_TILE, D), lambda b, i: (b, i, 0)),
        out_shape=jax.ShapeDtypeStruct((B, S, D), jnp.float32),
        compiler_params=pltpu.CompilerParams(
            dimension_semantics=("arbitrary", "arbitrary"),
            vmem_limit_bytes=V7X_VMEM_LIMIT_BYTES,
        ),
        name="pool_layer",
    )(x, x, gain.reshape(1, D), shift[:, None, :], scale[:, None, :], gate1[:, None, :],
      pool_scale.reshape(1, D), w_grp.astype(jnp.bfloat16))


def _head_norm(v, seg_ref, gain):
    sq = v * v
    hi = sq.astype(jnp.bfloat16)
    lo = (sq - hi.astype(jnp.float32)).astype(jnp.bfloat16)
    ss = (jnp.dot(hi, seg_ref[...], preferred_element_type=jnp.float32)
          + jnp.dot(lo, seg_ref[...], preferred_element_type=jnp.float32))
    return v * lax.rsqrt(ss * (1.0 / HEAD_DIM) + EPS) * gain


def _qkv_proj_body(x_ref, gq_ref, shq_ref, scq_ref, gkv_ref, shkv_ref, sckv_ref, wq_ref, wg_ref, wkv_ref,
                   seg_ref, qgain_ref, kgain_ref, q_ref, gates_ref, kvc_ref, kvsw_ref):
    x = x_ref[...]
    xhat = x * lax.rsqrt(jnp.mean(x * x, axis=-1, keepdims=True) + EPS)
    hq = ((xhat * gq_ref[...]) * (1.0 + scq_ref[0]) + shq_ref[0]).astype(jnp.bfloat16)
    hkv = ((xhat * gkv_ref[...]) * (1.0 + sckv_ref[0]) + shkv_ref[0]).astype(jnp.bfloat16)
    pq = jnp.dot(hq, wq_ref[...], preferred_element_type=jnp.float32)
    gates_ref[...] = jax.nn.sigmoid(jnp.dot(hq, wg_ref[...], preferred_element_type=jnp.float32))
    pkv = jnp.dot(hkv, wkv_ref[...], preferred_element_type=jnp.float32)
    for g in range(N_KV_GROUPS):
        lanes = slice(g * GROUP_W, (g + 1) * GROUP_W)
        q_ref[:, lanes] = _head_norm(pq[:, lanes], seg_ref, qgain_ref[...]).astype(jnp.bfloat16)
    kvc_ref[...] = pkv[:, :2 * GROUP_W]
    k_s = _head_norm(pkv[:, 2 * GROUP_W:3 * GROUP_W], seg_ref, kgain_ref[0:1, :])
    k_w = _head_norm(pkv[:, 4 * GROUP_W:5 * GROUP_W], seg_ref, kgain_ref[1:2, :])
    kvsw_ref[...] = jnp.concatenate([k_s, pkv[:, 3 * GROUP_W:4 * GROUP_W], k_w, pkv[:, 5 * GROUP_W:]],
                                    axis=-1).astype(jnp.bfloat16)


def _qkv_proj(x, gain_q, shift_q, scale_q, gain_kv, shift_kv, scale_kv, w_qg, kv_w, q_gain, k_gain):
    B, S, D = x.shape
    T = B * S
    qd = N_HEADS * HEAD_DIM
    bf = jnp.bfloat16
    tiles_per_batch = S // ROW_TILE
    per_batch = pl.BlockSpec((1, 1, D), lambda i: (i // tiles_per_batch, 0, 0))
    const = lambda shape: pl.BlockSpec(shape, lambda i: (0,) * len(shape))
    seg = jnp.asarray(np.kron(np.eye(HEADS_PER_GROUP), np.ones((HEAD_DIM, HEAD_DIM))), bf)
    w_gate = jnp.pad(w_qg[:, qd:], ((0, 0), (0, LANES - 3 * N_HEADS))).astype(bf)
    qgain = jnp.tile(q_gain * HEAD_DIM ** -0.5, HEADS_PER_GROUP).reshape(1, GROUP_W)
    kgain = jnp.stack([jnp.tile(k_gain[1] * LOG2_E, N_KV_GROUPS), jnp.tile(k_gain[2], N_KV_GROUPS)])
    rows = lambda width: pl.BlockSpec((ROW_TILE, width), lambda i: (i, 0))
    return pl.pallas_call(
        _qkv_proj_body,
        grid=(T // ROW_TILE,),
        in_specs=[rows(D), const((1, D)), per_batch, per_batch, const((1, D)), per_batch, per_batch,
                  const((D, qd)), const((D, LANES)), const((D, 6 * GROUP_W)), const((GROUP_W, GROUP_W)),
                  const((1, GROUP_W)), const((2, GROUP_W))],
        out_specs=[rows(qd), rows(LANES), rows(2 * GROUP_W), rows(4 * GROUP_W)],
        out_shape=[jax.ShapeDtypeStruct((T, qd), bf), jax.ShapeDtypeStruct((T, LANES), jnp.float32),
                   jax.ShapeDtypeStruct((T, 2 * GROUP_W), jnp.float32),
                   jax.ShapeDtypeStruct((T, 4 * GROUP_W), bf)],
        compiler_params=pltpu.CompilerParams(
            dimension_semantics=("arbitrary",), vmem_limit_bytes=V7X_VMEM_LIMIT_BYTES),
        name="qkv_proj",
    )(x.reshape(T, D), gain_q.reshape(1, D), shift_q[:, None, :], scale_q[:, None, :],
      gain_kv.reshape(1, D), shift_kv[:, None, :], scale_kv[:, None, :],
      w_qg[:, :qd].astype(bf), w_gate, kv_w.astype(bf), seg, qgain, kgain)


def _compressed_kv(kvc, B, S, cmp_pe_k, cmp_pe_v, cmp_k_w1, cmp_k_w2, cmp_v_w1, cmp_v_w2, k_gain):
    kv = kvc.reshape(B, S, 2, N_KV_GROUPS, HEAD_DIM).transpose(2, 0, 3, 1, 4)

    def compress(u, pe, w1, w2):
        r = u.reshape(B, N_KV_GROUPS, S // D_STRIDE, D_STRIDE, HEAD_DIM)
        nc = S // D_STRIDE - R_CMP + 1
        blocks = jnp.concatenate([r[:, :, i:i + nc] for i in range(R_CMP)], axis=3) + pe
        flat = blocks.reshape(B, N_KV_GROUPS, nc, L_CMP * HEAD_DIM)
        return jax.nn.gelu(flat @ w1) @ w2

    kc = _rms_norm(compress(kv[0], cmp_pe_k, cmp_k_w1, cmp_k_w2), k_gain[0])
    vc = compress(kv[1], cmp_pe_v, cmp_v_w1, cmp_v_w2)
    return kc, vc


def _oproj_router_body(a_ref, x_ref, gate_ref, wo_ref, gain_ref, shift_ref, scale_ref, wr_ref, br_ref,
                       o_ref, logit_ref):
    mix = jnp.dot(a_ref[...].astype(jnp.bfloat16), wo_ref[...], preferred_element_type=jnp.float32)
    x = x_ref[...] + gate_ref[0] * mix
    o_ref[...] = x
    h = _norm_modulate(x, gain_ref[...], shift_ref[0], scale_ref[0]).astype(jnp.bfloat16)
    logit_ref[...] = jnp.dot(h, wr_ref[...], preferred_element_type=jnp.float32) + br_ref[...]


def _oproj_router(attn, x, gate1, w_o, gain, shift, scale, w_router, b_router):
    B, S, D = x.shape
    T = B * S
    bf = jnp.bfloat16
    tiles_per_batch = S // ROW_TILE
    per_batch = pl.BlockSpec((1, 1, D), lambda i: (i // tiles_per_batch, 0, 0))
    const = lambda shape: pl.BlockSpec(shape, lambda i: (0,) * len(shape))
    rows = lambda width: pl.BlockSpec((ROW_TILE, width), lambda i: (i, 0))
    pad = ((0, 0), (0, LANES - N_EXPERTS))
    out, logits = pl.pallas_call(
        _oproj_router_body,
        grid=(T // ROW_TILE,),
        in_specs=[rows(D), rows(D), per_batch, const((D, D)), const((1, D)), per_batch, per_batch,
                  const((D, LANES)), const((1, LANES))],
        out_specs=[rows(D), rows(LANES)],
        out_shape=[jax.ShapeDtypeStruct((T, D), jnp.float32), jax.ShapeDtypeStruct((T, LANES), jnp.float32)],
        compiler_params=pltpu.CompilerParams(
            dimension_semantics=("arbitrary",), vmem_limit_bytes=V7X_VMEM_LIMIT_BYTES),
        name="oproj_router",
    )(attn.reshape(T, D), x.reshape(T, D), gate1[:, None, :], w_o.astype(bf), gain.reshape(1, D),
      shift[:, None, :], scale[:, None, :], jnp.pad(w_router, pad).astype(bf),
      jnp.pad(b_router.reshape(1, N_EXPERTS), pad))
    return out.reshape(B, S, D), logits[:, :N_EXPERTS]


def _route(logits):
    T = logits.shape[0]
    lane = jnp.arange(N_EXPERTS)[None, :]
    l0 = jnp.max(logits, axis=-1)
    e0 = jnp.argmax(logits, axis=-1)
    rest = jnp.where(lane == e0[:, None], -jnp.inf, logits)
    l1 = jnp.max(rest, axis=-1)
    e1 = jnp.argmax(rest, axis=-1)
    z = jnp.exp(l1 - l0)
    top_w = jnp.stack([1.0 / (1.0 + z), z / (1.0 + z)], axis=-1)
    oh = [(lane == e[:, None]).astype(jnp.int32) for e in (e0, e1)]
    cnt = oh[0] + oh[1]
    before = jnp.cumsum(cnt, axis=0) - cnt
    counts = before[-1] + cnt[-1]
    pcounts = (counts + ROW_TILE - 1) // ROW_TILE * ROW_TILE
    pends = jnp.cumsum(pcounts)
    pstarts = pends - pcounts
    dest = jnp.stack([jnp.sum((pstarts[None, :] + before) * o, axis=-1) for o in oh], axis=-1)
    n_blocks = -(-(T * TOP_K + N_EXPERTS * (ROW_TILE - 1)) // ROW_TILE)
    blk_start = jnp.arange(n_blocks)[:, None] * ROW_TILE
    blk_e = jnp.minimum(jnp.sum(pends[None, :] <= blk_start, axis=-1), N_EXPERTS - 1).astype(jnp.int32)
    return top_w, dest.astype(jnp.int32), blk_e, n_blocks * ROW_TILE


def _moe_ffn(x, logits, gain, shift, scale, gate2, w_gu, w_dn):
    top_w, dest, blk_e, n_rows = _route(logits)
    rows = _moe_dispatch(x, gain, shift, scale, dest, n_rows)
    y = _grouped_swiglu(rows, blk_e, w_gu.astype(jnp.bfloat16), w_dn.astype(jnp.bfloat16),
                        ff_tile=FF_TILE_EXPERT)
    return _moe_combine(x, top_w, gate2, y, dest)


def kernel(x, c, ada_w, ada_b, norm_g, pool_w, pool_scale, q_w, q_gain, o_w, kv_ada_w, kv_ada_b, kv_norm_g, kv_w, cmp_pe_k, cmp_pe_v, cmp_k_w1, cmp_k_w2, cmp_v_w1, cmp_v_w2, k_gain, rel_bias, ffn_gu, ffn_dn, router_w, router_b, exp_gu, exp_dn):
    assert DEPTH == 2 and N_A_LAYERS == 1
    B, S, D = x.shape
    silu_c = jax.nn.silu(c)
    sh1, sc1, g1, sh2, sc2, g2 = jnp.split(silu_c @ ada_w[0] + ada_b[0], 6, axis=-1)
    x = _pool_layer(x, norm_g[0, 0], sh1, sc1, g1, pool_w[0], pool_scale[0])
    x = _dense_ffn(x, norm_g[0, 1], sh2, sc2, g2, ffn_gu[0], ffn_dn[0])
    sh1, sc1, g1, sh2, sc2, g2 = jnp.split(silu_c @ ada_w[1] + ada_b[1], 6, axis=-1)
    sh_kv, sc_kv = jnp.split(silu_c @ kv_ada_w + kv_ada_b, 2, axis=-1)
    q, gates, kvc, kvsw = _qkv_proj(x, norm_g[1, 0], sh1, sc1, kv_norm_g, sh_kv, sc_kv, q_w[0], kv_w,
                                    q_gain[0], k_gain)
    kc, vc = _compressed_kv(kvc, B, S, cmp_pe_k, cmp_pe_v, cmp_k_w1, cmp_k_w2, cmp_v_w1, cmp_v_w2, k_gain)
    attn = _nsa_attention(q.reshape(B, S, -1), gates.reshape(B, S, -1), kc, vc, kvsw.reshape(B, S, -1),
                          rel_bias)
    x, logits = _oproj_router(attn, x, g1, o_w[0], norm_g[1, 1], sh2, sc2, router_w[0], router_b[0])
    return _moe_ffn(x, logits, norm_g[1, 1], sh2, sc2, g2, exp_gu[0], exp_dn[0])
```

```python
import functools
import math

import jax
import jax.numpy as jnp
import numpy as np
from jax import lax
from jax.experimental import pallas as pl
from jax.experimental.pallas import tpu as pltpu

D_MODEL = 1024
DEPTH = 2
N_A_LAYERS = DEPTH // 2
POOL_WINDOWS = (2, 4, 8, 16)
N_POOL_GROUPS = len(POOL_WINDOWS)
POOL_GROUP_DIM = D_MODEL // N_POOL_GROUPS
HEAD_DIM = 64
N_HEADS = D_MODEL // HEAD_DIM
N_KV_GROUPS = 4
HEADS_PER_GROUP = N_HEADS // N_KV_GROUPS
L_CMP = 32
D_STRIDE = 16
L_SLC = 64
N_SEL = 16
WINDOW = 512
R_CMP = L_CMP // D_STRIDE
R_SLC = L_SLC // D_STRIDE
N_BUCKETS = 32
REL_EXACT = N_BUCKETS // 2
MAX_DISTANCE = 1024
N_EXPERTS = 8
TOP_K = 2
EPS = 1e-6
NEG_INF = -1e30
SEL_FORCE = 1e6
LOG2_E = math.log2(math.e)

V7X_VMEM_LIMIT_BYTES = 48 * 1024 * 1024
LANES = 128
ROW_TILE = 512
TOK_TILE = 256
DMA_ISSUE_UNROLL = 8
MAX_FORCED = 3
FF_TILE_DENSE = 1408
FF_TILE_EXPERT = 896
POOL_TILE = 512
POOL_HALO = 16

QB = 128
KEY_TILE = 2 * QB
SEL_UNROLL = 4
QROWS = HEADS_PER_GROUP * QB
GROUP_W = HEADS_PER_GROUP * HEAD_DIM
NBLK = 128
CMP_PAD = 128
FAR_DIST = MAX_DISTANCE
N_TOEP = FAR_DIST // QB + 3
BAND_LEFT = CMP_PAD - 16
assert BAND_LEFT * D_STRIDE + (L_CMP - 1) - 2 * QB >= FAR_DIST


def _rms_norm(x, g):
    xf = x.astype(jnp.float32)
    y = xf * lax.rsqrt(jnp.mean(xf * xf, axis=-1, keepdims=True) + EPS)
    return (y * g.astype(jnp.float32)).astype(x.dtype)


def _modulate(h, shift, scale):
    return h * (1 + scale[:, None, :]) + shift[:, None, :]


def _rel_bucket(dist):
    d = jnp.maximum(dist, 0)
    ratio = jnp.maximum(d, REL_EXACT).astype(jnp.float32) / REL_EXACT
    large = REL_EXACT + (jnp.log(ratio) / math.log(MAX_DISTANCE / REL_EXACT)
                         * (N_BUCKETS - REL_EXACT)).astype(jnp.int32)
    return jnp.where(d < REL_EXACT, d, jnp.minimum(large, N_BUCKETS - 1))


def _norm_modulate(x, gain, shift, scale):
    y = x * lax.rsqrt(jnp.mean(x * x, axis=-1, keepdims=True) + EPS)
    return (y * gain) * (1.0 + scale) + shift


def _swiglu_step(xb_ref, wg_ref, wu_ref, wd_ref, acc_ref, j):
    @pl.when(j == 0)
    def _():
        acc_ref[...] = jnp.zeros_like(acc_ref)

    x = xb_ref[...]
    gate = jnp.dot(x, wg_ref[0], preferred_element_type=jnp.float32)
    up = jnp.dot(x, wu_ref[0], preferred_element_type=jnp.float32)
    act = (gate * jax.nn.sigmoid(gate) * up).astype(jnp.bfloat16)
    acc_ref[...] += jnp.dot(act, wd_ref[0], preferred_element_type=jnp.float32)


def _grouped_swiglu_body(blk_e_ref, x_ref, wg_ref, wu_ref, wd_ref, o_ref, acc_ref, xb_ref, *, n_ff_steps):
    del blk_e_ref
    j = pl.program_id(1)

    @pl.when(j == 0)
    def _():
        xb_ref[...] = x_ref[...].astype(jnp.bfloat16)

    _swiglu_step(xb_ref, wg_ref, wu_ref, wd_ref, acc_ref, j)

    @pl.when(j == n_ff_steps - 1)
    def _():
        o_ref[...] = acc_ref[...]


def _grouped_swiglu(x_rows, blk_e, w_gu, w_dn, *, ff_tile):
    n_rows, d = x_rows.shape
    d_ff = w_dn.shape[1]
    assert n_rows % ROW_TILE == 0 and d_ff % ff_tile == 0
    n_ff_steps = d_ff // ff_tile
    grid = (n_rows // ROW_TILE, n_ff_steps)
    return pl.pallas_call(
        functools.partial(_grouped_swiglu_body, n_ff_steps=n_ff_steps),
        grid_spec=pltpu.PrefetchScalarGridSpec(
            num_scalar_prefetch=1,
            grid=grid,
            in_specs=[
                pl.BlockSpec((ROW_TILE, d), lambda i, j, e: (i, 0)),
                pl.BlockSpec((1, d, ff_tile), lambda i, j, e: (e[i], 0, j)),
                pl.BlockSpec((1, d, ff_tile), lambda i, j, e: (e[i], 0, j + n_ff_steps)),
                pl.BlockSpec((1, ff_tile, d), lambda i, j, e: (e[i], j, 0)),
            ],
            out_specs=pl.BlockSpec((ROW_TILE, d), lambda i, j, e: (i, 0)),
            scratch_shapes=[pltpu.VMEM((ROW_TILE, d), jnp.float32), pltpu.VMEM((ROW_TILE, d), jnp.bfloat16)],
        ),
        out_shape=jax.ShapeDtypeStruct((n_rows, d), jnp.float32),
        compiler_params=pltpu.CompilerParams(
            dimension_semantics=("arbitrary", "arbitrary"),
            vmem_limit_bytes=V7X_VMEM_LIMIT_BYTES,
        ),
        name="grouped_swiglu",
    )(blk_e, x_rows, w_gu, w_gu, w_dn)


def _dense_ffn_body(x_ref, gain_ref, shift_ref, scale_ref, gate2_ref, wg_ref, wu_ref, wd_ref, o_ref,
                    acc_ref, xb_ref, *, n_ff_steps):
    j = pl.program_id(1)

    @pl.when(j == 0)
    def _():
        xb_ref[...] = _norm_modulate(x_ref[...], gain_ref[...], shift_ref[0], scale_ref[0]).astype(jnp.bfloat16)

    _swiglu_step(xb_ref, wg_ref, wu_ref, wd_ref, acc_ref, j)

    @pl.when(j == n_ff_steps - 1)
    def _():
        o_ref[...] = x_ref[...] + gate2_ref[0] * acc_ref[...]


def _dense_ffn(x, gain, shift, scale, gate2, w_gu, w_dn):
    B, S, D = x.shape
    d_ff = w_dn.shape[0]
    n_ff_steps = d_ff // FF_TILE_DENSE
    tiles_per_batch = S // ROW_TILE
    per_batch = pl.BlockSpec((1, 1, D), lambda i, j: (i // tiles_per_batch, 0, 0))
    out = pl.pallas_call(
        functools.partial(_dense_ffn_body, n_ff_steps=n_ff_steps),
        grid=(B * S // ROW_TILE, n_ff_steps),
        in_specs=[
            pl.BlockSpec((ROW_TILE, D), lambda i, j: (i, 0)),
            pl.BlockSpec((1, D), lambda i, j: (0, 0)),
            per_batch, per_batch, per_batch,
            pl.BlockSpec((1, D, FF_TILE_DENSE), lambda i, j: (0, 0, j)),
            pl.BlockSpec((1, D, FF_TILE_DENSE), lambda i, j: (0, 0, j + n_ff_steps)),
            pl.BlockSpec((1, FF_TILE_DENSE, D), lambda i, j: (0, j, 0)),
        ],
        out_specs=pl.BlockSpec((ROW_TILE, D), lambda i, j: (i, 0)),
        out_shape=jax.ShapeDtypeStruct((B * S, D), jnp.float32),
        scratch_shapes=[pltpu.VMEM((ROW_TILE, D), jnp.float32), pltpu.VMEM((ROW_TILE, D), jnp.bfloat16)],
        compiler_params=pltpu.CompilerParams(
            dimension_semantics=("arbitrary", "arbitrary"),
            vmem_limit_bytes=V7X_VMEM_LIMIT_BYTES,
        ),
        name="dense_ffn",
    )(x.reshape(B * S, D), gain.reshape(1, D), shift[:, None, :], scale[:, None, :], gate2[:, None, :],
      w_gu.astype(jnp.bfloat16)[None], w_gu.astype(jnp.bfloat16)[None], w_dn.astype(jnp.bfloat16)[None])
    return out.reshape(B, S, D)


def _row_copies_wait(src_ref, dst_ref, sem, n_rows):
    pltpu.make_async_copy(src_ref.at[pl.ds(0, n_rows)], dst_ref.at[pl.ds(0, n_rows)], sem).wait()


def _dispatch_body(dest_ref, x_ref, gain_ref, shift_ref, scale_ref, rows_in_ref, rows_ref, h_ref, sem):
    del rows_in_ref
    h_ref[...] = _norm_modulate(x_ref[...], gain_ref[...], shift_ref[0], scale_ref[0])

    def issue(r, carry):
        for k in range(TOP_K):
            pltpu.make_async_copy(h_ref.at[pl.ds(r, 1)], rows_ref.at[pl.ds(dest_ref[TOP_K * r + k], 1)],
                                  sem).start()
        return carry

    lax.fori_loop(0, TOK_TILE, issue, 0, unroll=DMA_ISSUE_UNROLL)
    for k in range(TOP_K):
        _row_copies_wait(h_ref, rows_ref, sem, TOK_TILE)


def _moe_dispatch(x, gain, shift, scale, dest, n_rows):
    B, S, D = x.shape
    T = B * S
    tiles_per_batch = S // TOK_TILE
    per_batch = pl.BlockSpec((1, 1, D), lambda i: (i // tiles_per_batch, 0, 0))
    return pl.pallas_call(
        _dispatch_body,
        grid=(T // TOK_TILE,),
        in_specs=[
            pl.BlockSpec((TOP_K * TOK_TILE,), lambda i: (i,), memory_space=pltpu.SMEM),
            pl.BlockSpec((TOK_TILE, D), lambda i: (i, 0)),
            pl.BlockSpec((1, D), lambda i: (0, 0)),
            per_batch, per_batch,
            pl.BlockSpec(memory_space=pl.ANY),
        ],
        out_specs=pl.BlockSpec(memory_space=pl.ANY),
        out_shape=jax.ShapeDtypeStruct((n_rows, D), jnp.float32),
        scratch_shapes=[pltpu.VMEM((TOK_TILE, D), jnp.float32), pltpu.SemaphoreType.DMA(())],
        input_output_aliases={5: 0},
        compiler_params=pltpu.CompilerParams(dimension_semantics=("arbitrary",)),
        name="moe_dispatch",
    )(dest.reshape(T * TOP_K), x.reshape(T, D), gain.reshape(1, D), shift[:, None, :], scale[:, None, :],
      jnp.zeros((n_rows, D), jnp.float32))


def _combine_body(dest_ref, x_ref, w_ref, gate2_ref, y_ref, o_ref, buf_ref, sem):
    def issue(r, carry):
        for k in range(TOP_K):
            pltpu.make_async_copy(y_ref.at[pl.ds(dest_ref[TOP_K * r + k], 1)], buf_ref.at[k, pl.ds(r, 1)],
                                  sem).start()
        return carry

    lax.fori_loop(0, TOK_TILE, issue, 0, unroll=DMA_ISSUE_UNROLL)
    for k in range(TOP_K):
        _row_copies_wait(y_ref, buf_ref.at[k], sem, TOK_TILE)
    w = w_ref[...]
    f = w[:, 0:1] * buf_ref[0] + w[:, 1:2] * buf_ref[1]
    o_ref[...] = x_ref[...] + gate2_ref[0] * f


def _moe_combine(x, top_w, gate2, y, dest):
    B, S, D = x.shape
    T = B * S
    tiles_per_batch = S // TOK_TILE
    out = pl.pallas_call(
        _combine_body,
        grid=(T // TOK_TILE,),
        in_specs=[
            pl.BlockSpec((TOP_K * TOK_TILE,), lambda i: (i,), memory_space=pltpu.SMEM),
            pl.BlockSpec((TOK_TILE, D), lambda i: (i, 0)),
            pl.BlockSpec((TOK_TILE, TOP_K), lambda i: (i, 0)),
            pl.BlockSpec((1, 1, D), lambda i: (i // tiles_per_batch, 0, 0)),
            pl.BlockSpec(memory_space=pl.ANY),
        ],
        out_specs=pl.BlockSpec((TOK_TILE, D), lambda i: (i, 0)),
        out_shape=jax.ShapeDtypeStruct((T, D), jnp.float32),
        scratch_shapes=[pltpu.VMEM((TOP_K, TOK_TILE, D), jnp.float32), pltpu.SemaphoreType.DMA(())],
        compiler_params=pltpu.CompilerParams(dimension_semantics=("arbitrary",)),
        name="moe_combine",
    )(dest.reshape(T * TOP_K), x.reshape(T, D), top_w, gate2[:, None, :], y)
    return out.reshape(B, S, D)


def _bias_tables(rel_bias):
    x0 = (N_TOEP - 1) * QB
    width = x0 + 2 * QB
    period = width + QB
    n = np.arange(period)
    n = np.where(n < width, n, n - period)
    by_dist = rel_bias.astype(jnp.float32)[_rel_bucket(jnp.asarray(np.maximum(x0 - n, 0)))].T
    strip = jnp.tile(by_dist, (1, QB))[:, :QB * (period - 1)].reshape(N_HEADS, QB, period - 1)[:, :, :width]
    far = rel_bias.astype(jnp.float32)[N_BUCKETS - 1]

    i = np.arange(QB)[:, None]
    toep = jnp.stack([strip[:, :, x0 - QB * m:x0 - QB * m + QB] for m in range(-1, N_TOEP - 1)], axis=1)
    d_toep = QB * np.arange(-1, N_TOEP - 1)[:, None, None] + i[None] - np.arange(QB)[None, None, :]
    toep = jnp.where(d_toep >= 0, toep * LOG2_E, NEG_INF)
    toep = toep.reshape(N_KV_GROUPS, HEADS_PER_GROUP, N_TOEP, QB, QB).transpose(0, 2, 1, 3, 4)
    d_win = WINDOW + i - np.arange(WINDOW + QB)[None, :]
    win = jnp.where((d_win >= 0) & (d_win < WINDOW), strip[:, :, x0 - WINDOW:x0 + QB] * LOG2_E, NEG_INF)
    win = win.reshape(N_KV_GROUPS, HEADS_PER_GROUP, QB, WINDOW + QB)
    bands = []
    for par in range(2):
        off = x0 - QB * par - D_STRIDE * BAND_LEFT + (L_CMP - 1)
        c_first = -(off // D_STRIDE)
        cols = strip[:, :, D_STRIDE * c_first + off::D_STRIDE][:, :, :128 - c_first]
        left = jnp.broadcast_to(far[:, None, None], (N_HEADS, QB, c_first))
        right = jnp.zeros((N_HEADS, QB, 128 - c_first - cols.shape[2]), jnp.float32)
        d_band = QB * par + i - D_STRIDE * (np.arange(128)[None, :] - BAND_LEFT) - (L_CMP - 1)
        assert (d_band[:, 128 - right.shape[2]:] < 0).all()
        bands.append(jnp.where(d_band >= 0, jnp.concatenate([left, cols, right], axis=-1), NEG_INF))
    band = jnp.stack(bands).reshape(2, N_KV_GROUPS, HEADS_PER_GROUP, QB, 128)
    return toep, win, band, far


def _slc_map_matrix(n_cmp_cols):
    w = np.zeros((n_cmp_cols, NBLK), np.float32)
    for jb in range(NBLK):
        for mm in range(R_SLC):
            for nn in range(R_CMP):
                k = R_SLC * jb + mm - nn
                if 0 <= k < n_cmp_cols:
                    w[k, jb] += 1.0
    return w


def _dot_nt(a, b):
    return lax.dot_general(a, b, (((1,), (1,)), ((), ())), preferred_element_type=jnp.float32)


def _cmp_select_body(far_ref, q_ref, gate_ref, kc_ref, vc_ref, band_ref, wmap_ref, oc_ref, sel_ref,
                     *, n_far):
    qi = pl.program_id(1)
    par = qi % 2
    band0 = pl.multiple_of(16 * (qi // 2 + 1), 16)
    first_band_blk = band0 - CMP_PAD

    col = lax.broadcasted_iota(jnp.int32, (QB, 128), 1)
    row = lax.broadcasted_iota(jnp.int32, (QB, 128), 0)
    neg_pad = jnp.where(first_band_blk + col >= 0, 0.0, NEG_INF)
    far_col = lax.broadcasted_iota(jnp.int32, (1, n_far), 1)
    neg_far = jnp.where(far_col < first_band_blk, 0.0, NEG_INF)

    t = qi * QB + row
    cur = t // L_SLC
    forced = (col == 0) | (col == cur) | (col == cur - 1)
    n_forced = 1 + (cur[:, :1] >= 1).astype(jnp.int32) + (cur[:, :1] >= 2).astype(jnp.int32)
    valid = col * L_SLC <= t

    q_all = q_ref[0]
    wmap_all = jnp.concatenate([wmap_ref[CMP_PAD:CMP_PAD + n_far, :], wmap_ref[pl.ds(band0, 128), :]], axis=0)
    colf = col.astype(jnp.float32)

    def take_best(score, active=None):
        best = jnp.max(score, axis=-1, keepdims=True)
        first = jnp.min(jnp.where(score == best, colf, float(NBLK)), axis=-1, keepdims=True)
        hit = colf == first
        return jnp.where(hit if active is None else hit & active, -jnp.inf, score)

    scores = []
    oc_heads = []
    for g in range(N_KV_GROUPS):
        q4 = jnp.concatenate(
            [q_all[:, (g * HEADS_PER_GROUP + h) * HEAD_DIM:(g * HEADS_PER_GROUP + h + 1) * HEAD_DIM]
             for h in range(HEADS_PER_GROUP)], axis=0)
        k_far = kc_ref[0, g, CMP_PAD:CMP_PAD + n_far, :]
        k_band = kc_ref[0, g, pl.ds(band0, 128), :]
        v_far = vc_ref[0, g, CMP_PAD:CMP_PAD + n_far, :]
        v_band = vc_ref[0, g, pl.ds(band0, 128), :]
        s_far = _dot_nt(q4, k_far).reshape(HEADS_PER_GROUP, QB, n_far)
        s_band = _dot_nt(q4, k_band).reshape(HEADS_PER_GROUP, QB, 128)
        v_all = jnp.concatenate([v_far, v_band], axis=0)
        imp = jnp.zeros((QB, n_far + 128), jnp.float32)
        for h in range(HEADS_PER_GROUP):
            hh = g * HEADS_PER_GROUP + h
            s = jnp.concatenate([s_far[h] + far_ref[hh] + neg_far,
                                 s_band[h] + band_ref[par, g, h] + neg_pad], axis=-1)
            m = jnp.max(s, axis=-1, keepdims=True)
            p = jnp.where(m > 0.5 * NEG_INF, jnp.exp(s - m), 0.0)
            l = jnp.sum(p, axis=-1, keepdims=True)
            p = p * jnp.where(l > 0.0, 1.0 / l, 0.0)
            imp = imp + p
            o = jnp.dot(p.astype(jnp.bfloat16), v_all, preferred_element_type=jnp.float32)
            oc_heads.append(o * gate_ref[0, :, 3 * hh:3 * hh + 1])
        p_slc = jnp.zeros((QB, NBLK), jnp.float32)
        rest = imp
        for _ in range(3):
            term = rest.astype(jnp.bfloat16)
            p_slc = p_slc + jnp.dot(term, wmap_all, preferred_element_type=jnp.float32)
            rest = rest - term.astype(jnp.float32)
        score = jnp.where(forced, -jnp.inf, jnp.where(valid, p_slc, -SEL_FORCE))
        for _ in range(N_SEL - MAX_FORCED):
            score = take_best(score)
        scores.append(score)
    oc_ref[0] = jnp.concatenate(oc_heads, axis=-1)

    def early_rows(scores):
        for extra in range(MAX_FORCED - 1):
            scores = tuple(take_best(s, MAX_FORCED - n_forced > extra) for s in scores)
        return scores

    scores = lax.cond(qi == 0, early_rows, lambda s: s, tuple(scores))
    for g in range(N_KV_GROUPS):
        sel_ref[0, g] = jnp.where(scores[g] == -jnp.inf, 0.0, NEG_INF).astype(jnp.bfloat16)


def _cmp_select(q, gates, kc_pad, vc_pad, band, far, wmap):
    B, S, _ = q.shape
    n_far = S // D_STRIDE
    n_pad = kc_pad.shape[2]
    grid = (B, S // QB)
    return pl.pallas_call(
        functools.partial(_cmp_select_body, n_far=n_far),
        grid=grid,
        in_specs=[
            pl.BlockSpec(memory_space=pltpu.SMEM),
            pl.BlockSpec((1, QB, N_HEADS * HEAD_DIM), lambda b, i: (b, i, 0)),
            pl.BlockSpec((1, QB, LANES), lambda b, i: (b, i, 0)),
            pl.BlockSpec((1, N_KV_GROUPS, n_pad, HEAD_DIM), lambda b, i: (b, 0, 0, 0)),
            pl.BlockSpec((1, N_KV_GROUPS, n_pad, HEAD_DIM), lambda b, i: (b, 0, 0, 0)),
            pl.BlockSpec((2, N_KV_GROUPS, HEADS_PER_GROUP, QB, 128), lambda b, i: (0, 0, 0, 0, 0)),
            pl.BlockSpec((n_pad, NBLK), lambda b, i: (0, 0)),
        ],
        out_specs=[
            pl.BlockSpec((1, QB, N_HEADS * HEAD_DIM), lambda b, i: (b, i, 0)),
            pl.BlockSpec((1, N_KV_GROUPS, QB, NBLK), lambda b, i: (b, 0, i, 0)),
        ],
        out_shape=[
            jax.ShapeDtypeStruct((B, S, N_HEADS * HEAD_DIM), jnp.float32),
            jax.ShapeDtypeStruct((B, N_KV_GROUPS, S, NBLK), jnp.bfloat16),
        ],
        compiler_params=pltpu.CompilerParams(
            dimension_semantics=("arbitrary", "arbitrary"),
            vmem_limit_bytes=V7X_VMEM_LIMIT_BYTES,
        ),
        name="nsa_cmp_select",
    )(far, q, gates, kc_pad, vc_pad, band, wmap)


def _win_sel_body(q_ref, gate_ref, sel_ref, oc_ref, ks_ref, vs_ref, kw_ref, vw_ref, toep_ref, win_ref,
                  out_ref, s0_scr, s1_scr, p0_scr, p1_scr, m_scr, acc_scr):
    qi = pl.program_id(2)
    q0 = pl.multiple_of(qi * QB, QB)
    qg = q_ref[0]
    q_heads = [qg[:, h * HEAD_DIM:(h + 1) * HEAD_DIM] for h in range(HEADS_PER_GROUP)]
    q4 = jnp.concatenate(q_heads, axis=0)

    kw = kw_ref[0, 0, :, pl.ds(q0, WINDOW + QB)]
    s_w = jnp.dot(q4, kw, preferred_element_type=jnp.float32)
    wcol = lax.broadcasted_iota(jnp.int32, (1, WINDOW + QB), 1)
    neg_left = jnp.where(q0 + wcol >= WINDOW, 0.0, NEG_INF)
    s_w = s_w.reshape(HEADS_PER_GROUP, QB, WINDOW + QB) + win_ref[0] + neg_left
    m_w = jnp.max(s_w, axis=-1, keepdims=True)
    p_w = jnp.exp2(s_w - m_w)
    o_w = jnp.dot(p_w.reshape(QROWS, WINDOW + QB).astype(jnp.bfloat16),
                  vw_ref[0, 0, pl.ds(q0, WINDOW + QB), :], preferred_element_type=jnp.float32)
    o_w = o_w.reshape(HEADS_PER_GROUP, QB, LANES)
    o_w = o_w[:, :, :HEAD_DIM] / o_w[:, :, HEAD_DIM:HEAD_DIM + 1]

    sel = sel_ref[0, 0]
    qa = jnp.concatenate([jnp.concatenate([sel] * HEADS_PER_GROUP, axis=0), q4], axis=-1)
    c_diag = qi // 2

    def scores(c):
        col = pl.multiple_of(c * KEY_TILE, KEY_TILE)
        s = jnp.dot(qa, ks_ref[0, 0, :, pl.ds(col, KEY_TILE)], preferred_element_type=jnp.float32)
        mm = qi - 2 * c
        bias = jnp.concatenate([toep_ref[0, jnp.clip(mm + 1, 0, N_TOEP - 1)],
                                toep_ref[0, jnp.clip(mm, 0, N_TOEP - 1)]], axis=-1)
        return s.reshape(HEADS_PER_GROUP, QB, KEY_TILE) + bias

    s_slots = (s0_scr, s1_scr)
    p_slots = (p0_scr, p1_scr)
    m_scr[...] = jnp.full(m_scr.shape, NEG_INF, jnp.float32)
    acc_scr[...] = jnp.zeros_like(acc_scr)
    p_slots[0][...] = jnp.zeros_like(p0_scr)
    s_slots[0][...] = scores(0)

    def weighted_values(p, col):
        return jnp.dot(p, vs_ref[0, 0, pl.ds(pl.multiple_of(col, KEY_TILE), KEY_TILE), :],
                       preferred_element_type=jnp.float32)

    def half_step(c, col_prev, cur):
        pv = weighted_values(p_slots[cur][...], col_prev)
        s_slots[1 - cur][...] = scores(c + 1)
        for h in range(HEADS_PER_GROUP):
            s = s_slots[cur][h]
            m_old = m_scr[h]
            m_new = jnp.maximum(m_old, jnp.max(s, axis=-1, keepdims=True))
            alpha = jnp.exp2(m_old - m_new)
            p = jnp.exp2(s - jnp.concatenate([m_new] * (KEY_TILE // LANES), axis=-1))
            p_slots[1 - cur][h * QB:(h + 1) * QB, :] = p.astype(jnp.bfloat16)
            m_scr[h] = m_new
            acc_scr[h * QB:(h + 1) * QB, :] = alpha * (acc_scr[h * QB:(h + 1) * QB, :]
                                                      + pv[h * QB:(h + 1) * QB, :])
        return c * KEY_TILE

    def unrolled_steps(j, col_prev):
        for u in range(SEL_UNROLL):
            col_prev = half_step(SEL_UNROLL * j + u, col_prev, u % 2)
        return col_prev

    col_last = lax.fori_loop(0, (c_diag + SEL_UNROLL) // SEL_UNROLL, unrolled_steps, 0)
    acc_s = (acc_scr[...] + weighted_values(p_slots[0][...], col_last)).reshape(HEADS_PER_GROUP, QB, LANES)
    o_s = acc_s[:, :, :HEAD_DIM] / acc_s[:, :, HEAD_DIM:HEAD_DIM + 1]

    outs = []
    for h in range(HEADS_PER_GROUP):
        g_s = gate_ref[0, 0, :, 3 * h + 1:3 * h + 2]
        g_w = gate_ref[0, 0, :, 3 * h + 2:3 * h + 3]
        outs.append(g_s * o_s[h] + g_w * o_w[h])
    out_ref[0] = oc_ref[0] + jnp.concatenate(outs, axis=-1)


def _win_sel(q, gates, selneg, oc, ks_aug, vs, kw_t, vw_pad, toep, win):
    B, S, _ = q.shape
    gw = HEADS_PER_GROUP * HEAD_DIM
    grid = (B, N_KV_GROUPS, S // QB)
    return pl.pallas_call(
        _win_sel_body,
        grid=grid,
        in_specs=[
            pl.BlockSpec((1, QB, gw), lambda b, g, i: (b, i, g)),
            pl.BlockSpec((1, 1, QB, 3 * HEADS_PER_GROUP), lambda b, g, i: (b, g, i, 0)),
            pl.BlockSpec((1, 1, QB, NBLK), lambda b, g, i: (b, g, i, 0)),
            pl.BlockSpec((1, QB, gw), lambda b, g, i: (b, i, g)),
            pl.BlockSpec((1, 1, NBLK + HEAD_DIM, ks_aug.shape[3]), lambda b, g, i: (b, g, 0, 0)),
            pl.BlockSpec((1, 1, vs.shape[2], LANES), lambda b, g, i: (b, g, 0, 0)),
            pl.BlockSpec((1, 1, HEAD_DIM, S + WINDOW), lambda b, g, i: (b, g, 0, 0)),
            pl.BlockSpec((1, 1, S + WINDOW, LANES), lambda b, g, i: (b, g, 0, 0)),
            pl.BlockSpec((1, N_TOEP, HEADS_PER_GROUP, QB, QB), lambda b, g, i: (g, 0, 0, 0, 0)),
            pl.BlockSpec((1, HEADS_PER_GROUP, QB, WINDOW + QB), lambda b, g, i: (g, 0, 0, 0)),
        ],
        out_specs=pl.BlockSpec((1, QB, gw), lambda b, g, i: (b, i, g)),
        out_shape=jax.ShapeDtypeStruct((B, S, N_HEADS * HEAD_DIM), jnp.float32),
        scratch_shapes=[
            pltpu.VMEM((HEADS_PER_GROUP, QB, KEY_TILE), jnp.float32),
            pltpu.VMEM((HEADS_PER_GROUP, QB, KEY_TILE), jnp.float32),
            pltpu.VMEM((QROWS, KEY_TILE), jnp.bfloat16),
            pltpu.VMEM((QROWS, KEY_TILE), jnp.bfloat16),
            pltpu.VMEM((HEADS_PER_GROUP, QB, LANES), jnp.float32),
            pltpu.VMEM((QROWS, LANES), jnp.float32),
        ],
        compiler_params=pltpu.CompilerParams(
            dimension_semantics=("arbitrary", "arbitrary", "arbitrary"),
            vmem_limit_bytes=V7X_VMEM_LIMIT_BYTES,
        ),
        name="nsa_win_sel",
    )(q, gates, selneg, oc, ks_aug, vs, kw_t, vw_pad, toep, win)


def _nsa_attention(q, gates, kc, vc, kvsw, rel_bias):
    B, S, _ = q.shape
    assert S % KEY_TILE == 0 and S // L_SLC <= NBLK
    bf = jnp.bfloat16
    n_far = S // D_STRIDE
    toep, win, band, far = _bias_tables(rel_bias)
    cpad = ((0, 0), (0, 0), (CMP_PAD, n_far - kc.shape[2] + CMP_PAD), (0, 0))
    kc_pad = jnp.pad(kc, cpad).astype(bf)
    vc_pad = jnp.pad(vc, cpad).astype(bf)
    wmap = jnp.asarray(np.pad(_slc_map_matrix(n_far), ((CMP_PAD, CMP_PAD), (0, 0))), bf)
    oc, selneg = _cmp_select(q, gates, kc_pad, vc_pad, band, far, wmap)
    ks, vs, kw, vw = (kvsw.reshape(B, S, 4, N_KV_GROUPS, HEAD_DIM)[:, :, n] for n in range(4))
    blk_onehot = jnp.asarray((np.arange(S)[None, :] // L_SLC == np.arange(NBLK)[:, None]), bf)
    ks_aug = jnp.concatenate(
        [jnp.broadcast_to(blk_onehot, (B, N_KV_GROUPS, NBLK, S)), ks.transpose(0, 2, 3, 1)], axis=2)
    tail = SEL_UNROLL * KEY_TILE
    ks_aug = jnp.pad(ks_aug, ((0, 0), (0, 0), (0, 0), (0, tail)))
    kw_t = jnp.pad(kw.transpose(0, 2, 3, 1), ((0, 0), (0, 0), (0, 0), (WINDOW, 0)))

    def with_ones(v):
        v = v.transpose(0, 2, 1, 3)
        ones = jnp.ones(v.shape[:-1] + (1,), v.dtype)
        return jnp.pad(jnp.concatenate([v, ones], axis=-1), ((0, 0),) * 3 + ((0, LANES - HEAD_DIM - 1),))

    vw_aug = jnp.pad(with_ones(vw), ((0, 0), (0, 0), (WINDOW, 0), (0, 0)))
    vs_aug = jnp.pad(with_ones(vs), ((0, 0), (0, 0), (0, tail), (0, 0)))
    gates_g = gates[:, :, :3 * N_HEADS].reshape(B, S, N_KV_GROUPS, 3 * HEADS_PER_GROUP).swapaxes(1, 2)
    return _win_sel(q, gates_g, selneg, oc, ks_aug, vs_aug, kw_t, vw_aug, toep, win)


def _pool_layer_body(x_ref, halo_ref, gain_ref, shift_ref, scale_ref, gate_ref, pscale_ref, w_ref, o_ref):
    i = pl.program_id(1)
    x = x_ref[0]
    gain, shift, scale = gain_ref[...], shift_ref[0], scale_ref[0]
    h = _norm_modulate(x, gain, shift, scale)
    halo = _norm_modulate(halo_ref[0], gain, shift, scale) * (i > 0).astype(jnp.float32)
    hx = jnp.concatenate([halo, h], axis=0)
    t = i * POOL_TILE + lax.broadcasted_iota(jnp.int32, (POOL_TILE, 1), 0)
    outs = []
    run = hx
    width = 1
    for g, w in enumerate(POOL_WINDOWS):
        while width < w:
            run = run + pltpu.roll(run, width, axis=0)
            width *= 2
        lanes = slice(0, POOL_GROUP_DIM)
        cnt = jnp.minimum(t + 1, w).astype(jnp.float32)
        mix = run[POOL_HALO:, lanes] / cnt - h[:, g * POOL_GROUP_DIM:(g + 1) * POOL_GROUP_DIM]
        outs.append(jnp.dot(mix.astype(jnp.bfloat16), w_ref[g], preferred_element_type=jnp.float32))
        run = run[:, POOL_GROUP_DIM:]
    y = jnp.concatenate(outs, axis=-1) * pscale_ref[...]
    o_ref[0] = x + gate_ref[0] * y


def _pool_layer(x, gain, shift, scale, gate1, w_grp, pool_scale):
    B, S, D = x.shape
    assert S % POOL_TILE == 0 and POOL_HALO >= max(POOL_WINDOWS) - 1
    per_batch = pl.BlockSpec((1, 1, D), lambda b, i: (b, 0, 0))
    halo_blocks = POOL_TILE // POOL_HALO
    return pl.pallas_call(
        _pool_layer_body,
        grid=(B, S // POOL_TILE),
        in_specs=[
            pl.BlockSpec((1, POOL_TILE, D), lambda b, i: (b, i, 0)),
            pl.BlockSpec((1, POOL_HALO, D), lambda b, i: (b, jnp.maximum(i * halo_blocks - 1, 0), 0)),
            pl.BlockSpec((1, D), lambda b, i: (0, 0)),
            per_batch, per_batch, per_batch,
            pl.BlockSpec((1, D), lambda b, i: (0, 0)),
            pl.BlockSpec((N_POOL_GROUPS, POOL_GROUP_DIM, POOL_GROUP_DIM), lambda b, i: (0, 0, 0)),
        ],
        out_specs=pl.BlockSpec((1, POOL_TILE, D), lambda b, i: (b, i, 0)),
        out_shape=jax.ShapeDtypeStruct((B, S, D), jnp.float32),
        compiler_params=pltpu.CompilerParams(
            dimension_semantics=("arbitrary", "arbitrary"),
            vmem_limit_bytes=V7X_VMEM_LIMIT_BYTES,
        ),
        name="pool_layer",
    )(x, x, gain.reshape(1, D), shift[:, None, :], scale[:, None, :], gate1[:, None, :],
      pool_scale.reshape(1, D), w_grp.astype(jnp.bfloat16))


def _head_norm(v, seg_ref, gain):
    sq = v * v
    hi = sq.astype(jnp.bfloat16)
    lo = (sq - hi.astype(jnp.float32)).astype(jnp.bfloat16)
    ss = (jnp.dot(hi, seg_ref[...], preferred_element_type=jnp.float32)
          + jnp.dot(lo, seg_ref[...], preferred_element_type=jnp.float32))
    return v * lax.rsqrt(ss * (1.0 / HEAD_DIM) + EPS) * gain


def _qkv_proj_body(x_ref, gq_ref, shq_ref, scq_ref, gkv_ref, shkv_ref, sckv_ref, wq_ref, wg_ref, wkv_ref,
                   seg_ref, qgain_ref, kgain_ref, q_ref, gates_ref, kvc_ref, kvsw_ref):
    x = x_ref[...]
    xhat = x * lax.rsqrt(jnp.mean(x * x, axis=-1, keepdims=True) + EPS)
    hq = ((xhat * gq_ref[...]) * (1.0 + scq_ref[0]) + shq_ref[0]).astype(jnp.bfloat16)
    hkv = ((xhat * gkv_ref[...]) * (1.0 + sckv_ref[0]) + shkv_ref[0]).astype(jnp.bfloat16)
    pq = jnp.dot(hq, wq_ref[...], preferred_element_type=jnp.float32)
    gates_ref[...] = jax.nn.sigmoid(jnp.dot(hq, wg_ref[...], preferred_element_type=jnp.float32))
    pkv = jnp.dot(hkv, wkv_ref[...], preferred_element_type=jnp.float32)
    for g in range(N_KV_GROUPS):
        lanes = slice(g * GROUP_W, (g + 1) * GROUP_W)
        q_ref[:, lanes] = _head_norm(pq[:, lanes], seg_ref, qgain_ref[...]).astype(jnp.bfloat16)
    kvc_ref[...] = pkv[:, :2 * GROUP_W]
    k_s = _head_norm(pkv[:, 2 * GROUP_W:3 * GROUP_W], seg_ref, kgain_ref[0:1, :])
    k_w = _head_norm(pkv[:, 4 * GROUP_W:5 * GROUP_W], seg_ref, kgain_ref[1:2, :])
    kvsw_ref[...] = jnp.concatenate([k_s, pkv[:, 3 * GROUP_W:4 * GROUP_W], k_w, pkv[:, 5 * GROUP_W:]],
                                    axis=-1).astype(jnp.bfloat16)


def _qkv_proj(x, gain_q, shift_q, scale_q, gain_kv, shift_kv, scale_kv, w_qg, kv_w, q_gain, k_gain):
    B, S, D = x.shape
    T = B * S
    qd = N_HEADS * HEAD_DIM
    bf = jnp.bfloat16
    tiles_per_batch = S // ROW_TILE
    per_batch = pl.BlockSpec((1, 1, D), lambda i: (i // tiles_per_batch, 0, 0))
    const = lambda shape: pl.BlockSpec(shape, lambda i: (0,) * len(shape))
    seg = jnp.asarray(np.kron(np.eye(HEADS_PER_GROUP), np.ones((HEAD_DIM, HEAD_DIM))), bf)
    w_gate = jnp.pad(w_qg[:, qd:], ((0, 0), (0, LANES - 3 * N_HEADS))).astype(bf)
    qgain = jnp.tile(q_gain * HEAD_DIM ** -0.5, HEADS_PER_GROUP).reshape(1, GROUP_W)
    kgain = jnp.stack([jnp.tile(k_gain[1] * LOG2_E, N_KV_GROUPS), jnp.tile(k_gain[2] * LOG2_E, N_KV_GROUPS)])
    rows = lambda width: pl.BlockSpec((ROW_TILE, width), lambda i: (i, 0))
    return pl.pallas_call(
        _qkv_proj_body,
        grid=(T // ROW_TILE,),
        in_specs=[rows(D), const((1, D)), per_batch, per_batch, const((1, D)), per_batch, per_batch,
                  const((D, qd)), const((D, LANES)), const((D, 6 * GROUP_W)), const((GROUP_W, GROUP_W)),
                  const((1, GROUP_W)), const((2, GROUP_W))],
        out_specs=[rows(qd), rows(LANES), rows(2 * GROUP_W), rows(4 * GROUP_W)],
        out_shape=[jax.ShapeDtypeStruct((T, qd), bf), jax.ShapeDtypeStruct((T, LANES), jnp.float32),
                   jax.ShapeDtypeStruct((T, 2 * GROUP_W), jnp.float32),
                   jax.ShapeDtypeStruct((T, 4 * GROUP_W), bf)],
        compiler_params=pltpu.CompilerParams(
            dimension_semantics=("arbitrary",), vmem_limit_bytes=V7X_VMEM_LIMIT_BYTES),
        name="qkv_proj",
    )(x.reshape(T, D), gain_q.reshape(1, D), shift_q[:, None, :], scale_q[:, None, :],
      gain_kv.reshape(1, D), shift_kv[:, None, :], scale_kv[:, None, :],
      w_qg[:, :qd].astype(bf), w_gate, kv_w.astype(bf), seg, qgain, kgain)


def _compressed_kv(kvc, B, S, cmp_pe_k, cmp_pe_v, cmp_k_w1, cmp_k_w2, cmp_v_w1, cmp_v_w2, k_gain):
    kv = kvc.reshape(B, S, 2, N_KV_GROUPS, HEAD_DIM).transpose(2, 0, 3, 1, 4)

    def compress(u, pe, w1, w2):
        r = u.reshape(B, N_KV_GROUPS, S // D_STRIDE, D_STRIDE, HEAD_DIM)
        nc = S // D_STRIDE - R_CMP + 1
        blocks = jnp.concatenate([r[:, :, i:i + nc] for i in range(R_CMP)], axis=3) + pe
        flat = blocks.reshape(B, N_KV_GROUPS, nc, L_CMP * HEAD_DIM)
        return jax.nn.gelu(flat @ w1) @ w2

    kc = _rms_norm(compress(kv[0], cmp_pe_k, cmp_k_w1, cmp_k_w2), k_gain[0])
    vc = compress(kv[1], cmp_pe_v, cmp_v_w1, cmp_v_w2)
    return kc, vc


def _oproj_router_body(a_ref, x_ref, gate_ref, wo_ref, gain_ref, shift_ref, scale_ref, wr_ref, br_ref,
                       o_ref, logit_ref):
    mix = jnp.dot(a_ref[...].astype(jnp.bfloat16), wo_ref[...], preferred_element_type=jnp.float32)
    x = x_ref[...] + gate_ref[0] * mix
    o_ref[...] = x
    h = _norm_modulate(x, gain_ref[...], shift_ref[0], scale_ref[0]).astype(jnp.bfloat16)
    logit_ref[...] = jnp.dot(h, wr_ref[...], preferred_element_type=jnp.float32) + br_ref[...]


def _oproj_router(attn, x, gate1, w_o, gain, shift, scale, w_router, b_router):
    B, S, D = x.shape
    T = B * S
    bf = jnp.bfloat16
    tiles_per_batch = S // ROW_TILE
    per_batch = pl.BlockSpec((1, 1, D), lambda i: (i // tiles_per_batch, 0, 0))
    const = lambda shape: pl.BlockSpec(shape, lambda i: (0,) * len(shape))
    rows = lambda width: pl.BlockSpec((ROW_TILE, width), lambda i: (i, 0))
    pad = ((0, 0), (0, LANES - N_EXPERTS))
    out, logits = pl.pallas_call(
        _oproj_router_body,
        grid=(T // ROW_TILE,),
        in_specs=[rows(D), rows(D), per_batch, const((D, D)), const((1, D)), per_batch, per_batch,
                  const((D, LANES)), const((1, LANES))],
        out_specs=[rows(D), rows(LANES)],
        out_shape=[jax.ShapeDtypeStruct((T, D), jnp.float32), jax.ShapeDtypeStruct((T, LANES), jnp.float32)],
        compiler_params=pltpu.CompilerParams(
            dimension_semantics=("arbitrary",), vmem_limit_bytes=V7X_VMEM_LIMIT_BYTES),
        name="oproj_router",
    )(attn.reshape(T, D), x.reshape(T, D), gate1[:, None, :], w_o.astype(bf), gain.reshape(1, D),
      shift[:, None, :], scale[:, None, :], jnp.pad(w_router, pad).astype(bf),
      jnp.pad(b_router.reshape(1, N_EXPERTS), pad))
    return out.reshape(B, S, D), logits[:, :N_EXPERTS]


def _route(logits):
    T = logits.shape[0]
    lane = jnp.arange(N_EXPERTS)[None, :]
    l0 = jnp.max(logits, axis=-1)
    e0 = jnp.argmax(logits, axis=-1)
    rest = jnp.where(lane == e0[:, None], -jnp.inf, logits)
    l1 = jnp.max(rest, axis=-1)
    e1 = jnp.argmax(rest, axis=-1)
    z = jnp.exp(l1 - l0)
    top_w = jnp.stack([1.0 / (1.0 + z), z / (1.0 + z)], axis=-1)
    oh = [(lane == e[:, None]).astype(jnp.int32) for e in (e0, e1)]
    cnt = oh[0] + oh[1]
    before = jnp.cumsum(cnt, axis=0) - cnt
    counts = before[-1] + cnt[-1]
    pcounts = (counts + ROW_TILE - 1) // ROW_TILE * ROW_TILE
    pends = jnp.cumsum(pcounts)
    pstarts = pends - pcounts
    dest = jnp.stack([jnp.sum((pstarts[None, :] + before) * o, axis=-1) for o in oh], axis=-1)
    n_blocks = -(-(T * TOP_K + N_EXPERTS * (ROW_TILE - 1)) // ROW_TILE)
    blk_start = jnp.arange(n_blocks)[:, None] * ROW_TILE
    blk_e = jnp.minimum(jnp.sum(pends[None, :] <= blk_start, axis=-1), N_EXPERTS - 1).astype(jnp.int32)
    return top_w, dest.astype(jnp.int32), blk_e, n_blocks * ROW_TILE


def _moe_ffn(x, logits, gain, shift, scale, gate2, w_gu, w_dn):
    top_w, dest, blk_e, n_rows = _route(logits)
    rows = _moe_dispatch(x, gain, shift, scale, dest, n_rows)
    y = _grouped_swiglu(rows, blk_e, w_gu.astype(jnp.bfloat16), w_dn.astype(jnp.bfloat16),
                        ff_tile=FF_TILE_EXPERT)
    return _moe_combine(x, top_w, gate2, y, dest)


def kernel(x, c, ada_w, ada_b, norm_g, pool_w, pool_scale, q_w, q_gain, o_w, kv_ada_w, kv_ada_b, kv_norm_g, kv_w, cmp_pe_k, cmp_pe_v, cmp_k_w1, cmp_k_w2, cmp_v_w1, cmp_v_w2, k_gain, rel_bias, ffn_gu, ffn_dn, router_w, router_b, exp_gu, exp_dn):
    assert DEPTH == 2 and N_A_LAYERS == 1
    B, S, D = x.shape
    silu_c = jax.nn.silu(c)
    sh1, sc1, g1, sh2, sc2, g2 = jnp.split(silu_c @ ada_w[0] + ada_b[0], 6, axis=-1)
    x = _pool_layer(x, norm_g[0, 0], sh1, sc1, g1, pool_w[0], pool_scale[0])
    x = _dense_ffn(x, norm_g[0, 1], sh2, sc2, g2, ffn_gu[0], ffn_dn[0])
    sh1, sc1, g1, sh2, sc2, g2 = jnp.split(silu_c @ ada_w[1] + ada_b[1], 6, axis=-1)
    sh_kv, sc_kv = jnp.split(silu_c @ kv_ada_w + kv_ada_b, 2, axis=-1)
    q, gates, kvc, kvsw = _qkv_proj(x, norm_g[1, 0], sh1, sc1, kv_norm_g, sh_kv, sc_kv, q_w[0], kv_w,
                                    q_gain[0], k_gain)
    kc, vc = _compressed_kv(kvc, B, S, cmp_pe_k, cmp_pe_v, cmp_k_w1, cmp_k_w2, cmp_v_w1, cmp_v_w2, k_gain)
    attn = _nsa_attention(q.reshape(B, S, -1), gates.reshape(B, S, -1), kc, vc, kvsw.reshape(B, S, -1),
                          rel_bias)
    x, logits = _oproj_router(attn, x, g1, o_w[0], norm_g[1, 1], sh2, sc2, router_w[0], router_b[0])
    return _moe_ffn(x, logits, norm_g[1, 1], sh2, sc2, g2, exp_gu[0], exp_dn[0])
```

```python
import functools
import math

import jax
import jax.numpy as jnp
import numpy as np
from jax import lax
from jax.experimental import pallas as pl
from jax.experimental.pallas import tpu as pltpu

D_MODEL = 1024
DEPTH = 2
N_A_LAYERS = DEPTH // 2
POOL_WINDOWS = (2, 4, 8, 16)
N_POOL_GROUPS = len(POOL_WINDOWS)
POOL_GROUP_DIM = D_MODEL // N_POOL_GROUPS
HEAD_DIM = 64
N_HEADS = D_MODEL // HEAD_DIM
N_KV_GROUPS = 4
HEADS_PER_GROUP = N_HEADS // N_KV_GROUPS
L_CMP = 32
D_STRIDE = 16
L_SLC = 64
N_SEL = 16
WINDOW = 512
R_CMP = L_CMP // D_STRIDE
R_SLC = L_SLC // D_STRIDE
N_BUCKETS = 32
REL_EXACT = N_BUCKETS // 2
MAX_DISTANCE = 1024
N_EXPERTS = 8
TOP_K = 2
EPS = 1e-6
NEG_INF = -1e30
SEL_FORCE = 1e6
LOG2_E = math.log2(math.e)

V7X_VMEM_LIMIT_BYTES = 48 * 1024 * 1024
LANES = 128
ROW_TILE = 512
TOK_TILE = 256
DMA_ISSUE_UNROLL = 8
MAX_FORCED = 3
FF_TILE_DENSE = 1408
FF_TILE_EXPERT = 896
POOL_TILE = 512
POOL_HALO = 16

QB = 128
KEY_TILE = 2 * QB
SEL_UNROLL = 4
QROWS = HEADS_PER_GROUP * QB
GROUP_W = HEADS_PER_GROUP * HEAD_DIM
NBLK = 128
CMP_PAD = 128
FAR_DIST = MAX_DISTANCE
N_TOEP = FAR_DIST // QB + 3
BAND_LEFT = CMP_PAD - 16
assert BAND_LEFT * D_STRIDE + (L_CMP - 1) - 2 * QB >= FAR_DIST


def _rms_norm(x, g):
    xf = x.astype(jnp.float32)
    y = xf * lax.rsqrt(jnp.mean(xf * xf, axis=-1, keepdims=True) + EPS)
    return (y * g.astype(jnp.float32)).astype(x.dtype)


def _modulate(h, shift, scale):
    return h * (1 + scale[:, None, :]) + shift[:, None, :]


def _rel_bucket(dist):
    d = jnp.maximum(dist, 0)
    ratio = jnp.maximum(d, REL_EXACT).astype(jnp.float32) / REL_EXACT
    large = REL_EXACT + (jnp.log(ratio) / math.log(MAX_DISTANCE / REL_EXACT)
                         * (N_BUCKETS - REL_EXACT)).astype(jnp.int32)
    return jnp.where(d < REL_EXACT, d, jnp.minimum(large, N_BUCKETS - 1))


def _norm_modulate(x, gain, shift, scale):
    y = x * lax.rsqrt(jnp.mean(x * x, axis=-1, keepdims=True) + EPS)
    return (y * gain) * (1.0 + scale) + shift


def _swiglu_step(xb_ref, wg_ref, wu_ref, wd_ref, acc_ref, j):
    @pl.when(j == 0)
    def _():
        acc_ref[...] = jnp.zeros_like(acc_ref)

    x = xb_ref[...]
    gate = jnp.dot(x, wg_ref[0], preferred_element_type=jnp.float32)
    up = jnp.dot(x, wu_ref[0], preferred_element_type=jnp.float32)
    act = (gate * jax.nn.sigmoid(gate) * up).astype(jnp.bfloat16)
    acc_ref[...] += jnp.dot(act, wd_ref[0], preferred_element_type=jnp.float32)


def _grouped_swiglu_body(blk_e_ref, x_ref, wg_ref, wu_ref, wd_ref, o_ref, acc_ref, xb_ref, *, n_ff_steps):
    del blk_e_ref
    j = pl.program_id(1)

    @pl.when(j == 0)
    def _():
        xb_ref[...] = x_ref[...].astype(jnp.bfloat16)

    _swiglu_step(xb_ref, wg_ref, wu_ref, wd_ref, acc_ref, j)

    @pl.when(j == n_ff_steps - 1)
    def _():
        o_ref[...] = acc_ref[...]


def _grouped_swiglu(x_rows, blk_e, w_gu, w_dn, *, ff_tile):
    n_rows, d = x_rows.shape
    d_ff = w_dn.shape[1]
    assert n_rows % ROW_TILE == 0 and d_ff % ff_tile == 0
    n_ff_steps = d_ff // ff_tile
    grid = (n_rows // ROW_TILE, n_ff_steps)
    return pl.pallas_call(
        functools.partial(_grouped_swiglu_body, n_ff_steps=n_ff_steps),
        grid_spec=pltpu.PrefetchScalarGridSpec(
            num_scalar_prefetch=1,
            grid=grid,
            in_specs=[
                pl.BlockSpec((ROW_TILE, d), lambda i, j, e: (i, 0)),
                pl.BlockSpec((1, d, ff_tile), lambda i, j, e: (e[i], 0, j)),
                pl.BlockSpec((1, d, ff_tile), lambda i, j, e: (e[i], 0, j + n_ff_steps)),
                pl.BlockSpec((1, ff_tile, d), lambda i, j, e: (e[i], j, 0)),
            ],
            out_specs=pl.BlockSpec((ROW_TILE, d), lambda i, j, e: (i, 0)),
            scratch_shapes=[pltpu.VMEM((ROW_TILE, d), jnp.float32), pltpu.VMEM((ROW_TILE, d), jnp.bfloat16)],
        ),
        out_shape=jax.ShapeDtypeStruct((n_rows, d), jnp.float32),
        compiler_params=pltpu.CompilerParams(
            dimension_semantics=("arbitrary", "arbitrary"),
            vmem_limit_bytes=V7X_VMEM_LIMIT_BYTES,
        ),
        name="grouped_swiglu",
    )(blk_e, x_rows, w_gu, w_gu, w_dn)


def _dense_ffn_body(x_ref, gain_ref, shift_ref, scale_ref, gate2_ref, wg_ref, wu_ref, wd_ref, o_ref,
                    acc_ref, xb_ref, *, n_ff_steps):
    j = pl.program_id(1)

    @pl.when(j == 0)
    def _():
        xb_ref[...] = _norm_modulate(x_ref[...], gain_ref[...], shift_ref[0], scale_ref[0]).astype(jnp.bfloat16)

    _swiglu_step(xb_ref, wg_ref, wu_ref, wd_ref, acc_ref, j)

    @pl.when(j == n_ff_steps - 1)
    def _():
        o_ref[...] = x_ref[...] + gate2_ref[0] * acc_ref[...]


def _dense_ffn(x, gain, shift, scale, gate2, w_gu, w_dn):
    B, S, D = x.shape
    d_ff = w_dn.shape[0]
    n_ff_steps = d_ff // FF_TILE_DENSE
    tiles_per_batch = S // ROW_TILE
    per_batch = pl.BlockSpec((1, 1, D), lambda i, j: (i // tiles_per_batch, 0, 0))
    out = pl.pallas_call(
        functools.partial(_dense_ffn_body, n_ff_steps=n_ff_steps),
        grid=(B * S // ROW_TILE, n_ff_steps),
        in_specs=[
            pl.BlockSpec((ROW_TILE, D), lambda i, j: (i, 0)),
            pl.BlockSpec((1, D), lambda i, j: (0, 0)),
            per_batch, per_batch, per_batch,
            pl.BlockSpec((1, D, FF_TILE_DENSE), lambda i, j: (0, 0, j)),
            pl.BlockSpec((1, D, FF_TILE_DENSE), lambda i, j: (0, 0, j + n_ff_steps)),
            pl.BlockSpec((1, FF_TILE_DENSE, D), lambda i, j: (0, j, 0)),
        ],
        out_specs=pl.BlockSpec((ROW_TILE, D), lambda i, j: (i, 0)),
        out_shape=jax.ShapeDtypeStruct((B * S, D), jnp.float32),
        scratch_shapes=[pltpu.VMEM((ROW_TILE, D), jnp.float32), pltpu.VMEM((ROW_TILE, D), jnp.bfloat16)],
        compiler_params=pltpu.CompilerParams(
            dimension_semantics=("arbitrary", "arbitrary"),
            vmem_limit_bytes=V7X_VMEM_LIMIT_BYTES,
        ),
        name="dense_ffn",
    )(x.reshape(B * S, D), gain.reshape(1, D), shift[:, None, :], scale[:, None, :], gate2[:, None, :],
      w_gu.astype(jnp.bfloat16)[None], w_gu.astype(jnp.bfloat16)[None], w_dn.astype(jnp.bfloat16)[None])
    return out.reshape(B, S, D)


def _row_copies_wait(src_ref, dst_ref, sem, n_rows):
    pltpu.make_async_copy(src_ref.at[pl.ds(0, n_rows)], dst_ref.at[pl.ds(0, n_rows)], sem).wait()


def _dispatch_body(dest_ref, x_ref, gain_ref, shift_ref, scale_ref, rows_in_ref, rows_ref, h_ref, sem):
    del rows_in_ref
    h_ref[...] = _norm_modulate(x_ref[...], gain_ref[...], shift_ref[0], scale_ref[0])

    def issue(r, carry):
        for k in range(TOP_K):
            pltpu.make_async_copy(h_ref.at[pl.ds(r, 1)], rows_ref.at[pl.ds(dest_ref[TOP_K * r + k], 1)],
                                  sem).start()
        return carry

    lax.fori_loop(0, TOK_TILE, issue, 0, unroll=DMA_ISSUE_UNROLL)
    for k in range(TOP_K):
        _row_copies_wait(h_ref, rows_ref, sem, TOK_TILE)


def _moe_dispatch(x, gain, shift, scale, dest, n_rows):
    B, S, D = x.shape
    T = B * S
    tiles_per_batch = S // TOK_TILE
    per_batch = pl.BlockSpec((1, 1, D), lambda i: (i // tiles_per_batch, 0, 0))
    return pl.pallas_call(
        _dispatch_body,
        grid=(T // TOK_TILE,),
        in_specs=[
            pl.BlockSpec((TOP_K * TOK_TILE,), lambda i: (i,), memory_space=pltpu.SMEM),
            pl.BlockSpec((TOK_TILE, D), lambda i: (i, 0)),
            pl.BlockSpec((1, D), lambda i: (0, 0)),
            per_batch, per_batch,
            pl.BlockSpec(memory_space=pl.ANY),
        ],
        out_specs=pl.BlockSpec(memory_space=pl.ANY),
        out_shape=jax.ShapeDtypeStruct((n_rows, D), jnp.float32),
        scratch_shapes=[pltpu.VMEM((TOK_TILE, D), jnp.float32), pltpu.SemaphoreType.DMA(())],
        input_output_aliases={5: 0},
        compiler_params=pltpu.CompilerParams(dimension_semantics=("arbitrary",)),
        name="moe_dispatch",
    )(dest.reshape(T * TOP_K), x.reshape(T, D), gain.reshape(1, D), shift[:, None, :], scale[:, None, :],
      jnp.zeros((n_rows, D), jnp.float32))


def _combine_body(dest_ref, x_ref, w_ref, gate2_ref, y_ref, o_ref, buf_ref, sem):
    def issue(r, carry):
        for k in range(TOP_K):
            pltpu.make_async_copy(y_ref.at[pl.ds(dest_ref[TOP_K * r + k], 1)], buf_ref.at[k, pl.ds(r, 1)],
                                  sem).start()
        return carry

    lax.fori_loop(0, TOK_TILE, issue, 0, unroll=DMA_ISSUE_UNROLL)
    for k in range(TOP_K):
        _row_copies_wait(y_ref, buf_ref.at[k], sem, TOK_TILE)
    w = w_ref[...]
    f = w[:, 0:1] * buf_ref[0] + w[:, 1:2] * buf_ref[1]
    o_ref[...] = x_ref[...] + gate2_ref[0] * f


def _moe_combine(x, top_w, gate2, y, dest):
    B, S, D = x.shape
    T = B * S
    tiles_per_batch = S // TOK_TILE
    out = pl.pallas_call(
        _combine_body,
        grid=(T // TOK_TILE,),
        in_specs=[
            pl.BlockSpec((TOP_K * TOK_TILE,), lambda i: (i,), memory_space=pltpu.SMEM),
            pl.BlockSpec((TOK_TILE, D), lambda i: (i, 0)),
            pl.BlockSpec((TOK_TILE, TOP_K), lambda i: (i, 0)),
            pl.BlockSpec((1, 1, D), lambda i: (i // tiles_per_batch, 0, 0)),
            pl.BlockSpec(memory_space=pl.ANY),
        ],
        out_specs=pl.BlockSpec((TOK_TILE, D), lambda i: (i, 0)),
        out_shape=jax.ShapeDtypeStruct((T, D), jnp.float32),
        scratch_shapes=[pltpu.VMEM((TOP_K, TOK_TILE, D), jnp.float32), pltpu.SemaphoreType.DMA(())],
        compiler_params=pltpu.CompilerParams(dimension_semantics=("arbitrary",)),
        name="moe_combine",
    )(dest.reshape(T * TOP_K), x.reshape(T, D), top_w, gate2[:, None, :], y)
    return out.reshape(B, S, D)


def _bias_tables(rel_bias):
    x0 = (N_TOEP - 1) * QB
    width = x0 + 2 * QB
    period = width + QB
    n = np.arange(period)
    n = np.where(n < width, n, n - period)
    by_dist = rel_bias.astype(jnp.float32)[_rel_bucket(jnp.asarray(np.maximum(x0 - n, 0)))].T
    strip = jnp.tile(by_dist, (1, QB))[:, :QB * (period - 1)].reshape(N_HEADS, QB, period - 1)[:, :, :width]
    far = rel_bias.astype(jnp.float32)[N_BUCKETS - 1] * LOG2_E

    i = np.arange(QB)[:, None]
    toep = jnp.stack([strip[:, :, x0 - QB * m:x0 - QB * m + QB] for m in range(-1, N_TOEP - 1)], axis=1)
    d_toep = QB * np.arange(-1, N_TOEP - 1)[:, None, None] + i[None] - np.arange(QB)[None, None, :]
    toep = jnp.where(d_toep >= 0, toep * LOG2_E, NEG_INF)
    toep = toep.reshape(N_KV_GROUPS, HEADS_PER_GROUP, N_TOEP, QB, QB).transpose(0, 2, 1, 3, 4)
    d_win = WINDOW + i - np.arange(WINDOW + QB)[None, :]
    win = jnp.where((d_win >= 0) & (d_win < WINDOW), strip[:, :, x0 - WINDOW:x0 + QB] * LOG2_E, NEG_INF)
    win = win.reshape(N_KV_GROUPS, HEADS_PER_GROUP, QB, WINDOW + QB)
    bands = []
    for par in range(2):
        off = x0 - QB * par - D_STRIDE * BAND_LEFT + (L_CMP - 1)
        c_first = -(off // D_STRIDE)
        cols = strip[:, :, D_STRIDE * c_first + off::D_STRIDE][:, :, :128 - c_first] * LOG2_E
        left = jnp.broadcast_to(far[:, None, None], (N_HEADS, QB, c_first))
        right = jnp.zeros((N_HEADS, QB, 128 - c_first - cols.shape[2]), jnp.float32)
        d_band = QB * par + i - D_STRIDE * (np.arange(128)[None, :] - BAND_LEFT) - (L_CMP - 1)
        assert (d_band[:, 128 - right.shape[2]:] < 0).all()
        bands.append(jnp.where(d_band >= 0, jnp.concatenate([left, cols, right], axis=-1), NEG_INF))
    band = jnp.stack(bands).reshape(2, N_KV_GROUPS, HEADS_PER_GROUP, QB, 128)
    return toep, win, band, far


def _slc_map_matrix(n_cmp_cols):
    w = np.zeros((n_cmp_cols, NBLK), np.float32)
    for jb in range(NBLK):
        for mm in range(R_SLC):
            for nn in range(R_CMP):
                k = R_SLC * jb + mm - nn
                if 0 <= k < n_cmp_cols:
                    w[k, jb] += 1.0
    return w


def _dot_nt(a, b):
    return lax.dot_general(a, b, (((1,), (1,)), ((), ())), preferred_element_type=jnp.float32)


def _cmp_select_body(far_ref, q_ref, gate_ref, kc_ref, vc_ref, band_ref, wmap_ref, oc_ref, sel_ref,
                     *, n_far):
    qi = pl.program_id(1)
    par = qi % 2
    band0 = pl.multiple_of(16 * (qi // 2 + 1), 16)
    first_band_blk = band0 - CMP_PAD

    col = lax.broadcasted_iota(jnp.int32, (QB, 128), 1)
    row = lax.broadcasted_iota(jnp.int32, (QB, 128), 0)
    neg_pad = jnp.where(first_band_blk + col >= 0, 0.0, NEG_INF)
    far_col = lax.broadcasted_iota(jnp.int32, (1, n_far), 1)
    neg_far = jnp.where(far_col < first_band_blk, 0.0, NEG_INF)

    t = qi * QB + row
    cur = t // L_SLC
    forced = (col == 0) | (col == cur) | (col == cur - 1)
    n_forced = 1 + (cur[:, :1] >= 1).astype(jnp.int32) + (cur[:, :1] >= 2).astype(jnp.int32)
    valid = col * L_SLC <= t

    q_all = q_ref[0]
    wmap_all = jnp.concatenate([wmap_ref[CMP_PAD:CMP_PAD + n_far, :], wmap_ref[pl.ds(band0, 128), :]], axis=0)
    colf = col.astype(jnp.float32)

    def take_best(score, active=None):
        best = jnp.max(score, axis=-1, keepdims=True)
        first = jnp.min(jnp.where(score == best, colf, float(NBLK)), axis=-1, keepdims=True)
        hit = colf == first
        return jnp.where(hit if active is None else hit & active, -jnp.inf, score)

    scores = []
    oc_heads = []
    for g in range(N_KV_GROUPS):
        q4 = jnp.concatenate(
            [q_all[:, (g * HEADS_PER_GROUP + h) * HEAD_DIM:(g * HEADS_PER_GROUP + h + 1) * HEAD_DIM]
             for h in range(HEADS_PER_GROUP)], axis=0)
        k_far = kc_ref[0, g, CMP_PAD:CMP_PAD + n_far, :]
        k_band = kc_ref[0, g, pl.ds(band0, 128), :]
        v_far = vc_ref[0, g, CMP_PAD:CMP_PAD + n_far, :]
        v_band = vc_ref[0, g, pl.ds(band0, 128), :]
        s_far = _dot_nt(q4, k_far).reshape(HEADS_PER_GROUP, QB, n_far)
        s_band = _dot_nt(q4, k_band).reshape(HEADS_PER_GROUP, QB, 128)
        v_all = jnp.concatenate([v_far, v_band], axis=0)
        imp = jnp.zeros((QB, n_far + 128), jnp.float32)
        for h in range(HEADS_PER_GROUP):
            hh = g * HEADS_PER_GROUP + h
            s = jnp.concatenate([s_far[h] + (far_ref[hh] + neg_far),
                                 s_band[h] + band_ref[par, g, h] + neg_pad], axis=-1)
            m = jnp.maximum(jnp.max(s, axis=-1, keepdims=True), 1e-10 * NEG_INF)
            p = jnp.exp2(s - m)
            l = jnp.sum(p, axis=-1, keepdims=True)
            p = p * jnp.where(l > 0.0, 1.0 / l, 0.0)
            imp = imp + p
            o = jnp.dot(p.astype(jnp.bfloat16), v_all, preferred_element_type=jnp.float32)
            oc_heads.append(o * gate_ref[0, :, 3 * hh:3 * hh + 1])
        p_slc = jnp.zeros((QB, NBLK), jnp.float32)
        rest = imp
        for _ in range(3):
            term = rest.astype(jnp.bfloat16)
            p_slc = p_slc + jnp.dot(term, wmap_all, preferred_element_type=jnp.float32)
            rest = rest - term.astype(jnp.float32)
        score = jnp.where(forced, -jnp.inf, jnp.where(valid, p_slc, -SEL_FORCE))
        for _ in range(N_SEL - MAX_FORCED):
            score = take_best(score)
        scores.append(score)
    oc_ref[0] = jnp.concatenate(oc_heads, axis=-1)

    def early_rows(scores):
        for extra in range(MAX_FORCED - 1):
            scores = tuple(take_best(s, MAX_FORCED - n_forced > extra) for s in scores)
        return scores

    scores = lax.cond(qi == 0, early_rows, lambda s: s, tuple(scores))
    for g in range(N_KV_GROUPS):
        sel_ref[0, g] = jnp.where(scores[g] == -jnp.inf, 0.0, NEG_INF).astype(jnp.bfloat16)


def _cmp_select(q, gates, kc_pad, vc_pad, band, far, wmap):
    B, S, _ = q.shape
    n_far = S // D_STRIDE
    n_pad = kc_pad.shape[2]
    grid = (B, S // QB)
    return pl.pallas_call(
        functools.partial(_cmp_select_body, n_far=n_far),
        grid=grid,
        in_specs=[
            pl.BlockSpec(memory_space=pltpu.SMEM),
            pl.BlockSpec((1, QB, N_HEADS * HEAD_DIM), lambda b, i: (b, i, 0)),
            pl.BlockSpec((1, QB, LANES), lambda b, i: (b, i, 0)),
            pl.BlockSpec((1, N_KV_GROUPS, n_pad, HEAD_DIM), lambda b, i: (b, 0, 0, 0)),
            pl.BlockSpec((1, N_KV_GROUPS, n_pad, HEAD_DIM), lambda b, i: (b, 0, 0, 0)),
            pl.BlockSpec((2, N_KV_GROUPS, HEADS_PER_GROUP, QB, 128), lambda b, i: (0, 0, 0, 0, 0)),
            pl.BlockSpec((n_pad, NBLK), lambda b, i: (0, 0)),
        ],
        out_specs=[
            pl.BlockSpec((1, QB, N_HEADS * HEAD_DIM), lambda b, i: (b, i, 0)),
            pl.BlockSpec((1, N_KV_GROUPS, QB, NBLK), lambda b, i: (b, 0, i, 0)),
        ],
        out_shape=[
            jax.ShapeDtypeStruct((B, S, N_HEADS * HEAD_DIM), jnp.float32),
            jax.ShapeDtypeStruct((B, N_KV_GROUPS, S, NBLK), jnp.bfloat16),
        ],
        compiler_params=pltpu.CompilerParams(
            dimension_semantics=("arbitrary", "arbitrary"),
            vmem_limit_bytes=V7X_VMEM_LIMIT_BYTES,
        ),
        name="nsa_cmp_select",
    )(far, q, gates, kc_pad, vc_pad, band, wmap)


def _win_sel_body(q_ref, gate_ref, sel_ref, oc_ref, ks_ref, vs_ref, kw_ref, vw_ref, toep_ref, win_ref,
                  out_ref, s0_scr, s1_scr, p0_scr, p1_scr, m_scr, acc_scr):
    qi = pl.program_id(2)
    q0 = pl.multiple_of(qi * QB, QB)
    qg = q_ref[0]
    q_heads = [qg[:, h * HEAD_DIM:(h + 1) * HEAD_DIM] for h in range(HEADS_PER_GROUP)]
    q4 = jnp.concatenate(q_heads, axis=0)

    kw = kw_ref[0, 0, :, pl.ds(q0, WINDOW + QB)]
    s_w = jnp.dot(q4, kw, preferred_element_type=jnp.float32)
    wcol = lax.broadcasted_iota(jnp.int32, (1, WINDOW + QB), 1)
    neg_left = jnp.where(q0 + wcol >= WINDOW, 0.0, NEG_INF)
    s_w = s_w.reshape(HEADS_PER_GROUP, QB, WINDOW + QB) + win_ref[0] + neg_left
    m_w = jnp.max(s_w, axis=-1, keepdims=True)
    p_w = jnp.exp2(s_w - m_w)
    o_w = jnp.dot(p_w.reshape(QROWS, WINDOW + QB).astype(jnp.bfloat16),
                  vw_ref[0, 0, pl.ds(q0, WINDOW + QB), :], preferred_element_type=jnp.float32)
    o_w = o_w.reshape(HEADS_PER_GROUP, QB, LANES)
    o_w = o_w[:, :, :HEAD_DIM] / o_w[:, :, HEAD_DIM:HEAD_DIM + 1]

    sel = sel_ref[0, 0]
    qa = jnp.concatenate([jnp.concatenate([sel] * HEADS_PER_GROUP, axis=0), q4], axis=-1)
    c_diag = qi // 2

    def scores(c):
        col = pl.multiple_of(c * KEY_TILE, KEY_TILE)
        s = jnp.dot(qa, ks_ref[0, 0, :, pl.ds(col, KEY_TILE)], preferred_element_type=jnp.float32)
        mm = qi - 2 * c
        bias = jnp.concatenate([toep_ref[0, jnp.clip(mm + 1, 0, N_TOEP - 1)],
                                toep_ref[0, jnp.clip(mm, 0, N_TOEP - 1)]], axis=-1)
        return s.reshape(HEADS_PER_GROUP, QB, KEY_TILE) + bias

    s_slots = (s0_scr, s1_scr)
    p_slots = (p0_scr, p1_scr)
    m_scr[...] = jnp.full(m_scr.shape, NEG_INF, jnp.float32)
    acc_scr[...] = jnp.zeros_like(acc_scr)
    p_slots[0][...] = jnp.zeros_like(p0_scr)
    s_slots[0][...] = scores(0)

    def weighted_values(p, col):
        return jnp.dot(p, vs_ref[0, 0, pl.ds(pl.multiple_of(col, KEY_TILE), KEY_TILE), :],
                       preferred_element_type=jnp.float32)

    def half_step(c, col_prev, cur):
        pv = weighted_values(p_slots[cur][...], col_prev)
        s_slots[1 - cur][...] = scores(c + 1)
        for h in range(HEADS_PER_GROUP):
            s = s_slots[cur][h]
            m_old = m_scr[h]
            m_new = jnp.maximum(m_old, jnp.max(s, axis=-1, keepdims=True))
            alpha = jnp.exp2(m_old - m_new)
            p = jnp.exp2(s - jnp.concatenate([m_new] * (KEY_TILE // LANES), axis=-1))
            p_slots[1 - cur][h * QB:(h + 1) * QB, :] = p.astype(jnp.bfloat16)
            m_scr[h] = m_new
            acc_scr[h * QB:(h + 1) * QB, :] = alpha * (acc_scr[h * QB:(h + 1) * QB, :]
                                                      + pv[h * QB:(h + 1) * QB, :])
        return c * KEY_TILE

    def unrolled_steps(j, col_prev):
        for u in range(SEL_UNROLL):
            col_prev = half_step(SEL_UNROLL * j + u, col_prev, u % 2)
        return col_prev

    col_last = lax.fori_loop(0, (c_diag + SEL_UNROLL) // SEL_UNROLL, unrolled_steps, 0)
    acc_s = (acc_scr[...] + weighted_values(p_slots[0][...], col_last)).reshape(HEADS_PER_GROUP, QB, LANES)
    o_s = acc_s[:, :, :HEAD_DIM] / acc_s[:, :, HEAD_DIM:HEAD_DIM + 1]

    outs = []
    for h in range(HEADS_PER_GROUP):
        g_s = gate_ref[0, 0, :, 3 * h + 1:3 * h + 2]
        g_w = gate_ref[0, 0, :, 3 * h + 2:3 * h + 3]
        outs.append(g_s * o_s[h] + g_w * o_w[h])
    out_ref[0] = oc_ref[0] + jnp.concatenate(outs, axis=-1)


def _win_sel(q, gates, selneg, oc, ks_aug, vs, kw_t, vw_pad, toep, win):
    B, S, _ = q.shape
    gw = HEADS_PER_GROUP * HEAD_DIM
    grid = (B, N_KV_GROUPS, S // QB)
    return pl.pallas_call(
        _win_sel_body,
        grid=grid,
        in_specs=[
            pl.BlockSpec((1, QB, gw), lambda b, g, i: (b, i, g)),
            pl.BlockSpec((1, 1, QB, 3 * HEADS_PER_GROUP), lambda b, g, i: (b, g, i, 0)),
            pl.BlockSpec((1, 1, QB, NBLK), lambda b, g, i: (b, g, i, 0)),
            pl.BlockSpec((1, QB, gw), lambda b, g, i: (b, i, g)),
            pl.BlockSpec((1, 1, NBLK + HEAD_DIM, ks_aug.shape[3]), lambda b, g, i: (b, g, 0, 0)),
            pl.BlockSpec((1, 1, vs.shape[2], LANES), lambda b, g, i: (b, g, 0, 0)),
            pl.BlockSpec((1, 1, HEAD_DIM, S + WINDOW), lambda b, g, i: (b, g, 0, 0)),
            pl.BlockSpec((1, 1, S + WINDOW, LANES), lambda b, g, i: (b, g, 0, 0)),
            pl.BlockSpec((1, N_TOEP, HEADS_PER_GROUP, QB, QB), lambda b, g, i: (g, 0, 0, 0, 0)),
            pl.BlockSpec((1, HEADS_PER_GROUP, QB, WINDOW + QB), lambda b, g, i: (g, 0, 0, 0)),
        ],
        out_specs=pl.BlockSpec((1, QB, gw), lambda b, g, i: (b, i, g)),
        out_shape=jax.ShapeDtypeStruct((B, S, N_HEADS * HEAD_DIM), jnp.float32),
        scratch_shapes=[
            pltpu.VMEM((HEADS_PER_GROUP, QB, KEY_TILE), jnp.float32),
            pltpu.VMEM((HEADS_PER_GROUP, QB, KEY_TILE), jnp.float32),
            pltpu.VMEM((QROWS, KEY_TILE), jnp.bfloat16),
            pltpu.VMEM((QROWS, KEY_TILE), jnp.bfloat16),
            pltpu.VMEM((HEADS_PER_GROUP, QB, LANES), jnp.float32),
            pltpu.VMEM((QROWS, LANES), jnp.float32),
        ],
        compiler_params=pltpu.CompilerParams(
            dimension_semantics=("arbitrary", "arbitrary", "arbitrary"),
            vmem_limit_bytes=V7X_VMEM_LIMIT_BYTES,
        ),
        name="nsa_win_sel",
    )(q, gates, selneg, oc, ks_aug, vs, kw_t, vw_pad, toep, win)


def _nsa_attention(q, gates, kc, vc, kvsw, rel_bias):
    B, S, _ = q.shape
    assert S % KEY_TILE == 0 and S // L_SLC <= NBLK
    bf = jnp.bfloat16
    n_far = S // D_STRIDE
    toep, win, band, far = _bias_tables(rel_bias)
    cpad = ((0, 0), (0, 0), (CMP_PAD, n_far - kc.shape[2] + CMP_PAD), (0, 0))
    kc_pad = jnp.pad(kc, cpad).astype(bf)
    vc_pad = jnp.pad(vc, cpad).astype(bf)
    wmap = jnp.asarray(np.pad(_slc_map_matrix(n_far), ((CMP_PAD, CMP_PAD), (0, 0))), bf)
    oc, selneg = _cmp_select(q, gates, kc_pad, vc_pad, band, far, wmap)
    ks, vs, kw, vw = (kvsw.reshape(B, S, 4, N_KV_GROUPS, HEAD_DIM)[:, :, n] for n in range(4))
    blk_onehot = jnp.asarray((np.arange(S)[None, :] // L_SLC == np.arange(NBLK)[:, None]), bf)
    ks_aug = jnp.concatenate(
        [jnp.broadcast_to(blk_onehot, (B, N_KV_GROUPS, NBLK, S)), ks.transpose(0, 2, 3, 1)], axis=2)
    tail = SEL_UNROLL * KEY_TILE
    ks_aug = jnp.pad(ks_aug, ((0, 0), (0, 0), (0, 0), (0, tail)))
    kw_t = jnp.pad(kw.transpose(0, 2, 3, 1), ((0, 0), (0, 0), (0, 0), (WINDOW, 0)))

    def with_ones(v):
        v = v.transpose(0, 2, 1, 3)
        ones = jnp.ones(v.shape[:-1] + (1,), v.dtype)
        return jnp.pad(jnp.concatenate([v, ones], axis=-1), ((0, 0),) * 3 + ((0, LANES - HEAD_DIM - 1),))

    vw_aug = jnp.pad(with_ones(vw), ((0, 0), (0, 0), (WINDOW, 0), (0, 0)))
    vs_aug = jnp.pad(with_ones(vs), ((0, 0), (0, 0), (0, tail), (0, 0)))
    gates_g = gates[:, :, :3 * N_HEADS].reshape(B, S, N_KV_GROUPS, 3 * HEADS_PER_GROUP).swapaxes(1, 2)
    return _win_sel(q, gates_g, selneg, oc, ks_aug, vs_aug, kw_t, vw_aug, toep, win)


def _pool_layer_body(x_ref, halo_ref, gain_ref, shift_ref, scale_ref, gate_ref, pscale_ref, w_ref, o_ref):
    i = pl.program_id(1)
    x = x_ref[0]
    gain, shift, scale = gain_ref[...], shift_ref[0], scale_ref[0]
    h = _norm_modulate(x, gain, shift, scale)
    halo = _norm_modulate(halo_ref[0], gain, shift, scale) * (i > 0).astype(jnp.float32)
    hx = jnp.concatenate([halo, h], axis=0)
    t = i * POOL_TILE + lax.broadcasted_iota(jnp.int32, (POOL_TILE, 1), 0)
    outs = []
    run = hx
    width = 1
    for g, w in enumerate(POOL_WINDOWS):
        while width < w:
            run = run + pltpu.roll(run, width, axis=0)
            width *= 2
        lanes = slice(0, POOL_GROUP_DIM)
        cnt = jnp.minimum(t + 1, w).astype(jnp.float32)
        mix = run[POOL_HALO:, lanes] / cnt - h[:, g * POOL_GROUP_DIM:(g + 1) * POOL_GROUP_DIM]
        outs.append(jnp.dot(mix.astype(jnp.bfloat16), w_ref[g], preferred_element_type=jnp.float32))
        run = run[:, POOL_GROUP_DIM:]
    y = jnp.concatenate(outs, axis=-1) * pscale_ref[...]
    o_ref[0] = x + gate_ref[0] * y


def _pool_layer(x, gain, shift, scale, gate1, w_grp, pool_scale):
    B, S, D = x.shape
    assert S % POOL_TILE == 0 and POOL_HALO >= max(POOL_WINDOWS) - 1
    per_batch = pl.BlockSpec((1, 1, D), lambda b, i: (b, 0, 0))
    halo_blocks = POOL_TILE // POOL_HALO
    return pl.pallas_call(
        _pool_layer_body,
        grid=(B, S // POOL_TILE),
        in_specs=[
            pl.BlockSpec((1, POOL_TILE, D), lambda b, i: (b, i, 0)),
            pl.BlockSpec((1, POOL_HALO, D), lambda b, i: (b, jnp.maximum(i * halo_blocks - 1, 0), 0)),
            pl.BlockSpec((1, D), lambda b, i: (0, 0)),
            per_batch, per_batch, per_batch,
            pl.BlockSpec((1, D), lambda b, i: (0, 0)),
            pl.BlockSpec((N_POOL_GROUPS, POOL_GROUP_DIM, POOL_GROUP_DIM), lambda b, i: (0, 0, 0)),
        ],
        out_specs=pl.BlockSpec((1, POOL_TILE, D), lambda b, i: (b, i, 0)),
        out_shape=jax.ShapeDtypeStruct((B, S, D), jnp.float32),
        compiler_params=pltpu.CompilerParams(
            dimension_semantics=("arbitrary", "arbitrary"),
            vmem_limit_bytes=V7X_VMEM_LIMIT_BYTES,
        ),
        name="pool_layer",
    )(x, x, gain.reshape(1, D), shift[:, None, :], scale[:, None, :], gate1[:, None, :],
      pool_scale.reshape(1, D), w_grp.astype(jnp.bfloat16))


def _head_norm(v, seg_ref, gain):
    sq = v * v
    hi = sq.astype(jnp.bfloat16)
    lo = (sq - hi.astype(jnp.float32)).astype(jnp.bfloat16)
    ss = (jnp.dot(hi, seg_ref[...], preferred_element_type=jnp.float32)
          + jnp.dot(lo, seg_ref[...], preferred_element_type=jnp.float32))
    return v * lax.rsqrt(ss * (1.0 / HEAD_DIM) + EPS) * gain


def _qkv_proj_body(x_ref, gq_ref, shq_ref, scq_ref, gkv_ref, shkv_ref, sckv_ref, wq_ref, wg_ref, wkv_ref,
                   seg_ref, qgain_ref, kgain_ref, q_ref, gates_ref, kvc_ref, kvsw_ref):
    x = x_ref[...]
    xhat = x * lax.rsqrt(jnp.mean(x * x, axis=-1, keepdims=True) + EPS)
    hq = ((xhat * gq_ref[...]) * (1.0 + scq_ref[0]) + shq_ref[0]).astype(jnp.bfloat16)
    hkv = ((xhat * gkv_ref[...]) * (1.0 + sckv_ref[0]) + shkv_ref[0]).astype(jnp.bfloat16)
    pq = jnp.dot(hq, wq_ref[...], preferred_element_type=jnp.float32)
    gates_ref[...] = jax.nn.sigmoid(jnp.dot(hq, wg_ref[...], preferred_element_type=jnp.float32))
    pkv = jnp.dot(hkv, wkv_ref[...], preferred_element_type=jnp.float32)
    for g in range(N_KV_GROUPS):
        lanes = slice(g * GROUP_W, (g + 1) * GROUP_W)
        q_ref[:, lanes] = _head_norm(pq[:, lanes], seg_ref, qgain_ref[...]).astype(jnp.bfloat16)
    kvc_ref[...] = pkv[:, :2 * GROUP_W]
    k_s = _head_norm(pkv[:, 2 * GROUP_W:3 * GROUP_W], seg_ref, kgain_ref[0:1, :])
    k_w = _head_norm(pkv[:, 4 * GROUP_W:5 * GROUP_W], seg_ref, kgain_ref[1:2, :])
    kvsw_ref[...] = jnp.concatenate([k_s, pkv[:, 3 * GROUP_W:4 * GROUP_W], k_w, pkv[:, 5 * GROUP_W:]],
                                    axis=-1).astype(jnp.bfloat16)


def _qkv_proj(x, gain_q, shift_q, scale_q, gain_kv, shift_kv, scale_kv, w_qg, kv_w, q_gain, k_gain):
    B, S, D = x.shape
    T = B * S
    qd = N_HEADS * HEAD_DIM
    bf = jnp.bfloat16
    tiles_per_batch = S // ROW_TILE
    per_batch = pl.BlockSpec((1, 1, D), lambda i: (i // tiles_per_batch, 0, 0))
    const = lambda shape: pl.BlockSpec(shape, lambda i: (0,) * len(shape))
    seg = jnp.asarray(np.kron(np.eye(HEADS_PER_GROUP), np.ones((HEAD_DIM, HEAD_DIM))), bf)
    w_gate = jnp.pad(w_qg[:, qd:], ((0, 0), (0, LANES - 3 * N_HEADS))).astype(bf)
    qgain = jnp.tile(q_gain * HEAD_DIM ** -0.5, HEADS_PER_GROUP).reshape(1, GROUP_W)
    kgain = jnp.stack([jnp.tile(k_gain[1] * LOG2_E, N_KV_GROUPS), jnp.tile(k_gain[2] * LOG2_E, N_KV_GROUPS)])
    rows = lambda width: pl.BlockSpec((ROW_TILE, width), lambda i: (i, 0))
    return pl.pallas_call(
        _qkv_proj_body,
        grid=(T // ROW_TILE,),
        in_specs=[rows(D), const((1, D)), per_batch, per_batch, const((1, D)), per_batch, per_batch,
                  const((D, qd)), const((D, LANES)), const((D, 6 * GROUP_W)), const((GROUP_W, GROUP_W)),
                  const((1, GROUP_W)), const((2, GROUP_W))],
        out_specs=[rows(qd), rows(LANES), rows(2 * GROUP_W), rows(4 * GROUP_W)],
        out_shape=[jax.ShapeDtypeStruct((T, qd), bf), jax.ShapeDtypeStruct((T, LANES), jnp.float32),
                   jax.ShapeDtypeStruct((T, 2 * GROUP_W), jnp.float32),
                   jax.ShapeDtypeStruct((T, 4 * GROUP_W), bf)],
        compiler_params=pltpu.CompilerParams(
            dimension_semantics=("arbitrary",), vmem_limit_bytes=V7X_VMEM_LIMIT_BYTES),
        name="qkv_proj",
    )(x.reshape(T, D), gain_q.reshape(1, D), shift_q[:, None, :], scale_q[:, None, :],
      gain_kv.reshape(1, D), shift_kv[:, None, :], scale_kv[:, None, :],
      w_qg[:, :qd].astype(bf), w_gate, kv_w.astype(bf), seg, qgain, kgain)


def _compress_body(r_ref, pe_ref, w1_ref, w2_ref, gain_ref, o_ref):
    half = D_STRIDE * HEAD_DIM
    r = r_ref[0, 0, 0]
    top = jnp.dot(r, w1_ref[0, :half, :], preferred_element_type=jnp.float32)
    bot = jnp.dot(r, w1_ref[0, half:, :], preferred_element_type=jnp.float32)
    pe = jnp.dot(jnp.broadcast_to(pe_ref[0], (8, R_CMP * half)).astype(jnp.bfloat16), w1_ref[0],
                 preferred_element_type=jnp.float32)[0:1]
    n_chunks = r.shape[0]
    hidden = top + pltpu.roll(bot, n_chunks - 1, axis=0) + pe
    out = jnp.dot(jax.nn.gelu(hidden).astype(jnp.bfloat16), w2_ref[0], preferred_element_type=jnp.float32)
    normed = out * lax.rsqrt(jnp.mean(out * out, axis=-1, keepdims=True) + EPS) * gain_ref[...]
    o_ref[0, 0, 0] = jnp.where(pl.program_id(0) == 0, normed, out)


def _compressed_kv(kvc, B, S, cmp_pe_k, cmp_pe_v, cmp_k_w1, cmp_k_w2, cmp_v_w1, cmp_v_w2, k_gain):
    bf = jnp.bfloat16
    n_chunks = S // D_STRIDE
    half = D_STRIDE * HEAD_DIM
    hidden = cmp_k_w1.shape[1]
    r = kvc.astype(bf).reshape(B, n_chunks, D_STRIDE, 2, N_KV_GROUPS, HEAD_DIM)
    r = r.transpose(3, 0, 4, 1, 2, 5).reshape(2, B, N_KV_GROUPS, n_chunks, half)
    pe = jnp.stack([cmp_pe_k, cmp_pe_v]).reshape(2, 1, R_CMP * half)
    per_kind = lambda shape: pl.BlockSpec((1,) + shape, lambda kv, b, g: (kv,) + (0,) * len(shape))
    out = pl.pallas_call(
        _compress_body,
        grid=(2, B, N_KV_GROUPS),
        in_specs=[
            pl.BlockSpec((1, 1, 1, n_chunks, half), lambda kv, b, g: (kv, b, g, 0, 0)),
            per_kind((1, R_CMP * half)), per_kind((R_CMP * half, hidden)), per_kind((hidden, HEAD_DIM)),
            pl.BlockSpec((1, HEAD_DIM), lambda kv, b, g: (0, 0)),
        ],
        out_specs=pl.BlockSpec((1, 1, 1, n_chunks, HEAD_DIM), lambda kv, b, g: (kv, b, g, 0, 0)),
        out_shape=jax.ShapeDtypeStruct((2, B, N_KV_GROUPS, n_chunks, HEAD_DIM), jnp.float32),
        compiler_params=pltpu.CompilerParams(
            dimension_semantics=("arbitrary", "arbitrary", "arbitrary"),
            vmem_limit_bytes=V7X_VMEM_LIMIT_BYTES),
        name="cmp_mlp",
    )(r, pe, jnp.stack([cmp_k_w1, cmp_v_w1]).astype(bf), jnp.stack([cmp_k_w2, cmp_v_w2]).astype(bf),
      (k_gain[0] * LOG2_E).reshape(1, HEAD_DIM))
    n_cmp = n_chunks - R_CMP + 1
    return out[0, :, :, :n_cmp], out[1, :, :, :n_cmp]


def _oproj_router_body(a_ref, x_ref, gate_ref, wo_ref, gain_ref, shift_ref, scale_ref, wr_ref, br_ref,
                       o_ref, logit_ref):
    mix = jnp.dot(a_ref[...].astype(jnp.bfloat16), wo_ref[...], preferred_element_type=jnp.float32)
    x = x_ref[...] + gate_ref[0] * mix
    o_ref[...] = x
    h = _norm_modulate(x, gain_ref[...], shift_ref[0], scale_ref[0]).astype(jnp.bfloat16)
    logit_ref[...] = jnp.dot(h, wr_ref[...], preferred_element_type=jnp.float32) + br_ref[...]


def _oproj_router(attn, x, gate1, w_o, gain, shift, scale, w_router, b_router):
    B, S, D = x.shape
    T = B * S
    bf = jnp.bfloat16
    tiles_per_batch = S // ROW_TILE
    per_batch = pl.BlockSpec((1, 1, D), lambda i: (i // tiles_per_batch, 0, 0))
    const = lambda shape: pl.BlockSpec(shape, lambda i: (0,) * len(shape))
    rows = lambda width: pl.BlockSpec((ROW_TILE, width), lambda i: (i, 0))
    pad = ((0, 0), (0, LANES - N_EXPERTS))
    out, logits = pl.pallas_call(
        _oproj_router_body,
        grid=(T // ROW_TILE,),
        in_specs=[rows(D), rows(D), per_batch, const((D, D)), const((1, D)), per_batch, per_batch,
                  const((D, LANES)), const((1, LANES))],
        out_specs=[rows(D), rows(LANES)],
        out_shape=[jax.ShapeDtypeStruct((T, D), jnp.float32), jax.ShapeDtypeStruct((T, LANES), jnp.float32)],
        compiler_params=pltpu.CompilerParams(
            dimension_semantics=("arbitrary",), vmem_limit_bytes=V7X_VMEM_LIMIT_BYTES),
        name="oproj_router",
    )(attn.reshape(T, D), x.reshape(T, D), gate1[:, None, :], w_o.astype(bf), gain.reshape(1, D),
      shift[:, None, :], scale[:, None, :], jnp.pad(w_router, pad).astype(bf),
      jnp.pad(b_router.reshape(1, N_EXPERTS), pad))
    return out.reshape(B, S, D), logits[:, :N_EXPERTS]


def _route(logits):
    T = logits.shape[0]
    lane = jnp.arange(N_EXPERTS)[None, :]
    l0 = jnp.max(logits, axis=-1)
    e0 = jnp.argmax(logits, axis=-1)
    rest = jnp.where(lane == e0[:, None], -jnp.inf, logits)
    l1 = jnp.max(rest, axis=-1)
    e1 = jnp.argmax(rest, axis=-1)
    z = jnp.exp(l1 - l0)
    top_w = jnp.stack([1.0 / (1.0 + z), z / (1.0 + z)], axis=-1)
    oh = [(lane == e[:, None]).astype(jnp.int32) for e in (e0, e1)]
    cnt = oh[0] + oh[1]
    before = jnp.cumsum(cnt, axis=0) - cnt
    counts = before[-1] + cnt[-1]
    pcounts = (counts + ROW_TILE - 1) // ROW_TILE * ROW_TILE
    pends = jnp.cumsum(pcounts)
    pstarts = pends - pcounts
    dest = jnp.stack([jnp.sum((pstarts[None, :] + before) * o, axis=-1) for o in oh], axis=-1)
    n_blocks = -(-(T * TOP_K + N_EXPERTS * (ROW_TILE - 1)) // ROW_TILE)
    blk_start = jnp.arange(n_blocks)[:, None] * ROW_TILE
    blk_e = jnp.minimum(jnp.sum(pends[None, :] <= blk_start, axis=-1), N_EXPERTS - 1).astype(jnp.int32)
    return top_w, dest.astype(jnp.int32), blk_e, n_blocks * ROW_TILE


def _moe_ffn(x, logits, gain, shift, scale, gate2, w_gu, w_dn):
    top_w, dest, blk_e, n_rows = _route(logits)
    rows = _moe_dispatch(x, gain, shift, scale, dest, n_rows)
    y = _grouped_swiglu(rows, blk_e, w_gu.astype(jnp.bfloat16), w_dn.astype(jnp.bfloat16),
                        ff_tile=FF_TILE_EXPERT)
    return _moe_combine(x, top_w, gate2, y, dest)


def kernel(x, c, ada_w, ada_b, norm_g, pool_w, pool_scale, q_w, q_gain, o_w, kv_ada_w, kv_ada_b, kv_norm_g, kv_w, cmp_pe_k, cmp_pe_v, cmp_k_w1, cmp_k_w2, cmp_v_w1, cmp_v_w2, k_gain, rel_bias, ffn_gu, ffn_dn, router_w, router_b, exp_gu, exp_dn):
    assert DEPTH == 2 and N_A_LAYERS == 1
    B, S, D = x.shape
    silu_c = jax.nn.silu(c)
    sh1, sc1, g1, sh2, sc2, g2 = jnp.split(silu_c @ ada_w[0] + ada_b[0], 6, axis=-1)
    x = _pool_layer(x, norm_g[0, 0], sh1, sc1, g1, pool_w[0], pool_scale[0])
    x = _dense_ffn(x, norm_g[0, 1], sh2, sc2, g2, ffn_gu[0], ffn_dn[0])
    sh1, sc1, g1, sh2, sc2, g2 = jnp.split(silu_c @ ada_w[1] + ada_b[1], 6, axis=-1)
    sh_kv, sc_kv = jnp.split(silu_c @ kv_ada_w + kv_ada_b, 2, axis=-1)
    q, gates, kvc, kvsw = _qkv_proj(x, norm_g[1, 0], sh1, sc1, kv_norm_g, sh_kv, sc_kv, q_w[0], kv_w,
                                    q_gain[0], k_gain)
    kc, vc = _compressed_kv(kvc, B, S, cmp_pe_k, cmp_pe_v, cmp_k_w1, cmp_k_w2, cmp_v_w1, cmp_v_w2, k_gain)
    attn = _nsa_attention(q.reshape(B, S, -1), gates.reshape(B, S, -1), kc, vc, kvsw.reshape(B, S, -1),
                          rel_bias)
    x, logits = _oproj_router(attn, x, g1, o_w[0], norm_g[1, 1], sh2, sc2, router_w[0], router_b[0])
    return _moe_ffn(x, logits, norm_g[1, 1], sh2, sc2, g2, exp_gu[0], exp_dn[0])
```

```python
import functools
import math

import jax
import jax.numpy as jnp
import numpy as np
from jax import lax
from jax.experimental import pallas as pl
from jax.experimental.pallas import tpu as pltpu

D_MODEL = 1024
DEPTH = 2
N_A_LAYERS = DEPTH // 2
POOL_WINDOWS = (2, 4, 8, 16)
N_POOL_GROUPS = len(POOL_WINDOWS)
POOL_GROUP_DIM = D_MODEL // N_POOL_GROUPS
HEAD_DIM = 64
N_HEADS = D_MODEL // HEAD_DIM
N_KV_GROUPS = 4
HEADS_PER_GROUP = N_HEADS // N_KV_GROUPS
L_CMP = 32
D_STRIDE = 16
L_SLC = 64
N_SEL = 16
WINDOW = 512
R_CMP = L_CMP // D_STRIDE
R_SLC = L_SLC // D_STRIDE
N_BUCKETS = 32
REL_EXACT = N_BUCKETS // 2
MAX_DISTANCE = 1024
N_EXPERTS = 8
TOP_K = 2
EPS = 1e-6
NEG_INF = -1e30
SEL_FORCE = 1e6
LOG2_E = math.log2(math.e)

V7X_VMEM_LIMIT_BYTES = 48 * 1024 * 1024
LANES = 128
ROW_TILE = 512
TOK_TILE = 256
DMA_ISSUE_UNROLL = 8
MAX_FORCED = 3
FF_TILE_DENSE = 1408
FF_TILE_EXPERT = 896
POOL_TILE = 512
POOL_HALO = 16

QB = 128
KEY_TILE = 2 * QB
SEL_UNROLL = 4
QROWS = HEADS_PER_GROUP * QB
GROUP_W = HEADS_PER_GROUP * HEAD_DIM
NBLK = 128
CMP_PAD = 128
FAR_DIST = MAX_DISTANCE
N_TOEP = FAR_DIST // QB + 3
BAND_LEFT = CMP_PAD - 16
assert BAND_LEFT * D_STRIDE + (L_CMP - 1) - 2 * QB >= FAR_DIST


def _rms_norm(x, g):
    xf = x.astype(jnp.float32)
    y = xf * lax.rsqrt(jnp.mean(xf * xf, axis=-1, keepdims=True) + EPS)
    return (y * g.astype(jnp.float32)).astype(x.dtype)


def _modulate(h, shift, scale):
    return h * (1 + scale[:, None, :]) + shift[:, None, :]


def _rel_bucket(dist):
    d = jnp.maximum(dist, 0)
    ratio = jnp.maximum(d, REL_EXACT).astype(jnp.float32) / REL_EXACT
    large = REL_EXACT + (jnp.log(ratio) / math.log(MAX_DISTANCE / REL_EXACT)
                         * (N_BUCKETS - REL_EXACT)).astype(jnp.int32)
    return jnp.where(d < REL_EXACT, d, jnp.minimum(large, N_BUCKETS - 1))


def _norm_modulate(x, gain, shift, scale):
    y = x * lax.rsqrt(jnp.mean(x * x, axis=-1, keepdims=True) + EPS)
    return (y * gain) * (1.0 + scale) + shift


def _swiglu_step(xb_ref, wg_ref, wu_ref, wd_ref, acc_ref, j):
    @pl.when(j == 0)
    def _():
        acc_ref[...] = jnp.zeros_like(acc_ref)

    x = xb_ref[...]
    gate = jnp.dot(x, wg_ref[0], preferred_element_type=jnp.float32)
    up = jnp.dot(x, wu_ref[0], preferred_element_type=jnp.float32)
    act = (gate * jax.nn.sigmoid(gate) * up).astype(jnp.bfloat16)
    acc_ref[...] += jnp.dot(act, wd_ref[0], preferred_element_type=jnp.float32)


def _grouped_swiglu_body(blk_e_ref, x_ref, wg_ref, wu_ref, wd_ref, o_ref, acc_ref, xb_ref, *, n_ff_steps):
    del blk_e_ref
    j = pl.program_id(1)

    @pl.when(j == 0)
    def _():
        xb_ref[...] = x_ref[...].astype(jnp.bfloat16)

    _swiglu_step(xb_ref, wg_ref, wu_ref, wd_ref, acc_ref, j)

    @pl.when(j == n_ff_steps - 1)
    def _():
        o_ref[...] = acc_ref[...]


def _grouped_swiglu(x_rows, blk_e, w_gu, w_dn, *, ff_tile):
    n_rows, d = x_rows.shape
    d_ff = w_dn.shape[1]
    assert n_rows % ROW_TILE == 0 and d_ff % ff_tile == 0
    n_ff_steps = d_ff // ff_tile
    grid = (n_rows // ROW_TILE, n_ff_steps)
    return pl.pallas_call(
        functools.partial(_grouped_swiglu_body, n_ff_steps=n_ff_steps),
        grid_spec=pltpu.PrefetchScalarGridSpec(
            num_scalar_prefetch=1,
            grid=grid,
            in_specs=[
                pl.BlockSpec((ROW_TILE, d), lambda i, j, e: (i, 0)),
                pl.BlockSpec((1, d, ff_tile), lambda i, j, e: (e[i], 0, j)),
                pl.BlockSpec((1, d, ff_tile), lambda i, j, e: (e[i], 0, j + n_ff_steps)),
                pl.BlockSpec((1, ff_tile, d), lambda i, j, e: (e[i], j, 0)),
            ],
            out_specs=pl.BlockSpec((ROW_TILE, d), lambda i, j, e: (i, 0)),
            scratch_shapes=[pltpu.VMEM((ROW_TILE, d), jnp.float32), pltpu.VMEM((ROW_TILE, d), jnp.bfloat16)],
        ),
        out_shape=jax.ShapeDtypeStruct((n_rows, d), jnp.float32),
        compiler_params=pltpu.CompilerParams(
            dimension_semantics=("arbitrary", "arbitrary"),
            vmem_limit_bytes=V7X_VMEM_LIMIT_BYTES,
        ),
        name="grouped_swiglu",
    )(blk_e, x_rows, w_gu, w_gu, w_dn)


def _dense_ffn_body(x_ref, gain_ref, shift_ref, scale_ref, gate2_ref, wg_ref, wu_ref, wd_ref, o_ref,
                    acc_ref, xb_ref, *, n_ff_steps):
    j = pl.program_id(1)

    @pl.when(j == 0)
    def _():
        xb_ref[...] = _norm_modulate(x_ref[...], gain_ref[...], shift_ref[0], scale_ref[0]).astype(jnp.bfloat16)

    _swiglu_step(xb_ref, wg_ref, wu_ref, wd_ref, acc_ref, j)

    @pl.when(j == n_ff_steps - 1)
    def _():
        o_ref[...] = x_ref[...] + gate2_ref[0] * acc_ref[...]


def _dense_ffn(x, gain, shift, scale, gate2, w_gu, w_dn):
    B, S, D = x.shape
    d_ff = w_dn.shape[0]
    n_ff_steps = d_ff // FF_TILE_DENSE
    tiles_per_batch = S // ROW_TILE
    per_batch = pl.BlockSpec((1, 1, D), lambda i, j: (i // tiles_per_batch, 0, 0))
    out = pl.pallas_call(
        functools.partial(_dense_ffn_body, n_ff_steps=n_ff_steps),
        grid=(B * S // ROW_TILE, n_ff_steps),
        in_specs=[
            pl.BlockSpec((ROW_TILE, D), lambda i, j: (i, 0)),
            pl.BlockSpec((1, D), lambda i, j: (0, 0)),
            per_batch, per_batch, per_batch,
            pl.BlockSpec((1, D, FF_TILE_DENSE), lambda i, j: (0, 0, j)),
            pl.BlockSpec((1, D, FF_TILE_DENSE), lambda i, j: (0, 0, j + n_ff_steps)),
            pl.BlockSpec((1, FF_TILE_DENSE, D), lambda i, j: (0, j, 0)),
        ],
        out_specs=pl.BlockSpec((ROW_TILE, D), lambda i, j: (i, 0)),
        out_shape=jax.ShapeDtypeStruct((B * S, D), jnp.float32),
        scratch_shapes=[pltpu.VMEM((ROW_TILE, D), jnp.float32), pltpu.VMEM((ROW_TILE, D), jnp.bfloat16)],
        compiler_params=pltpu.CompilerParams(
            dimension_semantics=("arbitrary", "arbitrary"),
            vmem_limit_bytes=V7X_VMEM_LIMIT_BYTES,
        ),
        name="dense_ffn",
    )(x.reshape(B * S, D), gain.reshape(1, D), shift[:, None, :], scale[:, None, :], gate2[:, None, :],
      w_gu.astype(jnp.bfloat16)[None], w_gu.astype(jnp.bfloat16)[None], w_dn.astype(jnp.bfloat16)[None])
    return out.reshape(B, S, D)


def _row_copies_wait(src_ref, dst_ref, sem, n_rows):
    pltpu.make_async_copy(src_ref.at[pl.ds(0, n_rows)], dst_ref.at[pl.ds(0, n_rows)], sem).wait()


def _dispatch_body(dest_ref, x_ref, gain_ref, shift_ref, scale_ref, rows_in_ref, rows_ref, h_ref, sem):
    del rows_in_ref
    h_ref[...] = _norm_modulate(x_ref[...], gain_ref[...], shift_ref[0], scale_ref[0])

    def issue(r, carry):
        for k in range(TOP_K):
            pltpu.make_async_copy(h_ref.at[pl.ds(r, 1)], rows_ref.at[pl.ds(dest_ref[TOP_K * r + k], 1)],
                                  sem).start()
        return carry

    lax.fori_loop(0, TOK_TILE, issue, 0, unroll=DMA_ISSUE_UNROLL)
    for k in range(TOP_K):
        _row_copies_wait(h_ref, rows_ref, sem, TOK_TILE)


def _moe_dispatch(x, gain, shift, scale, dest, n_rows):
    B, S, D = x.shape
    T = B * S
    tiles_per_batch = S // TOK_TILE
    per_batch = pl.BlockSpec((1, 1, D), lambda i: (i // tiles_per_batch, 0, 0))
    return pl.pallas_call(
        _dispatch_body,
        grid=(T // TOK_TILE,),
        in_specs=[
            pl.BlockSpec((TOP_K * TOK_TILE,), lambda i: (i,), memory_space=pltpu.SMEM),
            pl.BlockSpec((TOK_TILE, D), lambda i: (i, 0)),
            pl.BlockSpec((1, D), lambda i: (0, 0)),
            per_batch, per_batch,
            pl.BlockSpec(memory_space=pl.ANY),
        ],
        out_specs=pl.BlockSpec(memory_space=pl.ANY),
        out_shape=jax.ShapeDtypeStruct((n_rows, D), jnp.float32),
        scratch_shapes=[pltpu.VMEM((TOK_TILE, D), jnp.float32), pltpu.SemaphoreType.DMA(())],
        input_output_aliases={5: 0},
        compiler_params=pltpu.CompilerParams(dimension_semantics=("arbitrary",)),
        name="moe_dispatch",
    )(dest.reshape(T * TOP_K), x.reshape(T, D), gain.reshape(1, D), shift[:, None, :], scale[:, None, :],
      jnp.zeros((n_rows, D), jnp.float32))


def _combine_body(dest_ref, x_ref, w_ref, gate2_ref, y_ref, o_ref, buf_ref, sem):
    def issue(r, carry):
        for k in range(TOP_K):
            pltpu.make_async_copy(y_ref.at[pl.ds(dest_ref[TOP_K * r + k], 1)], buf_ref.at[k, pl.ds(r, 1)],
                                  sem).start()
        return carry

    lax.fori_loop(0, TOK_TILE, issue, 0, unroll=DMA_ISSUE_UNROLL)
    for k in range(TOP_K):
        _row_copies_wait(y_ref, buf_ref.at[k], sem, TOK_TILE)
    w = w_ref[...]
    f = w[:, 0:1] * buf_ref[0] + w[:, 1:2] * buf_ref[1]
    o_ref[...] = x_ref[...] + gate2_ref[0] * f


def _moe_combine(x, top_w, gate2, y, dest):
    B, S, D = x.shape
    T = B * S
    tiles_per_batch = S // TOK_TILE
    out = pl.pallas_call(
        _combine_body,
        grid=(T // TOK_TILE,),
        in_specs=[
            pl.BlockSpec((TOP_K * TOK_TILE,), lambda i: (i,), memory_space=pltpu.SMEM),
            pl.BlockSpec((TOK_TILE, D), lambda i: (i, 0)),
            pl.BlockSpec((TOK_TILE, TOP_K), lambda i: (i, 0)),
            pl.BlockSpec((1, 1, D), lambda i: (i // tiles_per_batch, 0, 0)),
            pl.BlockSpec(memory_space=pl.ANY),
        ],
        out_specs=pl.BlockSpec((TOK_TILE, D), lambda i: (i, 0)),
        out_shape=jax.ShapeDtypeStruct((T, D), jnp.float32),
        scratch_shapes=[pltpu.VMEM((TOP_K, TOK_TILE, D), jnp.float32), pltpu.SemaphoreType.DMA(())],
        compiler_params=pltpu.CompilerParams(dimension_semantics=("arbitrary",)),
        name="moe_combine",
    )(dest.reshape(T * TOP_K), x.reshape(T, D), top_w, gate2[:, None, :], y)
    return out.reshape(B, S, D)


def _bias_tables(rel_bias):
    x0 = (N_TOEP - 1) * QB
    width = x0 + 2 * QB
    period = width + QB
    n = np.arange(period)
    n = np.where(n < width, n, n - period)
    by_dist = rel_bias.astype(jnp.float32)[_rel_bucket(jnp.asarray(np.maximum(x0 - n, 0)))].T
    strip = jnp.tile(by_dist, (1, QB))[:, :QB * (period - 1)].reshape(N_HEADS, QB, period - 1)[:, :, :width]
    far = rel_bias.astype(jnp.float32)[N_BUCKETS - 1] * LOG2_E

    i = np.arange(QB)[:, None]
    toep = jnp.stack([strip[:, :, x0 - QB * m:x0 - QB * m + QB] for m in range(-1, N_TOEP - 1)], axis=1)
    d_toep = QB * np.arange(-1, N_TOEP - 1)[:, None, None] + i[None] - np.arange(QB)[None, None, :]
    toep = jnp.where(d_toep >= 0, toep * LOG2_E, NEG_INF)
    toep = toep.reshape(N_KV_GROUPS, HEADS_PER_GROUP, N_TOEP, QB, QB).transpose(0, 2, 1, 3, 4)
    d_win = WINDOW + i - np.arange(WINDOW + QB)[None, :]
    win = jnp.where((d_win >= 0) & (d_win < WINDOW), strip[:, :, x0 - WINDOW:x0 + QB] * LOG2_E, NEG_INF)
    win = win.reshape(N_KV_GROUPS, HEADS_PER_GROUP, QB, WINDOW + QB)
    bands = []
    for par in range(2):
        off = x0 - QB * par - D_STRIDE * BAND_LEFT + (L_CMP - 1)
        c_first = -(off // D_STRIDE)
        cols = strip[:, :, D_STRIDE * c_first + off::D_STRIDE][:, :, :128 - c_first] * LOG2_E
        left = jnp.broadcast_to(far[:, None, None], (N_HEADS, QB, c_first))
        right = jnp.zeros((N_HEADS, QB, 128 - c_first - cols.shape[2]), jnp.float32)
        d_band = QB * par + i - D_STRIDE * (np.arange(128)[None, :] - BAND_LEFT) - (L_CMP - 1)
        assert (d_band[:, 128 - right.shape[2]:] < 0).all()
        bands.append(jnp.where(d_band >= 0, jnp.concatenate([left, cols, right], axis=-1), NEG_INF))
    band = jnp.stack(bands).reshape(2, N_KV_GROUPS, HEADS_PER_GROUP, QB, 128)
    return toep, win, band, far


def _slc_map_matrix(n_cmp_cols):
    w = np.zeros((n_cmp_cols, NBLK), np.float32)
    for jb in range(NBLK):
        for mm in range(R_SLC):
            for nn in range(R_CMP):
                k = R_SLC * jb + mm - nn
                if 0 <= k < n_cmp_cols:
                    w[k, jb] += 1.0
    return w


def _dot_nt(a, b):
    return lax.dot_general(a, b, (((1,), (1,)), ((), ())), preferred_element_type=jnp.float32)


def _cmp_select_body(far_ref, q_ref, gate_ref, kc_ref, vc_ref, band_ref, wmap_ref, oc_ref, sel_ref,
                     *, n_far):
    qi = pl.program_id(1)
    par = qi % 2
    band0 = pl.multiple_of(16 * (qi // 2 + 1), 16)
    first_band_blk = band0 - CMP_PAD

    col = lax.broadcasted_iota(jnp.int32, (QB, 128), 1)
    row = lax.broadcasted_iota(jnp.int32, (QB, 128), 0)
    neg_pad = jnp.where(first_band_blk + col >= 0, 0.0, NEG_INF)

    t = qi * QB + row
    cur = t // L_SLC
    forced = (col == 0) | (col == cur) | (col == cur - 1)
    n_forced = 1 + (cur[:, :1] >= 1).astype(jnp.int32) + (cur[:, :1] >= 2).astype(jnp.int32)
    valid = col * L_SLC <= t

    q_all = q_ref[0]
    colf = col.astype(jnp.float32)

    def take_best(score, active=None):
        best = jnp.max(score, axis=-1, keepdims=True)
        first = jnp.min(jnp.where(score == best, colf, float(NBLK)), axis=-1, keepdims=True)
        hit = colf == first
        return jnp.where(hit if active is None else hit & active, -jnp.inf, score)

    def branches(n_cols):
        far_col = lax.broadcasted_iota(jnp.int32, (1, max(n_cols, 1)), 1)
        neg_far = jnp.where(far_col < first_band_blk, 0.0, NEG_INF)

        def far_and_band(ref, *lead):
            band_rows = ref[(*lead, pl.ds(band0, 128), slice(None))]
            if not n_cols:
                return band_rows
            return jnp.concatenate([ref[(*lead, slice(CMP_PAD, CMP_PAD + n_cols), slice(None))], band_rows], axis=0)

        wmap_all = far_and_band(wmap_ref)
        scores, start_scores = [], []
        oc_heads = []
        for g in range(N_KV_GROUPS):
            q4 = jnp.concatenate(
                [q_all[:, (g * HEADS_PER_GROUP + h) * HEAD_DIM:(g * HEADS_PER_GROUP + h + 1) * HEAD_DIM]
                 for h in range(HEADS_PER_GROUP)], axis=0)
            k_all = far_and_band(kc_ref, 0, g)
            v_all = far_and_band(vc_ref, 0, g)
            s_all = _dot_nt(q4, k_all).reshape(HEADS_PER_GROUP, QB, n_cols + 128)
            imp = jnp.zeros((QB, n_cols + 128), jnp.float32)
            for h in range(HEADS_PER_GROUP):
                hh = g * HEADS_PER_GROUP + h
                bias = band_ref[par, g, h] + neg_pad
                if n_cols:
                    bias = jnp.concatenate([jnp.broadcast_to(far_ref[hh] + neg_far, (QB, n_cols)), bias],
                                           axis=-1)
                s = s_all[h] + bias
                m = jnp.maximum(jnp.max(s, axis=-1, keepdims=True), 1e-10 * NEG_INF)
                p = jnp.exp2(s - m)
                l = jnp.sum(p, axis=-1, keepdims=True)
                p = p * jnp.where(l > 0.0, 1.0 / l, 0.0)
                imp = imp + p
                o = jnp.dot(p.astype(jnp.bfloat16), v_all, preferred_element_type=jnp.float32)
                oc_heads.append(o * gate_ref[0, :, 3 * hh:3 * hh + 1])
            p_slc = jnp.zeros((QB, NBLK), jnp.float32)
            rest = imp
            for _ in range(3):
                term = rest.astype(jnp.bfloat16)
                p_slc = p_slc + jnp.dot(term, wmap_all, preferred_element_type=jnp.float32)
                rest = rest - term.astype(jnp.float32)
            score = jnp.where(forced, -jnp.inf, jnp.where(valid, p_slc, -SEL_FORCE - colf))
            start_scores.append(score)
            for _ in range(N_SEL - MAX_FORCED):
                score = jnp.where(score == jnp.max(score, axis=-1, keepdims=True), -jnp.inf, score)
            scores.append(score)
        return (jnp.concatenate(oc_heads, axis=-1),) + tuple(scores) + tuple(start_scores)

    n_variants = n_far // 128
    variant = jnp.clip((first_band_blk + 127) // 128, 0, n_variants - 1)
    res = lax.switch(variant, [functools.partial(branches, 128 * v) for v in range(n_variants)])
    oc_ref[0] = res[0]
    scores, start_scores = res[1:1 + N_KV_GROUPS], res[1 + N_KV_GROUPS:]

    expected = (n_forced + (N_SEL - MAX_FORCED)).astype(jnp.float32)
    excess = [jnp.sum(jnp.where(s == -jnp.inf, 1.0, 0.0), axis=-1, keepdims=True) - expected for s in scores]
    any_tie = jnp.max(sum(excess)) > 0.0

    def exact_passes(start_scores):
        for _ in range(N_SEL - MAX_FORCED):
            start_scores = tuple(take_best(s) for s in start_scores)
        return start_scores

    scores = lax.cond(any_tie, exact_passes, lambda _: tuple(scores), tuple(start_scores))

    def early_rows(scores):
        for extra in range(MAX_FORCED - 1):
            scores = tuple(take_best(s, MAX_FORCED - n_forced > extra) for s in scores)
        return scores

    scores = lax.cond(qi == 0, early_rows, lambda s: s, tuple(scores))
    for g in range(N_KV_GROUPS):
        sel_ref[0, g] = jnp.where(scores[g] == -jnp.inf, 0.0, NEG_INF).astype(jnp.bfloat16)


def _cmp_select(q, gates, kc_pad, vc_pad, band, far, wmap):
    B, S, _ = q.shape
    n_far = S // D_STRIDE
    n_pad = kc_pad.shape[2]
    grid = (B, S // QB)
    return pl.pallas_call(
        functools.partial(_cmp_select_body, n_far=n_far),
        grid=grid,
        in_specs=[
            pl.BlockSpec(memory_space=pltpu.SMEM),
            pl.BlockSpec((1, QB, N_HEADS * HEAD_DIM), lambda b, i: (b, i, 0)),
            pl.BlockSpec((1, QB, LANES), lambda b, i: (b, i, 0)),
            pl.BlockSpec((1, N_KV_GROUPS, n_pad, HEAD_DIM), lambda b, i: (b, 0, 0, 0)),
            pl.BlockSpec((1, N_KV_GROUPS, n_pad, HEAD_DIM), lambda b, i: (b, 0, 0, 0)),
            pl.BlockSpec((2, N_KV_GROUPS, HEADS_PER_GROUP, QB, 128), lambda b, i: (0, 0, 0, 0, 0)),
            pl.BlockSpec((n_pad, NBLK), lambda b, i: (0, 0)),
        ],
        out_specs=[
            pl.BlockSpec((1, QB, N_HEADS * HEAD_DIM), lambda b, i: (b, i, 0)),
            pl.BlockSpec((1, N_KV_GROUPS, QB, NBLK), lambda b, i: (b, 0, i, 0)),
        ],
        out_shape=[
            jax.ShapeDtypeStruct((B, S, N_HEADS * HEAD_DIM), jnp.float32),
            jax.ShapeDtypeStruct((B, N_KV_GROUPS, S, NBLK), jnp.bfloat16),
        ],
        compiler_params=pltpu.CompilerParams(
            dimension_semantics=("arbitrary", "arbitrary"),
            vmem_limit_bytes=V7X_VMEM_LIMIT_BYTES,
        ),
        name="nsa_cmp_select",
    )(far, q, gates, kc_pad, vc_pad, band, wmap)


def _win_sel_body(q_ref, gate_ref, sel_ref, oc_ref, ks_ref, vs_ref, kw_ref, vw_ref, toep_ref, win_ref,
                  out_ref, s0_scr, s1_scr, p0_scr, p1_scr, m_scr, acc_scr):
    qi = pl.program_id(2)
    q0 = pl.multiple_of(qi * QB, QB)
    qg = q_ref[0]
    q_heads = [qg[:, h * HEAD_DIM:(h + 1) * HEAD_DIM] for h in range(HEADS_PER_GROUP)]
    q4 = jnp.concatenate(q_heads, axis=0)

    sel = sel_ref[0, 0]
    qa = jnp.concatenate([jnp.concatenate([sel] * HEADS_PER_GROUP, axis=0), q4], axis=-1)
    c_diag = qi // 2

    def scores(c):
        col = pl.multiple_of(c * KEY_TILE, KEY_TILE)
        s = jnp.dot(qa, ks_ref[0, 0, :, pl.ds(col, KEY_TILE)], preferred_element_type=jnp.float32)
        mm = qi - 2 * c
        bias = jnp.concatenate([toep_ref[0, jnp.clip(mm + 1, 0, N_TOEP - 1)],
                                toep_ref[0, jnp.clip(mm, 0, N_TOEP - 1)]], axis=-1)
        return s.reshape(HEADS_PER_GROUP, QB, KEY_TILE) + bias

    s_slots = (s0_scr, s1_scr)
    p_slots = (p0_scr, p1_scr)
    m_scr[...] = jnp.full(m_scr.shape, NEG_INF, jnp.float32)
    acc_scr[...] = jnp.zeros_like(acc_scr)
    p_slots[0][...] = jnp.zeros_like(p0_scr)
    s_slots[0][...] = scores(0)

    def weighted_values(p, col):
        return jnp.dot(p, vs_ref[0, 0, pl.ds(pl.multiple_of(col, KEY_TILE), KEY_TILE), :],
                       preferred_element_type=jnp.float32)

    def half_step(c, col_prev, cur):
        pv = weighted_values(p_slots[cur][...], col_prev)
        s_slots[1 - cur][...] = scores(c + 1)
        for h in range(HEADS_PER_GROUP):
            s = s_slots[cur][h]
            m_old = m_scr[h]
            m_new = jnp.maximum(m_old, jnp.max(s, axis=-1, keepdims=True))
            alpha = jnp.exp2(m_old - m_new)
            p = jnp.exp2(s - jnp.concatenate([m_new] * (KEY_TILE // LANES), axis=-1))
            p_slots[1 - cur][h * QB:(h + 1) * QB, :] = p.astype(jnp.bfloat16)
            m_scr[h] = m_new
            acc_scr[h * QB:(h + 1) * QB, :] = alpha * (acc_scr[h * QB:(h + 1) * QB, :]
                                                      + pv[h * QB:(h + 1) * QB, :])
        return c * KEY_TILE

    def unrolled_steps(j, col_prev):
        for u in range(SEL_UNROLL):
            col_prev = half_step(SEL_UNROLL * j + u, col_prev, u % 2)
        return col_prev

    col_last = lax.fori_loop(0, (c_diag + SEL_UNROLL) // SEL_UNROLL, unrolled_steps, 0)
    acc_s = (acc_scr[...] + weighted_values(p_slots[0][...], col_last)).reshape(HEADS_PER_GROUP, QB, LANES)
    o_s = acc_s[:, :, :HEAD_DIM] / acc_s[:, :, HEAD_DIM:HEAD_DIM + 1]

    kw = kw_ref[0, 0, :, pl.ds(q0, WINDOW + QB)]
    s_w = jnp.dot(q4, kw, preferred_element_type=jnp.float32)
    wcol = lax.broadcasted_iota(jnp.int32, (1, WINDOW + QB), 1)
    neg_left = jnp.where(q0 + wcol >= WINDOW, 0.0, NEG_INF)
    s_w = s_w.reshape(HEADS_PER_GROUP, QB, WINDOW + QB) + win_ref[0] + neg_left
    m_w = jnp.max(s_w, axis=-1, keepdims=True)
    p_w = jnp.exp2(s_w - m_w)
    o_w = jnp.dot(p_w.reshape(QROWS, WINDOW + QB).astype(jnp.bfloat16),
                  vw_ref[0, 0, pl.ds(q0, WINDOW + QB), :], preferred_element_type=jnp.float32)
    o_w = o_w.reshape(HEADS_PER_GROUP, QB, LANES)
    o_w = o_w[:, :, :HEAD_DIM] / o_w[:, :, HEAD_DIM:HEAD_DIM + 1]

    outs = []
    for h in range(HEADS_PER_GROUP):
        g_s = gate_ref[0, 0, :, 3 * h + 1:3 * h + 2]
        g_w = gate_ref[0, 0, :, 3 * h + 2:3 * h + 3]
        outs.append(g_s * o_s[h] + g_w * o_w[h])
    out_ref[0] = oc_ref[0] + jnp.concatenate(outs, axis=-1)


def _win_sel(q, gates, selneg, oc, ks_aug, vs, kw_t, vw_pad, toep, win):
    B, S, _ = q.shape
    gw = HEADS_PER_GROUP * HEAD_DIM
    grid = (B, N_KV_GROUPS, S // QB)
    return pl.pallas_call(
        _win_sel_body,
        grid=grid,
        in_specs=[
            pl.BlockSpec((1, QB, gw), lambda b, g, i: (b, i, g)),
            pl.BlockSpec((1, 1, QB, 3 * HEADS_PER_GROUP), lambda b, g, i: (b, g, i, 0)),
            pl.BlockSpec((1, 1, QB, NBLK), lambda b, g, i: (b, g, i, 0)),
            pl.BlockSpec((1, QB, gw), lambda b, g, i: (b, i, g)),
            pl.BlockSpec((1, 1, NBLK + HEAD_DIM, ks_aug.shape[3]), lambda b, g, i: (b, g, 0, 0)),
            pl.BlockSpec((1, 1, vs.shape[2], LANES), lambda b, g, i: (b, g, 0, 0)),
            pl.BlockSpec((1, 1, HEAD_DIM, S + WINDOW), lambda b, g, i: (b, g, 0, 0)),
            pl.BlockSpec((1, 1, S + WINDOW, LANES), lambda b, g, i: (b, g, 0, 0)),
            pl.BlockSpec((1, N_TOEP, HEADS_PER_GROUP, QB, QB), lambda b, g, i: (g, 0, 0, 0, 0)),
            pl.BlockSpec((1, HEADS_PER_GROUP, QB, WINDOW + QB), lambda b, g, i: (g, 0, 0, 0)),
        ],
        out_specs=pl.BlockSpec((1, QB, gw), lambda b, g, i: (b, i, g)),
        out_shape=jax.ShapeDtypeStruct((B, S, N_HEADS * HEAD_DIM), jnp.float32),
        scratch_shapes=[
            pltpu.VMEM((HEADS_PER_GROUP, QB, KEY_TILE), jnp.float32),
            pltpu.VMEM((HEADS_PER_GROUP, QB, KEY_TILE), jnp.float32),
            pltpu.VMEM((QROWS, KEY_TILE), jnp.bfloat16),
            pltpu.VMEM((QROWS, KEY_TILE), jnp.bfloat16),
            pltpu.VMEM((HEADS_PER_GROUP, QB, LANES), jnp.float32),
            pltpu.VMEM((QROWS, LANES), jnp.float32),
        ],
        compiler_params=pltpu.CompilerParams(
            dimension_semantics=("arbitrary", "arbitrary", "arbitrary"),
            vmem_limit_bytes=V7X_VMEM_LIMIT_BYTES,
        ),
        name="nsa_win_sel",
    )(q, gates, selneg, oc, ks_aug, vs, kw_t, vw_pad, toep, win)


def _nsa_attention(q, gates, kc, vc, kvsw, rel_bias):
    B, S, _ = q.shape
    assert S % KEY_TILE == 0 and S // L_SLC <= NBLK
    bf = jnp.bfloat16
    n_far = S // D_STRIDE
    toep, win, band, far = _bias_tables(rel_bias)
    cpad = ((0, 0), (0, 0), (CMP_PAD, n_far - kc.shape[2] + CMP_PAD), (0, 0))
    kc_pad = jnp.pad(kc, cpad).astype(bf)
    vc_pad = jnp.pad(vc, cpad).astype(bf)
    wmap = jnp.asarray(np.pad(_slc_map_matrix(n_far), ((CMP_PAD, CMP_PAD), (0, 0))), bf)
    oc, selneg = _cmp_select(q, gates, kc_pad, vc_pad, band, far, wmap)
    ks, vs, kw, vw = (kvsw.reshape(B, S, 4, N_KV_GROUPS, HEAD_DIM)[:, :, n] for n in range(4))
    blk_onehot = jnp.asarray((np.arange(S)[None, :] // L_SLC == np.arange(NBLK)[:, None]), bf)
    ks_aug = jnp.concatenate(
        [jnp.broadcast_to(blk_onehot, (B, N_KV_GROUPS, NBLK, S)), ks.transpose(0, 2, 3, 1)], axis=2)
    tail = SEL_UNROLL * KEY_TILE
    ks_aug = jnp.pad(ks_aug, ((0, 0), (0, 0), (0, 0), (0, tail)))
    kw_t = jnp.pad(kw.transpose(0, 2, 3, 1), ((0, 0), (0, 0), (0, 0), (WINDOW, 0)))

    def with_ones(v):
        v = v.transpose(0, 2, 1, 3)
        ones = jnp.ones(v.shape[:-1] + (1,), v.dtype)
        return jnp.pad(jnp.concatenate([v, ones], axis=-1), ((0, 0),) * 3 + ((0, LANES - HEAD_DIM - 1),))

    vw_aug = jnp.pad(with_ones(vw), ((0, 0), (0, 0), (WINDOW, 0), (0, 0)))
    vs_aug = jnp.pad(with_ones(vs), ((0, 0), (0, 0), (0, tail), (0, 0)))
    gates_g = gates[:, :, :3 * N_HEADS].reshape(B, S, N_KV_GROUPS, 3 * HEADS_PER_GROUP).swapaxes(1, 2)
    return _win_sel(q, gates_g, selneg, oc, ks_aug, vs_aug, kw_t, vw_aug, toep, win)


def _pool_layer_body(x_ref, halo_ref, gain_ref, shift_ref, scale_ref, gate_ref, pscale_ref, w_ref, o_ref):
    i = pl.program_id(1)
    x = x_ref[0]
    gain, shift, scale = gain_ref[...], shift_ref[0], scale_ref[0]
    h = _norm_modulate(x, gain, shift, scale)
    halo = _norm_modulate(halo_ref[0], gain, shift, scale) * (i > 0).astype(jnp.float32)
    hx = jnp.concatenate([halo, h], axis=0)
    t = i * POOL_TILE + lax.broadcasted_iota(jnp.int32, (POOL_TILE, 1), 0)
    outs = []
    run = hx
    width = 1
    for g, w in enumerate(POOL_WINDOWS):
        while width < w:
            run = run + pltpu.roll(run, width, axis=0)
            width *= 2
        lanes = slice(0, POOL_GROUP_DIM)
        cnt = jnp.minimum(t + 1, w).astype(jnp.float32)
        mix = run[POOL_HALO:, lanes] / cnt - h[:, g * POOL_GROUP_DIM:(g + 1) * POOL_GROUP_DIM]
        outs.append(jnp.dot(mix.astype(jnp.bfloat16), w_ref[g], preferred_element_type=jnp.float32))
        run = run[:, POOL_GROUP_DIM:]
    y = jnp.concatenate(outs, axis=-1) * pscale_ref[...]
    o_ref[0] = x + gate_ref[0] * y


def _pool_layer(x, gain, shift, scale, gate1, w_grp, pool_scale):
    B, S, D = x.shape
    assert S % POOL_TILE == 0 and POOL_HALO >= max(POOL_WINDOWS) - 1
    per_batch = pl.BlockSpec((1, 1, D), lambda b, i: (b, 0, 0))
    halo_blocks = POOL_TILE // POOL_HALO
    return pl.pallas_call(
        _pool_layer_body,
        grid=(B, S // POOL_TILE),
        in_specs=[
            pl.BlockSpec((1, POOL_TILE, D), lambda b, i: (b, i, 0)),
            pl.BlockSpec((1, POOL_HALO, D), lambda b, i: (b, jnp.maximum(i * halo_blocks - 1, 0), 0)),
            pl.BlockSpec((1, D), lambda b, i: (0, 0)),
            per_batch, per_batch, per_batch,
            pl.BlockSpec((1, D), lambda b, i: (0, 0)),
            pl.BlockSpec((N_POOL_GROUPS, POOL_GROUP_DIM, POOL_GROUP_DIM), lambda b, i: (0, 0, 0)),
        ],
        out_specs=pl.BlockSpec((1, POOL_TILE, D), lambda b, i: (b, i, 0)),
        out_shape=jax.ShapeDtypeStruct((B, S, D), jnp.float32),
        compiler_params=pltpu.CompilerParams(
            dimension_semantics=("arbitrary", "arbitrary"),
            vmem_limit_bytes=V7X_VMEM_LIMIT_BYTES,
        ),
        name="pool_layer",
    )(x, x, gain.reshape(1, D), shift[:, None, :], scale[:, None, :], gate1[:, None, :],
      pool_scale.reshape(1, D), w_grp.astype(jnp.bfloat16))


def _head_norm(v, seg_ref, gain):
    sq = v * v
    hi = sq.astype(jnp.bfloat16)
    lo = (sq - hi.astype(jnp.float32)).astype(jnp.bfloat16)
    ss = (jnp.dot(hi, seg_ref[...], preferred_element_type=jnp.float32)
          + jnp.dot(lo, seg_ref[...], preferred_element_type=jnp.float32))
    return v * lax.rsqrt(ss * (1.0 / HEAD_DIM) + EPS) * gain


def _qkv_proj_body(x_ref, gq_ref, shq_ref, scq_ref, gkv_ref, shkv_ref, sckv_ref, wq_ref, wg_ref, wkv_ref,
                   seg_ref, qgain_ref, kgain_ref, q_ref, gates_ref, kvc_ref, kvsw_ref):
    x = x_ref[...]
    xhat = x * lax.rsqrt(jnp.mean(x * x, axis=-1, keepdims=True) + EPS)
    hq = ((xhat * gq_ref[...]) * (1.0 + scq_ref[0]) + shq_ref[0]).astype(jnp.bfloat16)
    hkv = ((xhat * gkv_ref[...]) * (1.0 + sckv_ref[0]) + shkv_ref[0]).astype(jnp.bfloat16)
    pq = jnp.dot(hq, wq_ref[...], preferred_element_type=jnp.float32)
    gates_ref[...] = jax.nn.sigmoid(jnp.dot(hq, wg_ref[...], preferred_element_type=jnp.float32))
    pkv = jnp.dot(hkv, wkv_ref[...], preferred_element_type=jnp.float32)
    for g in range(N_KV_GROUPS):
        lanes = slice(g * GROUP_W, (g + 1) * GROUP_W)
        q_ref[:, lanes] = _head_norm(pq[:, lanes], seg_ref, qgain_ref[...]).astype(jnp.bfloat16)
    kvc_ref[...] = pkv[:, :2 * GROUP_W]
    k_s = _head_norm(pkv[:, 2 * GROUP_W:3 * GROUP_W], seg_ref, kgain_ref[0:1, :])
    k_w = _head_norm(pkv[:, 4 * GROUP_W:5 * GROUP_W], seg_ref, kgain_ref[1:2, :])
    kvsw_ref[...] = jnp.concatenate([k_s, pkv[:, 3 * GROUP_W:4 * GROUP_W], k_w, pkv[:, 5 * GROUP_W:]],
                                    axis=-1).astype(jnp.bfloat16)


def _qkv_proj(x, gain_q, shift_q, scale_q, gain_kv, shift_kv, scale_kv, w_qg, kv_w, q_gain, k_gain):
    B, S, D = x.shape
    T = B * S
    qd = N_HEADS * HEAD_DIM
    bf = jnp.bfloat16
    tiles_per_batch = S // ROW_TILE
    per_batch = pl.BlockSpec((1, 1, D), lambda i: (i // tiles_per_batch, 0, 0))
    const = lambda shape: pl.BlockSpec(shape, lambda i: (0,) * len(shape))
    seg = jnp.asarray(np.kron(np.eye(HEADS_PER_GROUP), np.ones((HEAD_DIM, HEAD_DIM))), bf)
    w_gate = jnp.pad(w_qg[:, qd:], ((0, 0), (0, LANES - 3 * N_HEADS))).astype(bf)
    qgain = jnp.tile(q_gain * HEAD_DIM ** -0.5, HEADS_PER_GROUP).reshape(1, GROUP_W)
    kgain = jnp.stack([jnp.tile(k_gain[1] * LOG2_E, N_KV_GROUPS), jnp.tile(k_gain[2] * LOG2_E, N_KV_GROUPS)])
    rows = lambda width: pl.BlockSpec((ROW_TILE, width), lambda i: (i, 0))
    return pl.pallas_call(
        _qkv_proj_body,
        grid=(T // ROW_TILE,),
        in_specs=[rows(D), const((1, D)), per_batch, per_batch, const((1, D)), per_batch, per_batch,
                  const((D, qd)), const((D, LANES)), const((D, 6 * GROUP_W)), const((GROUP_W, GROUP_W)),
                  const((1, GROUP_W)), const((2, GROUP_W))],
        out_specs=[rows(qd), rows(LANES), rows(2 * GROUP_W), rows(4 * GROUP_W)],
        out_shape=[jax.ShapeDtypeStruct((T, qd), bf), jax.ShapeDtypeStruct((T, LANES), jnp.float32),
                   jax.ShapeDtypeStruct((T, 2 * GROUP_W), jnp.float32),
                   jax.ShapeDtypeStruct((T, 4 * GROUP_W), bf)],
        compiler_params=pltpu.CompilerParams(
            dimension_semantics=("arbitrary",), vmem_limit_bytes=V7X_VMEM_LIMIT_BYTES),
        name="qkv_proj",
    )(x.reshape(T, D), gain_q.reshape(1, D), shift_q[:, None, :], scale_q[:, None, :],
      gain_kv.reshape(1, D), shift_kv[:, None, :], scale_kv[:, None, :],
      w_qg[:, :qd].astype(bf), w_gate, kv_w.astype(bf), seg, qgain, kgain)


def _compress_body(r_ref, pe_ref, w1_ref, w2_ref, gain_ref, o_ref):
    half = D_STRIDE * HEAD_DIM
    r = r_ref[0, 0, 0]
    top = jnp.dot(r, w1_ref[0, :half, :], preferred_element_type=jnp.float32)
    bot = jnp.dot(r, w1_ref[0, half:, :], preferred_element_type=jnp.float32)
    pe = jnp.dot(jnp.broadcast_to(pe_ref[0], (8, R_CMP * half)).astype(jnp.bfloat16), w1_ref[0],
                 preferred_element_type=jnp.float32)[0:1]
    n_chunks = r.shape[0]
    hidden = top + pltpu.roll(bot, n_chunks - 1, axis=0) + pe
    out = jnp.dot(jax.nn.gelu(hidden).astype(jnp.bfloat16), w2_ref[0], preferred_element_type=jnp.float32)
    normed = out * lax.rsqrt(jnp.mean(out * out, axis=-1, keepdims=True) + EPS) * gain_ref[...]
    o_ref[0, 0, 0] = jnp.where(pl.program_id(0) == 0, normed, out)


def _compressed_kv(kvc, B, S, cmp_pe_k, cmp_pe_v, cmp_k_w1, cmp_k_w2, cmp_v_w1, cmp_v_w2, k_gain):
    bf = jnp.bfloat16
    n_chunks = S // D_STRIDE
    half = D_STRIDE * HEAD_DIM
    hidden = cmp_k_w1.shape[1]
    r = kvc.astype(bf).reshape(B, n_chunks, D_STRIDE, 2, N_KV_GROUPS, HEAD_DIM)
    r = r.transpose(3, 0, 4, 1, 2, 5).reshape(2, B, N_KV_GROUPS, n_chunks, half)
    pe = jnp.stack([cmp_pe_k, cmp_pe_v]).reshape(2, 1, R_CMP * half)
    per_kind = lambda shape: pl.BlockSpec((1,) + shape, lambda kv, b, g: (kv,) + (0,) * len(shape))
    out = pl.pallas_call(
        _compress_body,
        grid=(2, B, N_KV_GROUPS),
        in_specs=[
            pl.BlockSpec((1, 1, 1, n_chunks, half), lambda kv, b, g: (kv, b, g, 0, 0)),
            per_kind((1, R_CMP * half)), per_kind((R_CMP * half, hidden)), per_kind((hidden, HEAD_DIM)),
            pl.BlockSpec((1, HEAD_DIM), lambda kv, b, g: (0, 0)),
        ],
        out_specs=pl.BlockSpec((1, 1, 1, n_chunks, HEAD_DIM), lambda kv, b, g: (kv, b, g, 0, 0)),
        out_shape=jax.ShapeDtypeStruct((2, B, N_KV_GROUPS, n_chunks, HEAD_DIM), jnp.float32),
        compiler_params=pltpu.CompilerParams(
            dimension_semantics=("arbitrary", "arbitrary", "arbitrary"),
            vmem_limit_bytes=V7X_VMEM_LIMIT_BYTES),
        name="cmp_mlp",
    )(r, pe, jnp.stack([cmp_k_w1, cmp_v_w1]).astype(bf), jnp.stack([cmp_k_w2, cmp_v_w2]).astype(bf),
      (k_gain[0] * LOG2_E).reshape(1, HEAD_DIM))
    n_cmp = n_chunks - R_CMP + 1
    return out[0, :, :, :n_cmp], out[1, :, :, :n_cmp]


def _oproj_router_body(a_ref, x_ref, gate_ref, wo_ref, gain_ref, shift_ref, scale_ref, wr_ref, br_ref,
                       o_ref, logit_ref):
    mix = jnp.dot(a_ref[...].astype(jnp.bfloat16), wo_ref[...], preferred_element_type=jnp.float32)
    x = x_ref[...] + gate_ref[0] * mix
    o_ref[...] = x
    h = _norm_modulate(x, gain_ref[...], shift_ref[0], scale_ref[0]).astype(jnp.bfloat16)
    logit_ref[...] = jnp.dot(h, wr_ref[...], preferred_element_type=jnp.float32) + br_ref[...]


def _oproj_router(attn, x, gate1, w_o, gain, shift, scale, w_router, b_router):
    B, S, D = x.shape
    T = B * S
    bf = jnp.bfloat16
    tiles_per_batch = S // ROW_TILE
    per_batch = pl.BlockSpec((1, 1, D), lambda i: (i // tiles_per_batch, 0, 0))
    const = lambda shape: pl.BlockSpec(shape, lambda i: (0,) * len(shape))
    rows = lambda width: pl.BlockSpec((ROW_TILE, width), lambda i: (i, 0))
    pad = ((0, 0), (0, LANES - N_EXPERTS))
    out, logits = pl.pallas_call(
        _oproj_router_body,
        grid=(T // ROW_TILE,),
        in_specs=[rows(D), rows(D), per_batch, const((D, D)), const((1, D)), per_batch, per_batch,
                  const((D, LANES)), const((1, LANES))],
        out_specs=[rows(D), rows(LANES)],
        out_shape=[jax.ShapeDtypeStruct((T, D), jnp.float32), jax.ShapeDtypeStruct((T, LANES), jnp.float32)],
        compiler_params=pltpu.CompilerParams(
            dimension_semantics=("arbitrary",), vmem_limit_bytes=V7X_VMEM_LIMIT_BYTES),
        name="oproj_router",
    )(attn.reshape(T, D), x.reshape(T, D), gate1[:, None, :], w_o.astype(bf), gain.reshape(1, D),
      shift[:, None, :], scale[:, None, :], jnp.pad(w_router, pad).astype(bf),
      jnp.pad(b_router.reshape(1, N_EXPERTS), pad))
    return out.reshape(B, S, D), logits[:, :N_EXPERTS]


def _route(logits):
    T = logits.shape[0]
    lane = jnp.arange(N_EXPERTS)[None, :]
    l0 = jnp.max(logits, axis=-1)
    e0 = jnp.argmax(logits, axis=-1)
    rest = jnp.where(lane == e0[:, None], -jnp.inf, logits)
    l1 = jnp.max(rest, axis=-1)
    e1 = jnp.argmax(rest, axis=-1)
    z = jnp.exp(l1 - l0)
    top_w = jnp.stack([1.0 / (1.0 + z), z / (1.0 + z)], axis=-1)
    oh = [(lane == e[:, None]).astype(jnp.int32) for e in (e0, e1)]
    cnt = oh[0] + oh[1]
    before = jnp.cumsum(cnt, axis=0) - cnt
    counts = before[-1] + cnt[-1]
    pcounts = (counts + ROW_TILE - 1) // ROW_TILE * ROW_TILE
    pends = jnp.cumsum(pcounts)
    pstarts = pends - pcounts
    dest = jnp.stack([jnp.sum((pstarts[None, :] + before) * o, axis=-1) for o in oh], axis=-1)
    n_blocks = -(-(T * TOP_K + N_EXPERTS * (ROW_TILE - 1)) // ROW_TILE)
    blk_start = jnp.arange(n_blocks)[:, None] * ROW_TILE
    blk_e = jnp.minimum(jnp.sum(pends[None, :] <= blk_start, axis=-1), N_EXPERTS - 1).astype(jnp.int32)
    return top_w, dest.astype(jnp.int32), blk_e, n_blocks * ROW_TILE


def _moe_ffn(x, logits, gain, shift, scale, gate2, w_gu, w_dn):
    top_w, dest, blk_e, n_rows = _route(logits)
    rows = _moe_dispatch(x, gain, shift, scale, dest, n_rows)
    y = _grouped_swiglu(rows, blk_e, w_gu.astype(jnp.bfloat16), w_dn.astype(jnp.bfloat16),
                        ff_tile=FF_TILE_EXPERT)
    return _moe_combine(x, top_w, gate2, y, dest)


def kernel(x, c, ada_w, ada_b, norm_g, pool_w, pool_scale, q_w, q_gain, o_w, kv_ada_w, kv_ada_b, kv_norm_g, kv_w, cmp_pe_k, cmp_pe_v, cmp_k_w1, cmp_k_w2, cmp_v_w1, cmp_v_w2, k_gain, rel_bias, ffn_gu, ffn_dn, router_w, router_b, exp_gu, exp_dn):
    assert DEPTH == 2 and N_A_LAYERS == 1
    B, S, D = x.shape
    silu_c = jax.nn.silu(c)
    sh1, sc1, g1, sh2, sc2, g2 = jnp.split(silu_c @ ada_w[0] + ada_b[0], 6, axis=-1)
    x = _pool_layer(x, norm_g[0, 0], sh1, sc1, g1, pool_w[0], pool_scale[0])
    x = _dense_ffn(x, norm_g[0, 1], sh2, sc2, g2, ffn_gu[0], ffn_dn[0])
    sh1, sc1, g1, sh2, sc2, g2 = jnp.split(silu_c @ ada_w[1] + ada_b[1], 6, axis=-1)
    sh_kv, sc_kv = jnp.split(silu_c @ kv_ada_w + kv_ada_b, 2, axis=-1)
    q, gates, kvc, kvsw = _qkv_proj(x, norm_g[1, 0], sh1, sc1, kv_norm_g, sh_kv, sc_kv, q_w[0], kv_w,
                                    q_gain[0], k_gain)
    kc, vc = _compressed_kv(kvc, B, S, cmp_pe_k, cmp_pe_v, cmp_k_w1, cmp_k_w2, cmp_v_w1, cmp_v_w2, k_gain)
    attn = _nsa_attention(q.reshape(B, S, -1), gates.reshape(B, S, -1), kc, vc, kvsw.reshape(B, S, -1),
                          rel_bias)
    x, logits = _oproj_router(attn, x, g1, o_w[0], norm_g[1, 1], sh2, sc2, router_w[0], router_b[0])
    return _moe_ffn(x, logits, norm_g[1, 1], sh2, sc2, g2, exp_gu[0], exp_dn[0])
```

```python
import functools
import math

import jax
import jax.numpy as jnp
import numpy as np
from jax import lax
from jax.experimental import pallas as pl
from jax.experimental.pallas import tpu as pltpu

D_MODEL = 1024
DEPTH = 2
N_A_LAYERS = DEPTH // 2
POOL_WINDOWS = (2, 4, 8, 16)
N_POOL_GROUPS = len(POOL_WINDOWS)
POOL_GROUP_DIM = D_MODEL // N_POOL_GROUPS
HEAD_DIM = 64
N_HEADS = D_MODEL // HEAD_DIM
N_KV_GROUPS = 4
HEADS_PER_GROUP = N_HEADS // N_KV_GROUPS
L_CMP = 32
D_STRIDE = 16
L_SLC = 64
N_SEL = 16
WINDOW = 512
R_CMP = L_CMP // D_STRIDE
R_SLC = L_SLC // D_STRIDE
N_BUCKETS = 32
REL_EXACT = N_BUCKETS // 2
MAX_DISTANCE = 1024
N_EXPERTS = 8
TOP_K = 2
EPS = 1e-6
NEG_INF = -1e30
SEL_FORCE = 1e6
LOG2_E = math.log2(math.e)

V7X_VMEM_LIMIT_BYTES = 48 * 1024 * 1024
LANES = 128
ROW_TILE = 512
TOK_TILE = 256
DMA_ISSUE_UNROLL = 8
MAX_FORCED = 3
FF_TILE_DENSE = 1408
FF_TILE_EXPERT = 896
POOL_TILE = 512
POOL_HALO = 16

QB = 128
KEY_TILE = 2 * QB
SEL_UNROLL = 8
QROWS = HEADS_PER_GROUP * QB
GROUP_W = HEADS_PER_GROUP * HEAD_DIM
NBLK = 128
CMP_PAD = 128
FAR_DIST = MAX_DISTANCE
N_TOEP = FAR_DIST // QB + 3
BAND_LEFT = CMP_PAD - 16
assert BAND_LEFT * D_STRIDE + (L_CMP - 1) - 2 * QB >= FAR_DIST


def _rms_norm(x, g):
    xf = x.astype(jnp.float32)
    y = xf * lax.rsqrt(jnp.mean(xf * xf, axis=-1, keepdims=True) + EPS)
    return (y * g.astype(jnp.float32)).astype(x.dtype)


def _modulate(h, shift, scale):
    return h * (1 + scale[:, None, :]) + shift[:, None, :]


def _rel_bucket(dist):
    d = jnp.maximum(dist, 0)
    ratio = jnp.maximum(d, REL_EXACT).astype(jnp.float32) / REL_EXACT
    large = REL_EXACT + (jnp.log(ratio) / math.log(MAX_DISTANCE / REL_EXACT)
                         * (N_BUCKETS - REL_EXACT)).astype(jnp.int32)
    return jnp.where(d < REL_EXACT, d, jnp.minimum(large, N_BUCKETS - 1))


def _norm_modulate(x, gain, shift, scale):
    y = x * lax.rsqrt(jnp.mean(x * x, axis=-1, keepdims=True) + EPS)
    return (y * gain) * (1.0 + scale) + shift


def _swiglu_step(xb_ref, wg_ref, wu_ref, wd_ref, acc_ref, j):
    @pl.when(j == 0)
    def _():
        acc_ref[...] = jnp.zeros_like(acc_ref)

    x = xb_ref[...]
    gate = jnp.dot(x, wg_ref[0], preferred_element_type=jnp.float32)
    up = jnp.dot(x, wu_ref[0], preferred_element_type=jnp.float32)
    act = (gate * jax.nn.sigmoid(gate) * up).astype(jnp.bfloat16)
    acc_ref[...] += jnp.dot(act, wd_ref[0], preferred_element_type=jnp.float32)


def _grouped_swiglu_body(blk_e_ref, x_ref, wg_ref, wu_ref, wd_ref, o_ref, acc_ref, xb_ref, *, n_ff_steps):
    del blk_e_ref
    j = pl.program_id(1)

    @pl.when(j == 0)
    def _():
        xb_ref[...] = x_ref[...].astype(jnp.bfloat16)

    _swiglu_step(xb_ref, wg_ref, wu_ref, wd_ref, acc_ref, j)

    @pl.when(j == n_ff_steps - 1)
    def _():
        o_ref[...] = acc_ref[...]


def _grouped_swiglu(x_rows, blk_e, w_gu, w_dn, *, ff_tile):
    n_rows, d = x_rows.shape
    d_ff = w_dn.shape[1]
    assert n_rows % ROW_TILE == 0 and d_ff % ff_tile == 0
    n_ff_steps = d_ff // ff_tile
    grid = (n_rows // ROW_TILE, n_ff_steps)
    return pl.pallas_call(
        functools.partial(_grouped_swiglu_body, n_ff_steps=n_ff_steps),
        grid_spec=pltpu.PrefetchScalarGridSpec(
            num_scalar_prefetch=1,
            grid=grid,
            in_specs=[
                pl.BlockSpec((ROW_TILE, d), lambda i, j, e: (i, 0)),
                pl.BlockSpec((1, d, ff_tile), lambda i, j, e: (e[i], 0, j)),
                pl.BlockSpec((1, d, ff_tile), lambda i, j, e: (e[i], 0, j + n_ff_steps)),
                pl.BlockSpec((1, ff_tile, d), lambda i, j, e: (e[i], j, 0)),
            ],
            out_specs=pl.BlockSpec((ROW_TILE, d), lambda i, j, e: (i, 0)),
            scratch_shapes=[pltpu.VMEM((ROW_TILE, d), jnp.float32), pltpu.VMEM((ROW_TILE, d), jnp.bfloat16)],
        ),
        out_shape=jax.ShapeDtypeStruct((n_rows, d), jnp.float32),
        compiler_params=pltpu.CompilerParams(
            dimension_semantics=("arbitrary", "arbitrary"),
            vmem_limit_bytes=V7X_VMEM_LIMIT_BYTES,
        ),
        name="grouped_swiglu",
    )(blk_e, x_rows, w_gu, w_gu, w_dn)


def _dense_ffn_body(x_ref, gain_ref, shift_ref, scale_ref, gate2_ref, wg_ref, wu_ref, wd_ref, o_ref,
                    acc_ref, xb_ref, *, n_ff_steps):
    j = pl.program_id(1)

    @pl.when(j == 0)
    def _():
        xb_ref[...] = _norm_modulate(x_ref[...], gain_ref[...], shift_ref[0], scale_ref[0]).astype(jnp.bfloat16)

    _swiglu_step(xb_ref, wg_ref, wu_ref, wd_ref, acc_ref, j)

    @pl.when(j == n_ff_steps - 1)
    def _():
        o_ref[...] = x_ref[...] + gate2_ref[0] * acc_ref[...]


def _dense_ffn(x, gain, shift, scale, gate2, w_gu, w_dn):
    B, S, D = x.shape
    d_ff = w_dn.shape[0]
    n_ff_steps = d_ff // FF_TILE_DENSE
    tiles_per_batch = S // ROW_TILE
    per_batch = pl.BlockSpec((1, 1, D), lambda i, j: (i // tiles_per_batch, 0, 0))
    out = pl.pallas_call(
        functools.partial(_dense_ffn_body, n_ff_steps=n_ff_steps),
        grid=(B * S // ROW_TILE, n_ff_steps),
        in_specs=[
            pl.BlockSpec((ROW_TILE, D), lambda i, j: (i, 0)),
            pl.BlockSpec((1, D), lambda i, j: (0, 0)),
            per_batch, per_batch, per_batch,
            pl.BlockSpec((1, D, FF_TILE_DENSE), lambda i, j: (0, 0, j)),
            pl.BlockSpec((1, D, FF_TILE_DENSE), lambda i, j: (0, 0, j + n_ff_steps)),
            pl.BlockSpec((1, FF_TILE_DENSE, D), lambda i, j: (0, j, 0)),
        ],
        out_specs=pl.BlockSpec((ROW_TILE, D), lambda i, j: (i, 0)),
        out_shape=jax.ShapeDtypeStruct((B * S, D), jnp.float32),
        scratch_shapes=[pltpu.VMEM((ROW_TILE, D), jnp.float32), pltpu.VMEM((ROW_TILE, D), jnp.bfloat16)],
        compiler_params=pltpu.CompilerParams(
            dimension_semantics=("arbitrary", "arbitrary"),
            vmem_limit_bytes=V7X_VMEM_LIMIT_BYTES,
        ),
        name="dense_ffn",
    )(x.reshape(B * S, D), gain.reshape(1, D), shift[:, None, :], scale[:, None, :], gate2[:, None, :],
      w_gu.astype(jnp.bfloat16)[None], w_gu.astype(jnp.bfloat16)[None], w_dn.astype(jnp.bfloat16)[None])
    return out.reshape(B, S, D)


def _row_copies_wait(src_ref, dst_ref, sem, n_rows):
    pltpu.make_async_copy(src_ref.at[pl.ds(0, n_rows)], dst_ref.at[pl.ds(0, n_rows)], sem).wait()


def _dispatch_body(dest_ref, x_ref, gain_ref, shift_ref, scale_ref, rows_in_ref, rows_ref, h_ref, sem):
    del rows_in_ref
    h_ref[...] = _norm_modulate(x_ref[...], gain_ref[...], shift_ref[0], scale_ref[0])

    def issue(r, carry):
        for k in range(TOP_K):
            pltpu.make_async_copy(h_ref.at[pl.ds(r, 1)], rows_ref.at[pl.ds(dest_ref[TOP_K * r + k], 1)],
                                  sem).start()
        return carry

    lax.fori_loop(0, TOK_TILE, issue, 0, unroll=DMA_ISSUE_UNROLL)
    for k in range(TOP_K):
        _row_copies_wait(h_ref, rows_ref, sem, TOK_TILE)


def _moe_dispatch(x, gain, shift, scale, dest, n_rows):
    B, S, D = x.shape
    T = B * S
    tiles_per_batch = S // TOK_TILE
    per_batch = pl.BlockSpec((1, 1, D), lambda i: (i // tiles_per_batch, 0, 0))
    return pl.pallas_call(
        _dispatch_body,
        grid=(T // TOK_TILE,),
        in_specs=[
            pl.BlockSpec((TOP_K * TOK_TILE,), lambda i: (i,), memory_space=pltpu.SMEM),
            pl.BlockSpec((TOK_TILE, D), lambda i: (i, 0)),
            pl.BlockSpec((1, D), lambda i: (0, 0)),
            per_batch, per_batch,
            pl.BlockSpec(memory_space=pl.ANY),
        ],
        out_specs=pl.BlockSpec(memory_space=pl.ANY),
        out_shape=jax.ShapeDtypeStruct((n_rows, D), jnp.float32),
        scratch_shapes=[pltpu.VMEM((TOK_TILE, D), jnp.float32), pltpu.SemaphoreType.DMA(())],
        input_output_aliases={5: 0},
        compiler_params=pltpu.CompilerParams(dimension_semantics=("arbitrary",)),
        name="moe_dispatch",
    )(dest.reshape(T * TOP_K), x.reshape(T, D), gain.reshape(1, D), shift[:, None, :], scale[:, None, :],
      jnp.zeros((n_rows, D), jnp.float32))


def _combine_body(dest_ref, x_ref, w_ref, gate2_ref, y_ref, o_ref, buf_ref, sem):
    def issue(r, carry):
        for k in range(TOP_K):
            pltpu.make_async_copy(y_ref.at[pl.ds(dest_ref[TOP_K * r + k], 1)], buf_ref.at[k, pl.ds(r, 1)],
                                  sem).start()
        return carry

    lax.fori_loop(0, TOK_TILE, issue, 0, unroll=DMA_ISSUE_UNROLL)
    for k in range(TOP_K):
        _row_copies_wait(y_ref, buf_ref.at[k], sem, TOK_TILE)
    w = w_ref[...]
    f = w[:, 0:1] * buf_ref[0] + w[:, 1:2] * buf_ref[1]
    o_ref[...] = x_ref[...] + gate2_ref[0] * f


def _moe_combine(x, top_w, gate2, y, dest):
    B, S, D = x.shape
    T = B * S
    tiles_per_batch = S // TOK_TILE
    out = pl.pallas_call(
        _combine_body,
        grid=(T // TOK_TILE,),
        in_specs=[
            pl.BlockSpec((TOP_K * TOK_TILE,), lambda i: (i,), memory_space=pltpu.SMEM),
            pl.BlockSpec((TOK_TILE, D), lambda i: (i, 0)),
            pl.BlockSpec((TOK_TILE, TOP_K), lambda i: (i, 0)),
            pl.BlockSpec((1, 1, D), lambda i: (i // tiles_per_batch, 0, 0)),
            pl.BlockSpec(memory_space=pl.ANY),
        ],
        out_specs=pl.BlockSpec((TOK_TILE, D), lambda i: (i, 0)),
        out_shape=jax.ShapeDtypeStruct((T, D), jnp.float32),
        scratch_shapes=[pltpu.VMEM((TOP_K, TOK_TILE, D), jnp.float32), pltpu.SemaphoreType.DMA(())],
        compiler_params=pltpu.CompilerParams(dimension_semantics=("arbitrary",)),
        name="moe_combine",
    )(dest.reshape(T * TOP_K), x.reshape(T, D), top_w, gate2[:, None, :], y)
    return out.reshape(B, S, D)


def _bias_tables(rel_bias):
    x0 = (N_TOEP - 1) * QB
    width = x0 + 2 * QB
    period = width + QB
    n = np.arange(period)
    n = np.where(n < width, n, n - period)
    by_dist = rel_bias.astype(jnp.float32)[_rel_bucket(jnp.asarray(np.maximum(x0 - n, 0)))].T
    strip = jnp.tile(by_dist, (1, QB))[:, :QB * (period - 1)].reshape(N_HEADS, QB, period - 1)[:, :, :width]
    far = rel_bias.astype(jnp.float32)[N_BUCKETS - 1] * LOG2_E

    i = np.arange(QB)[:, None]
    toep = jnp.stack([strip[:, :, x0 - QB * m:x0 - QB * m + QB] for m in range(-1, N_TOEP - 1)], axis=1)
    d_toep = QB * np.arange(-1, N_TOEP - 1)[:, None, None] + i[None] - np.arange(QB)[None, None, :]
    toep = jnp.where(d_toep >= 0, toep * LOG2_E, NEG_INF)
    toep = toep.reshape(N_KV_GROUPS, HEADS_PER_GROUP, N_TOEP, QB, QB).transpose(0, 2, 1, 3, 4)
    d_win = WINDOW + i - np.arange(WINDOW + QB)[None, :]
    win = jnp.where((d_win >= 0) & (d_win < WINDOW), strip[:, :, x0 - WINDOW:x0 + QB] * LOG2_E, NEG_INF)
    win = win.reshape(N_KV_GROUPS, HEADS_PER_GROUP, QB, WINDOW + QB)
    bands = []
    for par in range(2):
        off = x0 - QB * par - D_STRIDE * BAND_LEFT + (L_CMP - 1)
        c_first = -(off // D_STRIDE)
        cols = strip[:, :, D_STRIDE * c_first + off::D_STRIDE][:, :, :128 - c_first] * LOG2_E
        left = jnp.broadcast_to(far[:, None, None], (N_HEADS, QB, c_first))
        right = jnp.zeros((N_HEADS, QB, 128 - c_first - cols.shape[2]), jnp.float32)
        d_band = QB * par + i - D_STRIDE * (np.arange(128)[None, :] - BAND_LEFT) - (L_CMP - 1)
        assert (d_band[:, 128 - right.shape[2]:] < 0).all()
        bands.append(jnp.where(d_band >= 0, jnp.concatenate([left, cols, right], axis=-1), NEG_INF))
    band = jnp.stack(bands).reshape(2, N_KV_GROUPS, HEADS_PER_GROUP, QB, 128)
    return toep, win, band, far


def _slc_map_matrix(n_cmp_cols):
    w = np.zeros((n_cmp_cols, NBLK), np.float32)
    for jb in range(NBLK):
        for mm in range(R_SLC):
            for nn in range(R_CMP):
                k = R_SLC * jb + mm - nn
                if 0 <= k < n_cmp_cols:
                    w[k, jb] += 1.0
    return w


def _dot_nt(a, b):
    return lax.dot_general(a, b, (((1,), (1,)), ((), ())), preferred_element_type=jnp.float32)


def _cmp_select_body(far_ref, q_ref, gate_ref, kc_ref, vc_ref, band_ref, wmap_ref, oc_ref, sel_ref,
                     *, n_far):
    qi = pl.program_id(1)
    par = qi % 2
    band0 = pl.multiple_of(16 * (qi // 2 + 1), 16)
    first_band_blk = band0 - CMP_PAD

    col = lax.broadcasted_iota(jnp.int32, (QB, 128), 1)
    row = lax.broadcasted_iota(jnp.int32, (QB, 128), 0)
    neg_pad = jnp.where(first_band_blk + col >= 0, 0.0, NEG_INF)

    t = qi * QB + row
    cur = t // L_SLC
    forced = (col == 0) | (col == cur) | (col == cur - 1)
    n_forced = 1 + (cur[:, :1] >= 1).astype(jnp.int32) + (cur[:, :1] >= 2).astype(jnp.int32)
    valid = col * L_SLC <= t

    q_all = q_ref[0]
    colf = col.astype(jnp.float32)

    def take_best(score, active=None):
        best = jnp.max(score, axis=-1, keepdims=True)
        first = jnp.min(jnp.where(score == best, colf, float(NBLK)), axis=-1, keepdims=True)
        hit = colf == first
        return jnp.where(hit if active is None else hit & active, -jnp.inf, score)

    def branches(n_cols):
        far_col = lax.broadcasted_iota(jnp.int32, (1, max(n_cols, 1)), 1)
        neg_far = jnp.where(far_col < first_band_blk, 0.0, NEG_INF)

        def far_and_band(ref, *lead):
            band_rows = ref[(*lead, pl.ds(band0, 128), slice(None))]
            if not n_cols:
                return band_rows
            return jnp.concatenate([ref[(*lead, slice(CMP_PAD, CMP_PAD + n_cols), slice(None))], band_rows], axis=0)

        wmap_all = far_and_band(wmap_ref)
        scores, start_scores = [], []
        oc_heads = []
        for g in range(N_KV_GROUPS):
            q4 = jnp.concatenate(
                [q_all[:, (g * HEADS_PER_GROUP + h) * HEAD_DIM:(g * HEADS_PER_GROUP + h + 1) * HEAD_DIM]
                 for h in range(HEADS_PER_GROUP)], axis=0)
            k_all = far_and_band(kc_ref, 0, g)
            v_all = far_and_band(vc_ref, 0, g)
            s_all = _dot_nt(q4, k_all).reshape(HEADS_PER_GROUP, QB, n_cols + 128)
            imp = jnp.zeros((QB, n_cols + 128), jnp.float32)
            for h in range(HEADS_PER_GROUP):
                hh = g * HEADS_PER_GROUP + h
                bias = band_ref[par, g, h] + neg_pad
                if n_cols:
                    bias = jnp.concatenate([jnp.broadcast_to(far_ref[hh] + neg_far, (QB, n_cols)), bias],
                                           axis=-1)
                s = s_all[h] + bias
                m = jnp.maximum(jnp.max(s, axis=-1, keepdims=True), 1e-10 * NEG_INF)
                p = jnp.exp2(s - m)
                l = jnp.sum(p, axis=-1, keepdims=True)
                p = p * jnp.where(l > 0.0, 1.0 / l, 0.0)
                imp = imp + p
                o = jnp.dot(p.astype(jnp.bfloat16), v_all, preferred_element_type=jnp.float32)
                oc_heads.append(o * gate_ref[0, :, 3 * hh:3 * hh + 1])
            p_slc = jnp.zeros((QB, NBLK), jnp.float32)
            rest = imp
            for _ in range(3):
                term = rest.astype(jnp.bfloat16)
                p_slc = p_slc + jnp.dot(term, wmap_all, preferred_element_type=jnp.float32)
                rest = rest - term.astype(jnp.float32)
            score = jnp.where(forced, -jnp.inf, jnp.where(valid, p_slc, -SEL_FORCE - colf))
            start_scores.append(score)
            for _ in range(N_SEL - MAX_FORCED):
                score = jnp.where(score == jnp.max(score, axis=-1, keepdims=True), -jnp.inf, score)
            scores.append(score)
        return (jnp.concatenate(oc_heads, axis=-1),) + tuple(scores) + tuple(start_scores)

    n_variants = n_far // 128
    variant = jnp.clip((first_band_blk + 127) // 128, 0, n_variants - 1)
    res = lax.switch(variant, [functools.partial(branches, 128 * v) for v in range(n_variants)])
    oc_ref[0] = res[0]
    scores, start_scores = res[1:1 + N_KV_GROUPS], res[1 + N_KV_GROUPS:]

    expected = (n_forced + (N_SEL - MAX_FORCED)).astype(jnp.float32)
    excess = [jnp.sum(jnp.where(s == -jnp.inf, 1.0, 0.0), axis=-1, keepdims=True) - expected for s in scores]
    any_tie = jnp.max(sum(excess)) > 0.0

    def exact_passes(start_scores):
        for _ in range(N_SEL - MAX_FORCED):
            start_scores = tuple(take_best(s) for s in start_scores)
        return start_scores

    scores = lax.cond(any_tie, exact_passes, lambda _: tuple(scores), tuple(start_scores))

    def early_rows(scores):
        for extra in range(MAX_FORCED - 1):
            scores = tuple(take_best(s, MAX_FORCED - n_forced > extra) for s in scores)
        return scores

    scores = lax.cond(qi == 0, early_rows, lambda s: s, tuple(scores))
    for g in range(N_KV_GROUPS):
        sel_ref[0, g] = jnp.where(scores[g] == -jnp.inf, 0.0, NEG_INF).astype(jnp.bfloat16)


def _cmp_select(q, gates, kc_pad, vc_pad, band, far, wmap):
    B, S, _ = q.shape
    n_far = S // D_STRIDE
    n_pad = kc_pad.shape[2]
    grid = (B, S // QB)
    return pl.pallas_call(
        functools.partial(_cmp_select_body, n_far=n_far),
        grid=grid,
        in_specs=[
            pl.BlockSpec(memory_space=pltpu.SMEM),
            pl.BlockSpec((1, QB, N_HEADS * HEAD_DIM), lambda b, i: (b, i, 0)),
            pl.BlockSpec((1, QB, LANES), lambda b, i: (b, i, 0)),
            pl.BlockSpec((1, N_KV_GROUPS, n_pad, HEAD_DIM), lambda b, i: (b, 0, 0, 0)),
            pl.BlockSpec((1, N_KV_GROUPS, n_pad, HEAD_DIM), lambda b, i: (b, 0, 0, 0)),
            pl.BlockSpec((2, N_KV_GROUPS, HEADS_PER_GROUP, QB, 128), lambda b, i: (0, 0, 0, 0, 0)),
            pl.BlockSpec((n_pad, NBLK), lambda b, i: (0, 0)),
        ],
        out_specs=[
            pl.BlockSpec((1, QB, N_HEADS * HEAD_DIM), lambda b, i: (b, i, 0)),
            pl.BlockSpec((1, N_KV_GROUPS, QB, NBLK), lambda b, i: (b, 0, i, 0)),
        ],
        out_shape=[
            jax.ShapeDtypeStruct((B, S, N_HEADS * HEAD_DIM), jnp.float32),
            jax.ShapeDtypeStruct((B, N_KV_GROUPS, S, NBLK), jnp.bfloat16),
        ],
        compiler_params=pltpu.CompilerParams(
            dimension_semantics=("arbitrary", "arbitrary"),
            vmem_limit_bytes=V7X_VMEM_LIMIT_BYTES,
        ),
        name="nsa_cmp_select",
    )(far, q, gates, kc_pad, vc_pad, band, wmap)


def _win_sel_body(q_ref, gate_ref, sel_ref, oc_ref, ks_ref, vs_ref, kw_ref, vw_ref, toep_ref, win_ref,
                  out_ref, s0_scr, s1_scr, p0_scr, p1_scr, m_scr, acc_scr):
    qi = pl.program_id(2)
    q0 = pl.multiple_of(qi * QB, QB)
    qg = q_ref[0]
    q_heads = [qg[:, h * HEAD_DIM:(h + 1) * HEAD_DIM] for h in range(HEADS_PER_GROUP)]
    q4 = jnp.concatenate(q_heads, axis=0)

    sel = sel_ref[0, 0]
    qa = jnp.concatenate([jnp.concatenate([sel] * HEADS_PER_GROUP, axis=0), q4], axis=-1)
    c_diag = qi // 2

    def scores(c):
        col = pl.multiple_of(c * KEY_TILE, KEY_TILE)
        s = jnp.dot(qa, ks_ref[0, 0, :, pl.ds(col, KEY_TILE)], preferred_element_type=jnp.float32)
        mm = qi - 2 * c
        bias = jnp.concatenate([toep_ref[0, jnp.clip(mm + 1, 0, N_TOEP - 1)],
                                toep_ref[0, jnp.clip(mm, 0, N_TOEP - 1)]], axis=-1)
        return s.reshape(HEADS_PER_GROUP, QB, KEY_TILE) + bias

    s_slots = (s0_scr, s1_scr)
    p_slots = (p0_scr, p1_scr)
    m_scr[...] = jnp.full(m_scr.shape, NEG_INF, jnp.float32)
    acc_scr[...] = jnp.zeros_like(acc_scr)
    p_slots[0][...] = jnp.zeros_like(p0_scr)
    s_slots[0][...] = scores(0)

    def weighted_values(p, col):
        return jnp.dot(p, vs_ref[0, 0, pl.ds(pl.multiple_of(col, KEY_TILE), KEY_TILE), :],
                       preferred_element_type=jnp.float32)

    def half_step(c, col_prev, cur):
        pv = weighted_values(p_slots[cur][...], col_prev)
        s_slots[1 - cur][...] = scores(c + 1)
        for h in range(HEADS_PER_GROUP):
            s = s_slots[cur][h]
            m_old = m_scr[h]
            m_new = jnp.maximum(m_old, jnp.max(s, axis=-1, keepdims=True))
            alpha = jnp.exp2(m_old - m_new)
            p = jnp.exp2(s - jnp.concatenate([m_new] * (KEY_TILE // LANES), axis=-1))
            p_slots[1 - cur][h * QB:(h + 1) * QB, :] = p.astype(jnp.bfloat16)
            m_scr[h] = m_new
            acc_scr[h * QB:(h + 1) * QB, :] = alpha * (acc_scr[h * QB:(h + 1) * QB, :]
                                                      + pv[h * QB:(h + 1) * QB, :])
        return c * KEY_TILE

    def unrolled_steps(first_tile, n_steps):
        def body(j, col_prev):
            for u in range(n_steps):
                col_prev = half_step(first_tile + n_steps * j + u, col_prev, u % 2)
            return col_prev
        return body

    n_tiles = c_diag + 1
    n_long = n_tiles // SEL_UNROLL
    col_last = lax.fori_loop(0, n_long, unrolled_steps(0, SEL_UNROLL), 0)
    col_last = lax.fori_loop(0, (n_tiles - n_long * SEL_UNROLL + 1) // 2,
                             unrolled_steps(n_long * SEL_UNROLL, 2), col_last)
    acc_s = (acc_scr[...] + weighted_values(p_slots[0][...], col_last)).reshape(HEADS_PER_GROUP, QB, LANES)
    o_s = acc_s[:, :, :HEAD_DIM] / acc_s[:, :, HEAD_DIM:HEAD_DIM + 1]

    kw = kw_ref[0, 0, :, pl.ds(q0, WINDOW + QB)]
    s_w = jnp.dot(q4, kw, preferred_element_type=jnp.float32)
    wcol = lax.broadcasted_iota(jnp.int32, (1, WINDOW + QB), 1)
    neg_left = jnp.where(q0 + wcol >= WINDOW, 0.0, NEG_INF)
    s_w = s_w.reshape(HEADS_PER_GROUP, QB, WINDOW + QB) + win_ref[0] + neg_left
    m_w = jnp.max(s_w, axis=-1, keepdims=True)
    p_w = jnp.exp2(s_w - m_w)
    o_w = jnp.dot(p_w.reshape(QROWS, WINDOW + QB).astype(jnp.bfloat16),
                  vw_ref[0, 0, pl.ds(q0, WINDOW + QB), :], preferred_element_type=jnp.float32)
    o_w = o_w.reshape(HEADS_PER_GROUP, QB, LANES)
    o_w = o_w[:, :, :HEAD_DIM] / o_w[:, :, HEAD_DIM:HEAD_DIM + 1]

    outs = []
    for h in range(HEADS_PER_GROUP):
        g_s = gate_ref[0, 0, :, 3 * h + 1:3 * h + 2]
        g_w = gate_ref[0, 0, :, 3 * h + 2:3 * h + 3]
        outs.append(g_s * o_s[h] + g_w * o_w[h])
    out_ref[0] = oc_ref[0] + jnp.concatenate(outs, axis=-1)


def _win_sel(q, gates, selneg, oc, ks_aug, vs, kw_t, vw_pad, toep, win):
    B, S, _ = q.shape
    gw = HEADS_PER_GROUP * HEAD_DIM
    grid = (B, N_KV_GROUPS, S // QB)
    return pl.pallas_call(
        _win_sel_body,
        grid=grid,
        in_specs=[
            pl.BlockSpec((1, QB, gw), lambda b, g, i: (b, i, g)),
            pl.BlockSpec((1, 1, QB, 3 * HEADS_PER_GROUP), lambda b, g, i: (b, g, i, 0)),
            pl.BlockSpec((1, 1, QB, NBLK), lambda b, g, i: (b, g, i, 0)),
            pl.BlockSpec((1, QB, gw), lambda b, g, i: (b, i, g)),
            pl.BlockSpec((1, 1, NBLK + HEAD_DIM, ks_aug.shape[3]), lambda b, g, i: (b, g, 0, 0)),
            pl.BlockSpec((1, 1, vs.shape[2], LANES), lambda b, g, i: (b, g, 0, 0)),
            pl.BlockSpec((1, 1, HEAD_DIM, S + WINDOW), lambda b, g, i: (b, g, 0, 0)),
            pl.BlockSpec((1, 1, S + WINDOW, LANES), lambda b, g, i: (b, g, 0, 0)),
            pl.BlockSpec((1, N_TOEP, HEADS_PER_GROUP, QB, QB), lambda b, g, i: (g, 0, 0, 0, 0)),
            pl.BlockSpec((1, HEADS_PER_GROUP, QB, WINDOW + QB), lambda b, g, i: (g, 0, 0, 0)),
        ],
        out_specs=pl.BlockSpec((1, QB, gw), lambda b, g, i: (b, i, g)),
        out_shape=jax.ShapeDtypeStruct((B, S, N_HEADS * HEAD_DIM), jnp.float32),
        scratch_shapes=[
            pltpu.VMEM((HEADS_PER_GROUP, QB, KEY_TILE), jnp.float32),
            pltpu.VMEM((HEADS_PER_GROUP, QB, KEY_TILE), jnp.float32),
            pltpu.VMEM((QROWS, KEY_TILE), jnp.bfloat16),
            pltpu.VMEM((QROWS, KEY_TILE), jnp.bfloat16),
            pltpu.VMEM((HEADS_PER_GROUP, QB, LANES), jnp.float32),
            pltpu.VMEM((QROWS, LANES), jnp.float32),
        ],
        compiler_params=pltpu.CompilerParams(
            dimension_semantics=("arbitrary", "arbitrary", "arbitrary"),
            vmem_limit_bytes=V7X_VMEM_LIMIT_BYTES,
        ),
        name="nsa_win_sel",
    )(q, gates, selneg, oc, ks_aug, vs, kw_t, vw_pad, toep, win)


def _nsa_attention(q, gates, kc, vc, kvsw, rel_bias):
    B, S, _ = q.shape
    assert S % KEY_TILE == 0 and S // L_SLC <= NBLK
    bf = jnp.bfloat16
    n_far = S // D_STRIDE
    toep, win, band, far = _bias_tables(rel_bias)
    cpad = ((0, 0), (0, 0), (CMP_PAD, n_far - kc.shape[2] + CMP_PAD), (0, 0))
    kc_pad = jnp.pad(kc, cpad).astype(bf)
    vc_pad = jnp.pad(vc, cpad).astype(bf)
    wmap = jnp.asarray(np.pad(_slc_map_matrix(n_far), ((CMP_PAD, CMP_PAD), (0, 0))), bf)
    oc, selneg = _cmp_select(q, gates, kc_pad, vc_pad, band, far, wmap)
    ks, vs, kw, vw = (kvsw.reshape(B, S, 4, N_KV_GROUPS, HEAD_DIM)[:, :, n] for n in range(4))
    blk_onehot = jnp.asarray((np.arange(S)[None, :] // L_SLC == np.arange(NBLK)[:, None]), bf)
    ks_aug = jnp.concatenate(
        [jnp.broadcast_to(blk_onehot, (B, N_KV_GROUPS, NBLK, S)), ks.transpose(0, 2, 3, 1)], axis=2)
    tail = 2 * KEY_TILE
    ks_aug = jnp.pad(ks_aug, ((0, 0), (0, 0), (0, 0), (0, tail)))
    kw_t = jnp.pad(kw.transpose(0, 2, 3, 1), ((0, 0), (0, 0), (0, 0), (WINDOW, 0)))

    def with_ones(v):
        v = v.transpose(0, 2, 1, 3)
        ones = jnp.ones(v.shape[:-1] + (1,), v.dtype)
        return jnp.pad(jnp.concatenate([v, ones], axis=-1), ((0, 0),) * 3 + ((0, LANES - HEAD_DIM - 1),))

    vw_aug = jnp.pad(with_ones(vw), ((0, 0), (0, 0), (WINDOW, 0), (0, 0)))
    vs_aug = jnp.pad(with_ones(vs), ((0, 0), (0, 0), (0, tail), (0, 0)))
    gates_g = gates[:, :, :3 * N_HEADS].reshape(B, S, N_KV_GROUPS, 3 * HEADS_PER_GROUP).swapaxes(1, 2)
    return _win_sel(q, gates_g, selneg, oc, ks_aug, vs_aug, kw_t, vw_aug, toep, win)


def _pool_layer_body(x_ref, halo_ref, gain_ref, shift_ref, scale_ref, gate_ref, pscale_ref, w_ref, o_ref):
    i = pl.program_id(1)
    x = x_ref[0]
    gain, shift, scale = gain_ref[...], shift_ref[0], scale_ref[0]
    h = _norm_modulate(x, gain, shift, scale)
    halo = _norm_modulate(halo_ref[0], gain, shift, scale) * (i > 0).astype(jnp.float32)
    hx = jnp.concatenate([halo, h], axis=0)
    t = i * POOL_TILE + lax.broadcasted_iota(jnp.int32, (POOL_TILE, 1), 0)
    outs = []
    run = hx
    width = 1
    for g, w in enumerate(POOL_WINDOWS):
        while width < w:
            run = run + pltpu.roll(run, width, axis=0)
            width *= 2
        lanes = slice(0, POOL_GROUP_DIM)
        cnt = jnp.minimum(t + 1, w).astype(jnp.float32)
        mix = run[POOL_HALO:, lanes] / cnt - h[:, g * POOL_GROUP_DIM:(g + 1) * POOL_GROUP_DIM]
        outs.append(jnp.dot(mix.astype(jnp.bfloat16), w_ref[g], preferred_element_type=jnp.float32))
        run = run[:, POOL_GROUP_DIM:]
    y = jnp.concatenate(outs, axis=-1) * pscale_ref[...]
    o_ref[0] = x + gate_ref[0] * y


def _pool_layer(x, gain, shift, scale, gate1, w_grp, pool_scale):
    B, S, D = x.shape
    assert S % POOL_TILE == 0 and POOL_HALO >= max(POOL_WINDOWS) - 1
    per_batch = pl.BlockSpec((1, 1, D), lambda b, i: (b, 0, 0))
    halo_blocks = POOL_TILE // POOL_HALO
    return pl.pallas_call(
        _pool_layer_body,
        grid=(B, S // POOL_TILE),
        in_specs=[
            pl.BlockSpec((1, POOL_TILE, D), lambda b, i: (b, i, 0)),
            pl.BlockSpec((1, POOL_HALO, D), lambda b, i: (b, jnp.maximum(i * halo_blocks - 1, 0), 0)),
            pl.BlockSpec((1, D), lambda b, i: (0, 0)),
            per_batch, per_batch, per_batch,
            pl.BlockSpec((1, D), lambda b, i: (0, 0)),
            pl.BlockSpec((N_POOL_GROUPS, POOL_GROUP_DIM, POOL_GROUP_DIM), lambda b, i: (0, 0, 0)),
        ],
        out_specs=pl.BlockSpec((1, POOL_TILE, D), lambda b, i: (b, i, 0)),
        out_shape=jax.ShapeDtypeStruct((B, S, D), jnp.float32),
        compiler_params=pltpu.CompilerParams(
            dimension_semantics=("arbitrary", "arbitrary"),
            vmem_limit_bytes=V7X_VMEM_LIMIT_BYTES,
        ),
        name="pool_layer",
    )(x, x, gain.reshape(1, D), shift[:, None, :], scale[:, None, :], gate1[:, None, :],
      pool_scale.reshape(1, D), w_grp.astype(jnp.bfloat16))


def _head_norm(v, seg_ref, gain):
    sq = v * v
    hi = sq.astype(jnp.bfloat16)
    lo = (sq - hi.astype(jnp.float32)).astype(jnp.bfloat16)
    ss = (jnp.dot(hi, seg_ref[...], preferred_element_type=jnp.float32)
          + jnp.dot(lo, seg_ref[...], preferred_element_type=jnp.float32))
    return v * lax.rsqrt(ss * (1.0 / HEAD_DIM) + EPS) * gain


def _qkv_proj_body(x_ref, gq_ref, shq_ref, scq_ref, gkv_ref, shkv_ref, sckv_ref, wq_ref, wg_ref, wkv_ref,
                   seg_ref, qgain_ref, kgain_ref, q_ref, gates_ref, kvc_ref, kvsw_ref):
    x = x_ref[...]
    xhat = x * lax.rsqrt(jnp.mean(x * x, axis=-1, keepdims=True) + EPS)
    hq = ((xhat * gq_ref[...]) * (1.0 + scq_ref[0]) + shq_ref[0]).astype(jnp.bfloat16)
    hkv = ((xhat * gkv_ref[...]) * (1.0 + sckv_ref[0]) + shkv_ref[0]).astype(jnp.bfloat16)
    pq = jnp.dot(hq, wq_ref[...], preferred_element_type=jnp.float32)
    gates_ref[...] = jax.nn.sigmoid(jnp.dot(hq, wg_ref[...], preferred_element_type=jnp.float32))
    pkv = jnp.dot(hkv, wkv_ref[...], preferred_element_type=jnp.float32)
    for g in range(N_KV_GROUPS):
        lanes = slice(g * GROUP_W, (g + 1) * GROUP_W)
        q_ref[:, lanes] = _head_norm(pq[:, lanes], seg_ref, qgain_ref[...]).astype(jnp.bfloat16)
    kvc_ref[...] = pkv[:, :2 * GROUP_W]
    k_s = _head_norm(pkv[:, 2 * GROUP_W:3 * GROUP_W], seg_ref, kgain_ref[0:1, :])
    k_w = _head_norm(pkv[:, 4 * GROUP_W:5 * GROUP_W], seg_ref, kgain_ref[1:2, :])
    kvsw_ref[...] = jnp.concatenate([k_s, pkv[:, 3 * GROUP_W:4 * GROUP_W], k_w, pkv[:, 5 * GROUP_W:]],
                                    axis=-1).astype(jnp.bfloat16)


def _qkv_proj(x, gain_q, shift_q, scale_q, gain_kv, shift_kv, scale_kv, w_qg, kv_w, q_gain, k_gain):
    B, S, D = x.shape
    T = B * S
    qd = N_HEADS * HEAD_DIM
    bf = jnp.bfloat16
    tiles_per_batch = S // ROW_TILE
    per_batch = pl.BlockSpec((1, 1, D), lambda i: (i // tiles_per_batch, 0, 0))
    const = lambda shape: pl.BlockSpec(shape, lambda i: (0,) * len(shape))
    seg = jnp.asarray(np.kron(np.eye(HEADS_PER_GROUP), np.ones((HEAD_DIM, HEAD_DIM))), bf)
    w_gate = jnp.pad(w_qg[:, qd:], ((0, 0), (0, LANES - 3 * N_HEADS))).astype(bf)
    qgain = jnp.tile(q_gain * HEAD_DIM ** -0.5, HEADS_PER_GROUP).reshape(1, GROUP_W)
    kgain = jnp.stack([jnp.tile(k_gain[1] * LOG2_E, N_KV_GROUPS), jnp.tile(k_gain[2] * LOG2_E, N_KV_GROUPS)])
    rows = lambda width: pl.BlockSpec((ROW_TILE, width), lambda i: (i, 0))
    return pl.pallas_call(
        _qkv_proj_body,
        grid=(T // ROW_TILE,),
        in_specs=[rows(D), const((1, D)), per_batch, per_batch, const((1, D)), per_batch, per_batch,
                  const((D, qd)), const((D, LANES)), const((D, 6 * GROUP_W)), const((GROUP_W, GROUP_W)),
                  const((1, GROUP_W)), const((2, GROUP_W))],
        out_specs=[rows(qd), rows(LANES), rows(2 * GROUP_W), rows(4 * GROUP_W)],
        out_shape=[jax.ShapeDtypeStruct((T, qd), bf), jax.ShapeDtypeStruct((T, LANES), jnp.float32),
                   jax.ShapeDtypeStruct((T, 2 * GROUP_W), jnp.float32),
                   jax.ShapeDtypeStruct((T, 4 * GROUP_W), bf)],
        compiler_params=pltpu.CompilerParams(
            dimension_semantics=("arbitrary",), vmem_limit_bytes=V7X_VMEM_LIMIT_BYTES),
        name="qkv_proj",
    )(x.reshape(T, D), gain_q.reshape(1, D), shift_q[:, None, :], scale_q[:, None, :],
      gain_kv.reshape(1, D), shift_kv[:, None, :], scale_kv[:, None, :],
      w_qg[:, :qd].astype(bf), w_gate, kv_w.astype(bf), seg, qgain, kgain)


def _compress_body(r_ref, pe_ref, w1_ref, w2_ref, gain_ref, o_ref):
    half = D_STRIDE * HEAD_DIM
    r = r_ref[0, 0, 0]
    top = jnp.dot(r, w1_ref[0, :half, :], preferred_element_type=jnp.float32)
    bot = jnp.dot(r, w1_ref[0, half:, :], preferred_element_type=jnp.float32)
    pe = jnp.dot(jnp.broadcast_to(pe_ref[0], (8, R_CMP * half)).astype(jnp.bfloat16), w1_ref[0],
                 preferred_element_type=jnp.float32)[0:1]
    n_chunks = r.shape[0]
    hidden = top + pltpu.roll(bot, n_chunks - 1, axis=0) + pe
    out = jnp.dot(jax.nn.gelu(hidden).astype(jnp.bfloat16), w2_ref[0], preferred_element_type=jnp.float32)
    normed = out * lax.rsqrt(jnp.mean(out * out, axis=-1, keepdims=True) + EPS) * gain_ref[...]
    o_ref[0, 0, 0] = jnp.where(pl.program_id(0) == 0, normed, out)


def _compressed_kv(kvc, B, S, cmp_pe_k, cmp_pe_v, cmp_k_w1, cmp_k_w2, cmp_v_w1, cmp_v_w2, k_gain):
    bf = jnp.bfloat16
    n_chunks = S // D_STRIDE
    half = D_STRIDE * HEAD_DIM
    hidden = cmp_k_w1.shape[1]
    r = kvc.astype(bf).reshape(B, n_chunks, D_STRIDE, 2, N_KV_GROUPS, HEAD_DIM)
    r = r.transpose(3, 0, 4, 1, 2, 5).reshape(2, B, N_KV_GROUPS, n_chunks, half)
    pe = jnp.stack([cmp_pe_k, cmp_pe_v]).reshape(2, 1, R_CMP * half)
    per_kind = lambda shape: pl.BlockSpec((1,) + shape, lambda kv, b, g: (kv,) + (0,) * len(shape))
    out = pl.pallas_call(
        _compress_body,
        grid=(2, B, N_KV_GROUPS),
        in_specs=[
            pl.BlockSpec((1, 1, 1, n_chunks, half), lambda kv, b, g: (kv, b, g, 0, 0)),
            per_kind((1, R_CMP * half)), per_kind((R_CMP * half, hidden)), per_kind((hidden, HEAD_DIM)),
            pl.BlockSpec((1, HEAD_DIM), lambda kv, b, g: (0, 0)),
        ],
        out_specs=pl.BlockSpec((1, 1, 1, n_chunks, HEAD_DIM), lambda kv, b, g: (kv, b, g, 0, 0)),
        out_shape=jax.ShapeDtypeStruct((2, B, N_KV_GROUPS, n_chunks, HEAD_DIM), jnp.float32),
        compiler_params=pltpu.CompilerParams(
            dimension_semantics=("arbitrary", "arbitrary", "arbitrary"),
            vmem_limit_bytes=V7X_VMEM_LIMIT_BYTES),
        name="cmp_mlp",
    )(r, pe, jnp.stack([cmp_k_w1, cmp_v_w1]).astype(bf), jnp.stack([cmp_k_w2, cmp_v_w2]).astype(bf),
      (k_gain[0] * LOG2_E).reshape(1, HEAD_DIM))
    n_cmp = n_chunks - R_CMP + 1
    return out[0, :, :, :n_cmp], out[1, :, :, :n_cmp]


def _oproj_router_body(a_ref, x_ref, gate_ref, wo_ref, gain_ref, shift_ref, scale_ref, wr_ref, br_ref,
                       o_ref, logit_ref):
    mix = jnp.dot(a_ref[...].astype(jnp.bfloat16), wo_ref[...], preferred_element_type=jnp.float32)
    x = x_ref[...] + gate_ref[0] * mix
    o_ref[...] = x
    h = _norm_modulate(x, gain_ref[...], shift_ref[0], scale_ref[0]).astype(jnp.bfloat16)
    logit_ref[...] = jnp.dot(h, wr_ref[...], preferred_element_type=jnp.float32) + br_ref[...]


def _oproj_router(attn, x, gate1, w_o, gain, shift, scale, w_router, b_router):
    B, S, D = x.shape
    T = B * S
    bf = jnp.bfloat16
    tiles_per_batch = S // ROW_TILE
    per_batch = pl.BlockSpec((1, 1, D), lambda i: (i // tiles_per_batch, 0, 0))
    const = lambda shape: pl.BlockSpec(shape, lambda i: (0,) * len(shape))
    rows = lambda width: pl.BlockSpec((ROW_TILE, width), lambda i: (i, 0))
    pad = ((0, 0), (0, LANES - N_EXPERTS))
    out, logits = pl.pallas_call(
        _oproj_router_body,
        grid=(T // ROW_TILE,),
        in_specs=[rows(D), rows(D), per_batch, const((D, D)), const((1, D)), per_batch, per_batch,
                  const((D, LANES)), const((1, LANES))],
        out_specs=[rows(D), rows(LANES)],
        out_shape=[jax.ShapeDtypeStruct((T, D), jnp.float32), jax.ShapeDtypeStruct((T, LANES), jnp.float32)],
        compiler_params=pltpu.CompilerParams(
            dimension_semantics=("arbitrary",), vmem_limit_bytes=V7X_VMEM_LIMIT_BYTES),
        name="oproj_router",
    )(attn.reshape(T, D), x.reshape(T, D), gate1[:, None, :], w_o.astype(bf), gain.reshape(1, D),
      shift[:, None, :], scale[:, None, :], jnp.pad(w_router, pad).astype(bf),
      jnp.pad(b_router.reshape(1, N_EXPERTS), pad))
    return out.reshape(B, S, D), logits[:, :N_EXPERTS]


def _route(logits):
    T = logits.shape[0]
    lane = jnp.arange(N_EXPERTS)[None, :]
    l0 = jnp.max(logits, axis=-1)
    e0 = jnp.argmax(logits, axis=-1)
    rest = jnp.where(lane == e0[:, None], -jnp.inf, logits)
    l1 = jnp.max(rest, axis=-1)
    e1 = jnp.argmax(rest, axis=-1)
    z = jnp.exp(l1 - l0)
    top_w = jnp.stack([1.0 / (1.0 + z), z / (1.0 + z)], axis=-1)
    oh = [(lane == e[:, None]).astype(jnp.int32) for e in (e0, e1)]
    cnt = oh[0] + oh[1]
    before = jnp.cumsum(cnt, axis=0) - cnt
    counts = before[-1] + cnt[-1]
    pcounts = (counts + ROW_TILE - 1) // ROW_TILE * ROW_TILE
    pends = jnp.cumsum(pcounts)
    pstarts = pends - pcounts
    dest = jnp.stack([jnp.sum((pstarts[None, :] + before) * o, axis=-1) for o in oh], axis=-1)
    n_blocks = -(-(T * TOP_K + N_EXPERTS * (ROW_TILE - 1)) // ROW_TILE)
    blk_start = jnp.arange(n_blocks)[:, None] * ROW_TILE
    blk_e = jnp.minimum(jnp.sum(pends[None, :] <= blk_start, axis=-1), N_EXPERTS - 1).astype(jnp.int32)
    return top_w, dest.astype(jnp.int32), blk_e, n_blocks * ROW_TILE


def _moe_ffn(x, logits, gain, shift, scale, gate2, w_gu, w_dn):
    top_w, dest, blk_e, n_rows = _route(logits)
    rows = _moe_dispatch(x, gain, shift, scale, dest, n_rows)
    y = _grouped_swiglu(rows, blk_e, w_gu.astype(jnp.bfloat16), w_dn.astype(jnp.bfloat16),
                        ff_tile=FF_TILE_EXPERT)
    return _moe_combine(x, top_w, gate2, y, dest)


def kernel(x, c, ada_w, ada_b, norm_g, pool_w, pool_scale, q_w, q_gain, o_w, kv_ada_w, kv_ada_b, kv_norm_g, kv_w, cmp_pe_k, cmp_pe_v, cmp_k_w1, cmp_k_w2, cmp_v_w1, cmp_v_w2, k_gain, rel_bias, ffn_gu, ffn_dn, router_w, router_b, exp_gu, exp_dn):
    assert DEPTH == 2 and N_A_LAYERS == 1
    B, S, D = x.shape
    silu_c = jax.nn.silu(c)
    sh1, sc1, g1, sh2, sc2, g2 = jnp.split(silu_c @ ada_w[0] + ada_b[0], 6, axis=-1)
    x = _pool_layer(x, norm_g[0, 0], sh1, sc1, g1, pool_w[0], pool_scale[0])
    x = _dense_ffn(x, norm_g[0, 1], sh2, sc2, g2, ffn_gu[0], ffn_dn[0])
    sh1, sc1, g1, sh2, sc2, g2 = jnp.split(silu_c @ ada_w[1] + ada_b[1], 6, axis=-1)
    sh_kv, sc_kv = jnp.split(silu_c @ kv_ada_w + kv_ada_b, 2, axis=-1)
    q, gates, kvc, kvsw = _qkv_proj(x, norm_g[1, 0], sh1, sc1, kv_norm_g, sh_kv, sc_kv, q_w[0], kv_w,
                                    q_gain[0], k_gain)
    kc, vc = _compressed_kv(kvc, B, S, cmp_pe_k, cmp_pe_v, cmp_k_w1, cmp_k_w2, cmp_v_w1, cmp_v_w2, k_gain)
    attn = _nsa_attention(q.reshape(B, S, -1), gates.reshape(B, S, -1), kc, vc, kvsw.reshape(B, S, -1),
                          rel_bias)
    x, logits = _oproj_router(attn, x, g1, o_w[0], norm_g[1, 1], sh2, sc2, router_w[0], router_b[0])
    return _moe_ffn(x, logits, norm_g[1, 1], sh2, sc2, g2, exp_gu[0], exp_dn[0])
```

```python
import functools
import math

import jax
import jax.numpy as jnp
import numpy as np
from jax import lax
from jax.experimental import pallas as pl
from jax.experimental.pallas import tpu as pltpu

D_MODEL = 1024
DEPTH = 2
N_A_LAYERS = DEPTH // 2
POOL_WINDOWS = (2, 4, 8, 16)
N_POOL_GROUPS = len(POOL_WINDOWS)
POOL_GROUP_DIM = D_MODEL // N_POOL_GROUPS
HEAD_DIM = 64
N_HEADS = D_MODEL // HEAD_DIM
N_KV_GROUPS = 4
HEADS_PER_GROUP = N_HEADS // N_KV_GROUPS
L_CMP = 32
D_STRIDE = 16
L_SLC = 64
N_SEL = 16
WINDOW = 512
R_CMP = L_CMP // D_STRIDE
R_SLC = L_SLC // D_STRIDE
N_BUCKETS = 32
REL_EXACT = N_BUCKETS // 2
MAX_DISTANCE = 1024
N_EXPERTS = 8
TOP_K = 2
EPS = 1e-6
NEG_INF = -1e30
SEL_FORCE = 1e6
LOG2_E = math.log2(math.e)

V7X_VMEM_LIMIT_BYTES = 48 * 1024 * 1024
LANES = 128
ROW_TILE = 512
TOK_TILE = 512
DMA_ISSUE_UNROLL = 8
MAX_FORCED = 3
FF_TILE_DENSE = 1408
FF_TILE_EXPERT = 896
POOL_TILE = 512
POOL_HALO = 16

QB = 128
KEY_TILE = 2 * QB
SEL_UNROLL = 8
QROWS = HEADS_PER_GROUP * QB
WIN_HEADS = 4
GROUP_W = HEADS_PER_GROUP * HEAD_DIM
NBLK = 128
CMP_PAD = 128
FAR_DIST = MAX_DISTANCE
N_TOEP = FAR_DIST // QB + 3
BAND_LEFT = CMP_PAD - 16
assert BAND_LEFT * D_STRIDE + (L_CMP - 1) - 2 * QB >= FAR_DIST


def _rms_norm(x, g):
    xf = x.astype(jnp.float32)
    y = xf * lax.rsqrt(jnp.mean(xf * xf, axis=-1, keepdims=True) + EPS)
    return (y * g.astype(jnp.float32)).astype(x.dtype)


def _modulate(h, shift, scale):
    return h * (1 + scale[:, None, :]) + shift[:, None, :]


def _rel_bucket(dist):
    d = jnp.maximum(dist, 0)
    ratio = jnp.maximum(d, REL_EXACT).astype(jnp.float32) / REL_EXACT
    large = REL_EXACT + (jnp.log(ratio) / math.log(MAX_DISTANCE / REL_EXACT)
                         * (N_BUCKETS - REL_EXACT)).astype(jnp.int32)
    return jnp.where(d < REL_EXACT, d, jnp.minimum(large, N_BUCKETS - 1))


def _norm_modulate(x, gain, shift, scale):
    y = x * lax.rsqrt(jnp.mean(x * x, axis=-1, keepdims=True) + EPS)
    return (y * gain) * (1.0 + scale) + shift


def _swiglu_step(xb_ref, wg_ref, wu_ref, wd_ref, acc_ref, j):
    @pl.when(j == 0)
    def _():
        acc_ref[...] = jnp.zeros_like(acc_ref)

    x = xb_ref[...]
    gate = jnp.dot(x, wg_ref[0], preferred_element_type=jnp.float32)
    up = jnp.dot(x, wu_ref[0], preferred_element_type=jnp.float32)
    act = (gate * jax.nn.sigmoid(gate) * up).astype(jnp.bfloat16)
    acc_ref[...] += jnp.dot(act, wd_ref[0], preferred_element_type=jnp.float32)


def _grouped_swiglu_body(blk_e_ref, x_ref, wg_ref, wu_ref, wd_ref, o_ref, acc_ref, xb_ref, *, n_ff_steps):
    del blk_e_ref
    j = pl.program_id(1)

    @pl.when(j == 0)
    def _():
        xb_ref[...] = x_ref[...].astype(jnp.bfloat16)

    _swiglu_step(xb_ref, wg_ref, wu_ref, wd_ref, acc_ref, j)

    @pl.when(j == n_ff_steps - 1)
    def _():
        o_ref[...] = acc_ref[...]


def _grouped_swiglu(x_rows, blk_e, w_gu, w_dn, *, ff_tile):
    n_rows, d = x_rows.shape
    d_ff = w_dn.shape[1]
    assert n_rows % ROW_TILE == 0 and d_ff % ff_tile == 0
    n_ff_steps = d_ff // ff_tile
    grid = (n_rows // ROW_TILE, n_ff_steps)
    return pl.pallas_call(
        functools.partial(_grouped_swiglu_body, n_ff_steps=n_ff_steps),
        grid_spec=pltpu.PrefetchScalarGridSpec(
            num_scalar_prefetch=1,
            grid=grid,
            in_specs=[
                pl.BlockSpec((ROW_TILE, d), lambda i, j, e: (i, 0)),
                pl.BlockSpec((1, d, ff_tile), lambda i, j, e: (e[i], 0, j)),
                pl.BlockSpec((1, d, ff_tile), lambda i, j, e: (e[i], 0, j + n_ff_steps)),
                pl.BlockSpec((1, ff_tile, d), lambda i, j, e: (e[i], j, 0)),
            ],
            out_specs=pl.BlockSpec((ROW_TILE, d), lambda i, j, e: (i, 0)),
            scratch_shapes=[pltpu.VMEM((ROW_TILE, d), jnp.float32), pltpu.VMEM((ROW_TILE, d), jnp.bfloat16)],
        ),
        out_shape=jax.ShapeDtypeStruct((n_rows, d), jnp.float32),
        compiler_params=pltpu.CompilerParams(
            dimension_semantics=("arbitrary", "arbitrary"),
            vmem_limit_bytes=V7X_VMEM_LIMIT_BYTES,
        ),
        name="grouped_swiglu",
    )(blk_e, x_rows, w_gu, w_gu, w_dn)


def _dense_ffn_body(x_ref, gain_ref, shift_ref, scale_ref, gate2_ref, wg_ref, wu_ref, wd_ref, o_ref,
                    acc_ref, xb_ref, *, n_ff_steps):
    j = pl.program_id(1)

    @pl.when(j == 0)
    def _():
        xb_ref[...] = _norm_modulate(x_ref[...], gain_ref[...], shift_ref[0], scale_ref[0]).astype(jnp.bfloat16)

    _swiglu_step(xb_ref, wg_ref, wu_ref, wd_ref, acc_ref, j)

    @pl.when(j == n_ff_steps - 1)
    def _():
        o_ref[...] = x_ref[...] + gate2_ref[0] * acc_ref[...]


def _dense_ffn(x, gain, shift, scale, gate2, w_gu, w_dn):
    B, S, D = x.shape
    d_ff = w_dn.shape[0]
    n_ff_steps = d_ff // FF_TILE_DENSE
    tiles_per_batch = S // ROW_TILE
    per_batch = pl.BlockSpec((1, 1, D), lambda i, j: (i // tiles_per_batch, 0, 0))
    out = pl.pallas_call(
        functools.partial(_dense_ffn_body, n_ff_steps=n_ff_steps),
        grid=(B * S // ROW_TILE, n_ff_steps),
        in_specs=[
            pl.BlockSpec((ROW_TILE, D), lambda i, j: (i, 0)),
            pl.BlockSpec((1, D), lambda i, j: (0, 0)),
            per_batch, per_batch, per_batch,
            pl.BlockSpec((1, D, FF_TILE_DENSE), lambda i, j: (0, 0, j)),
            pl.BlockSpec((1, D, FF_TILE_DENSE), lambda i, j: (0, 0, j + n_ff_steps)),
            pl.BlockSpec((1, FF_TILE_DENSE, D), lambda i, j: (0, j, 0)),
        ],
        out_specs=pl.BlockSpec((ROW_TILE, D), lambda i, j: (i, 0)),
        out_shape=jax.ShapeDtypeStruct((B * S, D), jnp.float32),
        scratch_shapes=[pltpu.VMEM((ROW_TILE, D), jnp.float32), pltpu.VMEM((ROW_TILE, D), jnp.bfloat16)],
        compiler_params=pltpu.CompilerParams(
            dimension_semantics=("arbitrary", "arbitrary"),
            vmem_limit_bytes=V7X_VMEM_LIMIT_BYTES,
        ),
        name="dense_ffn",
    )(x.reshape(B * S, D), gain.reshape(1, D), shift[:, None, :], scale[:, None, :], gate2[:, None, :],
      w_gu.astype(jnp.bfloat16)[None], w_gu.astype(jnp.bfloat16)[None], w_dn.astype(jnp.bfloat16)[None])
    return out.reshape(B, S, D)


def _row_copies_wait(src_ref, dst_ref, sem, n_rows):
    pltpu.make_async_copy(src_ref.at[pl.ds(0, n_rows)], dst_ref.at[pl.ds(0, n_rows)], sem).wait()


def _dispatch_body(dest_ref, x_ref, gain_ref, shift_ref, scale_ref, rows_in_ref, rows_ref, h_ref, sem):
    del rows_in_ref
    h_ref[...] = _norm_modulate(x_ref[...], gain_ref[...], shift_ref[0], scale_ref[0])

    def issue(r, carry):
        for k in range(TOP_K):
            pltpu.make_async_copy(h_ref.at[pl.ds(r, 1)], rows_ref.at[pl.ds(dest_ref[TOP_K * r + k], 1)],
                                  sem).start()
        return carry

    lax.fori_loop(0, TOK_TILE, issue, 0, unroll=DMA_ISSUE_UNROLL)
    for k in range(TOP_K):
        _row_copies_wait(h_ref, rows_ref, sem, TOK_TILE)


def _moe_dispatch(x, gain, shift, scale, dest, n_rows):
    B, S, D = x.shape
    T = B * S
    tiles_per_batch = S // TOK_TILE
    per_batch = pl.BlockSpec((1, 1, D), lambda i: (i // tiles_per_batch, 0, 0))
    return pl.pallas_call(
        _dispatch_body,
        grid=(T // TOK_TILE,),
        in_specs=[
            pl.BlockSpec((TOP_K * TOK_TILE,), lambda i: (i,), memory_space=pltpu.SMEM),
            pl.BlockSpec((TOK_TILE, D), lambda i: (i, 0)),
            pl.BlockSpec((1, D), lambda i: (0, 0)),
            per_batch, per_batch,
            pl.BlockSpec(memory_space=pl.ANY),
        ],
        out_specs=pl.BlockSpec(memory_space=pl.ANY),
        out_shape=jax.ShapeDtypeStruct((n_rows, D), jnp.float32),
        scratch_shapes=[pltpu.VMEM((TOK_TILE, D), jnp.float32), pltpu.SemaphoreType.DMA(())],
        input_output_aliases={5: 0},
        compiler_params=pltpu.CompilerParams(dimension_semantics=("arbitrary",)),
        name="moe_dispatch",
    )(dest.reshape(T * TOP_K), x.reshape(T, D), gain.reshape(1, D), shift[:, None, :], scale[:, None, :],
      jnp.zeros((n_rows, D), jnp.float32))


def _combine_body(dest_ref, x_ref, w_ref, gate2_ref, y_ref, o_ref, buf_ref, sem):
    def issue(r, carry):
        for k in range(TOP_K):
            pltpu.make_async_copy(y_ref.at[pl.ds(dest_ref[TOP_K * r + k], 1)], buf_ref.at[k, pl.ds(r, 1)],
                                  sem).start()
        return carry

    lax.fori_loop(0, TOK_TILE, issue, 0, unroll=DMA_ISSUE_UNROLL)
    for k in range(TOP_K):
        _row_copies_wait(y_ref, buf_ref.at[k], sem, TOK_TILE)
    w = w_ref[...]
    f = w[:, 0:1] * buf_ref[0] + w[:, 1:2] * buf_ref[1]
    o_ref[...] = x_ref[...] + gate2_ref[0] * f


def _moe_combine(x, top_w, gate2, y, dest):
    B, S, D = x.shape
    T = B * S
    tiles_per_batch = S // TOK_TILE
    out = pl.pallas_call(
        _combine_body,
        grid=(T // TOK_TILE,),
        in_specs=[
            pl.BlockSpec((TOP_K * TOK_TILE,), lambda i: (i,), memory_space=pltpu.SMEM),
            pl.BlockSpec((TOK_TILE, D), lambda i: (i, 0)),
            pl.BlockSpec((TOK_TILE, TOP_K), lambda i: (i, 0)),
            pl.BlockSpec((1, 1, D), lambda i: (i // tiles_per_batch, 0, 0)),
            pl.BlockSpec(memory_space=pl.ANY),
        ],
        out_specs=pl.BlockSpec((TOK_TILE, D), lambda i: (i, 0)),
        out_shape=jax.ShapeDtypeStruct((T, D), jnp.float32),
        scratch_shapes=[pltpu.VMEM((TOP_K, TOK_TILE, D), jnp.float32), pltpu.SemaphoreType.DMA(())],
        compiler_params=pltpu.CompilerParams(dimension_semantics=("arbitrary",)),
        name="moe_combine",
    )(dest.reshape(T * TOP_K), x.reshape(T, D), top_w, gate2[:, None, :], y)
    return out.reshape(B, S, D)


def _bias_tables(rel_bias):
    x0 = (N_TOEP - 1) * QB
    width = x0 + 2 * QB
    period = width + QB
    n = np.arange(period)
    n = np.where(n < width, n, n - period)
    by_dist = rel_bias.astype(jnp.float32)[_rel_bucket(jnp.asarray(np.maximum(x0 - n, 0)))].T
    strip = jnp.tile(by_dist, (1, QB))[:, :QB * (period - 1)].reshape(N_HEADS, QB, period - 1)[:, :, :width]
    far = rel_bias.astype(jnp.float32)[N_BUCKETS - 1] * LOG2_E

    i = np.arange(QB)[:, None]
    toep = jnp.stack([strip[:, :, x0 - QB * m:x0 - QB * m + QB] for m in range(-1, N_TOEP - 1)], axis=1)
    d_toep = QB * np.arange(-1, N_TOEP - 1)[:, None, None] + i[None] - np.arange(QB)[None, None, :]
    toep = jnp.where(d_toep >= 0, toep * LOG2_E, NEG_INF)
    toep = toep.reshape(N_KV_GROUPS, HEADS_PER_GROUP, N_TOEP, QB, QB).transpose(0, 2, 1, 3, 4)
    d_win = WINDOW + i - np.arange(WINDOW + QB)[None, :]
    win = jnp.where((d_win >= 0) & (d_win < WINDOW), strip[:, :, x0 - WINDOW:x0 + QB] * LOG2_E, NEG_INF)
    win = win.reshape(N_KV_GROUPS, HEADS_PER_GROUP, QB, WINDOW + QB)
    bands = []
    for par in range(2):
        off = x0 - QB * par - D_STRIDE * BAND_LEFT + (L_CMP - 1)
        c_first = -(off // D_STRIDE)
        cols = strip[:, :, D_STRIDE * c_first + off::D_STRIDE][:, :, :128 - c_first] * LOG2_E
        left = jnp.broadcast_to(far[:, None, None], (N_HEADS, QB, c_first))
        right = jnp.zeros((N_HEADS, QB, 128 - c_first - cols.shape[2]), jnp.float32)
        d_band = QB * par + i - D_STRIDE * (np.arange(128)[None, :] - BAND_LEFT) - (L_CMP - 1)
        assert (d_band[:, 128 - right.shape[2]:] < 0).all()
        bands.append(jnp.where(d_band >= 0, jnp.concatenate([left, cols, right], axis=-1), NEG_INF))
    band = jnp.stack(bands).reshape(2, N_KV_GROUPS, HEADS_PER_GROUP, QB, 128)
    return toep, win, band, far


def _slc_map_matrix(n_cmp_cols):
    w = np.zeros((n_cmp_cols, NBLK), np.float32)
    for jb in range(NBLK):
        for mm in range(R_SLC):
            for nn in range(R_CMP):
                k = R_SLC * jb + mm - nn
                if 0 <= k < n_cmp_cols:
                    w[k, jb] += 1.0
    return w


def _dot_nt(a, b):
    return lax.dot_general(a, b, (((1,), (1,)), ((), ())), preferred_element_type=jnp.float32)


def _cmp_select_body(far_ref, q_ref, gate_ref, kc_ref, vc_ref, band_ref, wmap_ref, oc_ref, sel_ref,
                     *, n_far):
    qi = pl.program_id(1)
    par = qi % 2
    band0 = pl.multiple_of(16 * (qi // 2 + 1), 16)
    first_band_blk = band0 - CMP_PAD

    col = lax.broadcasted_iota(jnp.int32, (QB, 128), 1)
    row = lax.broadcasted_iota(jnp.int32, (QB, 128), 0)
    neg_pad = jnp.where(first_band_blk + col >= 0, 0.0, NEG_INF)

    t = qi * QB + row
    cur = t // L_SLC
    forced = (col == 0) | (col == cur) | (col == cur - 1)
    n_forced = 1 + (cur[:, :1] >= 1).astype(jnp.int32) + (cur[:, :1] >= 2).astype(jnp.int32)
    valid = col * L_SLC <= t

    q_all = q_ref[0]
    colf = col.astype(jnp.float32)

    def take_best(score, active=None):
        best = jnp.max(score, axis=-1, keepdims=True)
        first = jnp.min(jnp.where(score == best, colf, float(NBLK)), axis=-1, keepdims=True)
        hit = colf == first
        return jnp.where(hit if active is None else hit & active, -jnp.inf, score)

    def branches(n_cols):
        far_col = lax.broadcasted_iota(jnp.int32, (1, max(n_cols, 1)), 1)
        neg_far = jnp.where(far_col < first_band_blk, 0.0, NEG_INF)

        def far_and_band(ref, *lead):
            band_rows = ref[(*lead, pl.ds(band0, 128), slice(None))]
            if not n_cols:
                return band_rows
            return jnp.concatenate([ref[(*lead, slice(CMP_PAD, CMP_PAD + n_cols), slice(None))], band_rows], axis=0)

        wmap_all = far_and_band(wmap_ref)
        scores, start_scores = [], []
        oc_heads = []
        for g in range(N_KV_GROUPS):
            q4 = jnp.concatenate(
                [q_all[:, (g * HEADS_PER_GROUP + h) * HEAD_DIM:(g * HEADS_PER_GROUP + h + 1) * HEAD_DIM]
                 for h in range(HEADS_PER_GROUP)], axis=0)
            k_all = far_and_band(kc_ref, 0, g)
            v_all = far_and_band(vc_ref, 0, g)
            s_all = _dot_nt(q4, k_all).reshape(HEADS_PER_GROUP, QB, n_cols + 128)
            imp = jnp.zeros((QB, n_cols + 128), jnp.float32)
            for h in range(HEADS_PER_GROUP):
                hh = g * HEADS_PER_GROUP + h
                bias = band_ref[par, g, h] + neg_pad
                if n_cols:
                    bias = jnp.concatenate([jnp.broadcast_to(far_ref[hh] + neg_far, (QB, n_cols)), bias],
                                           axis=-1)
                s = s_all[h] + bias
                m = jnp.maximum(jnp.max(s, axis=-1, keepdims=True), 1e-10 * NEG_INF)
                p = jnp.exp2(s - m)
                l = jnp.sum(p, axis=-1, keepdims=True)
                p = p * jnp.where(l > 0.0, 1.0 / l, 0.0)
                imp = imp + p
                o = jnp.dot(p.astype(jnp.bfloat16), v_all, preferred_element_type=jnp.float32)
                oc_heads.append(o * gate_ref[0, :, 3 * hh:3 * hh + 1])
            p_slc = jnp.zeros((QB, NBLK), jnp.float32)
            rest = imp
            for _ in range(3):
                term = rest.astype(jnp.bfloat16)
                p_slc = p_slc + jnp.dot(term, wmap_all, preferred_element_type=jnp.float32)
                rest = rest - term.astype(jnp.float32)
            score = jnp.where(forced, -jnp.inf, jnp.where(valid, p_slc, -SEL_FORCE - colf))
            start_scores.append(score)
            for _ in range(N_SEL - MAX_FORCED):
                score = jnp.where(score == jnp.max(score, axis=-1, keepdims=True), -jnp.inf, score)
            scores.append(score)
        return (jnp.concatenate(oc_heads, axis=-1),) + tuple(scores) + tuple(start_scores)

    n_variants = n_far // 128
    variant = jnp.clip((first_band_blk + 127) // 128, 0, n_variants - 1)
    res = lax.switch(variant, [functools.partial(branches, 128 * v) for v in range(n_variants)])
    oc_ref[0] = res[0]
    scores, start_scores = res[1:1 + N_KV_GROUPS], res[1 + N_KV_GROUPS:]

    expected = (n_forced + (N_SEL - MAX_FORCED)).astype(jnp.float32)
    excess = [jnp.sum(jnp.where(s == -jnp.inf, 1.0, 0.0), axis=-1, keepdims=True) - expected for s in scores]
    any_tie = jnp.max(sum(excess)) > 0.0

    def exact_passes(start_scores):
        for _ in range(N_SEL - MAX_FORCED):
            start_scores = tuple(take_best(s) for s in start_scores)
        return start_scores

    scores = lax.cond(any_tie, exact_passes, lambda _: tuple(scores), tuple(start_scores))

    def early_rows(scores):
        for extra in range(MAX_FORCED - 1):
            scores = tuple(take_best(s, MAX_FORCED - n_forced > extra) for s in scores)
        return scores

    scores = lax.cond(qi == 0, early_rows, lambda s: s, tuple(scores))
    for g in range(N_KV_GROUPS):
        sel_ref[0, g] = jnp.where(scores[g] == -jnp.inf, 0.0, NEG_INF).astype(jnp.bfloat16)


def _cmp_select(q, gates, kc_pad, vc_pad, band, far, wmap):
    B, S, _ = q.shape
    n_far = S // D_STRIDE
    n_pad = kc_pad.shape[2]
    grid = (B, S // QB)
    return pl.pallas_call(
        functools.partial(_cmp_select_body, n_far=n_far),
        grid=grid,
        in_specs=[
            pl.BlockSpec(memory_space=pltpu.SMEM),
            pl.BlockSpec((1, QB, N_HEADS * HEAD_DIM), lambda b, i: (b, i, 0)),
            pl.BlockSpec((1, QB, LANES), lambda b, i: (b, i, 0)),
            pl.BlockSpec((1, N_KV_GROUPS, n_pad, HEAD_DIM), lambda b, i: (b, 0, 0, 0)),
            pl.BlockSpec((1, N_KV_GROUPS, n_pad, HEAD_DIM), lambda b, i: (b, 0, 0, 0)),
            pl.BlockSpec((2, N_KV_GROUPS, HEADS_PER_GROUP, QB, 128), lambda b, i: (0, 0, 0, 0, 0)),
            pl.BlockSpec((n_pad, NBLK), lambda b, i: (0, 0)),
        ],
        out_specs=[
            pl.BlockSpec((1, QB, N_HEADS * HEAD_DIM), lambda b, i: (b, i, 0)),
            pl.BlockSpec((1, N_KV_GROUPS, QB, NBLK), lambda b, i: (b, 0, i, 0)),
        ],
        out_shape=[
            jax.ShapeDtypeStruct((B, S, N_HEADS * HEAD_DIM), jnp.float32),
            jax.ShapeDtypeStruct((B, N_KV_GROUPS, S, NBLK), jnp.bfloat16),
        ],
        compiler_params=pltpu.CompilerParams(
            dimension_semantics=("arbitrary", "arbitrary"),
            vmem_limit_bytes=V7X_VMEM_LIMIT_BYTES,
        ),
        name="nsa_cmp_select",
    )(far, q, gates, kc_pad, vc_pad, band, wmap)


def _win_sel_body(q_ref, gate_ref, sel_ref, oc_ref, ks_ref, vs_ref, kw_ref, vw_ref, toep_ref, win_ref,
                  out_ref, s0_scr, s1_scr, p0_scr, p1_scr, m_scr, acc_scr):
    qi = pl.program_id(2)
    q0 = pl.multiple_of(qi * QB, QB)
    qg = q_ref[0]
    q_heads = [qg[:, h * HEAD_DIM:(h + 1) * HEAD_DIM] for h in range(HEADS_PER_GROUP)]
    q4 = jnp.concatenate(q_heads, axis=0)

    sel = sel_ref[0, 0]
    qa = jnp.concatenate([jnp.concatenate([sel] * HEADS_PER_GROUP, axis=0), q4], axis=-1)
    c_diag = qi // 2

    def scores(c):
        col = pl.multiple_of(c * KEY_TILE, KEY_TILE)
        s = jnp.dot(qa, ks_ref[0, 0, :, pl.ds(col, KEY_TILE)], preferred_element_type=jnp.float32)
        mm = qi - 2 * c
        bias = jnp.concatenate([toep_ref[0, jnp.clip(mm + 1, 0, N_TOEP - 1)],
                                toep_ref[0, jnp.clip(mm, 0, N_TOEP - 1)]], axis=-1)
        return s.reshape(HEADS_PER_GROUP, QB, KEY_TILE) + bias

    s_slots = (s0_scr, s1_scr)
    p_slots = (p0_scr, p1_scr)
    m_scr[...] = jnp.full(m_scr.shape, NEG_INF, jnp.float32)
    acc_scr[...] = jnp.zeros_like(acc_scr)
    p_slots[0][...] = jnp.zeros_like(p0_scr)
    s_slots[0][...] = scores(0)

    def weighted_values(p, col):
        return jnp.dot(p, vs_ref[0, 0, pl.ds(pl.multiple_of(col, KEY_TILE), KEY_TILE), :],
                       preferred_element_type=jnp.float32)

    def half_step(c, col_prev, cur):
        pv = weighted_values(p_slots[cur][...], col_prev)
        s_slots[1 - cur][...] = scores(c + 1)
        for h in range(HEADS_PER_GROUP):
            s = s_slots[cur][h]
            m_old = m_scr[h]
            m_new = jnp.maximum(m_old, jnp.max(s, axis=-1, keepdims=True))
            alpha = jnp.exp2(m_old - m_new)
            p = jnp.exp2(s - jnp.concatenate([m_new] * (KEY_TILE // LANES), axis=-1))
            p_slots[1 - cur][h * QB:(h + 1) * QB, :] = p.astype(jnp.bfloat16)
            m_scr[h] = m_new
            acc_scr[h * QB:(h + 1) * QB, :] = alpha * (acc_scr[h * QB:(h + 1) * QB, :]
                                                      + pv[h * QB:(h + 1) * QB, :])
        return c * KEY_TILE

    def unrolled_steps(first_tile, n_steps):
        def body(j, col_prev):
            for u in range(n_steps):
                col_prev = half_step(first_tile + n_steps * j + u, col_prev, u % 2)
            return col_prev
        return body

    n_tiles = c_diag + 1
    n_long = n_tiles // SEL_UNROLL
    col_last = lax.fori_loop(0, n_long, unrolled_steps(0, SEL_UNROLL), 0)
    col_last = lax.fori_loop(0, (n_tiles - n_long * SEL_UNROLL + 1) // 2,
                             unrolled_steps(n_long * SEL_UNROLL, 2), col_last)
    acc_s = (acc_scr[...] + weighted_values(p_slots[0][...], col_last)).reshape(HEADS_PER_GROUP, QB, LANES)
    o_s = acc_s[:, :, :HEAD_DIM] / acc_s[:, :, HEAD_DIM:HEAD_DIM + 1]

    kw = kw_ref[0, 0, :, pl.ds(q0, WINDOW + QB)]
    vw = vw_ref[0, 0, pl.ds(q0, WINDOW + QB), :]
    wcol = lax.broadcasted_iota(jnp.int32, (1, WINDOW + QB), 1)
    neg_left = jnp.where(q0 + wcol >= WINDOW, 0.0, NEG_INF)
    o_w = []
    for h0 in range(0, HEADS_PER_GROUP, WIN_HEADS):
        s_w = jnp.dot(q4[h0 * QB:(h0 + WIN_HEADS) * QB], kw, preferred_element_type=jnp.float32)
        s_w = s_w.reshape(WIN_HEADS, QB, WINDOW + QB) + win_ref[0, h0:h0 + WIN_HEADS] + neg_left
        p_w = jnp.exp2(s_w - jnp.max(s_w, axis=-1, keepdims=True))
        pv_w = jnp.dot(p_w.reshape(WIN_HEADS * QB, WINDOW + QB).astype(jnp.bfloat16), vw,
                       preferred_element_type=jnp.float32).reshape(WIN_HEADS, QB, LANES)
        o_w.extend(pv_w[h, :, :HEAD_DIM] / pv_w[h, :, HEAD_DIM:HEAD_DIM + 1] for h in range(WIN_HEADS))

    outs = []
    for h in range(HEADS_PER_GROUP):
        g_s = gate_ref[0, 0, :, 3 * h + 1:3 * h + 2]
        g_w = gate_ref[0, 0, :, 3 * h + 2:3 * h + 3]
        outs.append(g_s * o_s[h] + g_w * o_w[h])
    out_ref[0] = oc_ref[0] + jnp.concatenate(outs, axis=-1)


def _win_sel(q, gates, selneg, oc, ks_aug, vs, kw_t, vw_pad, toep, win):
    B, S, _ = q.shape
    gw = HEADS_PER_GROUP * HEAD_DIM
    grid = (B, N_KV_GROUPS, S // QB)
    return pl.pallas_call(
        _win_sel_body,
        grid=grid,
        in_specs=[
            pl.BlockSpec((1, QB, gw), lambda b, g, i: (b, i, g)),
            pl.BlockSpec((1, 1, QB, 3 * HEADS_PER_GROUP), lambda b, g, i: (b, g, i, 0)),
            pl.BlockSpec((1, 1, QB, NBLK), lambda b, g, i: (b, g, i, 0)),
            pl.BlockSpec((1, QB, gw), lambda b, g, i: (b, i, g)),
            pl.BlockSpec((1, 1, NBLK + HEAD_DIM, ks_aug.shape[3]), lambda b, g, i: (b, g, 0, 0)),
            pl.BlockSpec((1, 1, vs.shape[2], LANES), lambda b, g, i: (b, g, 0, 0)),
            pl.BlockSpec((1, 1, HEAD_DIM, S + WINDOW), lambda b, g, i: (b, g, 0, 0)),
            pl.BlockSpec((1, 1, S + WINDOW, LANES), lambda b, g, i: (b, g, 0, 0)),
            pl.BlockSpec((1, N_TOEP, HEADS_PER_GROUP, QB, QB), lambda b, g, i: (g, 0, 0, 0, 0)),
            pl.BlockSpec((1, HEADS_PER_GROUP, QB, WINDOW + QB), lambda b, g, i: (g, 0, 0, 0)),
        ],
        out_specs=pl.BlockSpec((1, QB, gw), lambda b, g, i: (b, i, g)),
        out_shape=jax.ShapeDtypeStruct((B, S, N_HEADS * HEAD_DIM), jnp.float32),
        scratch_shapes=[
            pltpu.VMEM((HEADS_PER_GROUP, QB, KEY_TILE), jnp.float32),
            pltpu.VMEM((HEADS_PER_GROUP, QB, KEY_TILE), jnp.float32),
            pltpu.VMEM((QROWS, KEY_TILE), jnp.bfloat16),
            pltpu.VMEM((QROWS, KEY_TILE), jnp.bfloat16),
            pltpu.VMEM((HEADS_PER_GROUP, QB, LANES), jnp.float32),
            pltpu.VMEM((QROWS, LANES), jnp.float32),
        ],
        compiler_params=pltpu.CompilerParams(
            dimension_semantics=("arbitrary", "arbitrary", "arbitrary"),
            vmem_limit_bytes=V7X_VMEM_LIMIT_BYTES,
        ),
        name="nsa_win_sel",
    )(q, gates, selneg, oc, ks_aug, vs, kw_t, vw_pad, toep, win)


def _nsa_attention(q, gates, kc, vc, kvsw, rel_bias):
    B, S, _ = q.shape
    assert S % KEY_TILE == 0 and S // L_SLC <= NBLK
    bf = jnp.bfloat16
    n_far = S // D_STRIDE
    toep, win, band, far = _bias_tables(rel_bias)
    cpad = ((0, 0), (0, 0), (CMP_PAD, n_far - kc.shape[2] + CMP_PAD), (0, 0))
    kc_pad = jnp.pad(kc, cpad).astype(bf)
    vc_pad = jnp.pad(vc, cpad).astype(bf)
    wmap = jnp.asarray(np.pad(_slc_map_matrix(n_far), ((CMP_PAD, CMP_PAD), (0, 0))), bf)
    oc, selneg = _cmp_select(q, gates, kc_pad, vc_pad, band, far, wmap)
    ks, vs, kw, vw = (kvsw.reshape(B, S, 4, N_KV_GROUPS, HEAD_DIM)[:, :, n] for n in range(4))
    blk_onehot = jnp.asarray((np.arange(S)[None, :] // L_SLC == np.arange(NBLK)[:, None]), bf)
    ks_aug = jnp.concatenate(
        [jnp.broadcast_to(blk_onehot, (B, N_KV_GROUPS, NBLK, S)), ks.transpose(0, 2, 3, 1)], axis=2)
    tail = 2 * KEY_TILE
    ks_aug = jnp.pad(ks_aug, ((0, 0), (0, 0), (0, 0), (0, tail)))
    kw_t = jnp.pad(kw.transpose(0, 2, 3, 1), ((0, 0), (0, 0), (0, 0), (WINDOW, 0)))

    def with_ones(v):
        v = v.transpose(0, 2, 1, 3)
        ones = jnp.ones(v.shape[:-1] + (1,), v.dtype)
        return jnp.pad(jnp.concatenate([v, ones], axis=-1), ((0, 0),) * 3 + ((0, LANES - HEAD_DIM - 1),))

    vw_aug = jnp.pad(with_ones(vw), ((0, 0), (0, 0), (WINDOW, 0), (0, 0)))
    vs_aug = jnp.pad(with_ones(vs), ((0, 0), (0, 0), (0, tail), (0, 0)))
    gates_g = gates[:, :, :3 * N_HEADS].reshape(B, S, N_KV_GROUPS, 3 * HEADS_PER_GROUP).swapaxes(1, 2)
    return _win_sel(q, gates_g, selneg, oc, ks_aug, vs_aug, kw_t, vw_aug, toep, win)


def _pool_layer_body(x_ref, halo_ref, gain_ref, shift_ref, scale_ref, gate_ref, pscale_ref, w_ref, o_ref):
    i = pl.program_id(1)
    x = x_ref[0]
    gain, shift, scale = gain_ref[...], shift_ref[0], scale_ref[0]
    h = _norm_modulate(x, gain, shift, scale)
    halo = _norm_modulate(halo_ref[0], gain, shift, scale) * (i > 0).astype(jnp.float32)
    hx = jnp.concatenate([halo, h], axis=0)
    t = i * POOL_TILE + lax.broadcasted_iota(jnp.int32, (POOL_TILE, 1), 0)
    outs = []
    run = hx
    width = 1
    for g, w in enumerate(POOL_WINDOWS):
        while width < w:
            run = run + pltpu.roll(run, width, axis=0)
            width *= 2
        lanes = slice(0, POOL_GROUP_DIM)
        cnt = jnp.minimum(t + 1, w).astype(jnp.float32)
        mix = run[POOL_HALO:, lanes] / cnt - h[:, g * POOL_GROUP_DIM:(g + 1) * POOL_GROUP_DIM]
        outs.append(jnp.dot(mix.astype(jnp.bfloat16), w_ref[g], preferred_element_type=jnp.float32))
        run = run[:, POOL_GROUP_DIM:]
    y = jnp.concatenate(outs, axis=-1) * pscale_ref[...]
    o_ref[0] = x + gate_ref[0] * y


def _pool_layer(x, gain, shift, scale, gate1, w_grp, pool_scale):
    B, S, D = x.shape
    assert S % POOL_TILE == 0 and POOL_HALO >= max(POOL_WINDOWS) - 1
    per_batch = pl.BlockSpec((1, 1, D), lambda b, i: (b, 0, 0))
    halo_blocks = POOL_TILE // POOL_HALO
    return pl.pallas_call(
        _pool_layer_body,
        grid=(B, S // POOL_TILE),
        in_specs=[
            pl.BlockSpec((1, POOL_TILE, D), lambda b, i: (b, i, 0)),
            pl.BlockSpec((1, POOL_HALO, D), lambda b, i: (b, jnp.maximum(i * halo_blocks - 1, 0), 0)),
            pl.BlockSpec((1, D), lambda b, i: (0, 0)),
            per_batch, per_batch, per_batch,
            pl.BlockSpec((1, D), lambda b, i: (0, 0)),
            pl.BlockSpec((N_POOL_GROUPS, POOL_GROUP_DIM, POOL_GROUP_DIM), lambda b, i: (0, 0, 0)),
        ],
        out_specs=pl.BlockSpec((1, POOL_TILE, D), lambda b, i: (b, i, 0)),
        out_shape=jax.ShapeDtypeStruct((B, S, D), jnp.float32),
        compiler_params=pltpu.CompilerParams(
            dimension_semantics=("arbitrary", "arbitrary"),
            vmem_limit_bytes=V7X_VMEM_LIMIT_BYTES,
        ),
        name="pool_layer",
    )(x, x, gain.reshape(1, D), shift[:, None, :], scale[:, None, :], gate1[:, None, :],
      pool_scale.reshape(1, D), w_grp.astype(jnp.bfloat16))


def _head_norm(v, seg_ref, gain):
    sq = v * v
    hi = sq.astype(jnp.bfloat16)
    lo = (sq - hi.astype(jnp.float32)).astype(jnp.bfloat16)
    ss = (jnp.dot(hi, seg_ref[...], preferred_element_type=jnp.float32)
          + jnp.dot(lo, seg_ref[...], preferred_element_type=jnp.float32))
    return v * lax.rsqrt(ss * (1.0 / HEAD_DIM) + EPS) * gain


def _qkv_proj_body(x_ref, gq_ref, shq_ref, scq_ref, gkv_ref, shkv_ref, sckv_ref, wq_ref, wg_ref, wkv_ref,
                   seg_ref, qgain_ref, kgain_ref, q_ref, gates_ref, kvc_ref, kvsw_ref):
    x = x_ref[...]
    xhat = x * lax.rsqrt(jnp.mean(x * x, axis=-1, keepdims=True) + EPS)
    hq = ((xhat * gq_ref[...]) * (1.0 + scq_ref[0]) + shq_ref[0]).astype(jnp.bfloat16)
    hkv = ((xhat * gkv_ref[...]) * (1.0 + sckv_ref[0]) + shkv_ref[0]).astype(jnp.bfloat16)
    pq = jnp.dot(hq, wq_ref[...], preferred_element_type=jnp.float32)
    gates_ref[...] = jax.nn.sigmoid(jnp.dot(hq, wg_ref[...], preferred_element_type=jnp.float32))
    pkv = jnp.dot(hkv, wkv_ref[...], preferred_element_type=jnp.float32)
    for g in range(N_KV_GROUPS):
        lanes = slice(g * GROUP_W, (g + 1) * GROUP_W)
        q_ref[:, lanes] = _head_norm(pq[:, lanes], seg_ref, qgain_ref[...]).astype(jnp.bfloat16)
    kvc_ref[...] = pkv[:, :2 * GROUP_W]
    k_s = _head_norm(pkv[:, 2 * GROUP_W:3 * GROUP_W], seg_ref, kgain_ref[0:1, :])
    k_w = _head_norm(pkv[:, 4 * GROUP_W:5 * GROUP_W], seg_ref, kgain_ref[1:2, :])
    kvsw_ref[...] = jnp.concatenate([k_s, pkv[:, 3 * GROUP_W:4 * GROUP_W], k_w, pkv[:, 5 * GROUP_W:]],
                                    axis=-1).astype(jnp.bfloat16)


def _qkv_proj(x, gain_q, shift_q, scale_q, gain_kv, shift_kv, scale_kv, w_qg, kv_w, q_gain, k_gain):
    B, S, D = x.shape
    T = B * S
    qd = N_HEADS * HEAD_DIM
    bf = jnp.bfloat16
    tiles_per_batch = S // ROW_TILE
    per_batch = pl.BlockSpec((1, 1, D), lambda i: (i // tiles_per_batch, 0, 0))
    const = lambda shape: pl.BlockSpec(shape, lambda i: (0,) * len(shape))
    seg = jnp.asarray(np.kron(np.eye(HEADS_PER_GROUP), np.ones((HEAD_DIM, HEAD_DIM))), bf)
    w_gate = jnp.pad(w_qg[:, qd:], ((0, 0), (0, LANES - 3 * N_HEADS))).astype(bf)
    qgain = jnp.tile(q_gain * HEAD_DIM ** -0.5, HEADS_PER_GROUP).reshape(1, GROUP_W)
    kgain = jnp.stack([jnp.tile(k_gain[1] * LOG2_E, N_KV_GROUPS), jnp.tile(k_gain[2] * LOG2_E, N_KV_GROUPS)])
    rows = lambda width: pl.BlockSpec((ROW_TILE, width), lambda i: (i, 0))
    return pl.pallas_call(
        _qkv_proj_body,
        grid=(T // ROW_TILE,),
        in_specs=[rows(D), const((1, D)), per_batch, per_batch, const((1, D)), per_batch, per_batch,
                  const((D, qd)), const((D, LANES)), const((D, 6 * GROUP_W)), const((GROUP_W, GROUP_W)),
                  const((1, GROUP_W)), const((2, GROUP_W))],
        out_specs=[rows(qd), rows(LANES), rows(2 * GROUP_W), rows(4 * GROUP_W)],
        out_shape=[jax.ShapeDtypeStruct((T, qd), bf), jax.ShapeDtypeStruct((T, LANES), jnp.float32),
                   jax.ShapeDtypeStruct((T, 2 * GROUP_W), jnp.float32),
                   jax.ShapeDtypeStruct((T, 4 * GROUP_W), bf)],
        compiler_params=pltpu.CompilerParams(
            dimension_semantics=("arbitrary",), vmem_limit_bytes=V7X_VMEM_LIMIT_BYTES),
        name="qkv_proj",
    )(x.reshape(T, D), gain_q.reshape(1, D), shift_q[:, None, :], scale_q[:, None, :],
      gain_kv.reshape(1, D), shift_kv[:, None, :], scale_kv[:, None, :],
      w_qg[:, :qd].astype(bf), w_gate, kv_w.astype(bf), seg, qgain, kgain)


def _compress_body(r_ref, pe_ref, w1_ref, w2_ref, gain_ref, o_ref):
    half = D_STRIDE * HEAD_DIM
    r = r_ref[0, 0, 0]
    top = jnp.dot(r, w1_ref[0, :half, :], preferred_element_type=jnp.float32)
    bot = jnp.dot(r, w1_ref[0, half:, :], preferred_element_type=jnp.float32)
    pe = jnp.dot(jnp.broadcast_to(pe_ref[0], (8, R_CMP * half)).astype(jnp.bfloat16), w1_ref[0],
                 preferred_element_type=jnp.float32)[0:1]
    n_chunks = r.shape[0]
    hidden = top + pltpu.roll(bot, n_chunks - 1, axis=0) + pe
    out = jnp.dot(jax.nn.gelu(hidden).astype(jnp.bfloat16), w2_ref[0], preferred_element_type=jnp.float32)
    normed = out * lax.rsqrt(jnp.mean(out * out, axis=-1, keepdims=True) + EPS) * gain_ref[...]
    o_ref[0, 0, 0] = jnp.where(pl.program_id(0) == 0, normed, out)


def _compressed_kv(kvc, B, S, cmp_pe_k, cmp_pe_v, cmp_k_w1, cmp_k_w2, cmp_v_w1, cmp_v_w2, k_gain):
    bf = jnp.bfloat16
    n_chunks = S // D_STRIDE
    half = D_STRIDE * HEAD_DIM
    hidden = cmp_k_w1.shape[1]
    r = kvc.astype(bf).reshape(B, n_chunks, D_STRIDE, 2, N_KV_GROUPS, HEAD_DIM)
    r = r.transpose(3, 0, 4, 1, 2, 5).reshape(2, B, N_KV_GROUPS, n_chunks, half)
    pe = jnp.stack([cmp_pe_k, cmp_pe_v]).reshape(2, 1, R_CMP * half)
    per_kind = lambda shape: pl.BlockSpec((1,) + shape, lambda kv, b, g: (kv,) + (0,) * len(shape))
    out = pl.pallas_call(
        _compress_body,
        grid=(2, B, N_KV_GROUPS),
        in_specs=[
            pl.BlockSpec((1, 1, 1, n_chunks, half), lambda kv, b, g: (kv, b, g, 0, 0)),
            per_kind((1, R_CMP * half)), per_kind((R_CMP * half, hidden)), per_kind((hidden, HEAD_DIM)),
            pl.BlockSpec((1, HEAD_DIM), lambda kv, b, g: (0, 0)),
        ],
        out_specs=pl.BlockSpec((1, 1, 1, n_chunks, HEAD_DIM), lambda kv, b, g: (kv, b, g, 0, 0)),
        out_shape=jax.ShapeDtypeStruct((2, B, N_KV_GROUPS, n_chunks, HEAD_DIM), jnp.float32),
        compiler_params=pltpu.CompilerParams(
            dimension_semantics=("arbitrary", "arbitrary", "arbitrary"),
            vmem_limit_bytes=V7X_VMEM_LIMIT_BYTES),
        name="cmp_mlp",
    )(r, pe, jnp.stack([cmp_k_w1, cmp_v_w1]).astype(bf), jnp.stack([cmp_k_w2, cmp_v_w2]).astype(bf),
      (k_gain[0] * LOG2_E).reshape(1, HEAD_DIM))
    n_cmp = n_chunks - R_CMP + 1
    return out[0, :, :, :n_cmp], out[1, :, :, :n_cmp]


def _oproj_router_body(a_ref, x_ref, gate_ref, wo_ref, gain_ref, shift_ref, scale_ref, wr_ref, br_ref,
                       o_ref, logit_ref):
    mix = jnp.dot(a_ref[...].astype(jnp.bfloat16), wo_ref[...], preferred_element_type=jnp.float32)
    x = x_ref[...] + gate_ref[0] * mix
    o_ref[...] = x
    h = _norm_modulate(x, gain_ref[...], shift_ref[0], scale_ref[0]).astype(jnp.bfloat16)
    logit_ref[...] = jnp.dot(h, wr_ref[...], preferred_element_type=jnp.float32) + br_ref[...]


def _oproj_router(attn, x, gate1, w_o, gain, shift, scale, w_router, b_router):
    B, S, D = x.shape
    T = B * S
    bf = jnp.bfloat16
    tiles_per_batch = S // ROW_TILE
    per_batch = pl.BlockSpec((1, 1, D), lambda i: (i // tiles_per_batch, 0, 0))
    const = lambda shape: pl.BlockSpec(shape, lambda i: (0,) * len(shape))
    rows = lambda width: pl.BlockSpec((ROW_TILE, width), lambda i: (i, 0))
    pad = ((0, 0), (0, LANES - N_EXPERTS))
    out, logits = pl.pallas_call(
        _oproj_router_body,
        grid=(T // ROW_TILE,),
        in_specs=[rows(D), rows(D), per_batch, const((D, D)), const((1, D)), per_batch, per_batch,
                  const((D, LANES)), const((1, LANES))],
        out_specs=[rows(D), rows(LANES)],
        out_shape=[jax.ShapeDtypeStruct((T, D), jnp.float32), jax.ShapeDtypeStruct((T, LANES), jnp.float32)],
        compiler_params=pltpu.CompilerParams(
            dimension_semantics=("arbitrary",), vmem_limit_bytes=V7X_VMEM_LIMIT_BYTES),
        name="oproj_router",
    )(attn.reshape(T, D), x.reshape(T, D), gate1[:, None, :], w_o.astype(bf), gain.reshape(1, D),
      shift[:, None, :], scale[:, None, :], jnp.pad(w_router, pad).astype(bf),
      jnp.pad(b_router.reshape(1, N_EXPERTS), pad))
    return out.reshape(B, S, D), logits[:, :N_EXPERTS]


def _route(logits):
    T = logits.shape[0]
    lane = jnp.arange(N_EXPERTS)[None, :]
    l0 = jnp.max(logits, axis=-1)
    e0 = jnp.argmax(logits, axis=-1)
    rest = jnp.where(lane == e0[:, None], -jnp.inf, logits)
    l1 = jnp.max(rest, axis=-1)
    e1 = jnp.argmax(rest, axis=-1)
    z = jnp.exp(l1 - l0)
    top_w = jnp.stack([1.0 / (1.0 + z), z / (1.0 + z)], axis=-1)
    oh = [(lane == e[:, None]).astype(jnp.int32) for e in (e0, e1)]
    cnt = oh[0] + oh[1]
    before = jnp.cumsum(cnt, axis=0) - cnt
    counts = before[-1] + cnt[-1]
    pcounts = (counts + ROW_TILE - 1) // ROW_TILE * ROW_TILE
    pends = jnp.cumsum(pcounts)
    pstarts = pends - pcounts
    dest = jnp.stack([jnp.sum((pstarts[None, :] + before) * o, axis=-1) for o in oh], axis=-1)
    n_blocks = -(-(T * TOP_K + N_EXPERTS * (ROW_TILE - 1)) // ROW_TILE)
    blk_start = jnp.arange(n_blocks)[:, None] * ROW_TILE
    blk_e = jnp.minimum(jnp.sum(pends[None, :] <= blk_start, axis=-1), N_EXPERTS - 1).astype(jnp.int32)
    return top_w, dest.astype(jnp.int32), blk_e, n_blocks * ROW_TILE


def _moe_ffn(x, logits, gain, shift, scale, gate2, w_gu, w_dn):
    top_w, dest, blk_e, n_rows = _route(logits)
    rows = _moe_dispatch(x, gain, shift, scale, dest, n_rows)
    y = _grouped_swiglu(rows, blk_e, w_gu.astype(jnp.bfloat16), w_dn.astype(jnp.bfloat16),
                        ff_tile=FF_TILE_EXPERT)
    return _moe_combine(x, top_w, gate2, y, dest)


def kernel(x, c, ada_w, ada_b, norm_g, pool_w, pool_scale, q_w, q_gain, o_w, kv_ada_w, kv_ada_b, kv_norm_g, kv_w, cmp_pe_k, cmp_pe_v, cmp_k_w1, cmp_k_w2, cmp_v_w1, cmp_v_w2, k_gain, rel_bias, ffn_gu, ffn_dn, router_w, router_b, exp_gu, exp_dn):
    assert DEPTH == 2 and N_A_LAYERS == 1
    B, S, D = x.shape
    silu_c = jax.nn.silu(c)
    sh1, sc1, g1, sh2, sc2, g2 = jnp.split(silu_c @ ada_w[0] + ada_b[0], 6, axis=-1)
    x = _pool_layer(x, norm_g[0, 0], sh1, sc1, g1, pool_w[0], pool_scale[0])
    x = _dense_ffn(x, norm_g[0, 1], sh2, sc2, g2, ffn_gu[0], ffn_dn[0])
    sh1, sc1, g1, sh2, sc2, g2 = jnp.split(silu_c @ ada_w[1] + ada_b[1], 6, axis=-1)
    sh_kv, sc_kv = jnp.split(silu_c @ kv_ada_w + kv_ada_b, 2, axis=-1)
    q, gates, kvc, kvsw = _qkv_proj(x, norm_g[1, 0], sh1, sc1, kv_norm_g, sh_kv, sc_kv, q_w[0], kv_w,
                                    q_gain[0], k_gain)
    kc, vc = _compressed_kv(kvc, B, S, cmp_pe_k, cmp_pe_v, cmp_k_w1, cmp_k_w2, cmp_v_w1, cmp_v_w2, k_gain)
    attn = _nsa_attention(q.reshape(B, S, -1), gates.reshape(B, S, -1), kc, vc, kvsw.reshape(B, S, -1),
                          rel_bias)
    x, logits = _oproj_router(attn, x, g1, o_w[0], norm_g[1, 1], sh2, sc2, router_w[0], router_b[0])
    return _moe_ffn(x, logits, norm_g[1, 1], sh2, sc2, g2, exp_gu[0], exp_dn[0])
```

```python
import functools
import math

import jax
import jax.numpy as jnp
import numpy as np
from jax import lax
from jax.experimental import pallas as pl
from jax.experimental.pallas import tpu as pltpu

D_MODEL = 1024
DEPTH = 2
N_A_LAYERS = DEPTH // 2
POOL_WINDOWS = (2, 4, 8, 16)
N_POOL_GROUPS = len(POOL_WINDOWS)
POOL_GROUP_DIM = D_MODEL // N_POOL_GROUPS
HEAD_DIM = 64
N_HEADS = D_MODEL // HEAD_DIM
N_KV_GROUPS = 4
HEADS_PER_GROUP = N_HEADS // N_KV_GROUPS
L_CMP = 32
D_STRIDE = 16
L_SLC = 64
N_SEL = 16
WINDOW = 512
R_CMP = L_CMP // D_STRIDE
R_SLC = L_SLC // D_STRIDE
N_BUCKETS = 32
REL_EXACT = N_BUCKETS // 2
MAX_DISTANCE = 1024
N_EXPERTS = 8
TOP_K = 2
EPS = 1e-6
NEG_INF = -1e30
SEL_FORCE = 1e6
LOG2_E = math.log2(math.e)

V7X_VMEM_LIMIT_BYTES = 48 * 1024 * 1024
LANES = 128
ROW_TILE = 512
TOK_TILE = 512
DMA_ISSUE_UNROLL = 8
MAX_FORCED = 3
FF_TILE_DENSE = 2816
FF_TILE_EXPERT = 1792
POOL_TILE = 512
POOL_HALO = 16

QB = 128
KEY_TILE = 2 * QB
SEL_UNROLL = 8
QROWS = HEADS_PER_GROUP * QB
WIN_HEADS = 4
GROUP_W = HEADS_PER_GROUP * HEAD_DIM
NBLK = 128
CMP_PAD = 128
FAR_DIST = MAX_DISTANCE
N_TOEP = FAR_DIST // QB + 3
BAND_LEFT = CMP_PAD - 16
assert BAND_LEFT * D_STRIDE + (L_CMP - 1) - 2 * QB >= FAR_DIST


def _rms_norm(x, g):
    xf = x.astype(jnp.float32)
    y = xf * lax.rsqrt(jnp.mean(xf * xf, axis=-1, keepdims=True) + EPS)
    return (y * g.astype(jnp.float32)).astype(x.dtype)


def _modulate(h, shift, scale):
    return h * (1 + scale[:, None, :]) + shift[:, None, :]


def _rel_bucket(dist):
    d = jnp.maximum(dist, 0)
    ratio = jnp.maximum(d, REL_EXACT).astype(jnp.float32) / REL_EXACT
    large = REL_EXACT + (jnp.log(ratio) / math.log(MAX_DISTANCE / REL_EXACT)
                         * (N_BUCKETS - REL_EXACT)).astype(jnp.int32)
    return jnp.where(d < REL_EXACT, d, jnp.minimum(large, N_BUCKETS - 1))


def _norm_modulate(x, gain, shift, scale):
    y = x * lax.rsqrt(jnp.mean(x * x, axis=-1, keepdims=True) + EPS)
    return (y * gain) * (1.0 + scale) + shift


def _swiglu_step(xb_ref, wg_ref, wu_ref, wd_ref, acc_ref, j):
    @pl.when(j == 0)
    def _():
        acc_ref[...] = jnp.zeros_like(acc_ref)

    x = xb_ref[...]
    gate = jnp.dot(x, wg_ref[0], preferred_element_type=jnp.float32)
    up = jnp.dot(x, wu_ref[0], preferred_element_type=jnp.float32)
    act = (gate * jax.nn.sigmoid(gate) * up).astype(jnp.bfloat16)
    acc_ref[...] += jnp.dot(act, wd_ref[0], preferred_element_type=jnp.float32)


def _grouped_swiglu_body(blk_e_ref, x_ref, wg_ref, wu_ref, wd_ref, o_ref, acc_ref, xb_ref, *, n_ff_steps):
    del blk_e_ref
    j = pl.program_id(1)

    @pl.when(j == 0)
    def _():
        xb_ref[...] = x_ref[...].astype(jnp.bfloat16)

    _swiglu_step(xb_ref, wg_ref, wu_ref, wd_ref, acc_ref, j)

    @pl.when(j == n_ff_steps - 1)
    def _():
        o_ref[...] = acc_ref[...]


def _grouped_swiglu(x_rows, blk_e, w_gu, w_dn, *, ff_tile):
    n_rows, d = x_rows.shape
    d_ff = w_dn.shape[1]
    assert n_rows % ROW_TILE == 0 and d_ff % ff_tile == 0
    n_ff_steps = d_ff // ff_tile
    grid = (n_rows // ROW_TILE, n_ff_steps)
    return pl.pallas_call(
        functools.partial(_grouped_swiglu_body, n_ff_steps=n_ff_steps),
        grid_spec=pltpu.PrefetchScalarGridSpec(
            num_scalar_prefetch=1,
            grid=grid,
            in_specs=[
                pl.BlockSpec((ROW_TILE, d), lambda i, j, e: (i, 0)),
                pl.BlockSpec((1, d, ff_tile), lambda i, j, e: (e[i], 0, j)),
                pl.BlockSpec((1, d, ff_tile), lambda i, j, e: (e[i], 0, j + n_ff_steps)),
                pl.BlockSpec((1, ff_tile, d), lambda i, j, e: (e[i], j, 0)),
            ],
            out_specs=pl.BlockSpec((ROW_TILE, d), lambda i, j, e: (i, 0)),
            scratch_shapes=[pltpu.VMEM((ROW_TILE, d), jnp.float32), pltpu.VMEM((ROW_TILE, d), jnp.bfloat16)],
        ),
        out_shape=jax.ShapeDtypeStruct((n_rows, d), jnp.float32),
        compiler_params=pltpu.CompilerParams(
            dimension_semantics=("arbitrary", "arbitrary"),
            vmem_limit_bytes=V7X_VMEM_LIMIT_BYTES,
        ),
        name="grouped_swiglu",
    )(blk_e, x_rows, w_gu, w_gu, w_dn)


def _dense_ffn_body(x_ref, gain_ref, shift_ref, scale_ref, gate2_ref, wg_ref, wu_ref, wd_ref, o_ref,
                    acc_ref, xb_ref, *, n_ff_steps):
    j = pl.program_id(1)

    @pl.when(j == 0)
    def _():
        xb_ref[...] = _norm_modulate(x_ref[...], gain_ref[...], shift_ref[0], scale_ref[0]).astype(jnp.bfloat16)

    _swiglu_step(xb_ref, wg_ref, wu_ref, wd_ref, acc_ref, j)

    @pl.when(j == n_ff_steps - 1)
    def _():
        o_ref[...] = x_ref[...] + gate2_ref[0] * acc_ref[...]


def _dense_ffn(x, gain, shift, scale, gate2, w_gu, w_dn):
    B, S, D = x.shape
    d_ff = w_dn.shape[0]
    n_ff_steps = d_ff // FF_TILE_DENSE
    tiles_per_batch = S // ROW_TILE
    per_batch = pl.BlockSpec((1, 1, D), lambda i, j: (i // tiles_per_batch, 0, 0))
    weight_buffers = pl.Buffered(1) if n_ff_steps == 1 else None
    out = pl.pallas_call(
        functools.partial(_dense_ffn_body, n_ff_steps=n_ff_steps),
        grid=(B * S // ROW_TILE, n_ff_steps),
        in_specs=[
            pl.BlockSpec((ROW_TILE, D), lambda i, j: (i, 0)),
            pl.BlockSpec((1, D), lambda i, j: (0, 0)),
            per_batch, per_batch, per_batch,
            pl.BlockSpec((1, D, FF_TILE_DENSE), lambda i, j: (0, 0, j), pipeline_mode=weight_buffers),
            pl.BlockSpec((1, D, FF_TILE_DENSE), lambda i, j: (0, 0, j + n_ff_steps), pipeline_mode=weight_buffers),
            pl.BlockSpec((1, FF_TILE_DENSE, D), lambda i, j: (0, j, 0), pipeline_mode=weight_buffers),
        ],
        out_specs=pl.BlockSpec((ROW_TILE, D), lambda i, j: (i, 0)),
        out_shape=jax.ShapeDtypeStruct((B * S, D), jnp.float32),
        scratch_shapes=[pltpu.VMEM((ROW_TILE, D), jnp.float32), pltpu.VMEM((ROW_TILE, D), jnp.bfloat16)],
        compiler_params=pltpu.CompilerParams(
            dimension_semantics=("arbitrary", "arbitrary"),
            vmem_limit_bytes=V7X_VMEM_LIMIT_BYTES,
        ),
        name="dense_ffn",
    )(x.reshape(B * S, D), gain.reshape(1, D), shift[:, None, :], scale[:, None, :], gate2[:, None, :],
      w_gu.astype(jnp.bfloat16)[None], w_gu.astype(jnp.bfloat16)[None], w_dn.astype(jnp.bfloat16)[None])
    return out.reshape(B, S, D)


def _row_copies_wait(src_ref, dst_ref, sem, n_rows):
    pltpu.make_async_copy(src_ref.at[pl.ds(0, n_rows)], dst_ref.at[pl.ds(0, n_rows)], sem).wait()


def _dispatch_body(dest_ref, x_ref, gain_ref, shift_ref, scale_ref, rows_in_ref, rows_ref, h_ref, sem):
    del rows_in_ref
    h_ref[...] = _norm_modulate(x_ref[...], gain_ref[...], shift_ref[0], scale_ref[0])

    def issue(r, carry):
        for k in range(TOP_K):
            pltpu.make_async_copy(h_ref.at[pl.ds(r, 1)], rows_ref.at[pl.ds(dest_ref[TOP_K * r + k], 1)],
                                  sem).start()
        return carry

    lax.fori_loop(0, TOK_TILE, issue, 0, unroll=DMA_ISSUE_UNROLL)
    for k in range(TOP_K):
        _row_copies_wait(h_ref, rows_ref, sem, TOK_TILE)


def _moe_dispatch(x, gain, shift, scale, dest, n_rows):
    B, S, D = x.shape
    T = B * S
    tiles_per_batch = S // TOK_TILE
    per_batch = pl.BlockSpec((1, 1, D), lambda i: (i // tiles_per_batch, 0, 0))
    return pl.pallas_call(
        _dispatch_body,
        grid=(T // TOK_TILE,),
        in_specs=[
            pl.BlockSpec((TOP_K * TOK_TILE,), lambda i: (i,), memory_space=pltpu.SMEM),
            pl.BlockSpec((TOK_TILE, D), lambda i: (i, 0)),
            pl.BlockSpec((1, D), lambda i: (0, 0)),
            per_batch, per_batch,
            pl.BlockSpec(memory_space=pl.ANY),
        ],
        out_specs=pl.BlockSpec(memory_space=pl.ANY),
        out_shape=jax.ShapeDtypeStruct((n_rows, D), jnp.float32),
        scratch_shapes=[pltpu.VMEM((TOK_TILE, D), jnp.float32), pltpu.SemaphoreType.DMA(())],
        input_output_aliases={5: 0},
        compiler_params=pltpu.CompilerParams(dimension_semantics=("arbitrary",)),
        name="moe_dispatch",
    )(dest.reshape(T * TOP_K), x.reshape(T, D), gain.reshape(1, D), shift[:, None, :], scale[:, None, :],
      jnp.zeros((n_rows, D), jnp.float32))


def _combine_body(dest_ref, x_ref, w_ref, gate2_ref, y_ref, o_ref, buf_ref, sem):
    def issue(r, carry):
        for k in range(TOP_K):
            pltpu.make_async_copy(y_ref.at[pl.ds(dest_ref[TOP_K * r + k], 1)], buf_ref.at[k, pl.ds(r, 1)],
                                  sem).start()
        return carry

    lax.fori_loop(0, TOK_TILE, issue, 0, unroll=DMA_ISSUE_UNROLL)
    for k in range(TOP_K):
        _row_copies_wait(y_ref, buf_ref.at[k], sem, TOK_TILE)
    w = w_ref[...]
    f = w[:, 0:1] * buf_ref[0] + w[:, 1:2] * buf_ref[1]
    o_ref[...] = x_ref[...] + gate2_ref[0] * f


def _moe_combine(x, top_w, gate2, y, dest):
    B, S, D = x.shape
    T = B * S
    tiles_per_batch = S // TOK_TILE
    out = pl.pallas_call(
        _combine_body,
        grid=(T // TOK_TILE,),
        in_specs=[
            pl.BlockSpec((TOP_K * TOK_TILE,), lambda i: (i,), memory_space=pltpu.SMEM),
            pl.BlockSpec((TOK_TILE, D), lambda i: (i, 0)),
            pl.BlockSpec((TOK_TILE, TOP_K), lambda i: (i, 0)),
            pl.BlockSpec((1, 1, D), lambda i: (i // tiles_per_batch, 0, 0)),
            pl.BlockSpec(memory_space=pl.ANY),
        ],
        out_specs=pl.BlockSpec((TOK_TILE, D), lambda i: (i, 0)),
        out_shape=jax.ShapeDtypeStruct((T, D), jnp.float32),
        scratch_shapes=[pltpu.VMEM((TOP_K, TOK_TILE, D), jnp.float32), pltpu.SemaphoreType.DMA(())],
        compiler_params=pltpu.CompilerParams(dimension_semantics=("arbitrary",)),
        name="moe_combine",
    )(dest.reshape(T * TOP_K), x.reshape(T, D), top_w, gate2[:, None, :], y)
    return out.reshape(B, S, D)


def _bias_tables(rel_bias):
    x0 = (N_TOEP - 1) * QB
    width = x0 + 2 * QB
    period = width + QB
    n = np.arange(period)
    n = np.where(n < width, n, n - period)
    by_dist = rel_bias.astype(jnp.float32)[_rel_bucket(jnp.asarray(np.maximum(x0 - n, 0)))].T
    strip = jnp.tile(by_dist, (1, QB))[:, :QB * (period - 1)].reshape(N_HEADS, QB, period - 1)[:, :, :width]
    far = rel_bias.astype(jnp.float32)[N_BUCKETS - 1] * LOG2_E

    i = np.arange(QB)[:, None]
    toep = jnp.stack([strip[:, :, x0 - QB * m:x0 - QB * m + QB] for m in range(-1, N_TOEP - 1)], axis=1)
    d_toep = QB * np.arange(-1, N_TOEP - 1)[:, None, None] + i[None] - np.arange(QB)[None, None, :]
    toep = jnp.where(d_toep >= 0, toep * LOG2_E, NEG_INF)
    toep = toep.reshape(N_KV_GROUPS, HEADS_PER_GROUP, N_TOEP, QB, QB).transpose(0, 2, 1, 3, 4)
    d_win = WINDOW + i - np.arange(WINDOW + QB)[None, :]
    win = jnp.where((d_win >= 0) & (d_win < WINDOW), strip[:, :, x0 - WINDOW:x0 + QB] * LOG2_E, NEG_INF)
    win = win.reshape(N_KV_GROUPS, HEADS_PER_GROUP, QB, WINDOW + QB)
    bands = []
    for par in range(2):
        off = x0 - QB * par - D_STRIDE * BAND_LEFT + (L_CMP - 1)
        c_first = -(off // D_STRIDE)
        cols = strip[:, :, D_STRIDE * c_first + off::D_STRIDE][:, :, :128 - c_first] * LOG2_E
        left = jnp.broadcast_to(far[:, None, None], (N_HEADS, QB, c_first))
        right = jnp.zeros((N_HEADS, QB, 128 - c_first - cols.shape[2]), jnp.float32)
        d_band = QB * par + i - D_STRIDE * (np.arange(128)[None, :] - BAND_LEFT) - (L_CMP - 1)
        assert (d_band[:, 128 - right.shape[2]:] < 0).all()
        bands.append(jnp.where(d_band >= 0, jnp.concatenate([left, cols, right], axis=-1), NEG_INF))
    band = jnp.stack(bands).reshape(2, N_KV_GROUPS, HEADS_PER_GROUP, QB, 128)
    return toep, win, band, far


def _slc_map_matrix(n_cmp_cols):
    w = np.zeros((n_cmp_cols, NBLK), np.float32)
    for jb in range(NBLK):
        for mm in range(R_SLC):
            for nn in range(R_CMP):
                k = R_SLC * jb + mm - nn
                if 0 <= k < n_cmp_cols:
                    w[k, jb] += 1.0
    return w


def _dot_nt(a, b):
    return lax.dot_general(a, b, (((1,), (1,)), ((), ())), preferred_element_type=jnp.float32)


def _cmp_select_body(far_ref, q_ref, gate_ref, kc_ref, vc_ref, band_ref, wmap_ref, oc_ref, sel_ref,
                     *, n_far):
    qi = pl.program_id(1)
    par = qi % 2
    band0 = pl.multiple_of(16 * (qi // 2 + 1), 16)
    first_band_blk = band0 - CMP_PAD

    col = lax.broadcasted_iota(jnp.int32, (QB, 128), 1)
    row = lax.broadcasted_iota(jnp.int32, (QB, 128), 0)
    neg_pad = jnp.where(first_band_blk + col >= 0, 0.0, NEG_INF)

    t = qi * QB + row
    cur = t // L_SLC
    forced = (col == 0) | (col == cur) | (col == cur - 1)
    n_forced = 1 + (cur[:, :1] >= 1).astype(jnp.int32) + (cur[:, :1] >= 2).astype(jnp.int32)
    valid = col * L_SLC <= t

    q_all = q_ref[0]
    colf = col.astype(jnp.float32)

    def take_best(score, active=None):
        best = jnp.max(score, axis=-1, keepdims=True)
        first = jnp.min(jnp.where(score == best, colf, float(NBLK)), axis=-1, keepdims=True)
        hit = colf == first
        return jnp.where(hit if active is None else hit & active, -jnp.inf, score)

    def branches(n_cols):
        far_col = lax.broadcasted_iota(jnp.int32, (1, max(n_cols, 1)), 1)
        neg_far = jnp.where(far_col < first_band_blk, 0.0, NEG_INF)

        def far_and_band(ref, *lead):
            band_rows = ref[(*lead, pl.ds(band0, 128), slice(None))]
            if not n_cols:
                return band_rows
            return jnp.concatenate([ref[(*lead, slice(CMP_PAD, CMP_PAD + n_cols), slice(None))], band_rows], axis=0)

        wmap_all = far_and_band(wmap_ref)
        scores, start_scores = [], []
        oc_heads = []
        for g in range(N_KV_GROUPS):
            q4 = jnp.concatenate(
                [q_all[:, (g * HEADS_PER_GROUP + h) * HEAD_DIM:(g * HEADS_PER_GROUP + h + 1) * HEAD_DIM]
                 for h in range(HEADS_PER_GROUP)], axis=0)
            k_all = far_and_band(kc_ref, 0, g)
            v_all = far_and_band(vc_ref, 0, g)
            s_all = _dot_nt(q4, k_all).reshape(HEADS_PER_GROUP, QB, n_cols + 128)
            imp = jnp.zeros((QB, n_cols + 128), jnp.float32)
            for h in range(HEADS_PER_GROUP):
                hh = g * HEADS_PER_GROUP + h
                bias = band_ref[par, g, h] + neg_pad
                if n_cols:
                    bias = jnp.concatenate([jnp.broadcast_to(far_ref[hh] + neg_far, (QB, n_cols)), bias],
                                           axis=-1)
                s = s_all[h] + bias
                m = jnp.maximum(jnp.max(s, axis=-1, keepdims=True), 1e-10 * NEG_INF)
                p = jnp.exp2(s - m)
                l = jnp.sum(p, axis=-1, keepdims=True)
                p = p * jnp.where(l > 0.0, 1.0 / l, 0.0)
                imp = imp + p
                o = jnp.dot(p.astype(jnp.bfloat16), v_all, preferred_element_type=jnp.float32)
                oc_heads.append(o * gate_ref[0, :, 3 * hh:3 * hh + 1])
            p_slc = jnp.zeros((QB, NBLK), jnp.float32)
            rest = imp
            for _ in range(3):
                term = rest.astype(jnp.bfloat16)
                p_slc = p_slc + jnp.dot(term, wmap_all, preferred_element_type=jnp.float32)
                rest = rest - term.astype(jnp.float32)
            score = jnp.where(forced, -jnp.inf, jnp.where(valid, p_slc, -SEL_FORCE - colf))
            start_scores.append(score)
            for _ in range(N_SEL - MAX_FORCED):
                score = jnp.where(score == jnp.max(score, axis=-1, keepdims=True), -jnp.inf, score)
            scores.append(score)
        return (jnp.concatenate(oc_heads, axis=-1),) + tuple(scores) + tuple(start_scores)

    n_variants = n_far // 128
    variant = jnp.clip((first_band_blk + 127) // 128, 0, n_variants - 1)
    res = lax.switch(variant, [functools.partial(branches, 128 * v) for v in range(n_variants)])
    oc_ref[0] = res[0]
    scores, start_scores = res[1:1 + N_KV_GROUPS], res[1 + N_KV_GROUPS:]

    expected = (n_forced + (N_SEL - MAX_FORCED)).astype(jnp.float32)
    excess = [jnp.sum(jnp.where(s == -jnp.inf, 1.0, 0.0), axis=-1, keepdims=True) - expected for s in scores]
    any_tie = jnp.max(sum(excess)) > 0.0

    def exact_passes(start_scores):
        for _ in range(N_SEL - MAX_FORCED):
            start_scores = tuple(take_best(s) for s in start_scores)
        return start_scores

    scores = lax.cond(any_tie, exact_passes, lambda _: tuple(scores), tuple(start_scores))

    def early_rows(scores):
        for extra in range(MAX_FORCED - 1):
            scores = tuple(take_best(s, MAX_FORCED - n_forced > extra) for s in scores)
        return scores

    scores = lax.cond(qi == 0, early_rows, lambda s: s, tuple(scores))
    for g in range(N_KV_GROUPS):
        sel_ref[0, g] = jnp.where(scores[g] == -jnp.inf, 0.0, NEG_INF).astype(jnp.bfloat16)


def _cmp_select(q, gates, kc_pad, vc_pad, band, far, wmap):
    B, S, _ = q.shape
    n_far = S // D_STRIDE
    n_pad = kc_pad.shape[2]
    grid = (B, S // QB)
    return pl.pallas_call(
        functools.partial(_cmp_select_body, n_far=n_far),
        grid=grid,
        in_specs=[
            pl.BlockSpec(memory_space=pltpu.SMEM),
            pl.BlockSpec((1, QB, N_HEADS * HEAD_DIM), lambda b, i: (b, i, 0)),
            pl.BlockSpec((1, QB, LANES), lambda b, i: (b, i, 0)),
            pl.BlockSpec((1, N_KV_GROUPS, n_pad, HEAD_DIM), lambda b, i: (b, 0, 0, 0)),
            pl.BlockSpec((1, N_KV_GROUPS, n_pad, HEAD_DIM), lambda b, i: (b, 0, 0, 0)),
            pl.BlockSpec((2, N_KV_GROUPS, HEADS_PER_GROUP, QB, 128), lambda b, i: (0, 0, 0, 0, 0)),
            pl.BlockSpec((n_pad, NBLK), lambda b, i: (0, 0)),
        ],
        out_specs=[
            pl.BlockSpec((1, QB, N_HEADS * HEAD_DIM), lambda b, i: (b, i, 0)),
            pl.BlockSpec((1, N_KV_GROUPS, QB, NBLK), lambda b, i: (b, 0, i, 0)),
        ],
        out_shape=[
            jax.ShapeDtypeStruct((B, S, N_HEADS * HEAD_DIM), jnp.float32),
            jax.ShapeDtypeStruct((B, N_KV_GROUPS, S, NBLK), jnp.bfloat16),
        ],
        compiler_params=pltpu.CompilerParams(
            dimension_semantics=("arbitrary", "arbitrary"),
            vmem_limit_bytes=V7X_VMEM_LIMIT_BYTES,
        ),
        name="nsa_cmp_select",
    )(far, q, gates, kc_pad, vc_pad, band, wmap)


def _win_sel_body(q_ref, gate_ref, sel_ref, oc_ref, ks_ref, vs_ref, kw_ref, vw_ref, toep_ref, win_ref,
                  out_ref, s0_scr, s1_scr, p0_scr, p1_scr, m_scr, acc_scr):
    qi = pl.program_id(2)
    q0 = pl.multiple_of(qi * QB, QB)
    qg = q_ref[0]
    q_heads = [qg[:, h * HEAD_DIM:(h + 1) * HEAD_DIM] for h in range(HEADS_PER_GROUP)]
    q4 = jnp.concatenate(q_heads, axis=0)

    sel = sel_ref[0, 0]
    qa = jnp.concatenate([jnp.concatenate([sel] * HEADS_PER_GROUP, axis=0), q4], axis=-1)
    c_diag = qi // 2

    def scores(c):
        col = pl.multiple_of(c * KEY_TILE, KEY_TILE)
        s = jnp.dot(qa, ks_ref[0, 0, :, pl.ds(col, KEY_TILE)], preferred_element_type=jnp.float32)
        mm = qi - 2 * c
        bias = jnp.concatenate([toep_ref[0, jnp.clip(mm + 1, 0, N_TOEP - 1)],
                                toep_ref[0, jnp.clip(mm, 0, N_TOEP - 1)]], axis=-1)
        return s.reshape(HEADS_PER_GROUP, QB, KEY_TILE) + bias

    s_slots = (s0_scr, s1_scr)
    p_slots = (p0_scr, p1_scr)
    m_scr[...] = jnp.full(m_scr.shape, NEG_INF, jnp.float32)
    acc_scr[...] = jnp.zeros_like(acc_scr)
    p_slots[0][...] = jnp.zeros_like(p0_scr)
    s_slots[0][...] = scores(0)

    def weighted_values(p, col):
        return jnp.dot(p, vs_ref[0, 0, pl.ds(pl.multiple_of(col, KEY_TILE), KEY_TILE), :],
                       preferred_element_type=jnp.float32)

    def half_step(c, col_prev, cur):
        pv = weighted_values(p_slots[cur][...], col_prev)
        s_slots[1 - cur][...] = scores(c + 1)
        for h in range(HEADS_PER_GROUP):
            s = s_slots[cur][h]
            m_old = m_scr[h]
            m_new = jnp.maximum(m_old, jnp.max(s, axis=-1, keepdims=True))
            alpha = jnp.exp2(m_old - m_new)
            p = jnp.exp2(s - jnp.concatenate([m_new] * (KEY_TILE // LANES), axis=-1))
            p_slots[1 - cur][h * QB:(h + 1) * QB, :] = p.astype(jnp.bfloat16)
            m_scr[h] = m_new
            acc_scr[h * QB:(h + 1) * QB, :] = alpha * (acc_scr[h * QB:(h + 1) * QB, :]
                                                      + pv[h * QB:(h + 1) * QB, :])
        return c * KEY_TILE

    def unrolled_steps(first_tile, n_steps):
        def body(j, col_prev):
            for u in range(n_steps):
                col_prev = half_step(first_tile + n_steps * j + u, col_prev, u % 2)
            return col_prev
        return body

    n_tiles = c_diag + 1
    n_long = n_tiles // SEL_UNROLL
    col_last = lax.fori_loop(0, n_long, unrolled_steps(0, SEL_UNROLL), 0)
    col_last = lax.fori_loop(0, (n_tiles - n_long * SEL_UNROLL + 1) // 2,
                             unrolled_steps(n_long * SEL_UNROLL, 2), col_last)
    acc_s = (acc_scr[...] + weighted_values(p_slots[0][...], col_last)).reshape(HEADS_PER_GROUP, QB, LANES)
    o_s = acc_s[:, :, :HEAD_DIM] / acc_s[:, :, HEAD_DIM:HEAD_DIM + 1]

    kw = kw_ref[0, 0, :, pl.ds(q0, WINDOW + QB)]
    vw = vw_ref[0, 0, pl.ds(q0, WINDOW + QB), :]
    wcol = lax.broadcasted_iota(jnp.int32, (1, WINDOW + QB), 1)
    neg_left = jnp.where(q0 + wcol >= WINDOW, 0.0, NEG_INF)
    o_w = []
    for h0 in range(0, HEADS_PER_GROUP, WIN_HEADS):
        s_w = jnp.dot(q4[h0 * QB:(h0 + WIN_HEADS) * QB], kw, preferred_element_type=jnp.float32)
        s_w = s_w.reshape(WIN_HEADS, QB, WINDOW + QB) + win_ref[0, h0:h0 + WIN_HEADS] + neg_left
        p_w = jnp.exp2(s_w - jnp.max(s_w, axis=-1, keepdims=True))
        pv_w = jnp.dot(p_w.reshape(WIN_HEADS * QB, WINDOW + QB).astype(jnp.bfloat16), vw,
                       preferred_element_type=jnp.float32).reshape(WIN_HEADS, QB, LANES)
        o_w.extend(pv_w[h, :, :HEAD_DIM] / pv_w[h, :, HEAD_DIM:HEAD_DIM + 1] for h in range(WIN_HEADS))

    outs = []
    for h in range(HEADS_PER_GROUP):
        g_s = gate_ref[0, 0, :, 3 * h + 1:3 * h + 2]
        g_w = gate_ref[0, 0, :, 3 * h + 2:3 * h + 3]
        outs.append(g_s * o_s[h] + g_w * o_w[h])
    out_ref[0] = oc_ref[0] + jnp.concatenate(outs, axis=-1)


def _win_sel(q, gates, selneg, oc, ks_aug, vs, kw_t, vw_pad, toep, win):
    B, S, _ = q.shape
    gw = HEADS_PER_GROUP * HEAD_DIM
    grid = (B, N_KV_GROUPS, S // QB)
    return pl.pallas_call(
        _win_sel_body,
        grid=grid,
        in_specs=[
            pl.BlockSpec((1, QB, gw), lambda b, g, i: (b, i, g)),
            pl.BlockSpec((1, 1, QB, 3 * HEADS_PER_GROUP), lambda b, g, i: (b, g, i, 0)),
            pl.BlockSpec((1, 1, QB, NBLK), lambda b, g, i: (b, g, i, 0)),
            pl.BlockSpec((1, QB, gw), lambda b, g, i: (b, i, g)),
            pl.BlockSpec((1, 1, NBLK + HEAD_DIM, ks_aug.shape[3]), lambda b, g, i: (b, g, 0, 0)),
            pl.BlockSpec((1, 1, vs.shape[2], LANES), lambda b, g, i: (b, g, 0, 0)),
            pl.BlockSpec((1, 1, HEAD_DIM, S + WINDOW), lambda b, g, i: (b, g, 0, 0)),
            pl.BlockSpec((1, 1, S + WINDOW, LANES), lambda b, g, i: (b, g, 0, 0)),
            pl.BlockSpec((1, N_TOEP, HEADS_PER_GROUP, QB, QB), lambda b, g, i: (g, 0, 0, 0, 0)),
            pl.BlockSpec((1, HEADS_PER_GROUP, QB, WINDOW + QB), lambda b, g, i: (g, 0, 0, 0)),
        ],
        out_specs=pl.BlockSpec((1, QB, gw), lambda b, g, i: (b, i, g)),
        out_shape=jax.ShapeDtypeStruct((B, S, N_HEADS * HEAD_DIM), jnp.float32),
        scratch_shapes=[
            pltpu.VMEM((HEADS_PER_GROUP, QB, KEY_TILE), jnp.float32),
            pltpu.VMEM((HEADS_PER_GROUP, QB, KEY_TILE), jnp.float32),
            pltpu.VMEM((QROWS, KEY_TILE), jnp.bfloat16),
            pltpu.VMEM((QROWS, KEY_TILE), jnp.bfloat16),
            pltpu.VMEM((HEADS_PER_GROUP, QB, LANES), jnp.float32),
            pltpu.VMEM((QROWS, LANES), jnp.float32),
        ],
        compiler_params=pltpu.CompilerParams(
            dimension_semantics=("arbitrary", "arbitrary", "arbitrary"),
            vmem_limit_bytes=V7X_VMEM_LIMIT_BYTES,
        ),
        name="nsa_win_sel",
    )(q, gates, selneg, oc, ks_aug, vs, kw_t, vw_pad, toep, win)


def _nsa_attention(q, gates, kc, vc, kvsw, rel_bias):
    B, S, _ = q.shape
    assert S % KEY_TILE == 0 and S // L_SLC <= NBLK
    bf = jnp.bfloat16
    n_far = S // D_STRIDE
    toep, win, band, far = _bias_tables(rel_bias)
    cpad = ((0, 0), (0, 0), (CMP_PAD, n_far - kc.shape[2] + CMP_PAD), (0, 0))
    kc_pad = jnp.pad(kc, cpad).astype(bf)
    vc_pad = jnp.pad(vc, cpad).astype(bf)
    wmap = jnp.asarray(np.pad(_slc_map_matrix(n_far), ((CMP_PAD, CMP_PAD), (0, 0))), bf)
    oc, selneg = _cmp_select(q, gates, kc_pad, vc_pad, band, far, wmap)
    ks, vs, kw, vw = (kvsw.reshape(B, S, 4, N_KV_GROUPS, HEAD_DIM)[:, :, n] for n in range(4))
    blk_onehot = jnp.asarray((np.arange(S)[None, :] // L_SLC == np.arange(NBLK)[:, None]), bf)
    ks_aug = jnp.concatenate(
        [jnp.broadcast_to(blk_onehot, (B, N_KV_GROUPS, NBLK, S)), ks.transpose(0, 2, 3, 1)], axis=2)
    tail = 2 * KEY_TILE
    ks_aug = jnp.pad(ks_aug, ((0, 0), (0, 0), (0, 0), (0, tail)))
    kw_t = jnp.pad(kw.transpose(0, 2, 3, 1), ((0, 0), (0, 0), (0, 0), (WINDOW, 0)))

    def with_ones(v):
        v = v.transpose(0, 2, 1, 3)
        ones = jnp.ones(v.shape[:-1] + (1,), v.dtype)
        return jnp.pad(jnp.concatenate([v, ones], axis=-1), ((0, 0),) * 3 + ((0, LANES - HEAD_DIM - 1),))

    vw_aug = jnp.pad(with_ones(vw), ((0, 0), (0, 0), (WINDOW, 0), (0, 0)))
    vs_aug = jnp.pad(with_ones(vs), ((0, 0), (0, 0), (0, tail), (0, 0)))
    gates_g = gates[:, :, :3 * N_HEADS].reshape(B, S, N_KV_GROUPS, 3 * HEADS_PER_GROUP).swapaxes(1, 2)
    return _win_sel(q, gates_g, selneg, oc, ks_aug, vs_aug, kw_t, vw_aug, toep, win)


def _pool_layer_body(x_ref, halo_ref, gain_ref, shift_ref, scale_ref, gate_ref, pscale_ref, w_ref, o_ref):
    i = pl.program_id(1)
    x = x_ref[0]
    gain, shift, scale = gain_ref[...], shift_ref[0], scale_ref[0]
    h = _norm_modulate(x, gain, shift, scale)
    halo = _norm_modulate(halo_ref[0], gain, shift, scale) * (i > 0).astype(jnp.float32)
    hx = jnp.concatenate([halo, h], axis=0)
    t = i * POOL_TILE + lax.broadcasted_iota(jnp.int32, (POOL_TILE, 1), 0)
    outs = []
    run = hx
    width = 1
    for g, w in enumerate(POOL_WINDOWS):
        while width < w:
            run = run + pltpu.roll(run, width, axis=0)
            width *= 2
        lanes = slice(0, POOL_GROUP_DIM)
        cnt = jnp.minimum(t + 1, w).astype(jnp.float32)
        mix = run[POOL_HALO:, lanes] / cnt - h[:, g * POOL_GROUP_DIM:(g + 1) * POOL_GROUP_DIM]
        outs.append(jnp.dot(mix.astype(jnp.bfloat16), w_ref[g], preferred_element_type=jnp.float32))
        run = run[:, POOL_GROUP_DIM:]
    y = jnp.concatenate(outs, axis=-1) * pscale_ref[...]
    o_ref[0] = x + gate_ref[0] * y


def _pool_layer(x, gain, shift, scale, gate1, w_grp, pool_scale):
    B, S, D = x.shape
    assert S % POOL_TILE == 0 and POOL_HALO >= max(POOL_WINDOWS) - 1
    per_batch = pl.BlockSpec((1, 1, D), lambda b, i: (b, 0, 0))
    halo_blocks = POOL_TILE // POOL_HALO
    return pl.pallas_call(
        _pool_layer_body,
        grid=(B, S // POOL_TILE),
        in_specs=[
            pl.BlockSpec((1, POOL_TILE, D), lambda b, i: (b, i, 0)),
            pl.BlockSpec((1, POOL_HALO, D), lambda b, i: (b, jnp.maximum(i * halo_blocks - 1, 0), 0)),
            pl.BlockSpec((1, D), lambda b, i: (0, 0)),
            per_batch, per_batch, per_batch,
            pl.BlockSpec((1, D), lambda b, i: (0, 0)),
            pl.BlockSpec((N_POOL_GROUPS, POOL_GROUP_DIM, POOL_GROUP_DIM), lambda b, i: (0, 0, 0)),
        ],
        out_specs=pl.BlockSpec((1, POOL_TILE, D), lambda b, i: (b, i, 0)),
        out_shape=jax.ShapeDtypeStruct((B, S, D), jnp.float32),
        compiler_params=pltpu.CompilerParams(
            dimension_semantics=("arbitrary", "arbitrary"),
            vmem_limit_bytes=V7X_VMEM_LIMIT_BYTES,
        ),
        name="pool_layer",
    )(x, x, gain.reshape(1, D), shift[:, None, :], scale[:, None, :], gate1[:, None, :],
      pool_scale.reshape(1, D), w_grp.astype(jnp.bfloat16))


def _head_norm(v, seg_ref, gain):
    sq = v * v
    hi = sq.astype(jnp.bfloat16)
    lo = (sq - hi.astype(jnp.float32)).astype(jnp.bfloat16)
    ss = (jnp.dot(hi, seg_ref[...], preferred_element_type=jnp.float32)
          + jnp.dot(lo, seg_ref[...], preferred_element_type=jnp.float32))
    return v * lax.rsqrt(ss * (1.0 / HEAD_DIM) + EPS) * gain


def _qkv_proj_body(x_ref, gq_ref, shq_ref, scq_ref, gkv_ref, shkv_ref, sckv_ref, wq_ref, wg_ref, wkv_ref,
                   seg_ref, qgain_ref, kgain_ref, q_ref, gates_ref, kvc_ref, kvsw_ref):
    x = x_ref[...]
    xhat = x * lax.rsqrt(jnp.mean(x * x, axis=-1, keepdims=True) + EPS)
    hq = ((xhat * gq_ref[...]) * (1.0 + scq_ref[0]) + shq_ref[0]).astype(jnp.bfloat16)
    hkv = ((xhat * gkv_ref[...]) * (1.0 + sckv_ref[0]) + shkv_ref[0]).astype(jnp.bfloat16)
    pq = jnp.dot(hq, wq_ref[...], preferred_element_type=jnp.float32)
    gates_ref[...] = jax.nn.sigmoid(jnp.dot(hq, wg_ref[...], preferred_element_type=jnp.float32))
    pkv = jnp.dot(hkv, wkv_ref[...], preferred_element_type=jnp.float32)
    for g in range(N_KV_GROUPS):
        lanes = slice(g * GROUP_W, (g + 1) * GROUP_W)
        q_ref[:, lanes] = _head_norm(pq[:, lanes], seg_ref, qgain_ref[...]).astype(jnp.bfloat16)
    kvc_ref[...] = pkv[:, :2 * GROUP_W]
    k_s = _head_norm(pkv[:, 2 * GROUP_W:3 * GROUP_W], seg_ref, kgain_ref[0:1, :])
    k_w = _head_norm(pkv[:, 4 * GROUP_W:5 * GROUP_W], seg_ref, kgain_ref[1:2, :])
    kvsw_ref[...] = jnp.concatenate([k_s, pkv[:, 3 * GROUP_W:4 * GROUP_W], k_w, pkv[:, 5 * GROUP_W:]],
                                    axis=-1).astype(jnp.bfloat16)


def _qkv_proj(x, gain_q, shift_q, scale_q, gain_kv, shift_kv, scale_kv, w_qg, kv_w, q_gain, k_gain):
    B, S, D = x.shape
    T = B * S
    qd = N_HEADS * HEAD_DIM
    bf = jnp.bfloat16
    tiles_per_batch = S // ROW_TILE
    per_batch = pl.BlockSpec((1, 1, D), lambda i: (i // tiles_per_batch, 0, 0))
    const = lambda shape: pl.BlockSpec(shape, lambda i: (0,) * len(shape))
    seg = jnp.asarray(np.kron(np.eye(HEADS_PER_GROUP), np.ones((HEAD_DIM, HEAD_DIM))), bf)
    w_gate = jnp.pad(w_qg[:, qd:], ((0, 0), (0, LANES - 3 * N_HEADS))).astype(bf)
    qgain = jnp.tile(q_gain * HEAD_DIM ** -0.5, HEADS_PER_GROUP).reshape(1, GROUP_W)
    kgain = jnp.stack([jnp.tile(k_gain[1] * LOG2_E, N_KV_GROUPS), jnp.tile(k_gain[2] * LOG2_E, N_KV_GROUPS)])
    rows = lambda width: pl.BlockSpec((ROW_TILE, width), lambda i: (i, 0))
    return pl.pallas_call(
        _qkv_proj_body,
        grid=(T // ROW_TILE,),
        in_specs=[rows(D), const((1, D)), per_batch, per_batch, const((1, D)), per_batch, per_batch,
                  const((D, qd)), const((D, LANES)), const((D, 6 * GROUP_W)), const((GROUP_W, GROUP_W)),
                  const((1, GROUP_W)), const((2, GROUP_W))],
        out_specs=[rows(qd), rows(LANES), rows(2 * GROUP_W), rows(4 * GROUP_W)],
        out_shape=[jax.ShapeDtypeStruct((T, qd), bf), jax.ShapeDtypeStruct((T, LANES), jnp.float32),
                   jax.ShapeDtypeStruct((T, 2 * GROUP_W), jnp.float32),
                   jax.ShapeDtypeStruct((T, 4 * GROUP_W), bf)],
        compiler_params=pltpu.CompilerParams(
            dimension_semantics=("arbitrary",), vmem_limit_bytes=V7X_VMEM_LIMIT_BYTES),
        name="qkv_proj",
    )(x.reshape(T, D), gain_q.reshape(1, D), shift_q[:, None, :], scale_q[:, None, :],
      gain_kv.reshape(1, D), shift_kv[:, None, :], scale_kv[:, None, :],
      w_qg[:, :qd].astype(bf), w_gate, kv_w.astype(bf), seg, qgain, kgain)


def _compress_body(r_ref, pe_ref, w1_ref, w2_ref, gain_ref, o_ref):
    half = D_STRIDE * HEAD_DIM
    r = r_ref[0, 0, 0]
    top = jnp.dot(r, w1_ref[0, :half, :], preferred_element_type=jnp.float32)
    bot = jnp.dot(r, w1_ref[0, half:, :], preferred_element_type=jnp.float32)
    pe = jnp.dot(jnp.broadcast_to(pe_ref[0], (8, R_CMP * half)).astype(jnp.bfloat16), w1_ref[0],
                 preferred_element_type=jnp.float32)[0:1]
    n_chunks = r.shape[0]
    hidden = top + pltpu.roll(bot, n_chunks - 1, axis=0) + pe
    out = jnp.dot(jax.nn.gelu(hidden).astype(jnp.bfloat16), w2_ref[0], preferred_element_type=jnp.float32)
    normed = out * lax.rsqrt(jnp.mean(out * out, axis=-1, keepdims=True) + EPS) * gain_ref[...]
    o_ref[0, 0, 0] = jnp.where(pl.program_id(0) == 0, normed, out)


def _compressed_kv(kvc, B, S, cmp_pe_k, cmp_pe_v, cmp_k_w1, cmp_k_w2, cmp_v_w1, cmp_v_w2, k_gain):
    bf = jnp.bfloat16
    n_chunks = S // D_STRIDE
    half = D_STRIDE * HEAD_DIM
    hidden = cmp_k_w1.shape[1]
    r = kvc.astype(bf).reshape(B, n_chunks, D_STRIDE, 2, N_KV_GROUPS, HEAD_DIM)
    r = r.transpose(3, 0, 4, 1, 2, 5).reshape(2, B, N_KV_GROUPS, n_chunks, half)
    pe = jnp.stack([cmp_pe_k, cmp_pe_v]).reshape(2, 1, R_CMP * half)
    per_kind = lambda shape: pl.BlockSpec((1,) + shape, lambda kv, b, g: (kv,) + (0,) * len(shape))
    out = pl.pallas_call(
        _compress_body,
        grid=(2, B, N_KV_GROUPS),
        in_specs=[
            pl.BlockSpec((1, 1, 1, n_chunks, half), lambda kv, b, g: (kv, b, g, 0, 0)),
            per_kind((1, R_CMP * half)), per_kind((R_CMP * half, hidden)), per_kind((hidden, HEAD_DIM)),
            pl.BlockSpec((1, HEAD_DIM), lambda kv, b, g: (0, 0)),
        ],
        out_specs=pl.BlockSpec((1, 1, 1, n_chunks, HEAD_DIM), lambda kv, b, g: (kv, b, g, 0, 0)),
        out_shape=jax.ShapeDtypeStruct((2, B, N_KV_GROUPS, n_chunks, HEAD_DIM), jnp.float32),
        compiler_params=pltpu.CompilerParams(
            dimension_semantics=("arbitrary", "arbitrary", "arbitrary"),
            vmem_limit_bytes=V7X_VMEM_LIMIT_BYTES),
        name="cmp_mlp",
    )(r, pe, jnp.stack([cmp_k_w1, cmp_v_w1]).astype(bf), jnp.stack([cmp_k_w2, cmp_v_w2]).astype(bf),
      (k_gain[0] * LOG2_E).reshape(1, HEAD_DIM))
    n_cmp = n_chunks - R_CMP + 1
    return out[0, :, :, :n_cmp], out[1, :, :, :n_cmp]


def _oproj_router_body(a_ref, x_ref, gate_ref, wo_ref, gain_ref, shift_ref, scale_ref, wr_ref, br_ref,
                       o_ref, logit_ref):
    mix = jnp.dot(a_ref[...].astype(jnp.bfloat16), wo_ref[...], preferred_element_type=jnp.float32)
    x = x_ref[...] + gate_ref[0] * mix
    o_ref[...] = x
    h = _norm_modulate(x, gain_ref[...], shift_ref[0], scale_ref[0]).astype(jnp.bfloat16)
    logit_ref[...] = jnp.dot(h, wr_ref[...], preferred_element_type=jnp.float32) + br_ref[...]


def _oproj_router(attn, x, gate1, w_o, gain, shift, scale, w_router, b_router):
    B, S, D = x.shape
    T = B * S
    bf = jnp.bfloat16
    tiles_per_batch = S // ROW_TILE
    per_batch = pl.BlockSpec((1, 1, D), lambda i: (i // tiles_per_batch, 0, 0))
    const = lambda shape: pl.BlockSpec(shape, lambda i: (0,) * len(shape))
    rows = lambda width: pl.BlockSpec((ROW_TILE, width), lambda i: (i, 0))
    pad = ((0, 0), (0, LANES - N_EXPERTS))
    out, logits = pl.pallas_call(
        _oproj_router_body,
        grid=(T // ROW_TILE,),
        in_specs=[rows(D), rows(D), per_batch, const((D, D)), const((1, D)), per_batch, per_batch,
                  const((D, LANES)), const((1, LANES))],
        out_specs=[rows(D), rows(LANES)],
        out_shape=[jax.ShapeDtypeStruct((T, D), jnp.float32), jax.ShapeDtypeStruct((T, LANES), jnp.float32)],
        compiler_params=pltpu.CompilerParams(
            dimension_semantics=("arbitrary",), vmem_limit_bytes=V7X_VMEM_LIMIT_BYTES),
        name="oproj_router",
    )(attn.reshape(T, D), x.reshape(T, D), gate1[:, None, :], w_o.astype(bf), gain.reshape(1, D),
      shift[:, None, :], scale[:, None, :], jnp.pad(w_router, pad).astype(bf),
      jnp.pad(b_router.reshape(1, N_EXPERTS), pad))
    return out.reshape(B, S, D), logits[:, :N_EXPERTS]


def _route(logits):
    T = logits.shape[0]
    lane = jnp.arange(N_EXPERTS)[None, :]
    l0 = jnp.max(logits, axis=-1)
    e0 = jnp.argmax(logits, axis=-1)
    rest = jnp.where(lane == e0[:, None], -jnp.inf, logits)
    l1 = jnp.max(rest, axis=-1)
    e1 = jnp.argmax(rest, axis=-1)
    z = jnp.exp(l1 - l0)
    top_w = jnp.stack([1.0 / (1.0 + z), z / (1.0 + z)], axis=-1)
    oh = [(lane == e[:, None]).astype(jnp.int32) for e in (e0, e1)]
    cnt = oh[0] + oh[1]
    before = jnp.cumsum(cnt, axis=0) - cnt
    counts = before[-1] + cnt[-1]
    pcounts = (counts + ROW_TILE - 1) // ROW_TILE * ROW_TILE
    pends = jnp.cumsum(pcounts)
    pstarts = pends - pcounts
    dest = jnp.stack([jnp.sum((pstarts[None, :] + before) * o, axis=-1) for o in oh], axis=-1)
    n_blocks = -(-(T * TOP_K + N_EXPERTS * (ROW_TILE - 1)) // ROW_TILE)
    blk_start = jnp.arange(n_blocks)[:, None] * ROW_TILE
    blk_e = jnp.minimum(jnp.sum(pends[None, :] <= blk_start, axis=-1), N_EXPERTS - 1).astype(jnp.int32)
    return top_w, dest.astype(jnp.int32), blk_e, n_blocks * ROW_TILE


def _moe_ffn(x, logits, gain, shift, scale, gate2, w_gu, w_dn):
    top_w, dest, blk_e, n_rows = _route(logits)
    rows = _moe_dispatch(x, gain, shift, scale, dest, n_rows)
    y = _grouped_swiglu(rows, blk_e, w_gu.astype(jnp.bfloat16), w_dn.astype(jnp.bfloat16),
                        ff_tile=FF_TILE_EXPERT)
    return _moe_combine(x, top_w, gate2, y, dest)


def kernel(x, c, ada_w, ada_b, norm_g, pool_w, pool_scale, q_w, q_gain, o_w, kv_ada_w, kv_ada_b, kv_norm_g, kv_w, cmp_pe_k, cmp_pe_v, cmp_k_w1, cmp_k_w2, cmp_v_w1, cmp_v_w2, k_gain, rel_bias, ffn_gu, ffn_dn, router_w, router_b, exp_gu, exp_dn):
    assert DEPTH == 2 and N_A_LAYERS == 1
    B, S, D = x.shape
    silu_c = jax.nn.silu(c)
    sh1, sc1, g1, sh2, sc2, g2 = jnp.split(silu_c @ ada_w[0] + ada_b[0], 6, axis=-1)
    x = _pool_layer(x, norm_g[0, 0], sh1, sc1, g1, pool_w[0], pool_scale[0])
    x = _dense_ffn(x, norm_g[0, 1], sh2, sc2, g2, ffn_gu[0], ffn_dn[0])
    sh1, sc1, g1, sh2, sc2, g2 = jnp.split(silu_c @ ada_w[1] + ada_b[1], 6, axis=-1)
    sh_kv, sc_kv = jnp.split(silu_c @ kv_ada_w + kv_ada_b, 2, axis=-1)
    q, gates, kvc, kvsw = _qkv_proj(x, norm_g[1, 0], sh1, sc1, kv_norm_g, sh_kv, sc_kv, q_w[0], kv_w,
                                    q_gain[0], k_gain)
    kc, vc = _compressed_kv(kvc, B, S, cmp_pe_k, cmp_pe_v, cmp_k_w1, cmp_k_w2, cmp_v_w1, cmp_v_w2, k_gain)
    attn = _nsa_attention(q.reshape(B, S, -1), gates.reshape(B, S, -1), kc, vc, kvsw.reshape(B, S, -1),
                          rel_bias)
    x, logits = _oproj_router(attn, x, g1, o_w[0], norm_g[1, 1], sh2, sc2, router_w[0], router_b[0])
    return _moe_ffn(x, logits, norm_g[1, 1], sh2, sc2, g2, exp_gu[0], exp_dn[0])
```

```python
import functools
import math

import jax
import jax.numpy as jnp
import numpy as np
from jax import lax
from jax.experimental import pallas as pl
from jax.experimental.pallas import tpu as pltpu

D_MODEL = 1024
DEPTH = 2
N_A_LAYERS = DEPTH // 2
POOL_WINDOWS = (2, 4, 8, 16)
N_POOL_GROUPS = len(POOL_WINDOWS)
POOL_GROUP_DIM = D_MODEL // N_POOL_GROUPS
HEAD_DIM = 64
N_HEADS = D_MODEL // HEAD_DIM
N_KV_GROUPS = 4
HEADS_PER_GROUP = N_HEADS // N_KV_GROUPS
L_CMP = 32
D_STRIDE = 16
L_SLC = 64
N_SEL = 16
WINDOW = 512
R_CMP = L_CMP // D_STRIDE
R_SLC = L_SLC // D_STRIDE
N_BUCKETS = 32
REL_EXACT = N_BUCKETS // 2
MAX_DISTANCE = 1024
N_EXPERTS = 8
TOP_K = 2
EPS = 1e-6
NEG_INF = -1e30
SEL_FORCE = 1e6
LOG2_E = math.log2(math.e)

V7X_VMEM_LIMIT_BYTES = 48 * 1024 * 1024
LANES = 128
ROW_TILE = 512
TOK_TILE = 512
DMA_ISSUE_UNROLL = 8
MAX_FORCED = 3
FF_TILE_DENSE = 2816
FF_TILE_EXPERT = 1792
POOL_TILE = 512
POOL_HALO = 16

QB = 128
KEY_TILE = 2 * QB
SEL_UNROLL = 8
QROWS = HEADS_PER_GROUP * QB
WIN_HEADS = 4
GROUP_W = HEADS_PER_GROUP * HEAD_DIM
NBLK = 128
CMP_PAD = 128
FAR_DIST = MAX_DISTANCE
N_TOEP = FAR_DIST // QB + 3
BAND_LEFT = CMP_PAD - 16
assert BAND_LEFT * D_STRIDE + (L_CMP - 1) - 2 * QB >= FAR_DIST


def _rms_norm(x, g):
    xf = x.astype(jnp.float32)
    y = xf * lax.rsqrt(jnp.mean(xf * xf, axis=-1, keepdims=True) + EPS)
    return (y * g.astype(jnp.float32)).astype(x.dtype)


def _modulate(h, shift, scale):
    return h * (1 + scale[:, None, :]) + shift[:, None, :]


def _rel_bucket(dist):
    d = jnp.maximum(dist, 0)
    ratio = jnp.maximum(d, REL_EXACT).astype(jnp.float32) / REL_EXACT
    large = REL_EXACT + (jnp.log(ratio) / math.log(MAX_DISTANCE / REL_EXACT)
                         * (N_BUCKETS - REL_EXACT)).astype(jnp.int32)
    return jnp.where(d < REL_EXACT, d, jnp.minimum(large, N_BUCKETS - 1))


def _norm_modulate(x, gain, shift, scale):
    y = x * lax.rsqrt(jnp.mean(x * x, axis=-1, keepdims=True) + EPS)
    return (y * gain) * (1.0 + scale) + shift


def _swiglu_step(xb_ref, wg_ref, wu_ref, wd_ref, acc_ref, j):
    @pl.when(j == 0)
    def _():
        acc_ref[...] = jnp.zeros_like(acc_ref)

    x = xb_ref[...]
    gate = jnp.dot(x, wg_ref[0], preferred_element_type=jnp.float32)
    up = jnp.dot(x, wu_ref[0], preferred_element_type=jnp.float32)
    act = (gate * jax.nn.sigmoid(gate) * up).astype(jnp.bfloat16)
    acc_ref[...] += jnp.dot(act, wd_ref[0], preferred_element_type=jnp.float32)


def _grouped_swiglu_body(blk_e_ref, n_used_ref, x_ref, wg_ref, wu_ref, wd_ref, o_ref, acc_ref, xb_ref, *,
                         n_ff_steps):
    del blk_e_ref
    i, j = pl.program_id(0), pl.program_id(1)
    used = i < n_used_ref[0]

    @pl.when(used & (j == 0))
    def _():
        xb_ref[...] = x_ref[...].astype(jnp.bfloat16)

    @pl.when(used)
    def _():
        _swiglu_step(xb_ref, wg_ref, wu_ref, wd_ref, acc_ref, j)

    @pl.when(j == n_ff_steps - 1)
    def _():
        o_ref[...] = jnp.where(used, acc_ref[...], 0.0)


def _grouped_swiglu(x_rows, blk_e, n_used, w_gu, w_dn, *, ff_tile):
    n_rows, d = x_rows.shape
    d_ff = w_dn.shape[1]
    assert n_rows % ROW_TILE == 0 and d_ff % ff_tile == 0
    n_ff_steps = d_ff // ff_tile
    grid = (n_rows // ROW_TILE, n_ff_steps)
    return pl.pallas_call(
        functools.partial(_grouped_swiglu_body, n_ff_steps=n_ff_steps),
        grid_spec=pltpu.PrefetchScalarGridSpec(
            num_scalar_prefetch=2,
            grid=grid,
            in_specs=[
                pl.BlockSpec((ROW_TILE, d), lambda i, j, e, n: (i, 0)),
                pl.BlockSpec((1, d, ff_tile), lambda i, j, e, n: (e[i], 0, j)),
                pl.BlockSpec((1, d, ff_tile), lambda i, j, e, n: (e[i], 0, j + n_ff_steps)),
                pl.BlockSpec((1, ff_tile, d), lambda i, j, e, n: (e[i], j, 0)),
            ],
            out_specs=pl.BlockSpec((ROW_TILE, d), lambda i, j, e, n: (i, 0)),
            scratch_shapes=[pltpu.VMEM((ROW_TILE, d), jnp.float32), pltpu.VMEM((ROW_TILE, d), jnp.bfloat16)],
        ),
        out_shape=jax.ShapeDtypeStruct((n_rows, d), jnp.float32),
        compiler_params=pltpu.CompilerParams(
            dimension_semantics=("arbitrary", "arbitrary"),
            vmem_limit_bytes=V7X_VMEM_LIMIT_BYTES,
        ),
        name="grouped_swiglu",
    )(blk_e, n_used, x_rows, w_gu, w_gu, w_dn)


def _dense_ffn_body(x_ref, gain_ref, shift_ref, scale_ref, gate2_ref, wg_ref, wu_ref, wd_ref, o_ref,
                    acc_ref, xb_ref, *, n_ff_steps):
    j = pl.program_id(1)

    @pl.when(j == 0)
    def _():
        xb_ref[...] = _norm_modulate(x_ref[...], gain_ref[...], shift_ref[0], scale_ref[0]).astype(jnp.bfloat16)

    _swiglu_step(xb_ref, wg_ref, wu_ref, wd_ref, acc_ref, j)

    @pl.when(j == n_ff_steps - 1)
    def _():
        o_ref[...] = x_ref[...] + gate2_ref[0] * acc_ref[...]


def _dense_ffn(x, gain, shift, scale, gate2, w_gu, w_dn):
    B, S, D = x.shape
    d_ff = w_dn.shape[0]
    n_ff_steps = d_ff // FF_TILE_DENSE
    tiles_per_batch = S // ROW_TILE
    per_batch = pl.BlockSpec((1, 1, D), lambda i, j: (i // tiles_per_batch, 0, 0))
    weight_buffers = pl.Buffered(1) if n_ff_steps == 1 else None
    out = pl.pallas_call(
        functools.partial(_dense_ffn_body, n_ff_steps=n_ff_steps),
        grid=(B * S // ROW_TILE, n_ff_steps),
        in_specs=[
            pl.BlockSpec((ROW_TILE, D), lambda i, j: (i, 0)),
            pl.BlockSpec((1, D), lambda i, j: (0, 0)),
            per_batch, per_batch, per_batch,
            pl.BlockSpec((1, D, FF_TILE_DENSE), lambda i, j: (0, 0, j), pipeline_mode=weight_buffers),
            pl.BlockSpec((1, D, FF_TILE_DENSE), lambda i, j: (0, 0, j + n_ff_steps), pipeline_mode=weight_buffers),
            pl.BlockSpec((1, FF_TILE_DENSE, D), lambda i, j: (0, j, 0), pipeline_mode=weight_buffers),
        ],
        out_specs=pl.BlockSpec((ROW_TILE, D), lambda i, j: (i, 0)),
        out_shape=jax.ShapeDtypeStruct((B * S, D), jnp.float32),
        scratch_shapes=[pltpu.VMEM((ROW_TILE, D), jnp.float32), pltpu.VMEM((ROW_TILE, D), jnp.bfloat16)],
        compiler_params=pltpu.CompilerParams(
            dimension_semantics=("arbitrary", "arbitrary"),
            vmem_limit_bytes=V7X_VMEM_LIMIT_BYTES,
        ),
        name="dense_ffn",
    )(x.reshape(B * S, D), gain.reshape(1, D), shift[:, None, :], scale[:, None, :], gate2[:, None, :],
      w_gu.astype(jnp.bfloat16)[None], w_gu.astype(jnp.bfloat16)[None], w_dn.astype(jnp.bfloat16)[None])
    return out.reshape(B, S, D)


def _row_copies_wait(src_ref, dst_ref, sem, n_rows):
    pltpu.make_async_copy(src_ref.at[pl.ds(0, n_rows)], dst_ref.at[pl.ds(0, n_rows)], sem).wait()


def _dispatch_body(dest_ref, x_ref, gain_ref, shift_ref, scale_ref, rows_in_ref, rows_ref, h_ref, sem):
    del rows_in_ref
    h_ref[...] = _norm_modulate(x_ref[...], gain_ref[...], shift_ref[0], scale_ref[0])

    def issue(r, carry):
        for k in range(TOP_K):
            pltpu.make_async_copy(h_ref.at[pl.ds(r, 1)], rows_ref.at[pl.ds(dest_ref[TOP_K * r + k], 1)],
                                  sem).start()
        return carry

    lax.fori_loop(0, TOK_TILE, issue, 0, unroll=DMA_ISSUE_UNROLL)
    for k in range(TOP_K):
        _row_copies_wait(h_ref, rows_ref, sem, TOK_TILE)


def _moe_dispatch(x, gain, shift, scale, dest, n_rows):
    B, S, D = x.shape
    T = B * S
    tiles_per_batch = S // TOK_TILE
    per_batch = pl.BlockSpec((1, 1, D), lambda i: (i // tiles_per_batch, 0, 0))
    return pl.pallas_call(
        _dispatch_body,
        grid=(T // TOK_TILE,),
        in_specs=[
            pl.BlockSpec((TOP_K * TOK_TILE,), lambda i: (i,), memory_space=pltpu.SMEM),
            pl.BlockSpec((TOK_TILE, D), lambda i: (i, 0)),
            pl.BlockSpec((1, D), lambda i: (0, 0)),
            per_batch, per_batch,
            pl.BlockSpec(memory_space=pl.ANY),
        ],
        out_specs=pl.BlockSpec(memory_space=pl.ANY),
        out_shape=jax.ShapeDtypeStruct((n_rows, D), jnp.float32),
        scratch_shapes=[pltpu.VMEM((TOK_TILE, D), jnp.float32), pltpu.SemaphoreType.DMA(())],
        input_output_aliases={5: 0},
        compiler_params=pltpu.CompilerParams(dimension_semantics=("arbitrary",)),
        name="moe_dispatch",
    )(dest.reshape(T * TOP_K), x.reshape(T, D), gain.reshape(1, D), shift[:, None, :], scale[:, None, :],
      jnp.zeros((n_rows, D), jnp.float32))


def _combine_body(dest_ref, x_ref, w_ref, gate2_ref, y_ref, o_ref, buf_ref, sem):
    def issue(r, carry):
        for k in range(TOP_K):
            pltpu.make_async_copy(y_ref.at[pl.ds(dest_ref[TOP_K * r + k], 1)], buf_ref.at[k, pl.ds(r, 1)],
                                  sem).start()
        return carry

    lax.fori_loop(0, TOK_TILE, issue, 0, unroll=DMA_ISSUE_UNROLL)
    for k in range(TOP_K):
        _row_copies_wait(y_ref, buf_ref.at[k], sem, TOK_TILE)
    w = w_ref[...]
    f = w[:, 0:1] * buf_ref[0] + w[:, 1:2] * buf_ref[1]
    o_ref[...] = x_ref[...] + gate2_ref[0] * f


def _moe_combine(x, top_w, gate2, y, dest):
    B, S, D = x.shape
    T = B * S
    tiles_per_batch = S // TOK_TILE
    out = pl.pallas_call(
        _combine_body,
        grid=(T // TOK_TILE,),
        in_specs=[
            pl.BlockSpec((TOP_K * TOK_TILE,), lambda i: (i,), memory_space=pltpu.SMEM),
            pl.BlockSpec((TOK_TILE, D), lambda i: (i, 0)),
            pl.BlockSpec((TOK_TILE, TOP_K), lambda i: (i, 0)),
            pl.BlockSpec((1, 1, D), lambda i: (i // tiles_per_batch, 0, 0)),
            pl.BlockSpec(memory_space=pl.ANY),
        ],
        out_specs=pl.BlockSpec((TOK_TILE, D), lambda i: (i, 0)),
        out_shape=jax.ShapeDtypeStruct((T, D), jnp.float32),
        scratch_shapes=[pltpu.VMEM((TOP_K, TOK_TILE, D), jnp.float32), pltpu.SemaphoreType.DMA(())],
        compiler_params=pltpu.CompilerParams(dimension_semantics=("arbitrary",)),
        name="moe_combine",
    )(dest.reshape(T * TOP_K), x.reshape(T, D), top_w, gate2[:, None, :], y)
    return out.reshape(B, S, D)


def _bias_tables(rel_bias):
    x0 = (N_TOEP - 1) * QB
    width = x0 + 2 * QB
    period = width + QB
    n = np.arange(period)
    n = np.where(n < width, n, n - period)
    by_dist = rel_bias.astype(jnp.float32)[_rel_bucket(jnp.asarray(np.maximum(x0 - n, 0)))].T
    strip = jnp.tile(by_dist, (1, QB))[:, :QB * (period - 1)].reshape(N_HEADS, QB, period - 1)[:, :, :width]
    far = rel_bias.astype(jnp.float32)[N_BUCKETS - 1] * LOG2_E

    i = np.arange(QB)[:, None]
    toep = jnp.stack([strip[:, :, x0 - QB * m:x0 - QB * m + QB] for m in range(-1, N_TOEP - 1)], axis=1)
    d_toep = QB * np.arange(-1, N_TOEP - 1)[:, None, None] + i[None] - np.arange(QB)[None, None, :]
    toep = jnp.where(d_toep >= 0, toep * LOG2_E, NEG_INF)
    toep = toep.reshape(N_KV_GROUPS, HEADS_PER_GROUP, N_TOEP, QB, QB).transpose(0, 2, 1, 3, 4)
    d_win = WINDOW + i - np.arange(WINDOW + QB)[None, :]
    win = jnp.where((d_win >= 0) & (d_win < WINDOW), strip[:, :, x0 - WINDOW:x0 + QB] * LOG2_E, NEG_INF)
    win = win.reshape(N_KV_GROUPS, HEADS_PER_GROUP, QB, WINDOW + QB)
    bands = []
    for par in range(2):
        off = x0 - QB * par - D_STRIDE * BAND_LEFT + (L_CMP - 1)
        c_first = -(off // D_STRIDE)
        cols = strip[:, :, D_STRIDE * c_first + off::D_STRIDE][:, :, :128 - c_first] * LOG2_E
        left = jnp.broadcast_to(far[:, None, None], (N_HEADS, QB, c_first))
        right = jnp.zeros((N_HEADS, QB, 128 - c_first - cols.shape[2]), jnp.float32)
        d_band = QB * par + i - D_STRIDE * (np.arange(128)[None, :] - BAND_LEFT) - (L_CMP - 1)
        assert (d_band[:, 128 - right.shape[2]:] < 0).all()
        bands.append(jnp.where(d_band >= 0, jnp.concatenate([left, cols, right], axis=-1), NEG_INF))
    band = jnp.stack(bands).reshape(2, N_KV_GROUPS, HEADS_PER_GROUP, QB, 128)
    return toep, win, band, far


def _slc_map_matrix(n_cmp_cols):
    w = np.zeros((n_cmp_cols, NBLK), np.float32)
    for jb in range(NBLK):
        for mm in range(R_SLC):
            for nn in range(R_CMP):
                k = R_SLC * jb + mm - nn
                if 0 <= k < n_cmp_cols:
                    w[k, jb] += 1.0
    return w


def _dot_nt(a, b):
    return lax.dot_general(a, b, (((1,), (1,)), ((), ())), preferred_element_type=jnp.float32)


def _cmp_select_body(far_ref, q_ref, gate_ref, kc_ref, vc_ref, band_ref, wmap_ref, oc_ref, sel_ref,
                     *, n_far):
    qi = pl.program_id(1)
    par = qi % 2
    band0 = pl.multiple_of(16 * (qi // 2 + 1), 16)
    first_band_blk = band0 - CMP_PAD

    col = lax.broadcasted_iota(jnp.int32, (QB, 128), 1)
    row = lax.broadcasted_iota(jnp.int32, (QB, 128), 0)
    neg_pad = jnp.where(first_band_blk + col >= 0, 0.0, NEG_INF)

    t = qi * QB + row
    cur = t // L_SLC
    forced = (col == 0) | (col == cur) | (col == cur - 1)
    n_forced = 1 + (cur[:, :1] >= 1).astype(jnp.int32) + (cur[:, :1] >= 2).astype(jnp.int32)
    valid = col * L_SLC <= t

    q_all = q_ref[0]
    colf = col.astype(jnp.float32)

    def take_best(score, active=None):
        best = jnp.max(score, axis=-1, keepdims=True)
        first = jnp.min(jnp.where(score == best, colf, float(NBLK)), axis=-1, keepdims=True)
        hit = colf == first
        return jnp.where(hit if active is None else hit & active, -jnp.inf, score)

    def branches(n_cols):
        far_col = lax.broadcasted_iota(jnp.int32, (1, max(n_cols, 1)), 1)
        neg_far = jnp.where(far_col < first_band_blk, 0.0, NEG_INF)

        def far_and_band(ref, *lead):
            band_rows = ref[(*lead, pl.ds(band0, 128), slice(None))]
            if not n_cols:
                return band_rows
            return jnp.concatenate([ref[(*lead, slice(CMP_PAD, CMP_PAD + n_cols), slice(None))], band_rows], axis=0)

        wmap_all = far_and_band(wmap_ref)
        scores, start_scores = [], []
        oc_heads = []
        for g in range(N_KV_GROUPS):
            q4 = jnp.concatenate(
                [q_all[:, (g * HEADS_PER_GROUP + h) * HEAD_DIM:(g * HEADS_PER_GROUP + h + 1) * HEAD_DIM]
                 for h in range(HEADS_PER_GROUP)], axis=0)
            k_all = far_and_band(kc_ref, 0, g)
            v_all = far_and_band(vc_ref, 0, g)
            s_all = _dot_nt(q4, k_all).reshape(HEADS_PER_GROUP, QB, n_cols + 128)
            imp = jnp.zeros((QB, n_cols + 128), jnp.float32)
            for h in range(HEADS_PER_GROUP):
                hh = g * HEADS_PER_GROUP + h
                bias = band_ref[par, g, h] + neg_pad
                if n_cols:
                    bias = jnp.concatenate([jnp.broadcast_to(far_ref[hh] + neg_far, (QB, n_cols)), bias],
                                           axis=-1)
                s = s_all[h] + bias
                m = jnp.maximum(jnp.max(s, axis=-1, keepdims=True), 1e-10 * NEG_INF)
                p = jnp.exp2(s - m)
                l = jnp.sum(p, axis=-1, keepdims=True)
                p = p * jnp.where(l > 0.0, 1.0 / l, 0.0)
                imp = imp + p
                o = jnp.dot(p.astype(jnp.bfloat16), v_all, preferred_element_type=jnp.float32)
                oc_heads.append(o * gate_ref[0, :, 3 * hh:3 * hh + 1])
            p_slc = jnp.zeros((QB, NBLK), jnp.float32)
            rest = imp
            for _ in range(3):
                term = rest.astype(jnp.bfloat16)
                p_slc = p_slc + jnp.dot(term, wmap_all, preferred_element_type=jnp.float32)
                rest = rest - term.astype(jnp.float32)
            score = jnp.where(forced, -jnp.inf, jnp.where(valid, p_slc, -SEL_FORCE - colf))
            start_scores.append(score)
            for _ in range(N_SEL - MAX_FORCED):
                score = jnp.where(score == jnp.max(score, axis=-1, keepdims=True), -jnp.inf, score)
            scores.append(score)
        return (jnp.concatenate(oc_heads, axis=-1),) + tuple(scores) + tuple(start_scores)

    n_variants = n_far // 128
    variant = jnp.clip((first_band_blk + 127) // 128, 0, n_variants - 1)
    res = lax.switch(variant, [functools.partial(branches, 128 * v) for v in range(n_variants)])
    oc_ref[0] = res[0]
    scores, start_scores = res[1:1 + N_KV_GROUPS], res[1 + N_KV_GROUPS:]

    expected = (n_forced + (N_SEL - MAX_FORCED)).astype(jnp.float32)
    excess = [jnp.sum(jnp.where(s == -jnp.inf, 1.0, 0.0), axis=-1, keepdims=True) - expected for s in scores]
    any_tie = jnp.max(sum(excess)) > 0.0

    def exact_passes(start_scores):
        for _ in range(N_SEL - MAX_FORCED):
            start_scores = tuple(take_best(s) for s in start_scores)
        return start_scores

    scores = lax.cond(any_tie, exact_passes, lambda _: tuple(scores), tuple(start_scores))

    def early_rows(scores):
        for extra in range(MAX_FORCED - 1):
            scores = tuple(take_best(s, MAX_FORCED - n_forced > extra) for s in scores)
        return scores

    scores = lax.cond(qi == 0, early_rows, lambda s: s, tuple(scores))
    for g in range(N_KV_GROUPS):
        sel_ref[0, g] = jnp.where(scores[g] == -jnp.inf, 0.0, NEG_INF).astype(jnp.bfloat16)


def _cmp_select(q, gates, kc_pad, vc_pad, band, far, wmap):
    B, S, _ = q.shape
    n_far = S // D_STRIDE
    n_pad = kc_pad.shape[2]
    grid = (B, S // QB)
    return pl.pallas_call(
        functools.partial(_cmp_select_body, n_far=n_far),
        grid=grid,
        in_specs=[
            pl.BlockSpec(memory_space=pltpu.SMEM),
            pl.BlockSpec((1, QB, N_HEADS * HEAD_DIM), lambda b, i: (b, i, 0)),
            pl.BlockSpec((1, QB, LANES), lambda b, i: (b, i, 0)),
            pl.BlockSpec((1, N_KV_GROUPS, n_pad, HEAD_DIM), lambda b, i: (b, 0, 0, 0)),
            pl.BlockSpec((1, N_KV_GROUPS, n_pad, HEAD_DIM), lambda b, i: (b, 0, 0, 0)),
            pl.BlockSpec((2, N_KV_GROUPS, HEADS_PER_GROUP, QB, 128), lambda b, i: (0, 0, 0, 0, 0)),
            pl.BlockSpec((n_pad, NBLK), lambda b, i: (0, 0)),
        ],
        out_specs=[
            pl.BlockSpec((1, QB, N_HEADS * HEAD_DIM), lambda b, i: (b, i, 0)),
            pl.BlockSpec((1, N_KV_GROUPS, QB, NBLK), lambda b, i: (b, 0, i, 0)),
        ],
        out_shape=[
            jax.ShapeDtypeStruct((B, S, N_HEADS * HEAD_DIM), jnp.float32),
            jax.ShapeDtypeStruct((B, N_KV_GROUPS, S, NBLK), jnp.bfloat16),
        ],
        compiler_params=pltpu.CompilerParams(
            dimension_semantics=("arbitrary", "arbitrary"),
            vmem_limit_bytes=V7X_VMEM_LIMIT_BYTES,
        ),
        name="nsa_cmp_select",
    )(far, q, gates, kc_pad, vc_pad, band, wmap)


def _win_sel_body(q_ref, gate_ref, sel_ref, oc_ref, ks_ref, vs_ref, kw_ref, vw_ref, toep_ref, win_ref,
                  out_ref, s0_scr, s1_scr, p0_scr, p1_scr, m_scr, acc_scr):
    qi = pl.program_id(2)
    q0 = pl.multiple_of(qi * QB, QB)
    qg = q_ref[0]
    q_heads = [qg[:, h * HEAD_DIM:(h + 1) * HEAD_DIM] for h in range(HEADS_PER_GROUP)]
    q4 = jnp.concatenate(q_heads, axis=0)

    sel = sel_ref[0, 0]
    qa = jnp.concatenate([jnp.concatenate([sel] * HEADS_PER_GROUP, axis=0), q4], axis=-1)
    c_diag = qi // 2

    def scores(c):
        col = pl.multiple_of(c * KEY_TILE, KEY_TILE)
        s = jnp.dot(qa, ks_ref[0, 0, :, pl.ds(col, KEY_TILE)], preferred_element_type=jnp.float32)
        mm = qi - 2 * c
        bias = jnp.concatenate([toep_ref[0, jnp.clip(mm + 1, 0, N_TOEP - 1)],
                                toep_ref[0, jnp.clip(mm, 0, N_TOEP - 1)]], axis=-1)
        return s.reshape(HEADS_PER_GROUP, QB, KEY_TILE) + bias

    s_slots = (s0_scr, s1_scr)
    p_slots = (p0_scr, p1_scr)
    m_scr[...] = jnp.full(m_scr.shape, NEG_INF, jnp.float32)
    acc_scr[...] = jnp.zeros_like(acc_scr)
    p_slots[0][...] = jnp.zeros_like(p0_scr)
    s_slots[0][...] = scores(0)

    def weighted_values(p, col):
        return jnp.dot(p, vs_ref[0, 0, pl.ds(pl.multiple_of(col, KEY_TILE), KEY_TILE), :],
                       preferred_element_type=jnp.float32)

    def half_step(c, col_prev, cur):
        pv = weighted_values(p_slots[cur][...], col_prev)
        s_slots[1 - cur][...] = scores(c + 1)
        for h in range(HEADS_PER_GROUP):
            s = s_slots[cur][h]
            m_old = m_scr[h]
            m_new = jnp.maximum(m_old, jnp.max(s, axis=-1, keepdims=True))
            alpha = jnp.exp2(m_old - m_new)
            p = jnp.exp2(s - jnp.concatenate([m_new] * (KEY_TILE // LANES), axis=-1))
            p_slots[1 - cur][h * QB:(h + 1) * QB, :] = p.astype(jnp.bfloat16)
            m_scr[h] = m_new
            acc_scr[h * QB:(h + 1) * QB, :] = alpha * (acc_scr[h * QB:(h + 1) * QB, :]
                                                      + pv[h * QB:(h + 1) * QB, :])
        return c * KEY_TILE

    def unrolled_steps(first_tile, n_steps):
        def body(j, col_prev):
            for u in range(n_steps):
                col_prev = half_step(first_tile + n_steps * j + u, col_prev, u % 2)
            return col_prev
        return body

    n_tiles = c_diag + 1
    n_long = n_tiles // SEL_UNROLL
    col_last = lax.fori_loop(0, n_long, unrolled_steps(0, SEL_UNROLL), 0)
    col_last = lax.fori_loop(0, (n_tiles - n_long * SEL_UNROLL + 1) // 2,
                             unrolled_steps(n_long * SEL_UNROLL, 2), col_last)
    acc_s = (acc_scr[...] + weighted_values(p_slots[0][...], col_last)).reshape(HEADS_PER_GROUP, QB, LANES)
    o_s = acc_s[:, :, :HEAD_DIM] / acc_s[:, :, HEAD_DIM:HEAD_DIM + 1]

    kw = kw_ref[0, 0, :, pl.ds(q0, WINDOW + QB)]
    vw = vw_ref[0, 0, pl.ds(q0, WINDOW + QB), :]
    wcol = lax.broadcasted_iota(jnp.int32, (1, WINDOW + QB), 1)
    neg_left = jnp.where(q0 + wcol >= WINDOW, 0.0, NEG_INF)
    o_w = []
    for h0 in range(0, HEADS_PER_GROUP, WIN_HEADS):
        s_w = jnp.dot(q4[h0 * QB:(h0 + WIN_HEADS) * QB], kw, preferred_element_type=jnp.float32)
        s_w = s_w.reshape(WIN_HEADS, QB, WINDOW + QB) + win_ref[0, h0:h0 + WIN_HEADS] + neg_left
        p_w = jnp.exp2(s_w - jnp.max(s_w, axis=-1, keepdims=True))
        pv_w = jnp.dot(p_w.reshape(WIN_HEADS * QB, WINDOW + QB).astype(jnp.bfloat16), vw,
                       preferred_element_type=jnp.float32).reshape(WIN_HEADS, QB, LANES)
        o_w.extend(pv_w[h, :, :HEAD_DIM] / pv_w[h, :, HEAD_DIM:HEAD_DIM + 1] for h in range(WIN_HEADS))

    outs = []
    for h in range(HEADS_PER_GROUP):
        g_s = gate_ref[0, 0, :, 3 * h + 1:3 * h + 2]
        g_w = gate_ref[0, 0, :, 3 * h + 2:3 * h + 3]
        outs.append(g_s * o_s[h] + g_w * o_w[h])
    out_ref[0] = oc_ref[0] + jnp.concatenate(outs, axis=-1)


def _win_sel(q, gates, selneg, oc, ks_aug, vs, kw_pad, vw_pad, toep, win):
    B, S, _ = q.shape
    gw = HEADS_PER_GROUP * HEAD_DIM
    grid = (B, N_KV_GROUPS, S // QB)
    return pl.pallas_call(
        _win_sel_body,
        grid=grid,
        in_specs=[
            pl.BlockSpec((1, QB, gw), lambda b, g, i: (b, i, g)),
            pl.BlockSpec((1, 1, QB, 3 * HEADS_PER_GROUP), lambda b, g, i: (b, g, i, 0)),
            pl.BlockSpec((1, 1, QB, NBLK), lambda b, g, i: (b, g, i, 0)),
            pl.BlockSpec((1, QB, gw), lambda b, g, i: (b, i, g)),
            pl.BlockSpec((1, 1, NBLK + HEAD_DIM, ks_aug.shape[3]), lambda b, g, i: (b, g, 0, 0)),
            pl.BlockSpec((1, 1, vs.shape[2], LANES), lambda b, g, i: (b, g, 0, 0)),
            pl.BlockSpec((1, 1, HEAD_DIM, S + WINDOW), lambda b, g, i: (b, g, 0, 0)),
            pl.BlockSpec((1, 1, S + WINDOW, LANES), lambda b, g, i: (b, g, 0, 0)),
            pl.BlockSpec((1, N_TOEP, HEADS_PER_GROUP, QB, QB), lambda b, g, i: (g, 0, 0, 0, 0)),
            pl.BlockSpec((1, HEADS_PER_GROUP, QB, WINDOW + QB), lambda b, g, i: (g, 0, 0, 0)),
        ],
        out_specs=pl.BlockSpec((1, QB, gw), lambda b, g, i: (b, i, g)),
        out_shape=jax.ShapeDtypeStruct((B, S, N_HEADS * HEAD_DIM), jnp.float32),
        scratch_shapes=[
            pltpu.VMEM((HEADS_PER_GROUP, QB, KEY_TILE), jnp.float32),
            pltpu.VMEM((HEADS_PER_GROUP, QB, KEY_TILE), jnp.float32),
            pltpu.VMEM((QROWS, KEY_TILE), jnp.bfloat16),
            pltpu.VMEM((QROWS, KEY_TILE), jnp.bfloat16),
            pltpu.VMEM((HEADS_PER_GROUP, QB, LANES), jnp.float32),
            pltpu.VMEM((QROWS, LANES), jnp.float32),
        ],
        compiler_params=pltpu.CompilerParams(
            dimension_semantics=("arbitrary", "arbitrary", "arbitrary"),
            vmem_limit_bytes=V7X_VMEM_LIMIT_BYTES,
        ),
        name="nsa_win_sel",
    )(q, gates, selneg, oc, ks_aug, vs, kw_pad, vw_pad, toep, win)


def _nsa_attention(q, gates, kc, vc, kvsw, rel_bias):
    B, S, _ = q.shape
    assert S % KEY_TILE == 0 and S // L_SLC <= NBLK
    bf = jnp.bfloat16
    n_far = S // D_STRIDE
    toep, win, band, far = _bias_tables(rel_bias)
    cpad = ((0, 0), (0, 0), (CMP_PAD, n_far - kc.shape[2] + CMP_PAD), (0, 0))
    kc_pad = jnp.pad(kc, cpad).astype(bf)
    vc_pad = jnp.pad(vc, cpad).astype(bf)
    wmap = jnp.asarray(np.pad(_slc_map_matrix(n_far), ((CMP_PAD, CMP_PAD), (0, 0))), bf)
    oc, selneg = _cmp_select(q, gates, kc_pad, vc_pad, band, far, wmap)
    ks, vs, kw, vw = (kvsw.reshape(B, S, 4, N_KV_GROUPS, HEAD_DIM)[:, :, n] for n in range(4))
    blk_onehot = jnp.asarray((np.arange(S)[None, :] // L_SLC == np.arange(NBLK)[:, None]), bf)
    ks_aug = jnp.concatenate(
        [jnp.broadcast_to(blk_onehot, (B, N_KV_GROUPS, NBLK, S)), ks.transpose(0, 2, 3, 1)], axis=2)
    tail = 2 * KEY_TILE
    ks_aug = jnp.pad(ks_aug, ((0, 0), (0, 0), (0, 0), (0, tail)))
    kw_pad = jnp.pad(kw.transpose(0, 2, 3, 1), ((0, 0), (0, 0), (0, 0), (WINDOW, 0)))

    def with_ones(v):
        v = v.transpose(0, 2, 1, 3)
        ones = jnp.ones(v.shape[:-1] + (1,), v.dtype)
        return jnp.pad(jnp.concatenate([v, ones], axis=-1), ((0, 0),) * 3 + ((0, LANES - HEAD_DIM - 1),))

    vw_aug = jnp.pad(with_ones(vw), ((0, 0), (0, 0), (WINDOW, 0), (0, 0)))
    vs_aug = jnp.pad(with_ones(vs), ((0, 0), (0, 0), (0, tail), (0, 0)))
    gates_g = gates[:, :, :3 * N_HEADS].reshape(B, S, N_KV_GROUPS, 3 * HEADS_PER_GROUP).swapaxes(1, 2)
    return _win_sel(q, gates_g, selneg, oc, ks_aug, vs_aug, kw_pad, vw_aug, toep, win)


def _pool_layer_body(x_ref, halo_ref, gain_ref, shift_ref, scale_ref, gate_ref, pscale_ref, w_ref, o_ref):
    i = pl.program_id(1)
    x = x_ref[0]
    gain, shift, scale = gain_ref[...], shift_ref[0], scale_ref[0]
    h = _norm_modulate(x, gain, shift, scale)
    halo = _norm_modulate(halo_ref[0], gain, shift, scale) * (i > 0).astype(jnp.float32)
    hx = jnp.concatenate([halo, h], axis=0)
    t = i * POOL_TILE + lax.broadcasted_iota(jnp.int32, (POOL_TILE, 1), 0)
    outs = []
    run = hx
    width = 1
    for g, w in enumerate(POOL_WINDOWS):
        while width < w:
            run = run + pltpu.roll(run, width, axis=0)
            width *= 2
        lanes = slice(0, POOL_GROUP_DIM)
        cnt = jnp.minimum(t + 1, w).astype(jnp.float32)
        mix = run[POOL_HALO:, lanes] / cnt - h[:, g * POOL_GROUP_DIM:(g + 1) * POOL_GROUP_DIM]
        outs.append(jnp.dot(mix.astype(jnp.bfloat16), w_ref[g], preferred_element_type=jnp.float32))
        run = run[:, POOL_GROUP_DIM:]
    y = jnp.concatenate(outs, axis=-1) * pscale_ref[...]
    o_ref[0] = x + gate_ref[0] * y


def _pool_layer(x, gain, shift, scale, gate1, w_grp, pool_scale):
    B, S, D = x.shape
    assert S % POOL_TILE == 0 and POOL_HALO >= max(POOL_WINDOWS) - 1
    per_batch = pl.BlockSpec((1, 1, D), lambda b, i: (b, 0, 0))
    halo_blocks = POOL_TILE // POOL_HALO
    return pl.pallas_call(
        _pool_layer_body,
        grid=(B, S // POOL_TILE),
        in_specs=[
            pl.BlockSpec((1, POOL_TILE, D), lambda b, i: (b, i, 0)),
            pl.BlockSpec((1, POOL_HALO, D), lambda b, i: (b, jnp.maximum(i * halo_blocks - 1, 0), 0)),
            pl.BlockSpec((1, D), lambda b, i: (0, 0)),
            per_batch, per_batch, per_batch,
            pl.BlockSpec((1, D), lambda b, i: (0, 0)),
            pl.BlockSpec((N_POOL_GROUPS, POOL_GROUP_DIM, POOL_GROUP_DIM), lambda b, i: (0, 0, 0)),
        ],
        out_specs=pl.BlockSpec((1, POOL_TILE, D), lambda b, i: (b, i, 0)),
        out_shape=jax.ShapeDtypeStruct((B, S, D), jnp.float32),
        compiler_params=pltpu.CompilerParams(
            dimension_semantics=("arbitrary", "arbitrary"),
            vmem_limit_bytes=V7X_VMEM_LIMIT_BYTES,
        ),
        name="pool_layer",
    )(x, x, gain.reshape(1, D), shift[:, None, :], scale[:, None, :], gate1[:, None, :],
      pool_scale.reshape(1, D), w_grp.astype(jnp.bfloat16))


def _head_norm(v, seg_ref, gain):
    sq = v * v
    hi = sq.astype(jnp.bfloat16)
    lo = (sq - hi.astype(jnp.float32)).astype(jnp.bfloat16)
    ss = (jnp.dot(hi, seg_ref[...], preferred_element_type=jnp.float32)
          + jnp.dot(lo, seg_ref[...], preferred_element_type=jnp.float32))
    return v * lax.rsqrt(ss * (1.0 / HEAD_DIM) + EPS) * gain


def _qkv_proj_body(x_ref, gq_ref, shq_ref, scq_ref, gkv_ref, shkv_ref, sckv_ref, wq_ref, wg_ref, wkv_ref,
                   seg_ref, qgain_ref, kgain_ref, q_ref, gates_ref, kvc_ref, kvsw_ref):
    x = x_ref[...]
    xhat = x * lax.rsqrt(jnp.mean(x * x, axis=-1, keepdims=True) + EPS)
    hq = ((xhat * gq_ref[...]) * (1.0 + scq_ref[0]) + shq_ref[0]).astype(jnp.bfloat16)
    hkv = ((xhat * gkv_ref[...]) * (1.0 + sckv_ref[0]) + shkv_ref[0]).astype(jnp.bfloat16)
    pq = jnp.dot(hq, wq_ref[...], preferred_element_type=jnp.float32)
    gates_ref[...] = jax.nn.sigmoid(jnp.dot(hq, wg_ref[...], preferred_element_type=jnp.float32))
    pkv = jnp.dot(hkv, wkv_ref[...], preferred_element_type=jnp.float32)
    for g in range(N_KV_GROUPS):
        lanes = slice(g * GROUP_W, (g + 1) * GROUP_W)
        q_ref[:, lanes] = _head_norm(pq[:, lanes], seg_ref, qgain_ref[...]).astype(jnp.bfloat16)
    kvc_ref[...] = pkv[:, :2 * GROUP_W]
    k_s = _head_norm(pkv[:, 2 * GROUP_W:3 * GROUP_W], seg_ref, kgain_ref[0:1, :])
    k_w = _head_norm(pkv[:, 4 * GROUP_W:5 * GROUP_W], seg_ref, kgain_ref[1:2, :])
    kvsw_ref[...] = jnp.concatenate([k_s, pkv[:, 3 * GROUP_W:4 * GROUP_W], k_w, pkv[:, 5 * GROUP_W:]],
                                    axis=-1).astype(jnp.bfloat16)


def _qkv_proj(x, gain_q, shift_q, scale_q, gain_kv, shift_kv, scale_kv, w_qg, kv_w, q_gain, k_gain):
    B, S, D = x.shape
    T = B * S
    qd = N_HEADS * HEAD_DIM
    bf = jnp.bfloat16
    tiles_per_batch = S // ROW_TILE
    per_batch = pl.BlockSpec((1, 1, D), lambda i: (i // tiles_per_batch, 0, 0))
    const = lambda shape: pl.BlockSpec(shape, lambda i: (0,) * len(shape))
    seg = jnp.asarray(np.kron(np.eye(HEADS_PER_GROUP), np.ones((HEAD_DIM, HEAD_DIM))), bf)
    w_gate = jnp.pad(w_qg[:, qd:], ((0, 0), (0, LANES - 3 * N_HEADS))).astype(bf)
    qgain = jnp.tile(q_gain * HEAD_DIM ** -0.5, HEADS_PER_GROUP).reshape(1, GROUP_W)
    kgain = jnp.stack([jnp.tile(k_gain[1] * LOG2_E, N_KV_GROUPS), jnp.tile(k_gain[2] * LOG2_E, N_KV_GROUPS)])
    rows = lambda width: pl.BlockSpec((ROW_TILE, width), lambda i: (i, 0))
    return pl.pallas_call(
        _qkv_proj_body,
        grid=(T // ROW_TILE,),
        in_specs=[rows(D), const((1, D)), per_batch, per_batch, const((1, D)), per_batch, per_batch,
                  const((D, qd)), const((D, LANES)), const((D, 6 * GROUP_W)), const((GROUP_W, GROUP_W)),
                  const((1, GROUP_W)), const((2, GROUP_W))],
        out_specs=[rows(qd), rows(LANES), rows(2 * GROUP_W), rows(4 * GROUP_W)],
        out_shape=[jax.ShapeDtypeStruct((T, qd), bf), jax.ShapeDtypeStruct((T, LANES), jnp.float32),
                   jax.ShapeDtypeStruct((T, 2 * GROUP_W), jnp.float32),
                   jax.ShapeDtypeStruct((T, 4 * GROUP_W), bf)],
        compiler_params=pltpu.CompilerParams(
            dimension_semantics=("arbitrary",), vmem_limit_bytes=V7X_VMEM_LIMIT_BYTES),
        name="qkv_proj",
    )(x.reshape(T, D), gain_q.reshape(1, D), shift_q[:, None, :], scale_q[:, None, :],
      gain_kv.reshape(1, D), shift_kv[:, None, :], scale_kv[:, None, :],
      w_qg[:, :qd].astype(bf), w_gate, kv_w.astype(bf), seg, qgain, kgain)


def _compress_body(r_ref, pe_ref, w1_ref, w2_ref, gain_ref, o_ref):
    half = D_STRIDE * HEAD_DIM
    r = r_ref[0, 0, 0]
    top = jnp.dot(r, w1_ref[0, :half, :], preferred_element_type=jnp.float32)
    bot = jnp.dot(r, w1_ref[0, half:, :], preferred_element_type=jnp.float32)
    pe = jnp.dot(jnp.broadcast_to(pe_ref[0], (8, R_CMP * half)).astype(jnp.bfloat16), w1_ref[0],
                 preferred_element_type=jnp.float32)[0:1]
    n_chunks = r.shape[0]
    hidden = top + pltpu.roll(bot, n_chunks - 1, axis=0) + pe
    out = jnp.dot(jax.nn.gelu(hidden).astype(jnp.bfloat16), w2_ref[0], preferred_element_type=jnp.float32)
    normed = out * lax.rsqrt(jnp.mean(out * out, axis=-1, keepdims=True) + EPS) * gain_ref[...]
    o_ref[0, 0, 0] = jnp.where(pl.program_id(0) == 0, normed, out)


def _compressed_kv(kvc, B, S, cmp_pe_k, cmp_pe_v, cmp_k_w1, cmp_k_w2, cmp_v_w1, cmp_v_w2, k_gain):
    bf = jnp.bfloat16
    n_chunks = S // D_STRIDE
    half = D_STRIDE * HEAD_DIM
    hidden = cmp_k_w1.shape[1]
    r = kvc.astype(bf).reshape(B, n_chunks, D_STRIDE, 2, N_KV_GROUPS, HEAD_DIM)
    r = r.transpose(3, 0, 4, 1, 2, 5).reshape(2, B, N_KV_GROUPS, n_chunks, half)
    pe = jnp.stack([cmp_pe_k, cmp_pe_v]).reshape(2, 1, R_CMP * half)
    per_kind = lambda shape: pl.BlockSpec((1,) + shape, lambda kv, b, g: (kv,) + (0,) * len(shape))
    out = pl.pallas_call(
        _compress_body,
        grid=(2, B, N_KV_GROUPS),
        in_specs=[
            pl.BlockSpec((1, 1, 1, n_chunks, half), lambda kv, b, g: (kv, b, g, 0, 0)),
            per_kind((1, R_CMP * half)), per_kind((R_CMP * half, hidden)), per_kind((hidden, HEAD_DIM)),
            pl.BlockSpec((1, HEAD_DIM), lambda kv, b, g: (0, 0)),
        ],
        out_specs=pl.BlockSpec((1, 1, 1, n_chunks, HEAD_DIM), lambda kv, b, g: (kv, b, g, 0, 0)),
        out_shape=jax.ShapeDtypeStruct((2, B, N_KV_GROUPS, n_chunks, HEAD_DIM), jnp.float32),
        compiler_params=pltpu.CompilerParams(
            dimension_semantics=("arbitrary", "arbitrary", "arbitrary"),
            vmem_limit_bytes=V7X_VMEM_LIMIT_BYTES),
        name="cmp_mlp",
    )(r, pe, jnp.stack([cmp_k_w1, cmp_v_w1]).astype(bf), jnp.stack([cmp_k_w2, cmp_v_w2]).astype(bf),
      (k_gain[0] * LOG2_E).reshape(1, HEAD_DIM))
    n_cmp = n_chunks - R_CMP + 1
    return out[0, :, :, :n_cmp], out[1, :, :, :n_cmp]


def _oproj_router_body(a_ref, x_ref, gate_ref, wo_ref, gain_ref, shift_ref, scale_ref, wr_ref, br_ref,
                       o_ref, logit_ref):
    mix = jnp.dot(a_ref[...].astype(jnp.bfloat16), wo_ref[...], preferred_element_type=jnp.float32)
    x = x_ref[...] + gate_ref[0] * mix
    o_ref[...] = x
    h = _norm_modulate(x, gain_ref[...], shift_ref[0], scale_ref[0]).astype(jnp.bfloat16)
    logit_ref[...] = jnp.dot(h, wr_ref[...], preferred_element_type=jnp.float32) + br_ref[...]


def _oproj_router(attn, x, gate1, w_o, gain, shift, scale, w_router, b_router):
    B, S, D = x.shape
    T = B * S
    bf = jnp.bfloat16
    tiles_per_batch = S // ROW_TILE
    per_batch = pl.BlockSpec((1, 1, D), lambda i: (i // tiles_per_batch, 0, 0))
    const = lambda shape: pl.BlockSpec(shape, lambda i: (0,) * len(shape))
    rows = lambda width: pl.BlockSpec((ROW_TILE, width), lambda i: (i, 0))
    pad = ((0, 0), (0, LANES - N_EXPERTS))
    out, logits = pl.pallas_call(
        _oproj_router_body,
        grid=(T // ROW_TILE,),
        in_specs=[rows(D), rows(D), per_batch, const((D, D)), const((1, D)), per_batch, per_batch,
                  const((D, LANES)), const((1, LANES))],
        out_specs=[rows(D), rows(LANES)],
        out_shape=[jax.ShapeDtypeStruct((T, D), jnp.float32), jax.ShapeDtypeStruct((T, LANES), jnp.float32)],
        compiler_params=pltpu.CompilerParams(
            dimension_semantics=("arbitrary",), vmem_limit_bytes=V7X_VMEM_LIMIT_BYTES),
        name="oproj_router",
    )(attn.reshape(T, D), x.reshape(T, D), gate1[:, None, :], w_o.astype(bf), gain.reshape(1, D),
      shift[:, None, :], scale[:, None, :], jnp.pad(w_router, pad).astype(bf),
      jnp.pad(b_router.reshape(1, N_EXPERTS), pad))
    return out.reshape(B, S, D), logits[:, :N_EXPERTS]


def _route(logits):
    T = logits.shape[0]
    lane = jnp.arange(N_EXPERTS)[None, :]
    l0 = jnp.max(logits, axis=-1)
    e0 = jnp.argmax(logits, axis=-1)
    rest = jnp.where(lane == e0[:, None], -jnp.inf, logits)
    l1 = jnp.max(rest, axis=-1)
    e1 = jnp.argmax(rest, axis=-1)
    z = jnp.exp(l1 - l0)
    top_w = jnp.stack([1.0 / (1.0 + z), z / (1.0 + z)], axis=-1)
    oh = [(lane == e[:, None]).astype(jnp.int32) for e in (e0, e1)]
    cnt = oh[0] + oh[1]
    before = jnp.cumsum(cnt, axis=0) - cnt
    counts = before[-1] + cnt[-1]
    pcounts = (counts + ROW_TILE - 1) // ROW_TILE * ROW_TILE
    pends = jnp.cumsum(pcounts)
    pstarts = pends - pcounts
    dest = jnp.stack([jnp.sum((pstarts[None, :] + before) * o, axis=-1) for o in oh], axis=-1)
    n_blocks = -(-(T * TOP_K + N_EXPERTS * (ROW_TILE - 1)) // ROW_TILE)
    blk_start = jnp.arange(n_blocks)[:, None] * ROW_TILE
    blk_e = jnp.minimum(jnp.sum(pends[None, :] <= blk_start, axis=-1), N_EXPERTS - 1).astype(jnp.int32)
    n_used = (pends[-1:] // ROW_TILE).astype(jnp.int32)
    return top_w, dest.astype(jnp.int32), blk_e, n_used, n_blocks * ROW_TILE


def _moe_ffn(x, logits, gain, shift, scale, gate2, w_gu, w_dn):
    top_w, dest, blk_e, n_used, n_rows = _route(logits)
    rows = _moe_dispatch(x, gain, shift, scale, dest, n_rows)
    y = _grouped_swiglu(rows, blk_e, n_used, w_gu.astype(jnp.bfloat16), w_dn.astype(jnp.bfloat16),
                        ff_tile=FF_TILE_EXPERT)
    return _moe_combine(x, top_w, gate2, y, dest)


def kernel(x, c, ada_w, ada_b, norm_g, pool_w, pool_scale, q_w, q_gain, o_w, kv_ada_w, kv_ada_b, kv_norm_g, kv_w, cmp_pe_k, cmp_pe_v, cmp_k_w1, cmp_k_w2, cmp_v_w1, cmp_v_w2, k_gain, rel_bias, ffn_gu, ffn_dn, router_w, router_b, exp_gu, exp_dn):
    assert DEPTH == 2 and N_A_LAYERS == 1
    B, S, D = x.shape
    silu_c = jax.nn.silu(c)
    sh1, sc1, g1, sh2, sc2, g2 = jnp.split(silu_c @ ada_w[0] + ada_b[0], 6, axis=-1)
    x = _pool_layer(x, norm_g[0, 0], sh1, sc1, g1, pool_w[0], pool_scale[0])
    x = _dense_ffn(x, norm_g[0, 1], sh2, sc2, g2, ffn_gu[0], ffn_dn[0])
    sh1, sc1, g1, sh2, sc2, g2 = jnp.split(silu_c @ ada_w[1] + ada_b[1], 6, axis=-1)
    sh_kv, sc_kv = jnp.split(silu_c @ kv_ada_w + kv_ada_b, 2, axis=-1)
    q, gates, kvc, kvsw = _qkv_proj(x, norm_g[1, 0], sh1, sc1, kv_norm_g, sh_kv, sc_kv, q_w[0], kv_w,
                                    q_gain[0], k_gain)
    kc, vc = _compressed_kv(kvc, B, S, cmp_pe_k, cmp_pe_v, cmp_k_w1, cmp_k_w2, cmp_v_w1, cmp_v_w2, k_gain)
    attn = _nsa_attention(q.reshape(B, S, -1), gates.reshape(B, S, -1), kc, vc, kvsw.reshape(B, S, -1),
                          rel_bias)
    x, logits = _oproj_router(attn, x, g1, o_w[0], norm_g[1, 1], sh2, sc2, router_w[0], router_b[0])
    return _moe_ffn(x, logits, norm_g[1, 1], sh2, sc2, g2, exp_gu[0], exp_dn[0])
```

```python
import functools
import math

import jax
import jax.numpy as jnp
import numpy as np
from jax import lax
from jax.experimental import pallas as pl
from jax.experimental.pallas import tpu as pltpu

D_MODEL = 1024
DEPTH = 2
N_A_LAYERS = DEPTH // 2
POOL_WINDOWS = (2, 4, 8, 16)
N_POOL_GROUPS = len(POOL_WINDOWS)
POOL_GROUP_DIM = D_MODEL // N_POOL_GROUPS
HEAD_DIM = 64
N_HEADS = D_MODEL // HEAD_DIM
N_KV_GROUPS = 4
HEADS_PER_GROUP = N_HEADS // N_KV_GROUPS
L_CMP = 32
D_STRIDE = 16
L_SLC = 64
N_SEL = 16
WINDOW = 512
R_CMP = L_CMP // D_STRIDE
R_SLC = L_SLC // D_STRIDE
N_BUCKETS = 32
REL_EXACT = N_BUCKETS // 2
MAX_DISTANCE = 1024
N_EXPERTS = 8
TOP_K = 2
EPS = 1e-6
NEG_INF = -1e30
SEL_FORCE = 1e6
LOG2_E = math.log2(math.e)

V7X_VMEM_LIMIT_BYTES = 48 * 1024 * 1024
LANES = 128
ROW_TILE = 512
TOK_TILE = 512
DMA_ISSUE_UNROLL = 8
MAX_FORCED = 3
FF_TILE_DENSE = 2816
FF_TILE_EXPERT = 1792
POOL_TILE = 512
POOL_HALO = 16

QB = 128
KEY_TILE = 2 * QB
SEL_UNROLLS = (16, 8, 2)
QROWS = HEADS_PER_GROUP * QB
GROUP_W = HEADS_PER_GROUP * HEAD_DIM
NBLK = 128
CMP_PAD = 128
FAR_DIST = MAX_DISTANCE
N_TOEP = FAR_DIST // QB + 3
BAND_LEFT = CMP_PAD - 16
assert BAND_LEFT * D_STRIDE + (L_CMP - 1) - 2 * QB >= FAR_DIST


def _rms_norm(x, g):
    xf = x.astype(jnp.float32)
    y = xf * lax.rsqrt(jnp.mean(xf * xf, axis=-1, keepdims=True) + EPS)
    return (y * g.astype(jnp.float32)).astype(x.dtype)


def _modulate(h, shift, scale):
    return h * (1 + scale[:, None, :]) + shift[:, None, :]


def _rel_bucket(dist):
    d = jnp.maximum(dist, 0)
    ratio = jnp.maximum(d, REL_EXACT).astype(jnp.float32) / REL_EXACT
    large = REL_EXACT + (jnp.log(ratio) / math.log(MAX_DISTANCE / REL_EXACT)
                         * (N_BUCKETS - REL_EXACT)).astype(jnp.int32)
    return jnp.where(d < REL_EXACT, d, jnp.minimum(large, N_BUCKETS - 1))


def _norm_modulate(x, gain, shift, scale):
    y = x * lax.rsqrt(jnp.mean(x * x, axis=-1, keepdims=True) + EPS)
    return (y * gain) * (1.0 + scale) + shift


def _swiglu_step(xb_ref, wg_ref, wu_ref, wd_ref, acc_ref, j):
    @pl.when(j == 0)
    def _():
        acc_ref[...] = jnp.zeros_like(acc_ref)

    x = xb_ref[...]
    gate = jnp.dot(x, wg_ref[0], preferred_element_type=jnp.float32)
    up = jnp.dot(x, wu_ref[0], preferred_element_type=jnp.float32)
    act = (gate * jax.nn.sigmoid(gate) * up).astype(jnp.bfloat16)
    acc_ref[...] += jnp.dot(act, wd_ref[0], preferred_element_type=jnp.float32)


def _grouped_swiglu_body(blk_e_ref, n_used_ref, x_ref, wg_ref, wu_ref, wd_ref, o_ref, acc_ref, xb_ref, *,
                         n_ff_steps):
    del blk_e_ref
    i, j = pl.program_id(0), pl.program_id(1)
    used = i < n_used_ref[0]

    @pl.when(used & (j == 0))
    def _():
        xb_ref[...] = x_ref[...].astype(jnp.bfloat16)

    @pl.when(used)
    def _():
        _swiglu_step(xb_ref, wg_ref, wu_ref, wd_ref, acc_ref, j)

    @pl.when(j == n_ff_steps - 1)
    def _():
        o_ref[...] = jnp.where(used, acc_ref[...], 0.0)


def _grouped_swiglu(x_rows, blk_e, n_used, w_gu, w_dn, *, ff_tile):
    n_rows, d = x_rows.shape
    d_ff = w_dn.shape[1]
    assert n_rows % ROW_TILE == 0 and d_ff % ff_tile == 0
    n_ff_steps = d_ff // ff_tile
    grid = (n_rows // ROW_TILE, n_ff_steps)
    return pl.pallas_call(
        functools.partial(_grouped_swiglu_body, n_ff_steps=n_ff_steps),
        grid_spec=pltpu.PrefetchScalarGridSpec(
            num_scalar_prefetch=2,
            grid=grid,
            in_specs=[
                pl.BlockSpec((ROW_TILE, d), lambda i, j, e, n: (i, 0)),
                pl.BlockSpec((1, d, ff_tile), lambda i, j, e, n: (e[i], 0, j)),
                pl.BlockSpec((1, d, ff_tile), lambda i, j, e, n: (e[i], 0, j + n_ff_steps)),
                pl.BlockSpec((1, ff_tile, d), lambda i, j, e, n: (e[i], j, 0)),
            ],
            out_specs=pl.BlockSpec((ROW_TILE, d), lambda i, j, e, n: (i, 0)),
            scratch_shapes=[pltpu.VMEM((ROW_TILE, d), jnp.float32), pltpu.VMEM((ROW_TILE, d), jnp.bfloat16)],
        ),
        out_shape=jax.ShapeDtypeStruct((n_rows, d), jnp.float32),
        compiler_params=pltpu.CompilerParams(
            dimension_semantics=("arbitrary", "arbitrary"),
            vmem_limit_bytes=V7X_VMEM_LIMIT_BYTES,
        ),
        name="grouped_swiglu",
    )(blk_e, n_used, x_rows, w_gu, w_gu, w_dn)


def _dense_ffn_body(x_ref, gain_ref, shift_ref, scale_ref, gate2_ref, wg_ref, wu_ref, wd_ref, o_ref,
                    acc_ref, xb_ref, *, n_ff_steps):
    j = pl.program_id(1)

    @pl.when(j == 0)
    def _():
        xb_ref[...] = _norm_modulate(x_ref[...], gain_ref[...], shift_ref[0], scale_ref[0]).astype(jnp.bfloat16)

    _swiglu_step(xb_ref, wg_ref, wu_ref, wd_ref, acc_ref, j)

    @pl.when(j == n_ff_steps - 1)
    def _():
        o_ref[...] = x_ref[...] + gate2_ref[0] * acc_ref[...]


def _dense_ffn(x, gain, shift, scale, gate2, w_gu, w_dn):
    B, S, D = x.shape
    d_ff = w_dn.shape[0]
    n_ff_steps = d_ff // FF_TILE_DENSE
    tiles_per_batch = S // ROW_TILE
    per_batch = pl.BlockSpec((1, 1, D), lambda i, j: (i // tiles_per_batch, 0, 0))
    weight_buffers = pl.Buffered(1) if n_ff_steps == 1 else None
    out = pl.pallas_call(
        functools.partial(_dense_ffn_body, n_ff_steps=n_ff_steps),
        grid=(B * S // ROW_TILE, n_ff_steps),
        in_specs=[
            pl.BlockSpec((ROW_TILE, D), lambda i, j: (i, 0)),
            pl.BlockSpec((1, D), lambda i, j: (0, 0)),
            per_batch, per_batch, per_batch,
            pl.BlockSpec((1, D, FF_TILE_DENSE), lambda i, j: (0, 0, j), pipeline_mode=weight_buffers),
            pl.BlockSpec((1, D, FF_TILE_DENSE), lambda i, j: (0, 0, j + n_ff_steps), pipeline_mode=weight_buffers),
            pl.BlockSpec((1, FF_TILE_DENSE, D), lambda i, j: (0, j, 0), pipeline_mode=weight_buffers),
        ],
        out_specs=pl.BlockSpec((ROW_TILE, D), lambda i, j: (i, 0)),
        out_shape=jax.ShapeDtypeStruct((B * S, D), jnp.float32),
        scratch_shapes=[pltpu.VMEM((ROW_TILE, D), jnp.float32), pltpu.VMEM((ROW_TILE, D), jnp.bfloat16)],
        compiler_params=pltpu.CompilerParams(
            dimension_semantics=("arbitrary", "arbitrary"),
            vmem_limit_bytes=V7X_VMEM_LIMIT_BYTES,
        ),
        name="dense_ffn",
    )(x.reshape(B * S, D), gain.reshape(1, D), shift[:, None, :], scale[:, None, :], gate2[:, None, :],
      w_gu.astype(jnp.bfloat16)[None], w_gu.astype(jnp.bfloat16)[None], w_dn.astype(jnp.bfloat16)[None])
    return out.reshape(B, S, D)


def _row_copies_wait(src_ref, dst_ref, sem, n_rows):
    pltpu.make_async_copy(src_ref.at[pl.ds(0, n_rows)], dst_ref.at[pl.ds(0, n_rows)], sem).wait()


def _dispatch_body(dest_ref, x_ref, gain_ref, shift_ref, scale_ref, rows_in_ref, rows_ref, h_ref, sem):
    del rows_in_ref
    h_ref[...] = _norm_modulate(x_ref[...], gain_ref[...], shift_ref[0], scale_ref[0])

    def issue(r, carry):
        for k in range(TOP_K):
            pltpu.make_async_copy(h_ref.at[pl.ds(r, 1)], rows_ref.at[pl.ds(dest_ref[TOP_K * r + k], 1)],
                                  sem).start()
        return carry

    lax.fori_loop(0, TOK_TILE, issue, 0, unroll=DMA_ISSUE_UNROLL)
    for k in range(TOP_K):
        _row_copies_wait(h_ref, rows_ref, sem, TOK_TILE)


def _moe_dispatch(x, gain, shift, scale, dest, n_rows):
    B, S, D = x.shape
    T = B * S
    tiles_per_batch = S // TOK_TILE
    per_batch = pl.BlockSpec((1, 1, D), lambda i: (i // tiles_per_batch, 0, 0))
    return pl.pallas_call(
        _dispatch_body,
        grid=(T // TOK_TILE,),
        in_specs=[
            pl.BlockSpec((TOP_K * TOK_TILE,), lambda i: (i,), memory_space=pltpu.SMEM),
            pl.BlockSpec((TOK_TILE, D), lambda i: (i, 0)),
            pl.BlockSpec((1, D), lambda i: (0, 0)),
            per_batch, per_batch,
            pl.BlockSpec(memory_space=pl.ANY),
        ],
        out_specs=pl.BlockSpec(memory_space=pl.ANY),
        out_shape=jax.ShapeDtypeStruct((n_rows, D), jnp.float32),
        scratch_shapes=[pltpu.VMEM((TOK_TILE, D), jnp.float32), pltpu.SemaphoreType.DMA(())],
        input_output_aliases={5: 0},
        compiler_params=pltpu.CompilerParams(dimension_semantics=("arbitrary",)),
        name="moe_dispatch",
    )(dest.reshape(T * TOP_K), x.reshape(T, D), gain.reshape(1, D), shift[:, None, :], scale[:, None, :],
      jnp.zeros((n_rows, D), jnp.float32))


def _combine_body(dest_ref, x_ref, w_ref, gate2_ref, y_ref, o_ref, buf_ref, sem):
    def issue(r, carry):
        for k in range(TOP_K):
            pltpu.make_async_copy(y_ref.at[pl.ds(dest_ref[TOP_K * r + k], 1)], buf_ref.at[k, pl.ds(r, 1)],
                                  sem).start()
        return carry

    lax.fori_loop(0, TOK_TILE, issue, 0, unroll=DMA_ISSUE_UNROLL)
    for k in range(TOP_K):
        _row_copies_wait(y_ref, buf_ref.at[k], sem, TOK_TILE)
    w = w_ref[...]
    f = w[:, 0:1] * buf_ref[0] + w[:, 1:2] * buf_ref[1]
    o_ref[...] = x_ref[...] + gate2_ref[0] * f


def _moe_combine(x, top_w, gate2, y, dest):
    B, S, D = x.shape
    T = B * S
    tiles_per_batch = S // TOK_TILE
    out = pl.pallas_call(
        _combine_body,
        grid=(T // TOK_TILE,),
        in_specs=[
            pl.BlockSpec((TOP_K * TOK_TILE,), lambda i: (i,), memory_space=pltpu.SMEM),
            pl.BlockSpec((TOK_TILE, D), lambda i: (i, 0)),
            pl.BlockSpec((TOK_TILE, TOP_K), lambda i: (i, 0)),
            pl.BlockSpec((1, 1, D), lambda i: (i // tiles_per_batch, 0, 0)),
            pl.BlockSpec(memory_space=pl.ANY),
        ],
        out_specs=pl.BlockSpec((TOK_TILE, D), lambda i: (i, 0)),
        out_shape=jax.ShapeDtypeStruct((T, D), jnp.float32),
        scratch_shapes=[pltpu.VMEM((TOP_K, TOK_TILE, D), jnp.float32), pltpu.SemaphoreType.DMA(())],
        compiler_params=pltpu.CompilerParams(dimension_semantics=("arbitrary",)),
        name="moe_combine",
    )(dest.reshape(T * TOP_K), x.reshape(T, D), top_w, gate2[:, None, :], y)
    return out.reshape(B, S, D)


def _bias_tables(rel_bias):
    x0 = (N_TOEP - 1) * QB
    width = x0 + 2 * QB
    period = width + QB
    n = np.arange(period)
    n = np.where(n < width, n, n - period)
    by_dist = rel_bias.astype(jnp.float32)[_rel_bucket(jnp.asarray(np.maximum(x0 - n, 0)))].T
    strip = jnp.tile(by_dist, (1, QB))[:, :QB * (period - 1)].reshape(N_HEADS, QB, period - 1)[:, :, :width]
    far = rel_bias.astype(jnp.float32)[N_BUCKETS - 1] * LOG2_E

    i = np.arange(QB)[:, None]
    toep = jnp.stack([strip[:, :, x0 - QB * m:x0 - QB * m + QB] for m in range(-1, N_TOEP - 1)], axis=1)
    d_toep = QB * np.arange(-1, N_TOEP - 1)[:, None, None] + i[None] - np.arange(QB)[None, None, :]
    toep = jnp.where(d_toep >= 0, toep * LOG2_E, NEG_INF)
    toep = toep.reshape(N_KV_GROUPS, HEADS_PER_GROUP, N_TOEP, QB, QB).transpose(0, 2, 1, 3, 4)
    d_win = WINDOW + i - np.arange(WINDOW + QB)[None, :]
    win = jnp.where((d_win >= 0) & (d_win < WINDOW), strip[:, :, x0 - WINDOW:x0 + QB] * LOG2_E, NEG_INF)
    win = win.reshape(N_KV_GROUPS, HEADS_PER_GROUP, QB, WINDOW + QB)
    bands = []
    for par in range(2):
        off = x0 - QB * par - D_STRIDE * BAND_LEFT + (L_CMP - 1)
        c_first = -(off // D_STRIDE)
        cols = strip[:, :, D_STRIDE * c_first + off::D_STRIDE][:, :, :128 - c_first] * LOG2_E
        left = jnp.broadcast_to(far[:, None, None], (N_HEADS, QB, c_first))
        right = jnp.zeros((N_HEADS, QB, 128 - c_first - cols.shape[2]), jnp.float32)
        d_band = QB * par + i - D_STRIDE * (np.arange(128)[None, :] - BAND_LEFT) - (L_CMP - 1)
        assert (d_band[:, 128 - right.shape[2]:] < 0).all()
        bands.append(jnp.where(d_band >= 0, jnp.concatenate([left, cols, right], axis=-1), NEG_INF))
    band = jnp.stack(bands).reshape(2, N_KV_GROUPS, HEADS_PER_GROUP, QB, 128)
    return toep, win, band, far


def _slc_map_matrix(n_cmp_cols):
    w = np.zeros((n_cmp_cols, NBLK), np.float32)
    for jb in range(NBLK):
        for mm in range(R_SLC):
            for nn in range(R_CMP):
                k = R_SLC * jb + mm - nn
                if 0 <= k < n_cmp_cols:
                    w[k, jb] += 1.0
    return w


def _dot_nt(a, b):
    return lax.dot_general(a, b, (((1,), (1,)), ((), ())), preferred_element_type=jnp.float32)


def _cmp_select_body(far_ref, q_ref, gate_ref, kc_ref, vc_ref, band_ref, wmap_ref, oc_ref, sel_ref,
                     *, n_far):
    qi = pl.program_id(1)
    par = qi % 2
    band0 = pl.multiple_of(16 * (qi // 2 + 1), 16)
    first_band_blk = band0 - CMP_PAD

    col = lax.broadcasted_iota(jnp.int32, (QB, 128), 1)
    row = lax.broadcasted_iota(jnp.int32, (QB, 128), 0)
    neg_pad = jnp.where(first_band_blk + col >= 0, 0.0, NEG_INF)

    t = qi * QB + row
    cur = t // L_SLC
    forced = (col == 0) | (col == cur) | (col == cur - 1)
    n_forced = 1 + (cur[:, :1] >= 1).astype(jnp.int32) + (cur[:, :1] >= 2).astype(jnp.int32)
    valid = col * L_SLC <= t

    q_all = q_ref[0]
    colf = col.astype(jnp.float32)

    def take_best(score, active=None):
        best = jnp.max(score, axis=-1, keepdims=True)
        first = jnp.min(jnp.where(score == best, colf, float(NBLK)), axis=-1, keepdims=True)
        hit = colf == first
        return jnp.where(hit if active is None else hit & active, -jnp.inf, score)

    def branches(n_cols):
        far_col = lax.broadcasted_iota(jnp.int32, (1, max(n_cols, 1)), 1)
        neg_far = jnp.where(far_col < first_band_blk, 0.0, NEG_INF)

        def far_and_band(ref, *lead):
            band_rows = ref[(*lead, pl.ds(band0, 128), slice(None))]
            if not n_cols:
                return band_rows
            return jnp.concatenate([ref[(*lead, slice(CMP_PAD, CMP_PAD + n_cols), slice(None))], band_rows], axis=0)

        wmap_all = far_and_band(wmap_ref)
        scores, start_scores = [], []
        oc_heads = []
        for g in range(N_KV_GROUPS):
            q4 = jnp.concatenate(
                [q_all[:, (g * HEADS_PER_GROUP + h) * HEAD_DIM:(g * HEADS_PER_GROUP + h + 1) * HEAD_DIM]
                 for h in range(HEADS_PER_GROUP)], axis=0)
            k_all = far_and_band(kc_ref, 0, g)
            v_all = far_and_band(vc_ref, 0, g)
            s_all = _dot_nt(q4, k_all).reshape(HEADS_PER_GROUP, QB, n_cols + 128)
            imp = jnp.zeros((QB, n_cols + 128), jnp.float32)
            for h in range(HEADS_PER_GROUP):
                hh = g * HEADS_PER_GROUP + h
                bias = band_ref[par, g, h] + neg_pad
                if n_cols:
                    bias = jnp.concatenate([jnp.broadcast_to(far_ref[hh] + neg_far, (QB, n_cols)), bias],
                                           axis=-1)
                s = s_all[h] + bias
                m = jnp.maximum(jnp.max(s, axis=-1, keepdims=True), 1e-10 * NEG_INF)
                p = jnp.exp2(s - m)
                l = jnp.sum(p, axis=-1, keepdims=True)
                p = p * jnp.where(l > 0.0, 1.0 / l, 0.0)
                imp = imp + p
                o = jnp.dot(p.astype(jnp.bfloat16), v_all, preferred_element_type=jnp.float32)
                oc_heads.append(o * gate_ref[0, :, 3 * hh:3 * hh + 1])
            p_slc = jnp.zeros((QB, NBLK), jnp.float32)
            rest = imp
            for _ in range(3):
                term = rest.astype(jnp.bfloat16)
                p_slc = p_slc + jnp.dot(term, wmap_all, preferred_element_type=jnp.float32)
                rest = rest - term.astype(jnp.float32)
            score = jnp.where(forced, -jnp.inf, jnp.where(valid, p_slc, -SEL_FORCE - colf))
            start_scores.append(score)
            for _ in range(N_SEL - MAX_FORCED):
                score = jnp.where(score == jnp.max(score, axis=-1, keepdims=True), -jnp.inf, score)
            scores.append(score)
        return (jnp.concatenate(oc_heads, axis=-1),) + tuple(scores) + tuple(start_scores)

    n_variants = n_far // 128
    variant = jnp.clip((first_band_blk + 127) // 128, 0, n_variants - 1)
    res = lax.switch(variant, [functools.partial(branches, 128 * v) for v in range(n_variants)])
    oc_ref[0] = res[0]
    scores, start_scores = res[1:1 + N_KV_GROUPS], res[1 + N_KV_GROUPS:]

    expected = (n_forced + (N_SEL - MAX_FORCED)).astype(jnp.float32)
    excess = [jnp.sum(jnp.where(s == -jnp.inf, 1.0, 0.0), axis=-1, keepdims=True) - expected for s in scores]
    any_tie = jnp.max(sum(excess)) > 0.0

    def exact_passes(start_scores):
        for _ in range(N_SEL - MAX_FORCED):
            start_scores = tuple(take_best(s) for s in start_scores)
        return start_scores

    scores = lax.cond(any_tie, exact_passes, lambda _: tuple(scores), tuple(start_scores))

    def early_rows(scores):
        for extra in range(MAX_FORCED - 1):
            scores = tuple(take_best(s, MAX_FORCED - n_forced > extra) for s in scores)
        return scores

    scores = lax.cond(qi == 0, early_rows, lambda s: s, tuple(scores))
    for g in range(N_KV_GROUPS):
        sel_ref[0, g] = jnp.where(scores[g] == -jnp.inf, 0.0, NEG_INF).astype(jnp.bfloat16)


def _cmp_select(q, gates, kc_pad, vc_pad, band, far, wmap):
    B, S, _ = q.shape
    n_far = S // D_STRIDE
    n_pad = kc_pad.shape[2]
    grid = (B, S // QB)
    return pl.pallas_call(
        functools.partial(_cmp_select_body, n_far=n_far),
        grid=grid,
        in_specs=[
            pl.BlockSpec(memory_space=pltpu.SMEM),
            pl.BlockSpec((1, QB, N_HEADS * HEAD_DIM), lambda b, i: (b, i, 0)),
            pl.BlockSpec((1, QB, LANES), lambda b, i: (b, i, 0)),
            pl.BlockSpec((1, N_KV_GROUPS, n_pad, HEAD_DIM), lambda b, i: (b, 0, 0, 0)),
            pl.BlockSpec((1, N_KV_GROUPS, n_pad, HEAD_DIM), lambda b, i: (b, 0, 0, 0)),
            pl.BlockSpec((2, N_KV_GROUPS, HEADS_PER_GROUP, QB, 128), lambda b, i: (0, 0, 0, 0, 0)),
            pl.BlockSpec((n_pad, NBLK), lambda b, i: (0, 0)),
        ],
        out_specs=[
            pl.BlockSpec((1, QB, N_HEADS * HEAD_DIM), lambda b, i: (b, i, 0)),
            pl.BlockSpec((1, N_KV_GROUPS, QB, NBLK), lambda b, i: (b, 0, i, 0)),
        ],
        out_shape=[
            jax.ShapeDtypeStruct((B, S, N_HEADS * HEAD_DIM), jnp.float32),
            jax.ShapeDtypeStruct((B, N_KV_GROUPS, S, NBLK), jnp.bfloat16),
        ],
        compiler_params=pltpu.CompilerParams(
            dimension_semantics=("arbitrary", "arbitrary"),
            vmem_limit_bytes=V7X_VMEM_LIMIT_BYTES,
        ),
        name="nsa_cmp_select",
    )(far, q, gates, kc_pad, vc_pad, band, wmap)


def _win_sel_body(q_ref, gate_ref, sel_ref, oc_ref, ks_ref, vs_ref, kw_ref, vw_ref, toep_ref, win_ref,
                  out_ref, s0_scr, s1_scr, p0_scr, p1_scr, m_scr, acc_scr, sw_scr):
    qi = pl.program_id(2)
    q0 = pl.multiple_of(qi * QB, QB)
    qg = q_ref[0]
    q_heads = [qg[:, h * HEAD_DIM:(h + 1) * HEAD_DIM] for h in range(HEADS_PER_GROUP)]
    q4 = jnp.concatenate(q_heads, axis=0)

    sel = sel_ref[0, 0]
    qa = jnp.concatenate([jnp.concatenate([sel] * HEADS_PER_GROUP, axis=0), q4], axis=-1)
    c_diag = qi // 2

    def scores(c):
        col = pl.multiple_of(c * KEY_TILE, KEY_TILE)
        s = jnp.dot(qa, ks_ref[0, 0, :, pl.ds(col, KEY_TILE)], preferred_element_type=jnp.float32)
        mm = qi - 2 * c
        bias = jnp.concatenate([toep_ref[0, jnp.clip(mm + 1, 0, N_TOEP - 1)],
                                toep_ref[0, jnp.clip(mm, 0, N_TOEP - 1)]], axis=-1)
        return s.reshape(HEADS_PER_GROUP, QB, KEY_TILE) + bias

    s_slots = (s0_scr, s1_scr)
    p_slots = (p0_scr, p1_scr)
    m_scr[...] = jnp.full(m_scr.shape, NEG_INF, jnp.float32)
    acc_scr[...] = jnp.zeros_like(acc_scr)
    p_slots[0][...] = jnp.zeros_like(p0_scr)
    s_slots[0][...] = scores(0)

    wcol = lax.broadcasted_iota(jnp.int32, (1, WINDOW + QB), 1)
    neg_left = jnp.where(q0 + wcol >= WINDOW, 0.0, NEG_INF)
    s_w = jnp.dot(q4, kw_ref[0, 0, :, pl.ds(q0, WINDOW + QB)], preferred_element_type=jnp.float32)
    sw_scr[...] = s_w.reshape(HEADS_PER_GROUP, QB, WINDOW + QB) + win_ref[0] + neg_left

    def weighted_values(p, col):
        return jnp.dot(p, vs_ref[0, 0, pl.ds(pl.multiple_of(col, KEY_TILE), KEY_TILE), :],
                       preferred_element_type=jnp.float32)

    def half_step(c, col_prev, cur):
        pv = weighted_values(p_slots[cur][...], col_prev)
        s_slots[1 - cur][...] = scores(c + 1)
        for h in range(HEADS_PER_GROUP):
            s = s_slots[cur][h]
            m_old = m_scr[h]
            m_new = jnp.maximum(m_old, jnp.max(s, axis=-1, keepdims=True))
            alpha = jnp.exp2(m_old - m_new)
            p = jnp.exp2(s - jnp.concatenate([m_new] * (KEY_TILE // LANES), axis=-1))
            p_slots[1 - cur][h * QB:(h + 1) * QB, :] = p.astype(jnp.bfloat16)
            m_scr[h] = m_new
            acc_scr[h * QB:(h + 1) * QB, :] = alpha * (acc_scr[h * QB:(h + 1) * QB, :]
                                                      + pv[h * QB:(h + 1) * QB, :])
        return c * KEY_TILE

    def unrolled_steps(first_tile, n_steps):
        def body(j, col_prev):
            for u in range(n_steps):
                col_prev = half_step(first_tile + n_steps * j + u, col_prev, u % 2)
            return col_prev
        return body

    n_tiles = c_diag + 1
    done = 0
    col_last = 0
    for n_steps in SEL_UNROLLS:
        last_level = n_steps == SEL_UNROLLS[-1]
        trips = (n_tiles - done + (n_steps - 1 if last_level else 0)) // n_steps
        col_last = lax.fori_loop(0, trips, unrolled_steps(done, n_steps), col_last)
        done = done + trips * n_steps
    acc_s = (acc_scr[...] + weighted_values(p_slots[0][...], col_last)).reshape(HEADS_PER_GROUP, QB, LANES)
    o_s = acc_s[:, :, :HEAD_DIM] / acc_s[:, :, HEAD_DIM:HEAD_DIM + 1]

    s_w = sw_scr[...]
    p_w = jnp.exp2(s_w - jnp.max(s_w, axis=-1, keepdims=True))
    pv_w = jnp.dot(p_w.reshape(QROWS, WINDOW + QB).astype(jnp.bfloat16),
                   vw_ref[0, 0, pl.ds(q0, WINDOW + QB), :],
                   preferred_element_type=jnp.float32).reshape(HEADS_PER_GROUP, QB, LANES)
    o_w = pv_w[:, :, :HEAD_DIM] / pv_w[:, :, HEAD_DIM:HEAD_DIM + 1]

    outs = []
    for h in range(HEADS_PER_GROUP):
        g_s = gate_ref[0, 0, :, 3 * h + 1:3 * h + 2]
        g_w = gate_ref[0, 0, :, 3 * h + 2:3 * h + 3]
        outs.append(g_s * o_s[h] + g_w * o_w[h])
    out_ref[0] = oc_ref[0] + jnp.concatenate(outs, axis=-1)


def _win_sel(q, gates, selneg, oc, ks_aug, vs, kw_pad, vw_pad, toep, win):
    B, S, _ = q.shape
    gw = HEADS_PER_GROUP * HEAD_DIM
    grid = (B, N_KV_GROUPS, S // QB)
    return pl.pallas_call(
        _win_sel_body,
        grid=grid,
        in_specs=[
            pl.BlockSpec((1, QB, gw), lambda b, g, i: (b, i, g)),
            pl.BlockSpec((1, 1, QB, 3 * HEADS_PER_GROUP), lambda b, g, i: (b, g, i, 0)),
            pl.BlockSpec((1, 1, QB, NBLK), lambda b, g, i: (b, g, i, 0)),
            pl.BlockSpec((1, QB, gw), lambda b, g, i: (b, i, g)),
            pl.BlockSpec((1, 1, NBLK + HEAD_DIM, ks_aug.shape[3]), lambda b, g, i: (b, g, 0, 0)),
            pl.BlockSpec((1, 1, vs.shape[2], LANES), lambda b, g, i: (b, g, 0, 0)),
            pl.BlockSpec((1, 1, HEAD_DIM, S + WINDOW), lambda b, g, i: (b, g, 0, 0)),
            pl.BlockSpec((1, 1, S + WINDOW, LANES), lambda b, g, i: (b, g, 0, 0)),
            pl.BlockSpec((1, N_TOEP, HEADS_PER_GROUP, QB, QB), lambda b, g, i: (g, 0, 0, 0, 0)),
            pl.BlockSpec((1, HEADS_PER_GROUP, QB, WINDOW + QB), lambda b, g, i: (g, 0, 0, 0)),
        ],
        out_specs=pl.BlockSpec((1, QB, gw), lambda b, g, i: (b, i, g)),
        out_shape=jax.ShapeDtypeStruct((B, S, N_HEADS * HEAD_DIM), jnp.float32),
        scratch_shapes=[
            pltpu.VMEM((HEADS_PER_GROUP, QB, KEY_TILE), jnp.float32),
            pltpu.VMEM((HEADS_PER_GROUP, QB, KEY_TILE), jnp.float32),
            pltpu.VMEM((QROWS, KEY_TILE), jnp.bfloat16),
            pltpu.VMEM((QROWS, KEY_TILE), jnp.bfloat16),
            pltpu.VMEM((HEADS_PER_GROUP, QB, LANES), jnp.float32),
            pltpu.VMEM((QROWS, LANES), jnp.float32),
            pltpu.VMEM((HEADS_PER_GROUP, QB, WINDOW + QB), jnp.float32),
        ],
        compiler_params=pltpu.CompilerParams(
            dimension_semantics=("arbitrary", "arbitrary", "arbitrary"),
            vmem_limit_bytes=V7X_VMEM_LIMIT_BYTES,
        ),
        name="nsa_win_sel",
    )(q, gates, selneg, oc, ks_aug, vs, kw_pad, vw_pad, toep, win)


def _nsa_attention(q, gates, kc, vc, kvsw, rel_bias):
    B, S, _ = q.shape
    assert S % KEY_TILE == 0 and S // L_SLC <= NBLK
    bf = jnp.bfloat16
    n_far = S // D_STRIDE
    toep, win, band, far = _bias_tables(rel_bias)
    cpad = ((0, 0), (0, 0), (CMP_PAD, n_far - kc.shape[2] + CMP_PAD), (0, 0))
    kc_pad = jnp.pad(kc, cpad).astype(bf)
    vc_pad = jnp.pad(vc, cpad).astype(bf)
    wmap = jnp.asarray(np.pad(_slc_map_matrix(n_far), ((CMP_PAD, CMP_PAD), (0, 0))), bf)
    oc, selneg = _cmp_select(q, gates, kc_pad, vc_pad, band, far, wmap)
    ks, vs, kw, vw = (kvsw.reshape(B, S, 4, N_KV_GROUPS, HEAD_DIM)[:, :, n] for n in range(4))
    blk_onehot = jnp.asarray((np.arange(S)[None, :] // L_SLC == np.arange(NBLK)[:, None]), bf)
    ks_aug = jnp.concatenate(
        [jnp.broadcast_to(blk_onehot, (B, N_KV_GROUPS, NBLK, S)), ks.transpose(0, 2, 3, 1)], axis=2)
    tail = 2 * KEY_TILE
    ks_aug = jnp.pad(ks_aug, ((0, 0), (0, 0), (0, 0), (0, tail)))
    kw_pad = jnp.pad(kw.transpose(0, 2, 3, 1), ((0, 0), (0, 0), (0, 0), (WINDOW, 0)))

    def with_ones(v):
        v = v.transpose(0, 2, 1, 3)
        ones = jnp.ones(v.shape[:-1] + (1,), v.dtype)
        return jnp.pad(jnp.concatenate([v, ones], axis=-1), ((0, 0),) * 3 + ((0, LANES - HEAD_DIM - 1),))

    vw_aug = jnp.pad(with_ones(vw), ((0, 0), (0, 0), (WINDOW, 0), (0, 0)))
    vs_aug = jnp.pad(with_ones(vs), ((0, 0), (0, 0), (0, tail), (0, 0)))
    gates_g = gates[:, :, :3 * N_HEADS].reshape(B, S, N_KV_GROUPS, 3 * HEADS_PER_GROUP).swapaxes(1, 2)
    return _win_sel(q, gates_g, selneg, oc, ks_aug, vs_aug, kw_pad, vw_aug, toep, win)


def _pool_layer_body(x_ref, halo_ref, gain_ref, shift_ref, scale_ref, gate_ref, pscale_ref, w_ref, o_ref):
    i = pl.program_id(1)
    x = x_ref[0]
    gain, shift, scale = gain_ref[...], shift_ref[0], scale_ref[0]
    h = _norm_modulate(x, gain, shift, scale)
    halo = _norm_modulate(halo_ref[0], gain, shift, scale) * (i > 0).astype(jnp.float32)
    hx = jnp.concatenate([halo, h], axis=0)
    t = i * POOL_TILE + lax.broadcasted_iota(jnp.int32, (POOL_TILE, 1), 0)
    outs = []
    run = hx
    width = 1
    for g, w in enumerate(POOL_WINDOWS):
        while width < w:
            run = run + pltpu.roll(run, width, axis=0)
            width *= 2
        lanes = slice(0, POOL_GROUP_DIM)
        cnt = jnp.minimum(t + 1, w).astype(jnp.float32)
        mix = run[POOL_HALO:, lanes] / cnt - h[:, g * POOL_GROUP_DIM:(g + 1) * POOL_GROUP_DIM]
        outs.append(jnp.dot(mix.astype(jnp.bfloat16), w_ref[g], preferred_element_type=jnp.float32))
        run = run[:, POOL_GROUP_DIM:]
    y = jnp.concatenate(outs, axis=-1) * pscale_ref[...]
    o_ref[0] = x + gate_ref[0] * y


def _pool_layer(x, gain, shift, scale, gate1, w_grp, pool_scale):
    B, S, D = x.shape
    assert S % POOL_TILE == 0 and POOL_HALO >= max(POOL_WINDOWS) - 1
    per_batch = pl.BlockSpec((1, 1, D), lambda b, i: (b, 0, 0))
    halo_blocks = POOL_TILE // POOL_HALO
    return pl.pallas_call(
        _pool_layer_body,
        grid=(B, S // POOL_TILE),
        in_specs=[
            pl.BlockSpec((1, POOL_TILE, D), lambda b, i: (b, i, 0)),
            pl.BlockSpec((1, POOL_HALO, D), lambda b, i: (b, jnp.maximum(i * halo_blocks - 1, 0), 0)),
            pl.BlockSpec((1, D), lambda b, i: (0, 0)),
            per_batch, per_batch, per_batch,
            pl.BlockSpec((1, D), lambda b, i: (0, 0)),
            pl.BlockSpec((N_POOL_GROUPS, POOL_GROUP_DIM, POOL_GROUP_DIM), lambda b, i: (0, 0, 0)),
        ],
        out_specs=pl.BlockSpec((1, POOL_TILE, D), lambda b, i: (b, i, 0)),
        out_shape=jax.ShapeDtypeStruct((B, S, D), jnp.float32),
        compiler_params=pltpu.CompilerParams(
            dimension_semantics=("arbitrary", "arbitrary"),
            vmem_limit_bytes=V7X_VMEM_LIMIT_BYTES,
        ),
        name="pool_layer",
    )(x, x, gain.reshape(1, D), shift[:, None, :], scale[:, None, :], gate1[:, None, :],
      pool_scale.reshape(1, D), w_grp.astype(jnp.bfloat16))


def _head_norm(v, seg_ref, gain):
    sq = v * v
    hi = sq.astype(jnp.bfloat16)
    lo = (sq - hi.astype(jnp.float32)).astype(jnp.bfloat16)
    ss = (jnp.dot(hi, seg_ref[...], preferred_element_type=jnp.float32)
          + jnp.dot(lo, seg_ref[...], preferred_element_type=jnp.float32))
    return v * lax.rsqrt(ss * (1.0 / HEAD_DIM) + EPS) * gain


def _qkv_proj_body(x_ref, gq_ref, shq_ref, scq_ref, gkv_ref, shkv_ref, sckv_ref, wq_ref, wg_ref, wkv_ref,
                   seg_ref, qgain_ref, kgain_ref, q_ref, gates_ref, kvc_ref, kvsw_ref):
    x = x_ref[...]
    xhat = x * lax.rsqrt(jnp.mean(x * x, axis=-1, keepdims=True) + EPS)
    hq = ((xhat * gq_ref[...]) * (1.0 + scq_ref[0]) + shq_ref[0]).astype(jnp.bfloat16)
    hkv = ((xhat * gkv_ref[...]) * (1.0 + sckv_ref[0]) + shkv_ref[0]).astype(jnp.bfloat16)
    pq = jnp.dot(hq, wq_ref[...], preferred_element_type=jnp.float32)
    gates_ref[...] = jax.nn.sigmoid(jnp.dot(hq, wg_ref[...], preferred_element_type=jnp.float32))
    pkv = jnp.dot(hkv, wkv_ref[...], preferred_element_type=jnp.float32)
    for g in range(N_KV_GROUPS):
        lanes = slice(g * GROUP_W, (g + 1) * GROUP_W)
        q_ref[:, lanes] = _head_norm(pq[:, lanes], seg_ref, qgain_ref[...]).astype(jnp.bfloat16)
    kvc_ref[...] = pkv[:, :2 * GROUP_W]
    k_s = _head_norm(pkv[:, 2 * GROUP_W:3 * GROUP_W], seg_ref, kgain_ref[0:1, :])
    k_w = _head_norm(pkv[:, 4 * GROUP_W:5 * GROUP_W], seg_ref, kgain_ref[1:2, :])
    kvsw_ref[...] = jnp.concatenate([k_s, pkv[:, 3 * GROUP_W:4 * GROUP_W], k_w, pkv[:, 5 * GROUP_W:]],
                                    axis=-1).astype(jnp.bfloat16)


def _qkv_proj(x, gain_q, shift_q, scale_q, gain_kv, shift_kv, scale_kv, w_qg, kv_w, q_gain, k_gain):
    B, S, D = x.shape
    T = B * S
    qd = N_HEADS * HEAD_DIM
    bf = jnp.bfloat16
    tiles_per_batch = S // ROW_TILE
    per_batch = pl.BlockSpec((1, 1, D), lambda i: (i // tiles_per_batch, 0, 0))
    const = lambda shape: pl.BlockSpec(shape, lambda i: (0,) * len(shape))
    seg = jnp.asarray(np.kron(np.eye(HEADS_PER_GROUP), np.ones((HEAD_DIM, HEAD_DIM))), bf)
    w_gate = jnp.pad(w_qg[:, qd:], ((0, 0), (0, LANES - 3 * N_HEADS))).astype(bf)
    qgain = jnp.tile(q_gain * HEAD_DIM ** -0.5, HEADS_PER_GROUP).reshape(1, GROUP_W)
    kgain = jnp.stack([jnp.tile(k_gain[1] * LOG2_E, N_KV_GROUPS), jnp.tile(k_gain[2] * LOG2_E, N_KV_GROUPS)])
    rows = lambda width: pl.BlockSpec((ROW_TILE, width), lambda i: (i, 0))
    return pl.pallas_call(
        _qkv_proj_body,
        grid=(T // ROW_TILE,),
        in_specs=[rows(D), const((1, D)), per_batch, per_batch, const((1, D)), per_batch, per_batch,
                  const((D, qd)), const((D, LANES)), const((D, 6 * GROUP_W)), const((GROUP_W, GROUP_W)),
                  const((1, GROUP_W)), const((2, GROUP_W))],
        out_specs=[rows(qd), rows(LANES), rows(2 * GROUP_W), rows(4 * GROUP_W)],
        out_shape=[jax.ShapeDtypeStruct((T, qd), bf), jax.ShapeDtypeStruct((T, LANES), jnp.float32),
                   jax.ShapeDtypeStruct((T, 2 * GROUP_W), jnp.float32),
                   jax.ShapeDtypeStruct((T, 4 * GROUP_W), bf)],
        compiler_params=pltpu.CompilerParams(
            dimension_semantics=("arbitrary",), vmem_limit_bytes=V7X_VMEM_LIMIT_BYTES),
        name="qkv_proj",
    )(x.reshape(T, D), gain_q.reshape(1, D), shift_q[:, None, :], scale_q[:, None, :],
      gain_kv.reshape(1, D), shift_kv[:, None, :], scale_kv[:, None, :],
      w_qg[:, :qd].astype(bf), w_gate, kv_w.astype(bf), seg, qgain, kgain)


def _compress_body(r_ref, pe_ref, w1_ref, w2_ref, gain_ref, o_ref):
    half = D_STRIDE * HEAD_DIM
    r = r_ref[0, 0, 0]
    top = jnp.dot(r, w1_ref[0, :half, :], preferred_element_type=jnp.float32)
    bot = jnp.dot(r, w1_ref[0, half:, :], preferred_element_type=jnp.float32)
    pe = jnp.dot(jnp.broadcast_to(pe_ref[0], (8, R_CMP * half)).astype(jnp.bfloat16), w1_ref[0],
                 preferred_element_type=jnp.float32)[0:1]
    n_chunks = r.shape[0]
    hidden = top + pltpu.roll(bot, n_chunks - 1, axis=0) + pe
    out = jnp.dot(jax.nn.gelu(hidden).astype(jnp.bfloat16), w2_ref[0], preferred_element_type=jnp.float32)
    normed = out * lax.rsqrt(jnp.mean(out * out, axis=-1, keepdims=True) + EPS) * gain_ref[...]
    o_ref[0, 0, 0] = jnp.where(pl.program_id(0) == 0, normed, out)


def _compressed_kv(kvc, B, S, cmp_pe_k, cmp_pe_v, cmp_k_w1, cmp_k_w2, cmp_v_w1, cmp_v_w2, k_gain):
    bf = jnp.bfloat16
    n_chunks = S // D_STRIDE
    half = D_STRIDE * HEAD_DIM
    hidden = cmp_k_w1.shape[1]
    r = kvc.astype(bf).reshape(B, n_chunks, D_STRIDE, 2, N_KV_GROUPS, HEAD_DIM)
    r = r.transpose(3, 0, 4, 1, 2, 5).reshape(2, B, N_KV_GROUPS, n_chunks, half)
    pe = jnp.stack([cmp_pe_k, cmp_pe_v]).reshape(2, 1, R_CMP * half)
    per_kind = lambda shape: pl.BlockSpec((1,) + shape, lambda kv, b, g: (kv,) + (0,) * len(shape))
    out = pl.pallas_call(
        _compress_body,
        grid=(2, B, N_KV_GROUPS),
        in_specs=[
            pl.BlockSpec((1, 1, 1, n_chunks, half), lambda kv, b, g: (kv, b, g, 0, 0)),
            per_kind((1, R_CMP * half)), per_kind((R_CMP * half, hidden)), per_kind((hidden, HEAD_DIM)),
            pl.BlockSpec((1, HEAD_DIM), lambda kv, b, g: (0, 0)),
        ],
        out_specs=pl.BlockSpec((1, 1, 1, n_chunks, HEAD_DIM), lambda kv, b, g: (kv, b, g, 0, 0)),
        out_shape=jax.ShapeDtypeStruct((2, B, N_KV_GROUPS, n_chunks, HEAD_DIM), jnp.float32),
        compiler_params=pltpu.CompilerParams(
            dimension_semantics=("arbitrary", "arbitrary", "arbitrary"),
            vmem_limit_bytes=V7X_VMEM_LIMIT_BYTES),
        name="cmp_mlp",
    )(r, pe, jnp.stack([cmp_k_w1, cmp_v_w1]).astype(bf), jnp.stack([cmp_k_w2, cmp_v_w2]).astype(bf),
      (k_gain[0] * LOG2_E).reshape(1, HEAD_DIM))
    n_cmp = n_chunks - R_CMP + 1
    return out[0, :, :, :n_cmp], out[1, :, :, :n_cmp]


def _oproj_router_body(a_ref, x_ref, gate_ref, wo_ref, gain_ref, shift_ref, scale_ref, wr_ref, br_ref,
                       o_ref, logit_ref):
    mix = jnp.dot(a_ref[...].astype(jnp.bfloat16), wo_ref[...], preferred_element_type=jnp.float32)
    x = x_ref[...] + gate_ref[0] * mix
    o_ref[...] = x
    h = _norm_modulate(x, gain_ref[...], shift_ref[0], scale_ref[0]).astype(jnp.bfloat16)
    logit_ref[...] = jnp.dot(h, wr_ref[...], preferred_element_type=jnp.float32) + br_ref[...]


def _oproj_router(attn, x, gate1, w_o, gain, shift, scale, w_router, b_router):
    B, S, D = x.shape
    T = B * S
    bf = jnp.bfloat16
    tiles_per_batch = S // ROW_TILE
    per_batch = pl.BlockSpec((1, 1, D), lambda i: (i // tiles_per_batch, 0, 0))
    const = lambda shape: pl.BlockSpec(shape, lambda i: (0,) * len(shape))
    rows = lambda width: pl.BlockSpec((ROW_TILE, width), lambda i: (i, 0))
    pad = ((0, 0), (0, LANES - N_EXPERTS))
    out, logits = pl.pallas_call(
        _oproj_router_body,
        grid=(T // ROW_TILE,),
        in_specs=[rows(D), rows(D), per_batch, const((D, D)), const((1, D)), per_batch, per_batch,
                  const((D, LANES)), const((1, LANES))],
        out_specs=[rows(D), rows(LANES)],
        out_shape=[jax.ShapeDtypeStruct((T, D), jnp.float32), jax.ShapeDtypeStruct((T, LANES), jnp.float32)],
        compiler_params=pltpu.CompilerParams(
            dimension_semantics=("arbitrary",), vmem_limit_bytes=V7X_VMEM_LIMIT_BYTES),
        name="oproj_router",
    )(attn.reshape(T, D), x.reshape(T, D), gate1[:, None, :], w_o.astype(bf), gain.reshape(1, D),
      shift[:, None, :], scale[:, None, :], jnp.pad(w_router, pad).astype(bf),
      jnp.pad(b_router.reshape(1, N_EXPERTS), pad))
    return out.reshape(B, S, D), logits[:, :N_EXPERTS]


def _route(logits):
    T = logits.shape[0]
    lane = jnp.arange(N_EXPERTS)[None, :]
    l0 = jnp.max(logits, axis=-1)
    e0 = jnp.argmax(logits, axis=-1)
    rest = jnp.where(lane == e0[:, None], -jnp.inf, logits)
    l1 = jnp.max(rest, axis=-1)
    e1 = jnp.argmax(rest, axis=-1)
    z = jnp.exp(l1 - l0)
    top_w = jnp.stack([1.0 / (1.0 + z), z / (1.0 + z)], axis=-1)
    oh = [(lane == e[:, None]).astype(jnp.int32) for e in (e0, e1)]
    cnt = oh[0] + oh[1]
    before = jnp.cumsum(cnt, axis=0) - cnt
    counts = before[-1] + cnt[-1]
    pcounts = (counts + ROW_TILE - 1) // ROW_TILE * ROW_TILE
    pends = jnp.cumsum(pcounts)
    pstarts = pends - pcounts
    dest = jnp.stack([jnp.sum((pstarts[None, :] + before) * o, axis=-1) for o in oh], axis=-1)
    n_blocks = -(-(T * TOP_K + N_EXPERTS * (ROW_TILE - 1)) // ROW_TILE)
    blk_start = jnp.arange(n_blocks)[:, None] * ROW_TILE
    blk_e = jnp.minimum(jnp.sum(pends[None, :] <= blk_start, axis=-1), N_EXPERTS - 1).astype(jnp.int32)
    n_used = (pends[-1:] // ROW_TILE).astype(jnp.int32)
    return top_w, dest.astype(jnp.int32), blk_e, n_used, n_blocks * ROW_TILE


def _moe_ffn(x, logits, gain, shift, scale, gate2, w_gu, w_dn):
    top_w, dest, blk_e, n_used, n_rows = _route(logits)
    rows = _moe_dispatch(x, gain, shift, scale, dest, n_rows)
    y = _grouped_swiglu(rows, blk_e, n_used, w_gu.astype(jnp.bfloat16), w_dn.astype(jnp.bfloat16),
                        ff_tile=FF_TILE_EXPERT)
    return _moe_combine(x, top_w, gate2, y, dest)


def kernel(x, c, ada_w, ada_b, norm_g, pool_w, pool_scale, q_w, q_gain, o_w, kv_ada_w, kv_ada_b, kv_norm_g, kv_w, cmp_pe_k, cmp_pe_v, cmp_k_w1, cmp_k_w2, cmp_v_w1, cmp_v_w2, k_gain, rel_bias, ffn_gu, ffn_dn, router_w, router_b, exp_gu, exp_dn):
    assert DEPTH == 2 and N_A_LAYERS == 1
    B, S, D = x.shape
    silu_c = jax.nn.silu(c)
    sh1, sc1, g1, sh2, sc2, g2 = jnp.split(silu_c @ ada_w[0] + ada_b[0], 6, axis=-1)
    x = _pool_layer(x, norm_g[0, 0], sh1, sc1, g1, pool_w[0], pool_scale[0])
    x = _dense_ffn(x, norm_g[0, 1], sh2, sc2, g2, ffn_gu[0], ffn_dn[0])
    sh1, sc1, g1, sh2, sc2, g2 = jnp.split(silu_c @ ada_w[1] + ada_b[1], 6, axis=-1)
    sh_kv, sc_kv = jnp.split(silu_c @ kv_ada_w + kv_ada_b, 2, axis=-1)
    q, gates, kvc, kvsw = _qkv_proj(x, norm_g[1, 0], sh1, sc1, kv_norm_g, sh_kv, sc_kv, q_w[0], kv_w,
                                    q_gain[0], k_gain)
    kc, vc = _compressed_kv(kvc, B, S, cmp_pe_k, cmp_pe_v, cmp_k_w1, cmp_k_w2, cmp_v_w1, cmp_v_w2, k_gain)
    attn = _nsa_attention(q.reshape(B, S, -1), gates.reshape(B, S, -1), kc, vc, kvsw.reshape(B, S, -1),
                          rel_bias)
    x, logits = _oproj_router(attn, x, g1, o_w[0], norm_g[1, 1], sh2, sc2, router_w[0], router_b[0])
    return _moe_ffn(x, logits, norm_g[1, 1], sh2, sc2, g2, exp_gu[0], exp_dn[0])
```

```python
import functools
import math

import jax
import jax.numpy as jnp
import numpy as np
from jax import lax
from jax.experimental import pallas as pl
from jax.experimental.pallas import tpu as pltpu

D_MODEL = 1024
DEPTH = 2
N_A_LAYERS = DEPTH // 2
POOL_WINDOWS = (2, 4, 8, 16)
N_POOL_GROUPS = len(POOL_WINDOWS)
POOL_GROUP_DIM = D_MODEL // N_POOL_GROUPS
HEAD_DIM = 64
N_HEADS = D_MODEL // HEAD_DIM
N_KV_GROUPS = 4
HEADS_PER_GROUP = N_HEADS // N_KV_GROUPS
L_CMP = 32
D_STRIDE = 16
L_SLC = 64
N_SEL = 16
WINDOW = 512
R_CMP = L_CMP // D_STRIDE
R_SLC = L_SLC // D_STRIDE
N_BUCKETS = 32
REL_EXACT = N_BUCKETS // 2
MAX_DISTANCE = 1024
N_EXPERTS = 8
TOP_K = 2
EPS = 1e-6
NEG_INF = -1e30
SEL_FORCE = 1e6
LOG2_E = math.log2(math.e)

V7X_VMEM_LIMIT_BYTES = 48 * 1024 * 1024
LANES = 128
ROW_TILE = 512
TOK_TILE = 512
DMA_ISSUE_UNROLL = 8
MAX_FORCED = 3
FF_TILE_DENSE = 2816
FF_TILE_EXPERT = 1792
POOL_TILE = 512
POOL_HALO = 16

QB = 128
KEY_TILE = 2 * QB
SEL_UNROLLS = (16, 8, 2)
QROWS = HEADS_PER_GROUP * QB
GROUP_W = HEADS_PER_GROUP * HEAD_DIM
NBLK = 128
CMP_PAD = 128
FAR_DIST = MAX_DISTANCE
N_TOEP = FAR_DIST // QB + 3
BAND_LEFT = CMP_PAD - 16
assert BAND_LEFT * D_STRIDE + (L_CMP - 1) - 2 * QB >= FAR_DIST


def _rms_norm(x, g):
    xf = x.astype(jnp.float32)
    y = xf * lax.rsqrt(jnp.mean(xf * xf, axis=-1, keepdims=True) + EPS)
    return (y * g.astype(jnp.float32)).astype(x.dtype)


def _modulate(h, shift, scale):
    return h * (1 + scale[:, None, :]) + shift[:, None, :]


def _rel_bucket(dist):
    d = jnp.maximum(dist, 0)
    ratio = jnp.maximum(d, REL_EXACT).astype(jnp.float32) / REL_EXACT
    large = REL_EXACT + (jnp.log(ratio) / math.log(MAX_DISTANCE / REL_EXACT)
                         * (N_BUCKETS - REL_EXACT)).astype(jnp.int32)
    return jnp.where(d < REL_EXACT, d, jnp.minimum(large, N_BUCKETS - 1))


def _norm_modulate(x, gain, shift, scale):
    y = x * lax.rsqrt(jnp.mean(x * x, axis=-1, keepdims=True) + EPS)
    return (y * gain) * (1.0 + scale) + shift


def _swiglu_step(xb_ref, wg_ref, wu_ref, wd_ref, acc_ref, j):
    @pl.when(j == 0)
    def _():
        acc_ref[...] = jnp.zeros_like(acc_ref)

    x = xb_ref[...]
    gate = jnp.dot(x, wg_ref[0], preferred_element_type=jnp.float32)
    up = jnp.dot(x, wu_ref[0], preferred_element_type=jnp.float32)
    act = (gate * jax.nn.sigmoid(gate) * up).astype(jnp.bfloat16)
    acc_ref[...] += jnp.dot(act, wd_ref[0], preferred_element_type=jnp.float32)


def _grouped_swiglu_body(blk_e_ref, n_used_ref, x_ref, wg_ref, wu_ref, wd_ref, o_ref, acc_ref, xb_ref, *,
                         n_ff_steps):
    del blk_e_ref
    i, j = pl.program_id(0), pl.program_id(1)
    used = i < n_used_ref[0]

    @pl.when(used & (j == 0))
    def _():
        xb_ref[...] = x_ref[...].astype(jnp.bfloat16)

    @pl.when(used)
    def _():
        _swiglu_step(xb_ref, wg_ref, wu_ref, wd_ref, acc_ref, j)

    @pl.when(j == n_ff_steps - 1)
    def _():
        o_ref[...] = jnp.where(used, acc_ref[...], 0.0)


def _grouped_swiglu(x_rows, blk_e, n_used, w_gu, w_dn, *, ff_tile):
    n_rows, d = x_rows.shape
    d_ff = w_dn.shape[1]
    assert n_rows % ROW_TILE == 0 and d_ff % ff_tile == 0
    n_ff_steps = d_ff // ff_tile
    grid = (n_rows // ROW_TILE, n_ff_steps)
    return pl.pallas_call(
        functools.partial(_grouped_swiglu_body, n_ff_steps=n_ff_steps),
        grid_spec=pltpu.PrefetchScalarGridSpec(
            num_scalar_prefetch=2,
            grid=grid,
            in_specs=[
                pl.BlockSpec((ROW_TILE, d), lambda i, j, e, n: (i, 0)),
                pl.BlockSpec((1, d, ff_tile), lambda i, j, e, n: (e[i], 0, j)),
                pl.BlockSpec((1, d, ff_tile), lambda i, j, e, n: (e[i], 0, j + n_ff_steps)),
                pl.BlockSpec((1, ff_tile, d), lambda i, j, e, n: (e[i], j, 0)),
            ],
            out_specs=pl.BlockSpec((ROW_TILE, d), lambda i, j, e, n: (i, 0)),
            scratch_shapes=[pltpu.VMEM((ROW_TILE, d), jnp.float32), pltpu.VMEM((ROW_TILE, d), jnp.bfloat16)],
        ),
        out_shape=jax.ShapeDtypeStruct((n_rows, d), jnp.float32),
        compiler_params=pltpu.CompilerParams(
            dimension_semantics=("arbitrary", "arbitrary"),
            vmem_limit_bytes=V7X_VMEM_LIMIT_BYTES,
        ),
        name="grouped_swiglu",
    )(blk_e, n_used, x_rows, w_gu, w_gu, w_dn)


def _dense_ffn_body(x_ref, gain_ref, shift_ref, scale_ref, gate2_ref, wg_ref, wu_ref, wd_ref, o_ref,
                    acc_ref, xb_ref, *, n_ff_steps):
    j = pl.program_id(1)

    @pl.when(j == 0)
    def _():
        xb_ref[...] = _norm_modulate(x_ref[...], gain_ref[...], shift_ref[0], scale_ref[0]).astype(jnp.bfloat16)

    _swiglu_step(xb_ref, wg_ref, wu_ref, wd_ref, acc_ref, j)

    @pl.when(j == n_ff_steps - 1)
    def _():
        o_ref[...] = x_ref[...] + gate2_ref[0] * acc_ref[...]


def _dense_ffn(x, gain, shift, scale, gate2, w_gu, w_dn):
    B, S, D = x.shape
    d_ff = w_dn.shape[0]
    n_ff_steps = d_ff // FF_TILE_DENSE
    tiles_per_batch = S // ROW_TILE
    per_batch = pl.BlockSpec((1, 1, D), lambda i, j: (i // tiles_per_batch, 0, 0))
    weight_buffers = pl.Buffered(1) if n_ff_steps == 1 else None
    out = pl.pallas_call(
        functools.partial(_dense_ffn_body, n_ff_steps=n_ff_steps),
        grid=(B * S // ROW_TILE, n_ff_steps),
        in_specs=[
            pl.BlockSpec((ROW_TILE, D), lambda i, j: (i, 0)),
            pl.BlockSpec((1, D), lambda i, j: (0, 0)),
            per_batch, per_batch, per_batch,
            pl.BlockSpec((1, D, FF_TILE_DENSE), lambda i, j: (0, 0, j), pipeline_mode=weight_buffers),
            pl.BlockSpec((1, D, FF_TILE_DENSE), lambda i, j: (0, 0, j + n_ff_steps), pipeline_mode=weight_buffers),
            pl.BlockSpec((1, FF_TILE_DENSE, D), lambda i, j: (0, j, 0), pipeline_mode=weight_buffers),
        ],
        out_specs=pl.BlockSpec((ROW_TILE, D), lambda i, j: (i, 0)),
        out_shape=jax.ShapeDtypeStruct((B * S, D), jnp.float32),
        scratch_shapes=[pltpu.VMEM((ROW_TILE, D), jnp.float32), pltpu.VMEM((ROW_TILE, D), jnp.bfloat16)],
        compiler_params=pltpu.CompilerParams(
            dimension_semantics=("arbitrary", "arbitrary"),
            vmem_limit_bytes=V7X_VMEM_LIMIT_BYTES,
        ),
        name="dense_ffn",
    )(x.reshape(B * S, D), gain.reshape(1, D), shift[:, None, :], scale[:, None, :], gate2[:, None, :],
      w_gu.astype(jnp.bfloat16)[None], w_gu.astype(jnp.bfloat16)[None], w_dn.astype(jnp.bfloat16)[None])
    return out.reshape(B, S, D)


def _row_copies_wait(src_ref, dst_ref, sem, n_rows):
    pltpu.make_async_copy(src_ref.at[pl.ds(0, n_rows)], dst_ref.at[pl.ds(0, n_rows)], sem).wait()


def _dispatch_body(dest_ref, x_ref, gain_ref, shift_ref, scale_ref, rows_in_ref, rows_ref, h_ref, sem):
    del rows_in_ref
    h_ref[...] = _norm_modulate(x_ref[...], gain_ref[...], shift_ref[0], scale_ref[0])

    def issue(r, carry):
        for k in range(TOP_K):
            pltpu.make_async_copy(h_ref.at[pl.ds(r, 1)], rows_ref.at[pl.ds(dest_ref[TOP_K * r + k], 1)],
                                  sem).start(priority=k % 2)
        return carry

    lax.fori_loop(0, TOK_TILE, issue, 0, unroll=DMA_ISSUE_UNROLL)
    for k in range(TOP_K):
        _row_copies_wait(h_ref, rows_ref, sem, TOK_TILE)


def _moe_dispatch(x, gain, shift, scale, dest, n_rows):
    B, S, D = x.shape
    T = B * S
    tiles_per_batch = S // TOK_TILE
    per_batch = pl.BlockSpec((1, 1, D), lambda i: (i // tiles_per_batch, 0, 0))
    return pl.pallas_call(
        _dispatch_body,
        grid=(T // TOK_TILE,),
        in_specs=[
            pl.BlockSpec((TOP_K * TOK_TILE,), lambda i: (i,), memory_space=pltpu.SMEM),
            pl.BlockSpec((TOK_TILE, D), lambda i: (i, 0)),
            pl.BlockSpec((1, D), lambda i: (0, 0)),
            per_batch, per_batch,
            pl.BlockSpec(memory_space=pl.ANY),
        ],
        out_specs=pl.BlockSpec(memory_space=pl.ANY),
        out_shape=jax.ShapeDtypeStruct((n_rows, D), jnp.float32),
        scratch_shapes=[pltpu.VMEM((TOK_TILE, D), jnp.float32), pltpu.SemaphoreType.DMA(())],
        input_output_aliases={5: 0},
        compiler_params=pltpu.CompilerParams(dimension_semantics=("arbitrary",)),
        name="moe_dispatch",
    )(dest.reshape(T * TOP_K), x.reshape(T, D), gain.reshape(1, D), shift[:, None, :], scale[:, None, :],
      jnp.zeros((n_rows, D), jnp.float32))


def _combine_body(dest_ref, x_ref, w_ref, gate2_ref, y_ref, o_ref, buf_ref, sem):
    def issue(r, carry):
        for k in range(TOP_K):
            pltpu.make_async_copy(y_ref.at[pl.ds(dest_ref[TOP_K * r + k], 1)], buf_ref.at[k, pl.ds(r, 1)],
                                  sem).start(priority=k % 2)
        return carry

    lax.fori_loop(0, TOK_TILE, issue, 0, unroll=DMA_ISSUE_UNROLL)
    for k in range(TOP_K):
        _row_copies_wait(y_ref, buf_ref.at[k], sem, TOK_TILE)
    w = w_ref[...]
    f = w[:, 0:1] * buf_ref[0] + w[:, 1:2] * buf_ref[1]
    o_ref[...] = x_ref[...] + gate2_ref[0] * f


def _moe_combine(x, top_w, gate2, y, dest):
    B, S, D = x.shape
    T = B * S
    tiles_per_batch = S // TOK_TILE
    out = pl.pallas_call(
        _combine_body,
        grid=(T // TOK_TILE,),
        in_specs=[
            pl.BlockSpec((TOP_K * TOK_TILE,), lambda i: (i,), memory_space=pltpu.SMEM),
            pl.BlockSpec((TOK_TILE, D), lambda i: (i, 0)),
            pl.BlockSpec((TOK_TILE, TOP_K), lambda i: (i, 0)),
            pl.BlockSpec((1, 1, D), lambda i: (i // tiles_per_batch, 0, 0)),
            pl.BlockSpec(memory_space=pl.ANY),
        ],
        out_specs=pl.BlockSpec((TOK_TILE, D), lambda i: (i, 0)),
        out_shape=jax.ShapeDtypeStruct((T, D), jnp.float32),
        scratch_shapes=[pltpu.VMEM((TOP_K, TOK_TILE, D), jnp.float32), pltpu.SemaphoreType.DMA(())],
        compiler_params=pltpu.CompilerParams(dimension_semantics=("arbitrary",)),
        name="moe_combine",
    )(dest.reshape(T * TOP_K), x.reshape(T, D), top_w, gate2[:, None, :], y)
    return out.reshape(B, S, D)


def _bias_tables(rel_bias):
    x0 = (N_TOEP - 1) * QB
    width = x0 + 2 * QB
    period = width + QB
    n = np.arange(period)
    n = np.where(n < width, n, n - period)
    by_dist = rel_bias.astype(jnp.float32)[_rel_bucket(jnp.asarray(np.maximum(x0 - n, 0)))].T
    strip = jnp.tile(by_dist, (1, QB))[:, :QB * (period - 1)].reshape(N_HEADS, QB, period - 1)[:, :, :width]
    far = rel_bias.astype(jnp.float32)[N_BUCKETS - 1] * LOG2_E

    i = np.arange(QB)[:, None]
    toep = jnp.stack([strip[:, :, x0 - QB * m:x0 - QB * m + QB] for m in range(-1, N_TOEP - 1)], axis=1)
    d_toep = QB * np.arange(-1, N_TOEP - 1)[:, None, None] + i[None] - np.arange(QB)[None, None, :]
    toep = jnp.where(d_toep >= 0, toep * LOG2_E, NEG_INF)
    toep = toep.reshape(N_KV_GROUPS, HEADS_PER_GROUP, N_TOEP, QB, QB).transpose(0, 2, 1, 3, 4)
    d_win = WINDOW + i - np.arange(WINDOW + QB)[None, :]
    win = jnp.where((d_win >= 0) & (d_win < WINDOW), strip[:, :, x0 - WINDOW:x0 + QB] * LOG2_E, NEG_INF)
    win = win.reshape(N_KV_GROUPS, HEADS_PER_GROUP, QB, WINDOW + QB)
    bands = []
    for par in range(2):
        off = x0 - QB * par - D_STRIDE * BAND_LEFT + (L_CMP - 1)
        c_first = -(off // D_STRIDE)
        cols = strip[:, :, D_STRIDE * c_first + off::D_STRIDE][:, :, :128 - c_first] * LOG2_E
        left = jnp.broadcast_to(far[:, None, None], (N_HEADS, QB, c_first))
        right = jnp.zeros((N_HEADS, QB, 128 - c_first - cols.shape[2]), jnp.float32)
        d_band = QB * par + i - D_STRIDE * (np.arange(128)[None, :] - BAND_LEFT) - (L_CMP - 1)
        assert (d_band[:, 128 - right.shape[2]:] < 0).all()
        bands.append(jnp.where(d_band >= 0, jnp.concatenate([left, cols, right], axis=-1), NEG_INF))
    band = jnp.stack(bands).reshape(2, N_KV_GROUPS, HEADS_PER_GROUP, QB, 128)
    return toep, win, band, far


def _slc_map_matrix(n_cmp_cols):
    w = np.zeros((n_cmp_cols, NBLK), np.float32)
    for jb in range(NBLK):
        for mm in range(R_SLC):
            for nn in range(R_CMP):
                k = R_SLC * jb + mm - nn
                if 0 <= k < n_cmp_cols:
                    w[k, jb] += 1.0
    return w


def _dot_nt(a, b):
    return lax.dot_general(a, b, (((1,), (1,)), ((), ())), preferred_element_type=jnp.float32)


def _cmp_select_body(far_ref, q_ref, gate_ref, kc_ref, vc_ref, band_ref, wmap_ref, oc_ref, sel_ref,
                     *, n_far):
    qi = pl.program_id(1)
    par = qi % 2
    band0 = pl.multiple_of(16 * (qi // 2 + 1), 16)
    first_band_blk = band0 - CMP_PAD

    col = lax.broadcasted_iota(jnp.int32, (QB, 128), 1)
    row = lax.broadcasted_iota(jnp.int32, (QB, 128), 0)
    neg_pad = jnp.where(first_band_blk + col >= 0, 0.0, NEG_INF)

    t = qi * QB + row
    cur = t // L_SLC
    forced = (col == 0) | (col == cur) | (col == cur - 1)
    n_forced = 1 + (cur[:, :1] >= 1).astype(jnp.int32) + (cur[:, :1] >= 2).astype(jnp.int32)
    valid = col * L_SLC <= t

    q_all = q_ref[0]
    colf = col.astype(jnp.float32)

    def take_best(score, active=None):
        best = jnp.max(score, axis=-1, keepdims=True)
        first = jnp.min(jnp.where(score == best, colf, float(NBLK)), axis=-1, keepdims=True)
        hit = colf == first
        return jnp.where(hit if active is None else hit & active, -jnp.inf, score)

    def branches(n_cols):
        far_col = lax.broadcasted_iota(jnp.int32, (1, max(n_cols, 1)), 1)
        neg_far = jnp.where(far_col < first_band_blk, 0.0, NEG_INF)

        def far_and_band(ref, *lead):
            band_rows = ref[(*lead, pl.ds(band0, 128), slice(None))]
            if not n_cols:
                return band_rows
            return jnp.concatenate([ref[(*lead, slice(CMP_PAD, CMP_PAD + n_cols), slice(None))], band_rows], axis=0)

        wmap_all = far_and_band(wmap_ref)
        scores, start_scores = [], []
        oc_heads = []
        for g in range(N_KV_GROUPS):
            q4 = jnp.concatenate(
                [q_all[:, (g * HEADS_PER_GROUP + h) * HEAD_DIM:(g * HEADS_PER_GROUP + h + 1) * HEAD_DIM]
                 for h in range(HEADS_PER_GROUP)], axis=0)
            k_all = far_and_band(kc_ref, 0, g)
            v_all = far_and_band(vc_ref, 0, g)
            s_all = _dot_nt(q4, k_all).reshape(HEADS_PER_GROUP, QB, n_cols + 128)
            imp = jnp.zeros((QB, n_cols + 128), jnp.float32)
            for h in range(HEADS_PER_GROUP):
                hh = g * HEADS_PER_GROUP + h
                bias = band_ref[par, g, h] + neg_pad
                if n_cols:
                    bias = jnp.concatenate([jnp.broadcast_to(far_ref[hh] + neg_far, (QB, n_cols)), bias],
                                           axis=-1)
                s = s_all[h] + bias
                m = jnp.maximum(jnp.max(s, axis=-1, keepdims=True), 1e-10 * NEG_INF)
                p = jnp.exp2(s - m)
                l = jnp.sum(p, axis=-1, keepdims=True)
                p = p * jnp.where(l > 0.0, 1.0 / l, 0.0)
                imp = imp + p
                o = jnp.dot(p.astype(jnp.bfloat16), v_all, preferred_element_type=jnp.float32)
                oc_heads.append(o * gate_ref[0, :, 3 * hh:3 * hh + 1])
            p_slc = jnp.zeros((QB, NBLK), jnp.float32)
            rest = imp
            for _ in range(3):
                term = rest.astype(jnp.bfloat16)
                p_slc = p_slc + jnp.dot(term, wmap_all, preferred_element_type=jnp.float32)
                rest = rest - term.astype(jnp.float32)
            score = jnp.where(forced, -jnp.inf, jnp.where(valid, p_slc, -SEL_FORCE - colf))
            start_scores.append(score)
            for _ in range(N_SEL - MAX_FORCED):
                score = jnp.where(score == jnp.max(score, axis=-1, keepdims=True), -jnp.inf, score)
            scores.append(score)
        return (jnp.concatenate(oc_heads, axis=-1),) + tuple(scores) + tuple(start_scores)

    n_variants = n_far // 128
    variant = jnp.clip((first_band_blk + 127) // 128, 0, n_variants - 1)
    res = lax.switch(variant, [functools.partial(branches, 128 * v) for v in range(n_variants)])
    oc_ref[0] = res[0]
    scores, start_scores = res[1:1 + N_KV_GROUPS], res[1 + N_KV_GROUPS:]

    expected = (n_forced + (N_SEL - MAX_FORCED)).astype(jnp.float32)
    excess = [jnp.sum(jnp.where(s == -jnp.inf, 1.0, 0.0), axis=-1, keepdims=True) - expected for s in scores]
    any_tie = jnp.max(sum(excess)) > 0.0

    def exact_passes(start_scores):
        for _ in range(N_SEL - MAX_FORCED):
            start_scores = tuple(take_best(s) for s in start_scores)
        return start_scores

    scores = lax.cond(any_tie, exact_passes, lambda _: tuple(scores), tuple(start_scores))

    def early_rows(scores):
        for extra in range(MAX_FORCED - 1):
            scores = tuple(take_best(s, MAX_FORCED - n_forced > extra) for s in scores)
        return scores

    scores = lax.cond(qi == 0, early_rows, lambda s: s, tuple(scores))
    for g in range(N_KV_GROUPS):
        sel_ref[0, g] = jnp.where(scores[g] == -jnp.inf, 0.0, NEG_INF).astype(jnp.bfloat16)


def _cmp_select(q, gates, kc_pad, vc_pad, band, far, wmap):
    B, S, _ = q.shape
    n_far = S // D_STRIDE
    n_pad = kc_pad.shape[2]
    grid = (B, S // QB)
    return pl.pallas_call(
        functools.partial(_cmp_select_body, n_far=n_far),
        grid=grid,
        in_specs=[
            pl.BlockSpec(memory_space=pltpu.SMEM),
            pl.BlockSpec((1, QB, N_HEADS * HEAD_DIM), lambda b, i: (b, i, 0)),
            pl.BlockSpec((1, QB, LANES), lambda b, i: (b, i, 0)),
            pl.BlockSpec((1, N_KV_GROUPS, n_pad, HEAD_DIM), lambda b, i: (b, 0, 0, 0)),
            pl.BlockSpec((1, N_KV_GROUPS, n_pad, HEAD_DIM), lambda b, i: (b, 0, 0, 0)),
            pl.BlockSpec((2, N_KV_GROUPS, HEADS_PER_GROUP, QB, 128), lambda b, i: (0, 0, 0, 0, 0)),
            pl.BlockSpec((n_pad, NBLK), lambda b, i: (0, 0)),
        ],
        out_specs=[
            pl.BlockSpec((1, QB, N_HEADS * HEAD_DIM), lambda b, i: (b, i, 0)),
            pl.BlockSpec((1, N_KV_GROUPS, QB, NBLK), lambda b, i: (b, 0, i, 0)),
        ],
        out_shape=[
            jax.ShapeDtypeStruct((B, S, N_HEADS * HEAD_DIM), jnp.float32),
            jax.ShapeDtypeStruct((B, N_KV_GROUPS, S, NBLK), jnp.bfloat16),
        ],
        compiler_params=pltpu.CompilerParams(
            dimension_semantics=("arbitrary", "arbitrary"),
            vmem_limit_bytes=V7X_VMEM_LIMIT_BYTES,
        ),
        name="nsa_cmp_select",
    )(far, q, gates, kc_pad, vc_pad, band, wmap)


def _win_sel_body(q_ref, gate_ref, sel_ref, oc_ref, ks_ref, vs_ref, kw_ref, vw_ref, toep_ref, win_ref,
                  out_ref, s0_scr, s1_scr, p0_scr, p1_scr, m_scr, acc_scr, sw_scr):
    qi = pl.program_id(2)
    q0 = pl.multiple_of(qi * QB, QB)
    qg = q_ref[0]
    q_heads = [qg[:, h * HEAD_DIM:(h + 1) * HEAD_DIM] for h in range(HEADS_PER_GROUP)]
    q4 = jnp.concatenate(q_heads, axis=0)

    sel = sel_ref[0, 0]
    qa = jnp.concatenate([jnp.concatenate([sel] * HEADS_PER_GROUP, axis=0), q4], axis=-1)
    c_diag = qi // 2

    def scores(c):
        col = pl.multiple_of(c * KEY_TILE, KEY_TILE)
        s = jnp.dot(qa, ks_ref[0, 0, :, pl.ds(col, KEY_TILE)], preferred_element_type=jnp.float32)
        mm = qi - 2 * c
        bias = jnp.concatenate([toep_ref[0, jnp.clip(mm + 1, 0, N_TOEP - 1)],
                                toep_ref[0, jnp.clip(mm, 0, N_TOEP - 1)]], axis=-1)
        return s.reshape(HEADS_PER_GROUP, QB, KEY_TILE) + bias

    s_slots = (s0_scr, s1_scr)
    p_slots = (p0_scr, p1_scr)
    m_scr[...] = jnp.full(m_scr.shape, NEG_INF, jnp.float32)
    acc_scr[...] = jnp.zeros_like(acc_scr)
    p_slots[0][...] = jnp.zeros_like(p0_scr)
    s_slots[0][...] = scores(0)

    wcol = lax.broadcasted_iota(jnp.int32, (1, WINDOW + QB), 1)
    neg_left = jnp.where(q0 + wcol >= WINDOW, 0.0, NEG_INF)
    s_w = jnp.dot(q4, kw_ref[0, 0, :, pl.ds(q0, WINDOW + QB)], preferred_element_type=jnp.float32)
    sw_scr[...] = s_w.reshape(HEADS_PER_GROUP, QB, WINDOW + QB) + win_ref[0] + neg_left

    def weighted_values(p, col):
        return jnp.dot(p, vs_ref[0, 0, pl.ds(pl.multiple_of(col, KEY_TILE), KEY_TILE), :],
                       preferred_element_type=jnp.float32)

    def half_step(c, col_prev, cur):
        pv = weighted_values(p_slots[cur][...], col_prev)
        s_slots[1 - cur][...] = scores(c + 1)
        for h in range(HEADS_PER_GROUP):
            s = s_slots[cur][h]
            m_old = m_scr[h]
            m_new = jnp.maximum(m_old, jnp.max(s, axis=-1, keepdims=True))
            alpha = jnp.exp2(m_old - m_new)
            p = jnp.exp2(s - jnp.concatenate([m_new] * (KEY_TILE // LANES), axis=-1))
            p_slots[1 - cur][h * QB:(h + 1) * QB, :] = p.astype(jnp.bfloat16)
            m_scr[h] = m_new
            acc_scr[h * QB:(h + 1) * QB, :] = alpha * (acc_scr[h * QB:(h + 1) * QB, :]
                                                      + pv[h * QB:(h + 1) * QB, :])
        return c * KEY_TILE

    def unrolled_steps(first_tile, n_steps):
        def body(j, col_prev):
            for u in range(n_steps):
                col_prev = half_step(first_tile + n_steps * j + u, col_prev, u % 2)
            return col_prev
        return body

    n_tiles = c_diag + 1
    done = 0
    col_last = 0
    for n_steps in SEL_UNROLLS:
        last_level = n_steps == SEL_UNROLLS[-1]
        trips = (n_tiles - done + (n_steps - 1 if last_level else 0)) // n_steps
        col_last = lax.fori_loop(0, trips, unrolled_steps(done, n_steps), col_last)
        done = done + trips * n_steps
    acc_s = (acc_scr[...] + weighted_values(p_slots[0][...], col_last)).reshape(HEADS_PER_GROUP, QB, LANES)
    o_s = acc_s[:, :, :HEAD_DIM] / acc_s[:, :, HEAD_DIM:HEAD_DIM + 1]

    s_w = sw_scr[...]
    p_w = jnp.exp2(s_w - jnp.max(s_w, axis=-1, keepdims=True))
    pv_w = jnp.dot(p_w.reshape(QROWS, WINDOW + QB).astype(jnp.bfloat16),
                   vw_ref[0, 0, pl.ds(q0, WINDOW + QB), :],
                   preferred_element_type=jnp.float32).reshape(HEADS_PER_GROUP, QB, LANES)
    o_w = pv_w[:, :, :HEAD_DIM] / pv_w[:, :, HEAD_DIM:HEAD_DIM + 1]

    outs = []
    for h in range(HEADS_PER_GROUP):
        g_s = gate_ref[0, 0, :, 3 * h + 1:3 * h + 2]
        g_w = gate_ref[0, 0, :, 3 * h + 2:3 * h + 3]
        outs.append(g_s * o_s[h] + g_w * o_w[h])
    out_ref[0] = oc_ref[0] + jnp.concatenate(outs, axis=-1)


def _win_sel(q, gates, selneg, oc, ks_aug, vs, kw_pad, vw_pad, toep, win):
    B, S, _ = q.shape
    gw = HEADS_PER_GROUP * HEAD_DIM
    grid = (B, N_KV_GROUPS, S // QB)
    return pl.pallas_call(
        _win_sel_body,
        grid=grid,
        in_specs=[
            pl.BlockSpec((1, QB, gw), lambda b, g, i: (b, i, g)),
            pl.BlockSpec((1, 1, QB, 3 * HEADS_PER_GROUP), lambda b, g, i: (b, g, i, 0)),
            pl.BlockSpec((1, 1, QB, NBLK), lambda b, g, i: (b, g, i, 0)),
            pl.BlockSpec((1, QB, gw), lambda b, g, i: (b, i, g)),
            pl.BlockSpec((1, 1, NBLK + HEAD_DIM, ks_aug.shape[3]), lambda b, g, i: (b, g, 0, 0)),
            pl.BlockSpec((1, 1, vs.shape[2], LANES), lambda b, g, i: (b, g, 0, 0)),
            pl.BlockSpec((1, 1, HEAD_DIM, S + WINDOW), lambda b, g, i: (b, g, 0, 0)),
            pl.BlockSpec((1, 1, S + WINDOW, LANES), lambda b, g, i: (b, g, 0, 0)),
            pl.BlockSpec((1, N_TOEP, HEADS_PER_GROUP, QB, QB), lambda b, g, i: (g, 0, 0, 0, 0)),
            pl.BlockSpec((1, HEADS_PER_GROUP, QB, WINDOW + QB), lambda b, g, i: (g, 0, 0, 0)),
        ],
        out_specs=pl.BlockSpec((1, QB, gw), lambda b, g, i: (b, i, g)),
        out_shape=jax.ShapeDtypeStruct((B, S, N_HEADS * HEAD_DIM), jnp.float32),
        scratch_shapes=[
            pltpu.VMEM((HEADS_PER_GROUP, QB, KEY_TILE), jnp.float32),
            pltpu.VMEM((HEADS_PER_GROUP, QB, KEY_TILE), jnp.float32),
            pltpu.VMEM((QROWS, KEY_TILE), jnp.bfloat16),
            pltpu.VMEM((QROWS, KEY_TILE), jnp.bfloat16),
            pltpu.VMEM((HEADS_PER_GROUP, QB, LANES), jnp.float32),
            pltpu.VMEM((QROWS, LANES), jnp.float32),
            pltpu.VMEM((HEADS_PER_GROUP, QB, WINDOW + QB), jnp.float32),
        ],
        compiler_params=pltpu.CompilerParams(
            dimension_semantics=("arbitrary", "arbitrary", "arbitrary"),
            vmem_limit_bytes=V7X_VMEM_LIMIT_BYTES,
        ),
        name="nsa_win_sel",
    )(q, gates, selneg, oc, ks_aug, vs, kw_pad, vw_pad, toep, win)


def _nsa_attention(q, gates, kc, vc, kvsw, rel_bias):
    B, S, _ = q.shape
    assert S % KEY_TILE == 0 and S // L_SLC <= NBLK
    bf = jnp.bfloat16
    n_far = S // D_STRIDE
    toep, win, band, far = _bias_tables(rel_bias)
    cpad = ((0, 0), (0, 0), (CMP_PAD, n_far - kc.shape[2] + CMP_PAD), (0, 0))
    kc_pad = jnp.pad(kc, cpad).astype(bf)
    vc_pad = jnp.pad(vc, cpad).astype(bf)
    wmap = jnp.asarray(np.pad(_slc_map_matrix(n_far), ((CMP_PAD, CMP_PAD), (0, 0))), bf)
    oc, selneg = _cmp_select(q, gates, kc_pad, vc_pad, band, far, wmap)
    ks, vs, kw, vw = (kvsw.reshape(B, S, 4, N_KV_GROUPS, HEAD_DIM)[:, :, n] for n in range(4))
    blk_onehot = jnp.asarray((np.arange(S)[None, :] // L_SLC == np.arange(NBLK)[:, None]), bf)
    ks_aug = jnp.concatenate(
        [jnp.broadcast_to(blk_onehot, (B, N_KV_GROUPS, NBLK, S)), ks.transpose(0, 2, 3, 1)], axis=2)
    tail = 2 * KEY_TILE
    ks_aug = jnp.pad(ks_aug, ((0, 0), (0, 0), (0, 0), (0, tail)))
    kw_pad = jnp.pad(kw.transpose(0, 2, 3, 1), ((0, 0), (0, 0), (0, 0), (WINDOW, 0)))

    def with_ones(v):
        v = v.transpose(0, 2, 1, 3)
        ones = jnp.ones(v.shape[:-1] + (1,), v.dtype)
        return jnp.pad(jnp.concatenate([v, ones], axis=-1), ((0, 0),) * 3 + ((0, LANES - HEAD_DIM - 1),))

    vw_aug = jnp.pad(with_ones(vw), ((0, 0), (0, 0), (WINDOW, 0), (0, 0)))
    vs_aug = jnp.pad(with_ones(vs), ((0, 0), (0, 0), (0, tail), (0, 0)))
    gates_g = gates[:, :, :3 * N_HEADS].reshape(B, S, N_KV_GROUPS, 3 * HEADS_PER_GROUP).swapaxes(1, 2)
    return _win_sel(q, gates_g, selneg, oc, ks_aug, vs_aug, kw_pad, vw_aug, toep, win)


def _pool_layer_body(x_ref, halo_ref, gain_ref, shift_ref, scale_ref, gate_ref, pscale_ref, w_ref, o_ref):
    i = pl.program_id(1)
    x = x_ref[0]
    gain, shift, scale = gain_ref[...], shift_ref[0], scale_ref[0]
    h = _norm_modulate(x, gain, shift, scale)
    halo = _norm_modulate(halo_ref[0], gain, shift, scale) * (i > 0).astype(jnp.float32)
    hx = jnp.concatenate([halo, h], axis=0)
    t = i * POOL_TILE + lax.broadcasted_iota(jnp.int32, (POOL_TILE, 1), 0)
    outs = []
    run = hx
    width = 1
    for g, w in enumerate(POOL_WINDOWS):
        while width < w:
            run = run + pltpu.roll(run, width, axis=0)
            width *= 2
        lanes = slice(0, POOL_GROUP_DIM)
        cnt = jnp.minimum(t + 1, w).astype(jnp.float32)
        mix = run[POOL_HALO:, lanes] / cnt - h[:, g * POOL_GROUP_DIM:(g + 1) * POOL_GROUP_DIM]
        outs.append(jnp.dot(mix.astype(jnp.bfloat16), w_ref[g], preferred_element_type=jnp.float32))
        run = run[:, POOL_GROUP_DIM:]
    y = jnp.concatenate(outs, axis=-1) * pscale_ref[...]
    o_ref[0] = x + gate_ref[0] * y


def _pool_layer(x, gain, shift, scale, gate1, w_grp, pool_scale):
    B, S, D = x.shape
    assert S % POOL_TILE == 0 and POOL_HALO >= max(POOL_WINDOWS) - 1
    per_batch = pl.BlockSpec((1, 1, D), lambda b, i: (b, 0, 0))
    halo_blocks = POOL_TILE // POOL_HALO
    return pl.pallas_call(
        _pool_layer_body,
        grid=(B, S // POOL_TILE),
        in_specs=[
            pl.BlockSpec((1, POOL_TILE, D), lambda b, i: (b, i, 0)),
            pl.BlockSpec((1, POOL_HALO, D), lambda b, i: (b, jnp.maximum(i * halo_blocks - 1, 0), 0)),
            pl.BlockSpec((1, D), lambda b, i: (0, 0)),
            per_batch, per_batch, per_batch,
            pl.BlockSpec((1, D), lambda b, i: (0, 0)),
            pl.BlockSpec((N_POOL_GROUPS, POOL_GROUP_DIM, POOL_GROUP_DIM), lambda b, i: (0, 0, 0)),
        ],
        out_specs=pl.BlockSpec((1, POOL_TILE, D), lambda b, i: (b, i, 0)),
        out_shape=jax.ShapeDtypeStruct((B, S, D), jnp.float32),
        compiler_params=pltpu.CompilerParams(
            dimension_semantics=("arbitrary", "arbitrary"),
            vmem_limit_bytes=V7X_VMEM_LIMIT_BYTES,
        ),
        name="pool_layer",
    )(x, x, gain.reshape(1, D), shift[:, None, :], scale[:, None, :], gate1[:, None, :],
      pool_scale.reshape(1, D), w_grp.astype(jnp.bfloat16))


def _head_norm(v, seg_ref, gain):
    sq = v * v
    hi = sq.astype(jnp.bfloat16)
    lo = (sq - hi.astype(jnp.float32)).astype(jnp.bfloat16)
    ss = (jnp.dot(hi, seg_ref[...], preferred_element_type=jnp.float32)
          + jnp.dot(lo, seg_ref[...], preferred_element_type=jnp.float32))
    return v * lax.rsqrt(ss * (1.0 / HEAD_DIM) + EPS) * gain


def _qkv_proj_body(x_ref, gq_ref, shq_ref, scq_ref, gkv_ref, shkv_ref, sckv_ref, wq_ref, wg_ref, wkv_ref,
                   seg_ref, qgain_ref, kgain_ref, q_ref, gates_ref, kvc_ref, kvsw_ref):
    x = x_ref[...]
    xhat = x * lax.rsqrt(jnp.mean(x * x, axis=-1, keepdims=True) + EPS)
    hq = ((xhat * gq_ref[...]) * (1.0 + scq_ref[0]) + shq_ref[0]).astype(jnp.bfloat16)
    hkv = ((xhat * gkv_ref[...]) * (1.0 + sckv_ref[0]) + shkv_ref[0]).astype(jnp.bfloat16)
    pq = jnp.dot(hq, wq_ref[...], preferred_element_type=jnp.float32)
    gates_ref[...] = jax.nn.sigmoid(jnp.dot(hq, wg_ref[...], preferred_element_type=jnp.float32))
    pkv = jnp.dot(hkv, wkv_ref[...], preferred_element_type=jnp.float32)
    for g in range(N_KV_GROUPS):
        lanes = slice(g * GROUP_W, (g + 1) * GROUP_W)
        q_ref[:, lanes] = _head_norm(pq[:, lanes], seg_ref, qgain_ref[...]).astype(jnp.bfloat16)
    kvc_ref[...] = pkv[:, :2 * GROUP_W]
    k_s = _head_norm(pkv[:, 2 * GROUP_W:3 * GROUP_W], seg_ref, kgain_ref[0:1, :])
    k_w = _head_norm(pkv[:, 4 * GROUP_W:5 * GROUP_W], seg_ref, kgain_ref[1:2, :])
    kvsw_ref[...] = jnp.concatenate([k_s, pkv[:, 3 * GROUP_W:4 * GROUP_W], k_w, pkv[:, 5 * GROUP_W:]],
                                    axis=-1).astype(jnp.bfloat16)


def _qkv_proj(x, gain_q, shift_q, scale_q, gain_kv, shift_kv, scale_kv, w_qg, kv_w, q_gain, k_gain):
    B, S, D = x.shape
    T = B * S
    qd = N_HEADS * HEAD_DIM
    bf = jnp.bfloat16
    tiles_per_batch = S // ROW_TILE
    per_batch = pl.BlockSpec((1, 1, D), lambda i: (i // tiles_per_batch, 0, 0))
    const = lambda shape: pl.BlockSpec(shape, lambda i: (0,) * len(shape))
    seg = jnp.asarray(np.kron(np.eye(HEADS_PER_GROUP), np.ones((HEAD_DIM, HEAD_DIM))), bf)
    w_gate = jnp.pad(w_qg[:, qd:], ((0, 0), (0, LANES - 3 * N_HEADS))).astype(bf)
    qgain = jnp.tile(q_gain * HEAD_DIM ** -0.5, HEADS_PER_GROUP).reshape(1, GROUP_W)
    kgain = jnp.stack([jnp.tile(k_gain[1] * LOG2_E, N_KV_GROUPS), jnp.tile(k_gain[2] * LOG2_E, N_KV_GROUPS)])
    rows = lambda width: pl.BlockSpec((ROW_TILE, width), lambda i: (i, 0))
    return pl.pallas_call(
        _qkv_proj_body,
        grid=(T // ROW_TILE,),
        in_specs=[rows(D), const((1, D)), per_batch, per_batch, const((1, D)), per_batch, per_batch,
                  const((D, qd)), const((D, LANES)), const((D, 6 * GROUP_W)), const((GROUP_W, GROUP_W)),
                  const((1, GROUP_W)), const((2, GROUP_W))],
        out_specs=[rows(qd), rows(LANES), rows(2 * GROUP_W), rows(4 * GROUP_W)],
        out_shape=[jax.ShapeDtypeStruct((T, qd), bf), jax.ShapeDtypeStruct((T, LANES), jnp.float32),
                   jax.ShapeDtypeStruct((T, 2 * GROUP_W), jnp.float32),
                   jax.ShapeDtypeStruct((T, 4 * GROUP_W), bf)],
        compiler_params=pltpu.CompilerParams(
            dimension_semantics=("arbitrary",), vmem_limit_bytes=V7X_VMEM_LIMIT_BYTES),
        name="qkv_proj",
    )(x.reshape(T, D), gain_q.reshape(1, D), shift_q[:, None, :], scale_q[:, None, :],
      gain_kv.reshape(1, D), shift_kv[:, None, :], scale_kv[:, None, :],
      w_qg[:, :qd].astype(bf), w_gate, kv_w.astype(bf), seg, qgain, kgain)


def _compress_body(r_ref, pe_ref, w1_ref, w2_ref, gain_ref, o_ref):
    half = D_STRIDE * HEAD_DIM
    r = r_ref[0, 0, 0]
    top = jnp.dot(r, w1_ref[0, :half, :], preferred_element_type=jnp.float32)
    bot = jnp.dot(r, w1_ref[0, half:, :], preferred_element_type=jnp.float32)
    pe = jnp.dot(jnp.broadcast_to(pe_ref[0], (8, R_CMP * half)).astype(jnp.bfloat16), w1_ref[0],
                 preferred_element_type=jnp.float32)[0:1]
    n_chunks = r.shape[0]
    hidden = top + pltpu.roll(bot, n_chunks - 1, axis=0) + pe
    out = jnp.dot(jax.nn.gelu(hidden).astype(jnp.bfloat16), w2_ref[0], preferred_element_type=jnp.float32)
    normed = out * lax.rsqrt(jnp.mean(out * out, axis=-1, keepdims=True) + EPS) * gain_ref[...]
    o_ref[0, 0, 0] = jnp.where(pl.program_id(0) == 0, normed, out)


def _compressed_kv(kvc, B, S, cmp_pe_k, cmp_pe_v, cmp_k_w1, cmp_k_w2, cmp_v_w1, cmp_v_w2, k_gain):
    bf = jnp.bfloat16
    n_chunks = S // D_STRIDE
    half = D_STRIDE * HEAD_DIM
    hidden = cmp_k_w1.shape[1]
    r = kvc.astype(bf).reshape(B, n_chunks, D_STRIDE, 2, N_KV_GROUPS, HEAD_DIM)
    r = r.transpose(3, 0, 4, 1, 2, 5).reshape(2, B, N_KV_GROUPS, n_chunks, half)
    pe = jnp.stack([cmp_pe_k, cmp_pe_v]).reshape(2, 1, R_CMP * half)
    per_kind = lambda shape: pl.BlockSpec((1,) + shape, lambda kv, b, g: (kv,) + (0,) * len(shape))
    out = pl.pallas_call(
        _compress_body,
        grid=(2, B, N_KV_GROUPS),
        in_specs=[
            pl.BlockSpec((1, 1, 1, n_chunks, half), lambda kv, b, g: (kv, b, g, 0, 0)),
            per_kind((1, R_CMP * half)), per_kind((R_CMP * half, hidden)), per_kind((hidden, HEAD_DIM)),
            pl.BlockSpec((1, HEAD_DIM), lambda kv, b, g: (0, 0)),
        ],
        out_specs=pl.BlockSpec((1, 1, 1, n_chunks, HEAD_DIM), lambda kv, b, g: (kv, b, g, 0, 0)),
        out_shape=jax.ShapeDtypeStruct((2, B, N_KV_GROUPS, n_chunks, HEAD_DIM), jnp.float32),
        compiler_params=pltpu.CompilerParams(
            dimension_semantics=("arbitrary", "arbitrary", "arbitrary"),
            vmem_limit_bytes=V7X_VMEM_LIMIT_BYTES),
        name="cmp_mlp",
    )(r, pe, jnp.stack([cmp_k_w1, cmp_v_w1]).astype(bf), jnp.stack([cmp_k_w2, cmp_v_w2]).astype(bf),
      (k_gain[0] * LOG2_E).reshape(1, HEAD_DIM))
    n_cmp = n_chunks - R_CMP + 1
    return out[0, :, :, :n_cmp], out[1, :, :, :n_cmp]


def _oproj_router_body(a_ref, x_ref, gate_ref, wo_ref, gain_ref, shift_ref, scale_ref, wr_ref, br_ref,
                       o_ref, logit_ref):
    mix = jnp.dot(a_ref[...].astype(jnp.bfloat16), wo_ref[...], preferred_element_type=jnp.float32)
    x = x_ref[...] + gate_ref[0] * mix
    o_ref[...] = x
    h = _norm_modulate(x, gain_ref[...], shift_ref[0], scale_ref[0]).astype(jnp.bfloat16)
    logit_ref[...] = jnp.dot(h, wr_ref[...], preferred_element_type=jnp.float32) + br_ref[...]


def _oproj_router(attn, x, gate1, w_o, gain, shift, scale, w_router, b_router):
    B, S, D = x.shape
    T = B * S
    bf = jnp.bfloat16
    tiles_per_batch = S // ROW_TILE
    per_batch = pl.BlockSpec((1, 1, D), lambda i: (i // tiles_per_batch, 0, 0))
    const = lambda shape: pl.BlockSpec(shape, lambda i: (0,) * len(shape))
    rows = lambda width: pl.BlockSpec((ROW_TILE, width), lambda i: (i, 0))
    pad = ((0, 0), (0, LANES - N_EXPERTS))
    out, logits = pl.pallas_call(
        _oproj_router_body,
        grid=(T // ROW_TILE,),
        in_specs=[rows(D), rows(D), per_batch, const((D, D)), const((1, D)), per_batch, per_batch,
                  const((D, LANES)), const((1, LANES))],
        out_specs=[rows(D), rows(LANES)],
        out_shape=[jax.ShapeDtypeStruct((T, D), jnp.float32), jax.ShapeDtypeStruct((T, LANES), jnp.float32)],
        compiler_params=pltpu.CompilerParams(
            dimension_semantics=("arbitrary",), vmem_limit_bytes=V7X_VMEM_LIMIT_BYTES),
        name="oproj_router",
    )(attn.reshape(T, D), x.reshape(T, D), gate1[:, None, :], w_o.astype(bf), gain.reshape(1, D),
      shift[:, None, :], scale[:, None, :], jnp.pad(w_router, pad).astype(bf),
      jnp.pad(b_router.reshape(1, N_EXPERTS), pad))
    return out.reshape(B, S, D), logits[:, :N_EXPERTS]


def _route(logits):
    T = logits.shape[0]
    lane = jnp.arange(N_EXPERTS)[None, :]
    l0 = jnp.max(logits, axis=-1)
    e0 = jnp.argmax(logits, axis=-1)
    rest = jnp.where(lane == e0[:, None], -jnp.inf, logits)
    l1 = jnp.max(rest, axis=-1)
    e1 = jnp.argmax(rest, axis=-1)
    z = jnp.exp(l1 - l0)
    top_w = jnp.stack([1.0 / (1.0 + z), z / (1.0 + z)], axis=-1)
    oh = [(lane == e[:, None]).astype(jnp.int32) for e in (e0, e1)]
    cnt = oh[0] + oh[1]
    before = jnp.cumsum(cnt, axis=0) - cnt
    counts = before[-1] + cnt[-1]
    pcounts = (counts + ROW_TILE - 1) // ROW_TILE * ROW_TILE
    pends = jnp.cumsum(pcounts)
    pstarts = pends - pcounts
    dest = jnp.stack([jnp.sum((pstarts[None, :] + before) * o, axis=-1) for o in oh], axis=-1)
    n_blocks = -(-(T * TOP_K + N_EXPERTS * (ROW_TILE - 1)) // ROW_TILE)
    blk_start = jnp.arange(n_blocks)[:, None] * ROW_TILE
    blk_e = jnp.minimum(jnp.sum(pends[None, :] <= blk_start, axis=-1), N_EXPERTS - 1).astype(jnp.int32)
    n_used = (pends[-1:] // ROW_TILE).astype(jnp.int32)
    return top_w, dest.astype(jnp.int32), blk_e, n_used, n_blocks * ROW_TILE


def _moe_ffn(x, logits, gain, shift, scale, gate2, w_gu, w_dn):
    top_w, dest, blk_e, n_used, n_rows = _route(logits)
    rows = _moe_dispatch(x, gain, shift, scale, dest, n_rows)
    y = _grouped_swiglu(rows, blk_e, n_used, w_gu.astype(jnp.bfloat16), w_dn.astype(jnp.bfloat16),
                        ff_tile=FF_TILE_EXPERT)
    return _moe_combine(x, top_w, gate2, y, dest)


def kernel(x, c, ada_w, ada_b, norm_g, pool_w, pool_scale, q_w, q_gain, o_w, kv_ada_w, kv_ada_b, kv_norm_g, kv_w, cmp_pe_k, cmp_pe_v, cmp_k_w1, cmp_k_w2, cmp_v_w1, cmp_v_w2, k_gain, rel_bias, ffn_gu, ffn_dn, router_w, router_b, exp_gu, exp_dn):
    assert DEPTH == 2 and N_A_LAYERS == 1
    B, S, D = x.shape
    silu_c = jax.nn.silu(c)
    sh1, sc1, g1, sh2, sc2, g2 = jnp.split(silu_c @ ada_w[0] + ada_b[0], 6, axis=-1)
    x = _pool_layer(x, norm_g[0, 0], sh1, sc1, g1, pool_w[0], pool_scale[0])
    x = _dense_ffn(x, norm_g[0, 1], sh2, sc2, g2, ffn_gu[0], ffn_dn[0])
    sh1, sc1, g1, sh2, sc2, g2 = jnp.split(silu_c @ ada_w[1] + ada_b[1], 6, axis=-1)
    sh_kv, sc_kv = jnp.split(silu_c @ kv_ada_w + kv_ada_b, 2, axis=-1)
    q, gates, kvc, kvsw = _qkv_proj(x, norm_g[1, 0], sh1, sc1, kv_norm_g, sh_kv, sc_kv, q_w[0], kv_w,
                                    q_gain[0], k_gain)
    kc, vc = _compressed_kv(kvc, B, S, cmp_pe_k, cmp_pe_v, cmp_k_w1, cmp_k_w2, cmp_v_w1, cmp_v_w2, k_gain)
    attn = _nsa_attention(q.reshape(B, S, -1), gates.reshape(B, S, -1), kc, vc, kvsw.reshape(B, S, -1),
                          rel_bias)
    x, logits = _oproj_router(attn, x, g1, o_w[0], norm_g[1, 1], sh2, sc2, router_w[0], router_b[0])
    return _moe_ffn(x, logits, norm_g[1, 1], sh2, sc2, g2, exp_gu[0], exp_dn[0])
```
